```python
import jax
import jax.numpy as jnp
from jax import lax
import numpy as np

D_MODEL = 1024
BATCH = 16
SEQ = 2048
DEPTH = 2

GRID_W = 64
CTX_LEN = 256
EPS = 1e-6
N_MOD = 6

LRU_WIDTH = 512
LRU_HEADS = 8
LRU_HEAD_DIM = LRU_WIDTH // LRU_HEADS
LRU_CONV_W = 4
LRU_CONV_PAD = (LRU_CONV_W // 2, LRU_CONV_W - 1 - LRU_CONV_W // 2)
LRU_C = 8.0

N_HEADS = 8
N_KV_HEADS = 2
HEAD_DIM = 64
WINDOW = 128
ATT_BLOCK = 128
ROPE_BASE = 10000.0
Q_WIDTH = N_HEADS * HEAD_DIM
KV_WIDTH = N_KV_HEADS * HEAD_DIM
AB_IN = 2 * LRU_WIDTH + Q_WIDTH + 2 * KV_WIDTH
AB_OUT = LRU_WIDTH + Q_WIDTH

CONV_K = 31
CONV_PAD = ((CONV_K - 1) // 2, (CONV_K - 1) // 2)

N_EXPERTS = 256
TOP_K = 8
N_GROUPS = 8
TOPK_GROUPS = 4
EXPERT_DIM = 256
SHARED_DIM = 256
ROUTED_SCALE = 2.5
MOE_BLOCK = 256

N_EVEN = (DEPTH + 1) // 2
N_ODD = DEPTH // 2

kernel_name = 'hybrid_rglru_swa_conformer_moe_dit'


def _rmsnorm(x, g):
    x32 = x.astype(jnp.float32)
    y = x32 * lax.rsqrt(jnp.mean(x32 * x32, axis=-1, keepdims=True) + EPS)
    return (y * g.astype(jnp.float32)).astype(x.dtype)


def _modulate(h, shift, scale):
    return h * (1 + scale) + shift


def _dwconv(x, w, b, pad):
    y = lax.conv_general_dilated(x, w[:, None, :].astype(x.dtype), window_strides=(1,), padding=[pad],
                                 dimension_numbers=('NWC', 'WIO', 'NWC'), feature_group_count=x.shape[-1])
    return y + b


def _axial_rope_tables(n_tokens):
    rows = n_tokens // GRID_W
    row = jnp.repeat(jnp.arange(rows, dtype=jnp.float32), GRID_W)
    col = jnp.tile(jnp.arange(GRID_W, dtype=jnp.float32), rows)
    n_freq = HEAD_DIM // 4
    inv = ROPE_BASE ** (-jnp.arange(n_freq, dtype=jnp.float32) / n_freq)
    ang_r = row[:, None] * inv
    ang_c = col[:, None] * inv
    return (jnp.cos(ang_r), jnp.sin(ang_r), jnp.cos(ang_c), jnp.sin(ang_c))


def _rotate(x, cos, sin):
    x1, x2 = jnp.split(x, 2, axis=-1)
    cos = cos[:, None, :]
    sin = sin[:, None, :]
    return jnp.concatenate([x1 * cos - x2 * sin, x2 * cos + x1 * sin], axis=-1)


def _apply_axial_rope(x, tables):
    cos_r, sin_r, cos_c, sin_c = tables
    x32 = x.astype(jnp.float32)
    half = x.shape[-1] // 2
    out = jnp.concatenate([_rotate(x32[..., :half], cos_r, sin_r),
                           _rotate(x32[..., half:], cos_c, sin_c)], axis=-1)
    return out.astype(x.dtype)


def _rglru_coeffs(u, w_r, b_r, w_i, b_i, lam):
    bsz, n, width = u.shape
    uh = u.reshape(bsz, n, LRU_HEADS, LRU_HEAD_DIM)
    r = jax.nn.sigmoid(jnp.einsum('bshi,hij->bshj', uh, w_r).reshape(bsz, n, width) + b_r)
    i = jax.nn.sigmoid(jnp.einsum('bshi,hij->bshj', uh, w_i).reshape(bsz, n, width) + b_i)
    log_a = -LRU_C * r.astype(jnp.float32) * jax.nn.softplus(-lam.astype(jnp.float32))
    a = jnp.exp(log_a)
    b = jnp.sqrt(-jnp.expm1(2.0 * log_a)) * (i * u).astype(jnp.float32)
    return a, b


def _linear_scan(a, b, h0):
    b = b.at[:, 0].add(a[:, 0] * h0)

    def combine(left, right):
        return left[0] * right[0], right[0] * left[1] + right[1]

    return lax.associative_scan(combine, (a, b), axis=1)[1]


def _flip_if(t, rev):
    return jnp.flip(t, axis=1) if rev else t


def _bidir_rglru(u, uc, w_r, b_r, w_i, b_i, lam, with_ctx):
    outs, outs_c = [], []
    for d in range(2):
        rev = d == 1
        dc, ic = _rglru_coeffs(uc, w_r[d], b_r[d], w_i[d], b_i[d], lam[d])
        dx, ix = _rglru_coeffs(u, w_r[d], b_r[d], w_i[d], b_i[d], lam[d])
        h_c = _linear_scan(_flip_if(dc, rev), _flip_if(ic, rev), jnp.zeros((uc.shape[0], uc.shape[2]), jnp.float32))
        h_x = _linear_scan(_flip_if(dx, rev), _flip_if(ix, rev), h_c[:, -1])
        outs.append(_flip_if(h_x, rev))
        if with_ctx:
            outs_c.append(_flip_if(h_c, rev))
    y = (outs[0] + outs[1]).astype(u.dtype)
    yc = (outs_c[0] + outs_c[1]).astype(uc.dtype) if with_ctx else None
    return y, yc


def _window_attention_with_ctx(q, k, v, kc, vc, sink):
    bsz, n, _, hd = q.shape
    grp = N_HEADS // N_KV_HEADS
    nb = n // ATT_BLOCK
    scale = hd ** -0.5
    qb = q.reshape(bsz, nb, ATT_BLOCK, N_KV_HEADS, grp, hd)

    def band(t):
        tp = jnp.pad(t, ((0, 0), (ATT_BLOCK, ATT_BLOCK), (0, 0), (0, 0)))
        tp = tp.reshape(bsz, nb + 2, ATT_BLOCK, N_KV_HEADS, hd)
        return jnp.concatenate([tp[:, :-2], tp[:, 1:-1], tp[:, 2:]], axis=2)

    kw, vw = band(k), band(v)
    s_w = jnp.einsum('bnqkgd,bnjkd->bnkgqj', qb, kw).astype(jnp.float32) * scale
    s_c = jnp.einsum('bnqkgd,blkd->bnkgql', qb, kc).astype(jnp.float32) * scale
    qi = jnp.arange(ATT_BLOCK)[:, None]
    kj = jnp.arange(3 * ATT_BLOCK)[None, :]
    kpos = (jnp.arange(nb)[:, None, None] - 1) * ATT_BLOCK + kj
    mask = (jnp.abs(kj - qi - ATT_BLOCK) <= WINDOW) & (kpos >= 0) & (kpos < n)
    s_w = jnp.where(mask[None, :, None, None], s_w, -jnp.inf)
    sk = sink.astype(jnp.float32).reshape(1, 1, N_KV_HEADS, grp, 1, 1)
    m = jnp.maximum(jnp.maximum(s_w.max(-1, keepdims=True), s_c.max(-1, keepdims=True)), sk)
    p_w = jnp.exp(s_w - m)
    p_c = jnp.exp(s_c - m)
    denom = p_w.sum(-1, keepdims=True) + p_c.sum(-1, keepdims=True) + jnp.exp(sk - m)
    o = (jnp.einsum('bnkgqj,bnjkd->bnqkgd', (p_w / denom).astype(v.dtype), vw)
         + jnp.einsum('bnkgql,blkd->bnqkgd', (p_c / denom).astype(vc.dtype), vc))
    return o.reshape(bsz, n, N_HEADS * hd)


def _ctx_attention(qc, kc, vc, sink):
    bsz, n_ctx, _, hd = qc.shape
    grp = N_HEADS // N_KV_HEADS
    qg = qc.reshape(bsz, n_ctx, N_KV_HEADS, grp, hd)
    s = jnp.einsum('blkgd,bmkd->bkglm', qg, kc).astype(jnp.float32) * hd ** -0.5
    sk = sink.astype(jnp.float32).reshape(1, N_KV_HEADS, grp, 1, 1)
    m = jnp.maximum(s.max(-1, keepdims=True), sk)
    p = jnp.exp(s - m)
    p = p / (p.sum(-1, keepdims=True) + jnp.exp(sk - m))
    o = jnp.einsum('bkglm,bmkd->blkgd', p.astype(vc.dtype), vc)
    return o.reshape(bsz, n_ctx, N_HEADS * hd)


def _mixer_rglru_swa(h, hc, w_in, w_out, conv_w, conv_b, w_r, b_r, w_i, b_i, lam, sink, rope, with_ctx):
    bsz, n, _ = h.shape
    n_ctx = hc.shape[1]
    r0, r1, r2 = LRU_WIDTH, 2 * LRU_WIDTH, 2 * LRU_WIDTH + Q_WIDTH
    proj = h @ w_in
    u, g, q, kv = proj[..., :r0], proj[..., r0:r1], proj[..., r1:r2], proj[..., r2:]
    uc = hc @ w_in[:, :r0]
    kvc = hc @ w_in[:, r2:]
    u = _dwconv(u, conv_w, conv_b, LRU_CONV_PAD)
    uc = _dwconv(uc, conv_w, conv_b, LRU_CONV_PAD)
    rec, rec_c = _bidir_rglru(u, uc, w_r, b_r, w_i, b_i, lam, with_ctx)
    q = _apply_axial_rope(q.reshape(bsz, n, N_HEADS, HEAD_DIM), rope)
    k = _apply_axial_rope(kv[..., :KV_WIDTH].reshape(bsz, n, N_KV_HEADS, HEAD_DIM), rope)
    v = kv[..., KV_WIDTH:].reshape(bsz, n, N_KV_HEADS, HEAD_DIM)
    kc = kvc[..., :KV_WIDTH].reshape(bsz, n_ctx, N_KV_HEADS, HEAD_DIM)
    vc = kvc[..., KV_WIDTH:].reshape(bsz, n_ctx, N_KV_HEADS, HEAD_DIM)
    att = _window_attention_with_ctx(q, k, v, kc, vc, sink)
    y = jnp.concatenate([jax.nn.gelu(g) * rec, att], axis=-1) @ w_out
    if not with_ctx:
        return y, None
    gc = hc @ w_in[:, r0:r1]
    qc = (hc @ w_in[:, r1:r2]).reshape(bsz, n_ctx, N_HEADS, HEAD_DIM)
    att_c = _ctx_attention(qc, kc, vc, sink)
    yc = jnp.concatenate([jax.nn.gelu(gc) * rec_c, att_c], axis=-1) @ w_out
    return y, yc


def _conformer_conv(h, w_in, b_in, dw_w, dw_b, ln_g, ln_b, w_out, b_out):
    z = h @ w_in + b_in
    val, gt = jnp.split(z, 2, axis=-1)
    z = _dwconv(val * jax.nn.sigmoid(gt), dw_w, dw_b, CONV_PAD)
    z32 = z.astype(jnp.float32)
    mu = jnp.mean(z32, axis=-1, keepdims=True)
    var = jnp.mean(jnp.square(z32 - mu), axis=-1, keepdims=True)
    z = ((z32 - mu) * lax.rsqrt(var + EPS) * ln_g + ln_b).astype(h.dtype)
    return jax.nn.silu(z) @ w_out + b_out


def _routed_experts(t, idx, gate, w1, w3, w2):
    n_tok, d = t.shape
    n_assign = n_tok * TOP_K
    flat_e = idx.reshape(n_assign)
    order = jnp.argsort(flat_e)
    sorted_e = flat_e[order]
    sorted_tok = (order // TOP_K).astype(jnp.int32)
    sorted_gate = gate.reshape(n_assign)[order]
    counts = jnp.zeros((N_EXPERTS,), jnp.int32).at[flat_e].add(1)
    starts = jnp.cumsum(counts) - counts
    padded = (counts + MOE_BLOCK - 1) // MOE_BLOCK * MOE_BLOCK
    pad_ends = jnp.cumsum(padded)
    dest = pad_ends[sorted_e] - padded[sorted_e] + jnp.arange(n_assign) - starts[sorted_e]
    n_blk = -(-n_assign // MOE_BLOCK) + N_EXPERTS
    slot_tok = jnp.full((n_blk * MOE_BLOCK,), n_tok, jnp.int32).at[dest].set(sorted_tok)
    slot_gate = jnp.zeros((n_blk * MOE_BLOCK,), t.dtype).at[dest].set(sorted_gate)
    blk_e = jnp.minimum(jnp.searchsorted(pad_ends, jnp.arange(n_blk) * MOE_BLOCK, side='right'), N_EXPERTS - 1)
    t_pad = jnp.concatenate([t, jnp.zeros((1, d), t.dtype)], axis=0)

    def step(acc, blk):
        tok, g, e = blk
        rows = t_pad[tok]
        hid = jax.nn.silu(rows @ w1[e]) * (rows @ w3[e])
        return acc.at[tok].add((hid @ w2[e]) * g[:, None]), None

    acc, _ = lax.scan(step, jnp.zeros((n_tok + 1, d), t.dtype),
                      (slot_tok.reshape(n_blk, MOE_BLOCK), slot_gate.reshape(n_blk, MOE_BLOCK), blk_e))
    return acc[:n_tok]


def _moe(t, router_w, router_b, w1, w3, w2, sw1, sw3, sw2):
    n_tok = t.shape[0]
    scores = jax.nn.sigmoid((t @ router_w).astype(jnp.float32))
    biased = scores + router_b.astype(jnp.float32)
    grp = biased.reshape(n_tok, N_GROUPS, N_EXPERTS // N_GROUPS)
    grp_score = lax.top_k(grp, 2)[0].sum(-1)
    gidx = lax.top_k(grp_score, TOPK_GROUPS)[1]
    gmask = (gidx[..., None] == jnp.arange(N_GROUPS)).any(axis=-2)
    allowed = jnp.repeat(gmask, N_EXPERTS // N_GROUPS, axis=-1)
    eidx = lax.top_k(jnp.where(allowed, biased, -jnp.inf), TOP_K)[1]
    w = jnp.take_along_axis(scores, eidx, axis=-1)
    w = ROUTED_SCALE * w / jnp.sum(w, axis=-1, keepdims=True)
    routed = _routed_experts(t, eidx, w.astype(t.dtype), w1, w3, w2)
    shared = (jax.nn.silu(t @ sw1) * (t @ sw3)) @ sw2
    return routed + shared


def setup_inputs(seed: int = 0) -> dict:
    key = jax.random.key(seed)
    ks = iter(jax.random.split(key, 48))

    def nrm(shape, s):
        return jax.random.normal(next(ks), shape, jnp.float32) * s

    def gain(shape):
        return 1.0 + nrm(shape, 0.02)

    D = D_MODEL
    ac = jax.random.uniform(next(ks), (N_EVEN, 2, LRU_WIDTH), jnp.float32, 0.9, 0.999)
    a = ac ** (1.0 / LRU_C)
    lam = jnp.log(a) - jnp.log1p(-a)
    return {
        'x': nrm((BATCH, SEQ, D), 1.0),
        'c': nrm((BATCH, D), 1.0),
        'ctx': nrm((BATCH, CTX_LEN, D), 1.0),
        'c_ctx': nrm((D,), 1.0),
        'mod_w': nrm((DEPTH, D, N_MOD * D), 0.5 * D ** -0.5),
        'mod_b': nrm((DEPTH, N_MOD * D), 0.02),
        'norm_mix_g': gain((DEPTH, D)),
        'norm_ffn_g': gain((DEPTH, D)),
        'final_norm_g': gain((D,)),
        'ab_w_in': nrm((N_EVEN, D, AB_IN), D ** -0.5),
        'ab_w_out': nrm((N_EVEN, AB_OUT, D), AB_OUT ** -0.5),
        'lru_conv_w': nrm((N_EVEN, LRU_CONV_W, LRU_WIDTH), LRU_CONV_W ** -0.5),
        'lru_conv_b': nrm((N_EVEN, LRU_WIDTH), 0.02),
        'lru_wr': nrm((N_EVEN, 2, LRU_HEADS, LRU_HEAD_DIM, LRU_HEAD_DIM), LRU_HEAD_DIM ** -0.5),
        'lru_br': nrm((N_EVEN, 2, LRU_WIDTH), 0.02),
        'lru_wi': nrm((N_EVEN, 2, LRU_HEADS, LRU_HEAD_DIM, LRU_HEAD_DIM), LRU_HEAD_DIM ** -0.5),
        'lru_bi': nrm((N_EVEN, 2, LRU_WIDTH), 0.02),
        'lru_lambda': lam,
        'attn_sink': nrm((N_EVEN, N_HEADS), 0.5),
        'cm_w_in': nrm((N_ODD, D, 2 * D), D ** -0.5),
        'cm_b_in': nrm((N_ODD, 2 * D), 0.02),
        'cm_dw_w': nrm((N_ODD, CONV_K, D), CONV_K ** -0.5),
        'cm_dw_b': nrm((N_ODD, D), 0.02),
        'cm_ln_g': gain((N_ODD, D)),
        'cm_ln_b': nrm((N_ODD, D), 0.02),
        'cm_w_out': nrm((N_ODD, D, D), D ** -0.5),
        'cm_b_out': nrm((N_ODD, D), 0.02),
        'router_w': nrm((DEPTH, D, N_EXPERTS), D ** -0.5),
        'router_b': nrm((DEPTH, N_EXPERTS), 0.01),
        'exp_w1': nrm((DEPTH, N_EXPERTS, D, EXPERT_DIM), D ** -0.5),
        'exp_w3': nrm((DEPTH, N_EXPERTS, D, EXPERT_DIM), D ** -0.5),
        'exp_w2': nrm((DEPTH, N_EXPERTS, EXPERT_DIM, D), EXPERT_DIM ** -0.5),
        'shared_w1': nrm((DEPTH, D, SHARED_DIM), D ** -0.5),
        'shared_w3': nrm((DEPTH, D, SHARED_DIM), D ** -0.5),
        'shared_w2': nrm((DEPTH, SHARED_DIM, D), SHARED_DIM ** -0.5),
    }


def reference(x, c, ctx, c_ctx, mod_w, mod_b, norm_mix_g, norm_ffn_g, final_norm_g,
              ab_w_in, ab_w_out, lru_conv_w, lru_conv_b, lru_wr, lru_br, lru_wi, lru_bi, lru_lambda, attn_sink,
              cm_w_in, cm_b_in, cm_dw_w, cm_dw_b, cm_ln_g, cm_ln_b, cm_w_out, cm_b_out,
              router_w, router_b, exp_w1, exp_w3, exp_w2, shared_w1, shared_w3, shared_w2):
    bsz, n, d = x.shape
    n_ctx = ctx.shape[1]
    rope = _axial_rope_tables(n)
    last_even = ((DEPTH - 1) // 2) * 2
    sc = jax.nn.silu(c)
    scc = jax.nn.silu(c_ctx)
    xc = ctx
    for l in range(DEPTH):
        even = l % 2 == 0
        ctx_live = l < last_even
        mod = (sc @ mod_w[l] + mod_b[l]).reshape(bsz, N_MOD, 1, d)
        h = _modulate(_rmsnorm(x, norm_mix_g[l]), mod[:, 0], mod[:, 1])
        if even or ctx_live:
            modc = (scc @ mod_w[l] + mod_b[l]).reshape(N_MOD, d)
            hc = _modulate(_rmsnorm(xc, norm_mix_g[l]), modc[0], modc[1])
        if even:
            e = l // 2
            y, yc = _mixer_rglru_swa(h, hc, ab_w_in[e], ab_w_out[e], lru_conv_w[e], lru_conv_b[e],
                                     lru_wr[e], lru_br[e], lru_wi[e], lru_bi[e], lru_lambda[e],
                                     attn_sink[e], rope, ctx_live)
        else:
            o = l // 2
            cm = (cm_w_in[o], cm_b_in[o], cm_dw_w[o], cm_dw_b[o], cm_ln_g[o], cm_ln_b[o], cm_w_out[o], cm_b_out[o])
            y = _conformer_conv(h, *cm)
            yc = _conformer_conv(hc, *cm) if ctx_live else None
        x = x + mod[:, 2] * y
        h2 = _modulate(_rmsnorm(x, norm_ffn_g[l]), mod[:, 3], mod[:, 4])
        moe_w = (router_w[l], router_b[l], exp_w1[l], exp_w3[l], exp_w2[l], shared_w1[l], shared_w3[l], shared_w2[l])
        if ctx_live:
            xc = xc + modc[2] * yc
            h2c = _modulate(_rmsnorm(xc, norm_ffn_g[l]), modc[3], modc[4])
            tok = jnp.concatenate([h2.reshape(bsz * n, d), h2c.reshape(bsz * n_ctx, d)], axis=0)
            f = _moe(tok, *moe_w)
            x = x + mod[:, 5] * f[:bsz * n].reshape(bsz, n, d)
            xc = xc + modc[5] * f[bsz * n:].reshape(bsz, n_ctx, d)
        else:
            x = x + mod[:, 5] * _moe(h2.reshape(bsz * n, d), *moe_w).reshape(bsz, n, d)
    return _rmsnorm(x, final_norm_g)
```

```python
import functools

import jax
import jax.numpy as jnp
from jax import lax
from jax.experimental import pallas as pl
from jax.experimental.pallas import tpu as pltpu

f32 = jnp.float32
bf16 = jnp.bfloat16
i32 = jnp.int32
u32 = jnp.uint32

D = 1024
EPS = 1e-6
LRU_W = 512
LRU_C = 8.0
N_HEADS = 8
HEAD_DIM = 64
GRID_W = 64
ROPE_BASE = 10000.0
Q_W = 512
KV_W = 128
ATT_BLK = 128
CONV_K = 31
N_EXP = 256
TOP_K = 8
N_GRP = 8
TOPK_GRP = 4
GRP_SZ = N_EXP // N_GRP
EXP_D = 256
ROUTED_SCALE = 2.5
MOE_BLK = 256

VMEM_LIMIT_V7X = 56 * 1024 * 1024
MOD_ROWS = 24

_NT = (((1,), (1,)), ((), ()))


def _params(sem):
    return pltpu.CompilerParams(dimension_semantics=sem, vmem_limit_bytes=VMEM_LIMIT_V7X)


def _sigmoid(x):
    return 1.0 / (1.0 + jnp.exp(-x))


def _silu(x):
    return x * _sigmoid(x)


def _gelu_tanh(x):
    return 0.5 * x * (1.0 + jnp.tanh(0.7978845608028654 * (x + 0.044715 * (x * x * x))))


def _rms(x, g):
    return x * lax.rsqrt(jnp.mean(x * x, axis=-1, keepdims=True) + EPS) * g


def _rms_mod(x, g, shift, scale):
    return _rms(x, g) * (1.0 + scale) + shift


def _dot(a, b):
    return jnp.dot(a, b, preferred_element_type=f32)


def _mod_kernel(c_ref, w_ref, b_ref, o_ref):
    a = _silu(c_ref[...]).astype(bf16)
    o_ref[0] = _dot(a, w_ref[0].astype(bf16)) + b_ref[0]


def _mod_call(cc, mod_w, mod_b):
    depth, _, n = mod_w.shape
    tn = 1536
    return pl.pallas_call(
        _mod_kernel,
        grid=(depth, n // tn),
        in_specs=[pl.BlockSpec((MOD_ROWS, D), lambda l, j: (0, 0)),
                  pl.BlockSpec((1, D, tn), lambda l, j: (l, 0, j)),
                  pl.BlockSpec((1, 1, tn), lambda l, j: (l, 0, j))],
        out_specs=pl.BlockSpec((1, MOD_ROWS, tn), lambda l, j: (l, 0, j)),
        out_shape=jax.ShapeDtypeStruct((depth, MOD_ROWS, n), f32),
        compiler_params=_params(("parallel", "parallel")),
        name="mod",
    )(cc, mod_w, mod_b.reshape(depth, 1, n))


def _proj_in_kernel(x_ref, mod_ref, g_ref, w_ref, cos_ref, sin_ref, u_ref, gt_ref, q_ref, k_ref, v_ref):
    m = mod_ref[0]
    h = _rms_mod(x_ref[0], g_ref[...], m[0:1], m[1:2]).astype(bf16)
    p = _dot(h, w_ref[...])
    u_ref[0] = p[:, 0:512]
    gt_ref[0] = p[:, 512:1024]
    cos = cos_ref[...]
    sin = sin_ref[...]
    qs = []
    for j in range(4):
        qj = p[:, 1024 + j * 128:1152 + j * 128] * cos + p[:, 1792 + j * 128:1920 + j * 128] * sin
        qs.append(qj * (HEAD_DIM ** -0.5))
    q_ref[0] = jnp.concatenate(qs, axis=1).astype(bf16)
    k_ref[0] = (p[:, 1536:1664] * cos + p[:, 2304:2432] * sin).astype(bf16)
    v_ref[0] = p[:, 1664:1792].astype(bf16)


def _proj_in_call(x, mod, g, w_ext, cos, sin, tm=512):
    b, s, _ = x.shape
    nw = w_ext.shape[1]
    row = lambda bb, i: (bb, i, 0)
    return pl.pallas_call(
        _proj_in_kernel,
        grid=(b, s // tm),
        in_specs=[pl.BlockSpec((1, tm, D), row),
                  pl.BlockSpec((1, 6, D), lambda bb, i: (bb, 0, 0)),
                  pl.BlockSpec((1, D), lambda bb, i: (0, 0)),
                  pl.BlockSpec((D, nw), lambda bb, i: (0, 0)),
                  pl.BlockSpec((tm, 128), lambda bb, i: (i, 0)),
                  pl.BlockSpec((tm, 128), lambda bb, i: (i, 0))],
        out_specs=[pl.BlockSpec((1, tm, LRU_W), row), pl.BlockSpec((1, tm, LRU_W), row),
                   pl.BlockSpec((1, tm, Q_W), row), pl.BlockSpec((1, tm, KV_W), row),
                   pl.BlockSpec((1, tm, KV_W), row)],
        out_shape=[jax.ShapeDtypeStruct((b, s, LRU_W), f32), jax.ShapeDtypeStruct((b, s, LRU_W), f32),
                   jax.ShapeDtypeStruct((b, s, Q_W), bf16), jax.ShapeDtypeStruct((b, s, KV_W), bf16),
                   jax.ShapeDtypeStruct((b, s, KV_W), bf16)],
        compiler_params=_params(("parallel", "parallel")),
        name="proj_in",
    )(x, mod, g, w_ext, cos, sin)


def _proj_ctx_kernel(x_ref, mod_ref, g_ref, w_ref, u_ref, k_ref, v_ref):
    m = mod_ref[0]
    h = _rms_mod(x_ref[0], g_ref[...], m[0:1], m[1:2]).astype(bf16)
    p = _dot(h, w_ref[...])
    u_ref[0] = p[:, 0:512]
    k_ref[0] = p[:, 512:640].astype(bf16)
    v_ref[0] = p[:, 640:768].astype(bf16)


def _proj_ctx_call(ctx, mod, g, w_ctx):
    b, n_ctx, _ = ctx.shape
    row = lambda bb: (bb, 0, 0)
    return pl.pallas_call(
        _proj_ctx_kernel,
        grid=(b,),
        in_specs=[pl.BlockSpec((1, n_ctx, D), row),
                  pl.BlockSpec((1, 6, D), lambda bb: (MOD_ROWS - 8, 0, 0)),
                  pl.BlockSpec((1, D), lambda bb: (0, 0)),
                  pl.BlockSpec((D, 768), lambda bb: (0, 0))],
        out_specs=[pl.BlockSpec((1, n_ctx, LRU_W), row), pl.BlockSpec((1, n_ctx, KV_W), row),
                   pl.BlockSpec((1, n_ctx, KV_W), row)],
        out_shape=[jax.ShapeDtypeStruct((b, n_ctx, LRU_W), f32), jax.ShapeDtypeStruct((b, n_ctx, KV_W), bf16),
                   jax.ShapeDtypeStruct((b, n_ctx, KV_W), bf16)],
        compiler_params=_params(("parallel",)),
        name="proj_ctx",
    )(ctx, mod, g, w_ctx)


LRU_CHUNK = 128


def _rglru_kernel(u_ref, uc_ref, cw_ref, cb_ref, wg_ref, bg_ref, lam_ref, o_ref, pad_ref, cx_ref, cc_ref):
    s = u_ref.shape[1]
    n_ctx = uc_ref.shape[1]
    tc = LRU_CHUNK

    def conv_segment(src_ref, n, dst_ref):
        pad_ref[0:8] = jnp.zeros((8, LRU_W), f32)
        pad_ref[8:8 + n] = src_ref[0]
        pad_ref[8 + n:16 + n] = jnp.zeros((8, LRU_W), f32)
        for c in range(n // 256):
            acc = jnp.broadcast_to(cb_ref[...], (256, LRU_W))
            for k in range(4):
                acc = acc + cw_ref[k:k + 1, :] * pad_ref[c * 256 + 6 + k:c * 256 + 6 + k + 256, :]
            dst_ref[c * 256:(c + 1) * 256] = acc

    conv_segment(uc_ref, n_ctx, cc_ref)
    conv_segment(u_ref, s, cx_ref)

    rowm = lax.broadcasted_iota(i32, (tc, LRU_W), 0) & 7

    def scan_segment(src_ref, n, d, h0, write):
        lam = lam_ref[d]
        sp = jnp.maximum(-lam, 0.0) + jnp.log(1.0 + jnp.exp(-jnp.abs(lam)))
        nch = n // tc

        def chunk(ci, h):
            c = ci if d == 0 else nch - 1 - ci
            t0 = pl.multiple_of(c * tc, tc)
            uc = src_ref[pl.ds(t0, tc), :]
            gates = _dot(uc.astype(bf16), wg_ref[d]) + bg_ref[d]
            r = _sigmoid(gates[:, 0:LRU_W])
            ig = _sigmoid(gates[:, LRU_W:2 * LRU_W])
            log_a = (-LRU_C * sp) * r
            a = jnp.exp(log_a)
            bb = jnp.sqrt(-jnp.tanh(log_a) * (a * a + 1.0)) * (ig * uc)
            for sh in (1, 2, 4):
                if d == 0:
                    keep = rowm >= sh
                    a_sh = jnp.where(keep, pltpu.roll(a, sh, 0), 1.0)
                    b_sh = jnp.where(keep, pltpu.roll(bb, sh, 0), 0.0)
                else:
                    keep = rowm < 8 - sh
                    a_sh = jnp.where(keep, pltpu.roll(a, tc - sh, 0), 1.0)
                    b_sh = jnp.where(keep, pltpu.roll(bb, tc - sh, 0), 0.0)
                bb = a * b_sh + bb
                a = a * a_sh
            outs = [None] * (tc // 8)
            order = range(tc // 8) if d == 0 else range(tc // 8 - 1, -1, -1)
            for gi in order:
                hg = bb[gi * 8:(gi + 1) * 8] + a[gi * 8:(gi + 1) * 8] * h
                outs[gi] = hg
                h = hg[7:8] if d == 0 else hg[0:1]
            if write:
                hs = jnp.concatenate(outs, axis=0)
                if d == 0:
                    o_ref[0, pl.ds(t0, tc), :] = hs
                else:
                    o_ref[0, pl.ds(t0, tc), :] = o_ref[0, pl.ds(t0, tc), :] + hs
            return h

        return lax.fori_loop(0, nch, chunk, h0)

    for d in range(2):
        h = jnp.zeros((1, LRU_W), f32)
        h = scan_segment(cc_ref, n_ctx, d, h, False)
        scan_segment(cx_ref, s, d, h, True)


def _rglru_call(u, uc, conv_w, conv_b, wg, bg, lam):
    b, s, _ = u.shape
    n_ctx = uc.shape[1]
    return pl.pallas_call(
        _rglru_kernel,
        grid=(b,),
        in_specs=[pl.BlockSpec((1, s, LRU_W), lambda bb: (bb, 0, 0)),
                  pl.BlockSpec((1, n_ctx, LRU_W), lambda bb: (bb, 0, 0)),
                  pl.BlockSpec((4, LRU_W), lambda bb: (0, 0)),
                  pl.BlockSpec((1, LRU_W), lambda bb: (0, 0)),
                  pl.BlockSpec((2, LRU_W, 2 * LRU_W), lambda bb: (0, 0, 0)),
                  pl.BlockSpec((2, 1, 2 * LRU_W), lambda bb: (0, 0, 0)),
                  pl.BlockSpec((2, 1, LRU_W), lambda bb: (0, 0, 0))],
        out_specs=pl.BlockSpec((1, s, LRU_W), lambda bb: (bb, 0, 0)),
        out_shape=jax.ShapeDtypeStruct((b, s, LRU_W), f32),
        scratch_shapes=[pltpu.VMEM((s + 16, LRU_W), f32), pltpu.VMEM((s, LRU_W), f32),
                        pltpu.VMEM((n_ctx, LRU_W), f32)],
        compiler_params=_params(("parallel",)),
        name="rglru",
    )(u, uc, conv_w, conv_b, wg, bg, lam)


def _attn_kernel(sink_ref, q_ref, kp_ref, kc_ref, kn_ref, vp_ref, vc_ref, vn_ref, kx_ref, vx_ref, o_ref):
    n = pl.program_id(1)
    nb = pl.num_programs(1)
    blk = ATT_BLK
    q = q_ref[0]
    qall = jnp.concatenate([q[:, j * 128:(j + 1) * 128] for j in range(4)], axis=0)
    kw = jnp.concatenate([kp_ref[0], kc_ref[0], kn_ref[0]], axis=0)
    vw = jnp.concatenate([vp_ref[0], vc_ref[0], vn_ref[0]], axis=0)
    kx = kx_ref[0]
    vx = vx_ref[0]
    n_ctx = kx.shape[0]
    lo_w = lax.broadcasted_iota(i32, (3 * blk, 128), 1) < HEAD_DIM
    lo_x = lax.broadcasted_iota(i32, (n_ctx, 128), 1) < HEAD_DIM
    qi = lax.broadcasted_iota(i32, (4 * blk, 3 * blk), 0) & (blk - 1)
    kr = lax.broadcasted_iota(i32, (4 * blk, 3 * blk), 1) - blk
    lo = jnp.where(n > 0, -blk, 0)
    hi = jnp.where(n < nb - 1, 2 * blk, blk)
    dlt = kr - qi
    pen = jnp.where(dlt >= -blk, 0.0, -jnp.inf)
    pen = jnp.where(dlt <= blk, pen, -jnp.inf)
    pen = jnp.where(kr >= lo, pen, -jnp.inf)
    pen = jnp.where(kr < hi, pen, -jnp.inf)
    rb = lax.broadcasted_iota(i32, (4 * blk, 1), 0) // blk
    zero = jnp.zeros((), bf16)
    out = jnp.zeros((4 * blk, 128), f32)
    for half in range(2):
        sel_w = lo_w if half == 0 else jnp.logical_not(lo_w)
        sel_x = lo_x if half == 0 else jnp.logical_not(lo_x)
        s_w = lax.dot_general(qall, jnp.where(sel_w, kw, zero), _NT, preferred_element_type=f32) + pen
        s_c = lax.dot_general(qall, jnp.where(sel_x, kx, zero), _NT, preferred_element_type=f32)
        sk = jnp.where(rb == 0, sink_ref[4 * half],
                       jnp.where(rb == 1, sink_ref[4 * half + 1],
                                 jnp.where(rb == 2, sink_ref[4 * half + 2], sink_ref[4 * half + 3])))
        m = jnp.maximum(jnp.maximum(jnp.max(s_w, axis=1, keepdims=True), jnp.max(s_c, axis=1, keepdims=True)), sk)
        p_w = jnp.exp(s_w - m)
        p_c = jnp.exp(s_c - m)
        den = jnp.sum(p_w, axis=1, keepdims=True) + jnp.sum(p_c, axis=1, keepdims=True) + jnp.exp(sk - m)
        o = _dot(p_w.astype(bf16), jnp.where(sel_w, vw, zero)) + _dot(p_c.astype(bf16), jnp.where(sel_x, vx, zero))
        out = out + o / den
    o_ref[0] = jnp.concatenate([out[j * blk:(j + 1) * blk] for j in range(4)], axis=1).astype(bf16)


def _attn_call(sink, q, k, v, kx, vx):
    b, s, _ = q.shape
    n_ctx = kx.shape[1]
    nb = s // ATT_BLK
    cur = lambda bb, n: (bb, n, 0)
    prev = lambda bb, n: (bb, jnp.maximum(n - 1, 0), 0)
    nxt = lambda bb, n: (bb, jnp.minimum(n + 1, nb - 1), 0)
    kvb = (1, ATT_BLK, KV_W)
    return pl.pallas_call(
        _attn_kernel,
        grid=(b, nb),
        in_specs=[pl.BlockSpec(memory_space=pltpu.SMEM),
                  pl.BlockSpec((1, ATT_BLK, Q_W), cur),
                  pl.BlockSpec(kvb, prev), pl.BlockSpec(kvb, cur), pl.BlockSpec(kvb, nxt),
                  pl.BlockSpec(kvb, prev), pl.BlockSpec(kvb, cur), pl.BlockSpec(kvb, nxt),
                  pl.BlockSpec((1, n_ctx, KV_W), lambda bb, n: (bb, 0, 0)),
                  pl.BlockSpec((1, n_ctx, KV_W), lambda bb, n: (bb, 0, 0))],
        out_specs=pl.BlockSpec((1, ATT_BLK, Q_W), cur),
        out_shape=jax.ShapeDtypeStruct((b, s, Q_W), bf16),
        compiler_params=_params(("parallel", "parallel")),
        name="attn",
    )(sink, q, k, k, k, v, v, v, kx, vx)


def _mix_out_kernel(x_ref, mod_ref, gt_ref, rec_ref, att_ref, wr_ref, wa_ref, o_ref):
    m = mod_ref[0]
    a = (_gelu_tanh(gt_ref[0]) * rec_ref[0]).astype(bf16)
    y = _dot(a, wr_ref[...]) + _dot(att_ref[0], wa_ref[...])
    o_ref[0] = x_ref[0] + m[2:3] * y


def _mix_out_call(x, mod, gt, rec, att, w_rec, w_att, tm=512):
    b, s, _ = x.shape
    row = lambda bb, i: (bb, i, 0)
    return pl.pallas_call(
        _mix_out_kernel,
        grid=(b, s // tm),
        in_specs=[pl.BlockSpec((1, tm, D), row),
                  pl.BlockSpec((1, 6, D), lambda bb, i: (bb, 0, 0)),
                  pl.BlockSpec((1, tm, LRU_W), row), pl.BlockSpec((1, tm, LRU_W), row),
                  pl.BlockSpec((1, tm, Q_W), row),
                  pl.BlockSpec((LRU_W, D), lambda bb, i: (0, 0)),
                  pl.BlockSpec((Q_W, D), lambda bb, i: (0, 0))],
        out_specs=pl.BlockSpec((1, tm, D), row),
        out_shape=jax.ShapeDtypeStruct((b, s, D), f32),
        compiler_params=_params(("parallel", "parallel")),
        name="mix_out",
    )(x, mod, gt, rec, att, w_rec, w_att)


def _conf_in_kernel(x_ref, mod_ref, g_ref, w_ref, b_ref, o_ref):
    m = mod_ref[0]
    h = _rms_mod(x_ref[0], g_ref[...], m[0:1], m[1:2]).astype(bf16)
    z = _dot(h, w_ref[...]) + b_ref[...]
    o_ref[0] = z[:, 0:D] * _sigmoid(z[:, D:2 * D])


def _conf_in_call(x, mod, g, w, bias, tm=512):
    b, s, _ = x.shape
    row = lambda bb, i: (bb, i, 0)
    return pl.pallas_call(
        _conf_in_kernel,
        grid=(b, s // tm),
        in_specs=[pl.BlockSpec((1, tm, D), row),
                  pl.BlockSpec((1, 6, D), lambda bb, i: (bb, 0, 0)),
                  pl.BlockSpec((1, D), lambda bb, i: (0, 0)),
                  pl.BlockSpec((D, 2 * D), lambda bb, i: (0, 0)),
                  pl.BlockSpec((1, 2 * D), lambda bb, i: (0, 0))],
        out_specs=pl.BlockSpec((1, tm, D), row),
        out_shape=jax.ShapeDtypeStruct((b, s, D), f32),
        compiler_params=_params(("parallel", "parallel")),
        name="conf_in",
    )(x, mod, g, w, bias)


CONF_HALO = 16
CONF_ROWS = 32


def _conf_out_kernel(x_ref, mod_ref, zc_ref, zp_ref, zn_ref, dw_ref, db_ref, lg_ref, lb_ref, w_ref, b_ref, o_ref,
                     pad_ref, sh_ref, cv_ref):
    i = pl.program_id(1)
    nt = pl.num_programs(1)
    tm = zc_ref.shape[1]
    zero = jnp.zeros((CONF_HALO, D), f32)
    pad_ref[0:CONF_HALO] = jnp.where(i > 0, zp_ref[0], zero)
    pad_ref[CONF_HALO:CONF_HALO + tm] = zc_ref[0]
    pad_ref[CONF_HALO + tm:2 * CONF_HALO + tm] = jnp.where(i < nt - 1, zn_ref[0], zero)
    for r in range(8):
        sh_ref[r] = pad_ref[r:r + tm + 24, :]

    def chunk(c, carry):
        t0 = pl.multiple_of(c * CONF_ROWS, CONF_ROWS)
        acc = jnp.broadcast_to(db_ref[...], (CONF_ROWS, D))
        for k in range(CONV_K):
            kp = k + 1
            acc = acc + dw_ref[k:k + 1, :] * sh_ref[kp % 8, pl.ds(t0 + 8 * (kp // 8), CONF_ROWS), :]
        cv_ref[pl.ds(t0, CONF_ROWS), :] = acc
        return carry

    lax.fori_loop(0, tm // CONF_ROWS, chunk, 0)
    z = cv_ref[...]
    mu = jnp.mean(z, axis=-1, keepdims=True)
    zc = z - mu
    var = jnp.mean(zc * zc, axis=-1, keepdims=True)
    zn = zc * lax.rsqrt(var + EPS) * lg_ref[...] + lb_ref[...]
    y = _dot(_silu(zn).astype(bf16), w_ref[...]) + b_ref[...]
    m = mod_ref[0]
    o_ref[0] = x_ref[0] + m[2:3] * y


def _conf_out_call(x, mod, zg, dw_w, dw_b, ln_g, ln_b, w_out, b_out, tm=256):
    b, s, _ = x.shape
    row = lambda bb, i: (bb, i, 0)
    hb = tm // CONF_HALO
    nh = s // CONF_HALO
    vec = lambda bb, i: (0, 0)
    return pl.pallas_call(
        _conf_out_kernel,
        grid=(b, s // tm),
        in_specs=[pl.BlockSpec((1, tm, D), row),
                  pl.BlockSpec((1, 6, D), lambda bb, i: (bb, 0, 0)),
                  pl.BlockSpec((1, tm, D), row),
                  pl.BlockSpec((1, CONF_HALO, D), lambda bb, i: (bb, jnp.maximum(i * hb - 1, 0), 0)),
                  pl.BlockSpec((1, CONF_HALO, D), lambda bb, i: (bb, jnp.minimum((i + 1) * hb, nh - 1), 0)),
                  pl.BlockSpec((CONV_K + 1, D), vec),
                  pl.BlockSpec((1, D), vec), pl.BlockSpec((1, D), vec), pl.BlockSpec((1, D), vec),
                  pl.BlockSpec((D, D), vec), pl.BlockSpec((1, D), vec)],
        out_specs=pl.BlockSpec((1, tm, D), row),
        out_shape=jax.ShapeDtypeStruct((b, s, D), f32),
        scratch_shapes=[pltpu.VMEM((tm + 2 * CONF_HALO, D), f32), pltpu.VMEM((8, tm + 24, D), f32),
                        pltpu.VMEM((tm, D), f32)],
        compiler_params=_params(("parallel", "parallel")),
        name="conf_out",
    )(x, mod, zg, zg, zg, dw_w, dw_b, ln_g, ln_b, w_out, b_out)


def _ffn_pre_kernel(x_ref, mod_ref, g_ref, rwh_ref, rwl_ref, s13_ref, s2_ref, hp_ref, lg_ref, sh_ref):
    m = mod_ref[0]
    h2 = _rms_mod(x_ref[0], g_ref[...], m[3:4], m[4:5])
    hb = h2.astype(bf16)
    hbf = hb.astype(f32)
    hl = (h2 - hbf).astype(bf16)
    lg_ref[...] = (lax.dot_general(rwh_ref[...], hb, _NT, preferred_element_type=f32)
                   + lax.dot_general(rwh_ref[...], hl, _NT, preferred_element_type=f32)
                   + lax.dot_general(rwl_ref[...], hb, _NT, preferred_element_type=f32))
    a = _dot(hb, s13_ref[...])
    hid = (_silu(a[:, 0:EXP_D]) * a[:, EXP_D:2 * EXP_D]).astype(bf16)
    sh_ref[...] = _dot(hid, s2_ref[...])
    lo = lax.shift_right_logical(lax.bitcast_convert_type(hbf[:, 0:512], u32), jnp.uint32(16))
    hi = lax.bitcast_convert_type(hbf[:, 512:1024], u32) & jnp.uint32(0xFFFF0000)
    hp_ref[...] = lo | hi


def _ffn_pre_call(x, mod, g, rwh, rwl, s13, s2, tm=512):
    b, s, _ = x.shape
    nt = s // tm
    t = b * s
    flat = lambda bb, i: (bb * nt + i, 0)
    vec = lambda bb, i: (0, 0)
    return pl.pallas_call(
        _ffn_pre_kernel,
        grid=(b, nt),
        in_specs=[pl.BlockSpec((1, tm, D), lambda bb, i: (bb, i, 0)),
                  pl.BlockSpec((1, 6, D), lambda bb, i: (bb, 0, 0)),
                  pl.BlockSpec((1, D), vec),
                  pl.BlockSpec((N_EXP, D), vec), pl.BlockSpec((N_EXP, D), vec),
                  pl.BlockSpec((D, 2 * EXP_D), vec), pl.BlockSpec((EXP_D, D), vec)],
        out_specs=[pl.BlockSpec((tm, D // 2), flat),
                   pl.BlockSpec((N_EXP, tm), lambda bb, i: (0, bb * nt + i)),
                   pl.BlockSpec((tm, D), flat)],
        out_shape=[jax.ShapeDtypeStruct((t, D // 2), u32), jax.ShapeDtypeStruct((N_EXP, t), f32),
                   jax.ShapeDtypeStruct((t, D), f32)],
        compiler_params=_params(("parallel", "parallel")),
        name="ffn_pre",
    )(x, mod, g, rwh, rwl, s13, s2)


ROUTE_TILE = 256


def _route_kernel(lg_ref, rb_ref, tri_ref, e_ref, w_ref, r_ref, c_ref, base_ref):
    i = pl.program_id(0)
    tr = lg_ref.shape[1]

    @pl.when(i == 0)
    def _():
        base_ref[...] = jnp.zeros_like(base_ref)

    scores = _sigmoid(lg_ref[...])
    biased = scores + rb_ref[...]
    neg = -jnp.inf
    rowf = lax.broadcasted_iota(i32, (N_EXP, tr), 0).astype(f32)
    r32 = lax.broadcasted_iota(i32, (GRP_SZ, tr), 0).astype(f32)
    gs = []
    for g in range(N_GRP):
        seg = biased[g * GRP_SZ:(g + 1) * GRP_SZ]
        m1 = jnp.max(seg, axis=0, keepdims=True)
        i1 = jnp.min(jnp.where(seg == m1, r32, 2.0 * GRP_SZ), axis=0, keepdims=True)
        m2 = jnp.max(jnp.where(r32 == i1, neg, seg), axis=0, keepdims=True)
        gs.append(m1 + m2)
    allowed = []
    for g in range(N_GRP):
        beat = jnp.zeros((1, tr), f32)
        for h in range(N_GRP):
            if h < g:
                beat = beat + jnp.where(gs[h] >= gs[g], 1.0, 0.0)
            elif h > g:
                beat = beat + jnp.where(gs[h] > gs[g], 1.0, 0.0)
        allowed.append(jnp.broadcast_to(beat, (GRP_SZ, tr)))
    allowed = jnp.concatenate(allowed, axis=0)
    masked = jnp.where(allowed < float(TOPK_GRP), biased, neg)
    cnt = jnp.zeros((N_EXP, tr), f32)
    idxs, ws = [], []
    for _ in range(TOP_K):
        m = jnp.max(masked, axis=0, keepdims=True)
        idx = jnp.min(jnp.where(masked == m, rowf, 2.0 * N_EXP), axis=0, keepdims=True)
        hit = rowf == idx
        ws.append(jnp.sum(jnp.where(hit, scores, 0.0), axis=0, keepdims=True))
        masked = jnp.where(hit, neg, masked)
        cnt = cnt + jnp.where(hit, 1.0, 0.0)
        idxs.append(idx)
    wsum = ws[0]
    for k in range(1, TOP_K):
        wsum = wsum + ws[k]
    pos = _dot(cnt.astype(bf16), tri_ref[...]) + base_ref[...]
    ranks = [jnp.sum(jnp.where(rowf == idxs[k], pos, 0.0), axis=0, keepdims=True) for k in range(TOP_K)]
    e_ref[...] = jnp.concatenate(idxs, axis=0).astype(i32)
    w_ref[...] = jnp.concatenate([ROUTED_SCALE * ws[k] / wsum for k in range(TOP_K)], axis=0)
    r_ref[...] = jnp.concatenate(ranks, axis=0).astype(i32)
    base_ref[...] = base_ref[...] + jnp.sum(cnt, axis=1, keepdims=True)
    c_ref[...] = base_ref[...]


def _route_call(logits_t, router_b, tri):
    t = logits_t.shape[1]
    tr = ROUTE_TILE
    col = lambda i: (0, i)
    return pl.pallas_call(
        _route_kernel,
        grid=(t // tr,),
        in_specs=[pl.BlockSpec((N_EXP, tr), col),
                  pl.BlockSpec((N_EXP, 1), lambda i: (0, 0)),
                  pl.BlockSpec((tr, tr), lambda i: (0, 0))],
        out_specs=[pl.BlockSpec((TOP_K, tr), col), pl.BlockSpec((TOP_K, tr), col), pl.BlockSpec((TOP_K, tr), col),
                   pl.BlockSpec((N_EXP, 1), lambda i: (0, 0))],
        out_shape=[jax.ShapeDtypeStruct((TOP_K, t), i32), jax.ShapeDtypeStruct((TOP_K, t), f32),
                   jax.ShapeDtypeStruct((TOP_K, t), i32), jax.ShapeDtypeStruct((N_EXP, 1), f32)],
        scratch_shapes=[pltpu.VMEM((N_EXP, 1), f32)],
        compiler_params=_params(("arbitrary",)),
        name="route",
    )(logits_t, router_b, tri)


def _dest_kernel(e_ref, r_ref, off_ref, d_ref):
    tr = e_ref.shape[1]
    rowi = lax.broadcasted_iota(i32, (N_EXP, tr), 0)
    off = off_ref[...]
    e = e_ref[...]
    rows = [jnp.sum(jnp.where(rowi == e[k:k + 1], off, 0.0), axis=0, keepdims=True) for k in range(TOP_K)]
    d_ref[...] = jnp.concatenate(rows, axis=0).astype(i32) + r_ref[...]


def _dest_call(eidx, rank, pad_off):
    t = eidx.shape[1]
    tr = 512
    col = lambda i: (0, i)
    return pl.pallas_call(
        _dest_kernel,
        grid=(t // tr,),
        in_specs=[pl.BlockSpec((TOP_K, tr), col), pl.BlockSpec((TOP_K, tr), col),
                  pl.BlockSpec((N_EXP, 1), lambda i: (0, 0))],
        out_specs=pl.BlockSpec((TOP_K, tr), col),
        out_shape=jax.ShapeDtypeStruct((TOP_K, t), i32),
        compiler_params=_params(("parallel",)),
        name="dest",
    )(eidx, rank, pad_off)


DISPATCH_TILE = 256


def _dispatch_kernel(dest_hbm, h_ref, xs_in, xs_out, idx_ref, sem_idx, sem_row):
    del xs_in
    i = pl.program_id(0)
    ts = h_ref.shape[0]
    n = ts * TOP_K
    cp = pltpu.make_async_copy(dest_hbm.at[pl.ds(pl.multiple_of(i * n, n), n)], idx_ref, sem_idx)
    cp.start()
    cp.wait()

    def body(t, carry):
        for k in range(TOP_K):
            d = idx_ref[t * TOP_K + k]
            pltpu.make_async_copy(h_ref.at[pl.ds(t, 1), :], xs_out.at[pl.ds(d, 1), :], sem_row).start()
        return carry

    lax.fori_loop(0, ts, body, 0)

    pltpu.make_async_copy(xs_out.at[pl.ds(0, n), :], xs_out.at[pl.ds(0, n), :], sem_row).wait()


def _dispatch_call(dest_flat, h2p, n_rows):
    t, w = h2p.shape
    ts = DISPATCH_TILE
    xs0 = jnp.zeros((n_rows, w), u32)
    return pl.pallas_call(
        _dispatch_kernel,
        grid=(t // ts,),
        in_specs=[pl.BlockSpec(memory_space=pl.ANY),
                  pl.BlockSpec((ts, w), lambda i: (i, 0)),
                  pl.BlockSpec(memory_space=pl.ANY)],
        out_specs=pl.BlockSpec(memory_space=pl.ANY),
        out_shape=jax.ShapeDtypeStruct((n_rows, w), u32),
        scratch_shapes=[pltpu.SMEM((ts * TOP_K,), i32), pltpu.SemaphoreType.DMA(()), pltpu.SemaphoreType.DMA(())],
        input_output_aliases={2: 0},
        compiler_params=_params(("arbitrary",)),
        name="dispatch",
    )(dest_flat, h2p, xs0)


def _gmlp_kernel(be_ref, nb_ref, xs_ref, w1_ref, w3_ref, w2_ref, y_ref, w13_s, w2_s):
    i = pl.program_id(0)

    @pl.when(i < nb_ref[0])
    def _():
        e = be_ref[i]
        prev = be_ref[jnp.maximum(i - 1, 0)]

        @pl.when(jnp.logical_or(i == 0, e != prev))
        def _():
            w13_s[:, 0:EXP_D] = w1_ref[0].astype(bf16)
            w13_s[:, EXP_D:2 * EXP_D] = w3_ref[0].astype(bf16)
            w2_s[...] = w2_ref[0].astype(bf16)

        word = xs_ref[...]
        xlo = lax.bitcast_convert_type(lax.shift_left(word, jnp.uint32(16)), f32).astype(bf16)
        xhi = lax.bitcast_convert_type(word & jnp.uint32(0xFFFF0000), f32).astype(bf16)
        h = _dot(xlo, w13_s[0:512, :]) + _dot(xhi, w13_s[512:1024, :])
        hid = (_silu(h[:, 0:EXP_D]) * h[:, EXP_D:2 * EXP_D]).astype(bf16)
        y_ref[...] = _dot(hid, w2_s[...])

    @pl.when(i >= nb_ref[0])
    def _():
        y_ref[...] = jnp.zeros_like(y_ref)


def _gmlp_call(layer, blk_e, nb_used, xs, w1, w3, w2):
    n_rows = xs.shape[0]
    n_blk = n_rows // MOE_BLK
    blk = lambda i, be, nb: (jnp.minimum(i, nb[0] - 1), 0)
    wsel = lambda i, be, nb: (layer, be[jnp.minimum(i, nb[0] - 1)], 0, 0)
    gs = pltpu.PrefetchScalarGridSpec(
        num_scalar_prefetch=2,
        grid=(n_blk,),
        in_specs=[pl.BlockSpec((MOE_BLK, D // 2), blk),
                  pl.BlockSpec((None, 1, D, EXP_D), wsel), pl.BlockSpec((None, 1, D, EXP_D), wsel),
                  pl.BlockSpec((None, 1, EXP_D, D), wsel)],
        out_specs=pl.BlockSpec((MOE_BLK, D), lambda i, be, nb: (i, 0)),
        scratch_shapes=[pltpu.VMEM((D, 2 * EXP_D), bf16), pltpu.VMEM((EXP_D, D), bf16)],
    )
    return pl.pallas_call(
        _gmlp_kernel,
        grid_spec=gs,
        out_shape=jax.ShapeDtypeStruct((n_rows, D), f32),
        compiler_params=_params(("arbitrary",)),
        name="gmlp",
    )(blk_e, nb_used, xs, w1, w3, w2)


COMBINE_TILE = 128


def _combine_kernel(final, dest_hbm, y_hbm, x_ref, mod_ref, sh_ref, w_ref, gf_ref, o_ref, idx_ref, buf_ref,
                    sem_idx, sem_row):
    b = pl.program_id(0)
    i = pl.program_id(1)
    tm = x_ref.shape[1]
    n = tm * TOP_K
    tile = b * pl.num_programs(1) + i
    cp = pltpu.make_async_copy(dest_hbm.at[pl.ds(pl.multiple_of(tile * n, n), n)], idx_ref, sem_idx)
    cp.start()
    cp.wait()

    def body(t, carry):
        for k in range(TOP_K):
            d = idx_ref[t * TOP_K + k]
            pltpu.make_async_copy(y_hbm.at[pl.ds(d, 1), :], buf_ref.at[pl.ds(k * tm + t, 1), :], sem_row).start()
        return carry

    lax.fori_loop(0, tm, body, 0)

    pltpu.make_async_copy(y_hbm.at[pl.ds(0, n), :], buf_ref, sem_row).wait()
    w = w_ref[...]
    acc = sh_ref[...]
    for k in range(TOP_K):
        acc = acc + w[:, k:k + 1] * buf_ref[k * tm:(k + 1) * tm, :]
    m = mod_ref[0]
    out = x_ref[0] + m[5:6] * acc
    if final:
        out = _rms(out, gf_ref[...])
    o_ref[0] = out


def _combine_call(dest_flat, y, x, mod, shared, w_tok, g_final, final):
    b, s, _ = x.shape
    tm = COMBINE_TILE
    nt = s // tm
    flat = lambda bb, i: (bb * nt + i, 0)
    return pl.pallas_call(
        functools.partial(_combine_kernel, final),
        grid=(b, nt),
        in_specs=[pl.BlockSpec(memory_space=pl.ANY), pl.BlockSpec(memory_space=pl.ANY),
                  pl.BlockSpec((1, tm, D), lambda bb, i: (bb, i, 0)),
                  pl.BlockSpec((1, 6, D), lambda bb, i: (bb, 0, 0)),
                  pl.BlockSpec((tm, D), flat),
                  pl.BlockSpec((tm, TOP_K), flat),
                  pl.BlockSpec((1, D), lambda bb, i: (0, 0))],
        out_specs=pl.BlockSpec((1, tm, D), lambda bb, i: (bb, i, 0)),
        out_shape=jax.ShapeDtypeStruct((b, s, D), f32),
        scratch_shapes=[pltpu.SMEM((tm * TOP_K,), i32), pltpu.VMEM((TOP_K * tm, D), f32),
                        pltpu.SemaphoreType.DMA(()), pltpu.SemaphoreType.DMA(())],
        compiler_params=_params(("arbitrary", "arbitrary")),
        name="combine",
    )(dest_flat, y, x, mod, shared, w_tok, g_final)


def _moe_layer(layer, x1, mod, norm_g, router_w, router_b, w1, w3, w2, sw1, sw3, sw2, g_final, final):
    b, s, _ = x1.shape
    t = b * s
    rwt = router_w.T
    rwh = rwt.astype(bf16)
    rwl = (rwt - rwh.astype(f32)).astype(bf16)
    s13 = jnp.concatenate([sw1, sw3], axis=1).astype(bf16)
    h2p, logits_t, shared = _ffn_pre_call(x1, mod, norm_g.reshape(1, D), rwh, rwl, s13, sw2.astype(bf16))
    tri = (lax.broadcasted_iota(i32, (ROUTE_TILE, ROUTE_TILE), 0)
           < lax.broadcasted_iota(i32, (ROUTE_TILE, ROUTE_TILE), 1)).astype(bf16)
    eidx, w_t, rank, counts = _route_call(logits_t, router_b.reshape(N_EXP, 1).astype(f32), tri)
    cnt = counts.reshape(N_EXP).astype(i32)
    padded = (cnt + MOE_BLK - 1) // MOE_BLK * MOE_BLK
    pad_ends = jnp.cumsum(padded)
    pad_off = pad_ends - padded
    n_blk = t * TOP_K // MOE_BLK + N_EXP
    nb_used = (pad_ends[-1] // MOE_BLK).astype(i32)
    blk_start = jnp.minimum(jnp.arange(n_blk, dtype=i32), nb_used - 1) * MOE_BLK
    blk_e = jnp.minimum(jnp.searchsorted(pad_ends, blk_start, side="right"), N_EXP - 1).astype(i32)
    dest = _dest_call(eidx, rank, pad_off.astype(f32).reshape(N_EXP, 1))
    dest_flat = dest.T.reshape(t * TOP_K)
    xs = _dispatch_call(dest_flat, h2p, n_blk * MOE_BLK)
    y = _gmlp_call(layer, blk_e, nb_used.reshape(1), xs, w1, w3, w2)
    return _combine_call(dest_flat, y, x1, mod, shared, w_t.T, g_final.reshape(1, D), final)


_HEAD_ORDER = (0, 4, 1, 5, 2, 6, 3, 7)


def _rot_cols(w):
    d, n = w.shape
    w4 = w.reshape(d, n // 32, 2, 16)
    return jnp.stack([-w4[:, :, 1], w4[:, :, 0]], axis=2).reshape(d, n)


def _rope_tables(s):
    rows = s // GRID_W
    row = jnp.repeat(jnp.arange(rows, dtype=f32), GRID_W)
    col = jnp.tile(jnp.arange(GRID_W, dtype=f32), rows)
    n_freq = HEAD_DIM // 4
    inv = ROPE_BASE ** (-jnp.arange(n_freq, dtype=f32) / n_freq)
    ang_r = row[:, None] * inv
    ang_c = col[:, None] * inv
    cos = jnp.concatenate([jnp.cos(ang_r)] * 2 + [jnp.cos(ang_c)] * 2, axis=1)
    sin = jnp.concatenate([jnp.sin(ang_r)] * 2 + [jnp.sin(ang_c)] * 2, axis=1)
    return jnp.tile(cos, (1, 2)), jnp.tile(sin, (1, 2))


def _block_diag(w):
    h, dh, _ = w.shape
    eye = jnp.eye(h, dtype=w.dtype)
    return (eye[:, None, :, None] * w[:, :, None, :]).reshape(h * dh, h * dh)


def _even_layer_mixer(x, ctx, mod, norm_g, w_in, w_out, conv_w, conv_b, w_r, b_r, w_i, b_i, lam, sink):
    b, s, _ = x.shape
    r0, r1, r2 = LRU_W, 2 * LRU_W, 2 * LRU_W + Q_W
    wq = w_in[:, r1:r2].reshape(D, N_HEADS, HEAD_DIM)[:, jnp.array(_HEAD_ORDER)].reshape(D, Q_W)
    wk = w_in[:, r2:r2 + KV_W]
    w_ext = jnp.concatenate([w_in[:, :r1], wq, w_in[:, r2:], _rot_cols(wq), _rot_cols(wk)], axis=1).astype(bf16)
    w_ctx = jnp.concatenate([w_in[:, :r0], w_in[:, r2:]], axis=1).astype(bf16)
    cos, sin = _rope_tables(s)
    g = norm_g.reshape(1, D)
    u, gt, q, k, v = _proj_in_call(x, mod, g, w_ext, cos, sin)
    uc, kx, vx = _proj_ctx_call(ctx, mod, g, w_ctx)
    wg = jnp.stack([jnp.concatenate([_block_diag(w_r[d]), _block_diag(w_i[d])], axis=1) for d in range(2)]).astype(bf16)
    bg = jnp.stack([jnp.concatenate([b_r[d], b_i[d]])[None, :] for d in range(2)])
    rec = _rglru_call(u, uc, conv_w, conv_b.reshape(1, LRU_W), wg, bg, lam.reshape(2, 1, LRU_W))
    att = _attn_call(sink, q, k, v, kx, vx)
    w_att = w_out[LRU_W:].reshape(N_HEADS, HEAD_DIM, D)[jnp.array(_HEAD_ORDER)].reshape(Q_W, D).astype(bf16)
    return _mix_out_call(x, mod, gt, rec, att, w_out[:LRU_W].astype(bf16), w_att)


def kernel(x, c, ctx, c_ctx, mod_w, mod_b, norm_mix_g, norm_ffn_g, final_norm_g, ab_w_in, ab_w_out, lru_conv_w,
           lru_conv_b, lru_wr, lru_br, lru_wi, lru_bi, lru_lambda, attn_sink, cm_w_in, cm_b_in, cm_dw_w, cm_dw_b,
           cm_ln_g, cm_ln_b, cm_w_out, cm_b_out, router_w, router_b, exp_w1, exp_w3, exp_w2, shared_w1, shared_w3,
           shared_w2):
    bsz = x.shape[0]
    depth = mod_w.shape[0]
    assert bsz + 1 <= MOD_ROWS - 7
    cc = jnp.zeros((MOD_ROWS, D), f32).at[:bsz].set(c).at[MOD_ROWS - 8].set(c_ctx)
    mod_all = _mod_call(cc, mod_w, mod_b).reshape(depth, MOD_ROWS, 6, D)
    for l in range(depth):
        mod = mod_all[l]
        last = l == depth - 1
        if l % 2 == 0:
            e = l // 2
            assert depth <= 2
            x1 = _even_layer_mixer(x, ctx, mod, norm_mix_g[l], ab_w_in[e], ab_w_out[e], lru_conv_w[e], lru_conv_b[e],
                                   lru_wr[e], lru_br[e], lru_wi[e], lru_bi[e], lru_lambda[e], attn_sink[e])
        else:
            o = l // 2
            zg = _conf_in_call(x, mod, norm_mix_g[l].reshape(1, D), cm_w_in[o].astype(bf16), cm_b_in[o].reshape(1, 2 * D))
            dw = jnp.concatenate([cm_dw_w[o], jnp.zeros((1, D), f32)], axis=0)
            x1 = _conf_out_call(x, mod, zg, dw, cm_dw_b[o].reshape(1, D), cm_ln_g[o].reshape(1, D),
                                cm_ln_b[o].reshape(1, D), cm_w_out[o].astype(bf16), cm_b_out[o].reshape(1, D))
        x = _moe_layer(l, x1, mod, norm_ffn_g[l], router_w[l], router_b[l], exp_w1, exp_w3, exp_w2,
                       shared_w1[l], shared_w3[l], shared_w2[l], final_norm_g, last)
    return x
```

```python
import functools

import jax
import jax.numpy as jnp
from jax import lax
from jax.experimental import pallas as pl
from jax.experimental.pallas import tpu as pltpu

f32 = jnp.float32
bf16 = jnp.bfloat16
i32 = jnp.int32
u32 = jnp.uint32

D = 1024
EPS = 1e-6
LRU_W = 512
LRU_C = 8.0
N_HEADS = 8
HEAD_DIM = 64
GRID_W = 64
ROPE_BASE = 10000.0
Q_W = 512
KV_W = 128
ATT_BLK = 128
CONV_K = 31
N_EXP = 256
TOP_K = 8
N_GRP = 8
TOPK_GRP = 4
GRP_SZ = N_EXP // N_GRP
EXP_D = 256
ROUTED_SCALE = 2.5
MOE_BLK = 256
PK_CHUNKS = D // 2 // 128
Y_CHUNKS = D // 128

VMEM_LIMIT_V7X = 56 * 1024 * 1024
MOD_ROWS = 24

_NT = (((1,), (1,)), ((), ()))


def _params(sem):
    return pltpu.CompilerParams(dimension_semantics=sem, vmem_limit_bytes=VMEM_LIMIT_V7X)


def _sigmoid(x):
    return 1.0 / (1.0 + jnp.exp(-x))


def _silu(x):
    return x * _sigmoid(x)


def _gelu_tanh(x):
    return 0.5 * x * (1.0 + jnp.tanh(0.7978845608028654 * (x + 0.044715 * (x * x * x))))


def _rms(x, g):
    return x * lax.rsqrt(jnp.mean(x * x, axis=-1, keepdims=True) + EPS) * g


def _rms_mod(x, g, shift, scale):
    return _rms(x, g) * (1.0 + scale) + shift


def _dot(a, b):
    return jnp.dot(a, b, preferred_element_type=f32)


def _mod_kernel(c_ref, w_ref, b_ref, o_ref):
    a = _silu(c_ref[...]).astype(bf16)
    o_ref[0] = _dot(a, w_ref[0].astype(bf16)) + b_ref[0]


def _mod_call(cc, mod_w, mod_b):
    depth, _, n = mod_w.shape
    tn = 1536
    return pl.pallas_call(
        _mod_kernel,
        grid=(depth, n // tn),
        in_specs=[pl.BlockSpec((MOD_ROWS, D), lambda l, j: (0, 0)),
                  pl.BlockSpec((1, D, tn), lambda l, j: (l, 0, j)),
                  pl.BlockSpec((1, 1, tn), lambda l, j: (l, 0, j))],
        out_specs=pl.BlockSpec((1, MOD_ROWS, tn), lambda l, j: (l, 0, j)),
        out_shape=jax.ShapeDtypeStruct((depth, MOD_ROWS, n), f32),
        compiler_params=_params(("parallel", "parallel")),
        name="mod",
    )(cc, mod_w, mod_b.reshape(depth, 1, n))


def _proj_in_kernel(x_ref, mod_ref, g_ref, w_ref, cos_ref, sin_ref, u_ref, gt_ref, q_ref, k_ref, v_ref):
    m = mod_ref[0]
    h = _rms_mod(x_ref[0], g_ref[...], m[0:1], m[1:2]).astype(bf16)
    p = _dot(h, w_ref[...])
    u_ref[0] = p[:, 0:512]
    gt_ref[0] = p[:, 512:1024]
    cos = cos_ref[...]
    sin = sin_ref[...]
    qs = []
    for j in range(4):
        qj = p[:, 1024 + j * 128:1152 + j * 128] * cos + p[:, 1792 + j * 128:1920 + j * 128] * sin
        qs.append(qj * (HEAD_DIM ** -0.5))
    q_ref[0] = jnp.concatenate(qs, axis=1).astype(bf16)
    k_ref[0] = (p[:, 1536:1664] * cos + p[:, 2304:2432] * sin).astype(bf16)
    v_ref[0] = p[:, 1664:1792].astype(bf16)


def _proj_in_call(x, mod, g, w_ext, cos, sin, tm=512):
    b, s, _ = x.shape
    nw = w_ext.shape[1]
    row = lambda bb, i: (bb, i, 0)
    return pl.pallas_call(
        _proj_in_kernel,
        grid=(b, s // tm),
        in_specs=[pl.BlockSpec((1, tm, D), row),
                  pl.BlockSpec((1, 6, D), lambda bb, i: (bb, 0, 0)),
                  pl.BlockSpec((1, D), lambda bb, i: (0, 0)),
                  pl.BlockSpec((D, nw), lambda bb, i: (0, 0)),
                  pl.BlockSpec((tm, 128), lambda bb, i: (i, 0)),
                  pl.BlockSpec((tm, 128), lambda bb, i: (i, 0))],
        out_specs=[pl.BlockSpec((1, tm, LRU_W), row), pl.BlockSpec((1, tm, LRU_W), row),
                   pl.BlockSpec((1, tm, Q_W), row), pl.BlockSpec((1, tm, KV_W), row),
                   pl.BlockSpec((1, tm, KV_W), row)],
        out_shape=[jax.ShapeDtypeStruct((b, s, LRU_W), f32), jax.ShapeDtypeStruct((b, s, LRU_W), f32),
                   jax.ShapeDtypeStruct((b, s, Q_W), bf16), jax.ShapeDtypeStruct((b, s, KV_W), bf16),
                   jax.ShapeDtypeStruct((b, s, KV_W), bf16)],
        compiler_params=_params(("parallel", "parallel")),
        name="proj_in",
    )(x, mod, g, w_ext, cos, sin)


def _proj_ctx_kernel(x_ref, mod_ref, g_ref, w_ref, u_ref, k_ref, v_ref):
    m = mod_ref[0]
    h = _rms_mod(x_ref[0], g_ref[...], m[0:1], m[1:2]).astype(bf16)
    p = _dot(h, w_ref[...])
    u_ref[0] = p[:, 0:512]
    k_ref[0] = p[:, 512:640].astype(bf16)
    v_ref[0] = p[:, 640:768].astype(bf16)


def _proj_ctx_call(ctx, mod, g, w_ctx):
    b, n_ctx, _ = ctx.shape
    row = lambda bb: (bb, 0, 0)
    return pl.pallas_call(
        _proj_ctx_kernel,
        grid=(b,),
        in_specs=[pl.BlockSpec((1, n_ctx, D), row),
                  pl.BlockSpec((1, 6, D), lambda bb: (MOD_ROWS - 8, 0, 0)),
                  pl.BlockSpec((1, D), lambda bb: (0, 0)),
                  pl.BlockSpec((D, 768), lambda bb: (0, 0))],
        out_specs=[pl.BlockSpec((1, n_ctx, LRU_W), row), pl.BlockSpec((1, n_ctx, KV_W), row),
                   pl.BlockSpec((1, n_ctx, KV_W), row)],
        out_shape=[jax.ShapeDtypeStruct((b, n_ctx, LRU_W), f32), jax.ShapeDtypeStruct((b, n_ctx, KV_W), bf16),
                   jax.ShapeDtypeStruct((b, n_ctx, KV_W), bf16)],
        compiler_params=_params(("parallel",)),
        name="proj_ctx",
    )(ctx, mod, g, w_ctx)


LRU_CHUNK = 128


def _rglru_kernel(u_ref, uc_ref, cw_ref, cb_ref, wg_ref, bg_ref, lam_ref, o_ref, pad_ref, cx_ref, cc_ref):
    s = u_ref.shape[1]
    n_ctx = uc_ref.shape[1]
    tc = LRU_CHUNK

    def conv_segment(src_ref, n, dst_ref):
        pad_ref[0:8] = jnp.zeros((8, LRU_W), f32)
        pad_ref[8:8 + n] = src_ref[0]
        pad_ref[8 + n:16 + n] = jnp.zeros((8, LRU_W), f32)
        for c in range(n // 256):
            acc = jnp.broadcast_to(cb_ref[...], (256, LRU_W))
            for k in range(4):
                acc = acc + cw_ref[k:k + 1, :] * pad_ref[c * 256 + 6 + k:c * 256 + 6 + k + 256, :]
            dst_ref[c * 256:(c + 1) * 256] = acc

    conv_segment(uc_ref, n_ctx, cc_ref)
    conv_segment(u_ref, s, cx_ref)

    rowm = lax.broadcasted_iota(i32, (tc, LRU_W), 0) & 7

    def scan_segment(src_ref, n, d, h0, write):
        lam = lam_ref[d]
        sp = jnp.maximum(-lam, 0.0) + jnp.log(1.0 + jnp.exp(-jnp.abs(lam)))
        nch = n // tc

        def chunk(ci, h):
            c = ci if d == 0 else nch - 1 - ci
            t0 = pl.multiple_of(c * tc, tc)
            uc = src_ref[pl.ds(t0, tc), :]
            gates = _dot(uc.astype(bf16), wg_ref[d]) + bg_ref[d]
            r = _sigmoid(gates[:, 0:LRU_W])
            ig = _sigmoid(gates[:, LRU_W:2 * LRU_W])
            log_a = (-LRU_C * sp) * r
            a = jnp.exp(log_a)
            bb = jnp.sqrt(-jnp.tanh(log_a) * (a * a + 1.0)) * (ig * uc)
            for sh in (1, 2, 4):
                if d == 0:
                    keep = rowm >= sh
                    a_sh = jnp.where(keep, pltpu.roll(a, sh, 0), 1.0)
                    b_sh = jnp.where(keep, pltpu.roll(bb, sh, 0), 0.0)
                else:
                    keep = rowm < 8 - sh
                    a_sh = jnp.where(keep, pltpu.roll(a, tc - sh, 0), 1.0)
                    b_sh = jnp.where(keep, pltpu.roll(bb, tc - sh, 0), 0.0)
                bb = a * b_sh + bb
                a = a * a_sh
            outs = [None] * (tc // 8)
            order = range(tc // 8) if d == 0 else range(tc // 8 - 1, -1, -1)
            for gi in order:
                hg = bb[gi * 8:(gi + 1) * 8] + a[gi * 8:(gi + 1) * 8] * h
                outs[gi] = hg
                h = hg[7:8] if d == 0 else hg[0:1]
            if write:
                hs = jnp.concatenate(outs, axis=0)
                if d == 0:
                    o_ref[0, pl.ds(t0, tc), :] = hs
                else:
                    o_ref[0, pl.ds(t0, tc), :] = o_ref[0, pl.ds(t0, tc), :] + hs
            return h

        return lax.fori_loop(0, nch, chunk, h0)

    for d in range(2):
        h = jnp.zeros((1, LRU_W), f32)
        h = scan_segment(cc_ref, n_ctx, d, h, False)
        scan_segment(cx_ref, s, d, h, True)


def _rglru_call(u, uc, conv_w, conv_b, wg, bg, lam):
    b, s, _ = u.shape
    n_ctx = uc.shape[1]
    return pl.pallas_call(
        _rglru_kernel,
        grid=(b,),
        in_specs=[pl.BlockSpec((1, s, LRU_W), lambda bb: (bb, 0, 0)),
                  pl.BlockSpec((1, n_ctx, LRU_W), lambda bb: (bb, 0, 0)),
                  pl.BlockSpec((4, LRU_W), lambda bb: (0, 0)),
                  pl.BlockSpec((1, LRU_W), lambda bb: (0, 0)),
                  pl.BlockSpec((2, LRU_W, 2 * LRU_W), lambda bb: (0, 0, 0)),
                  pl.BlockSpec((2, 1, 2 * LRU_W), lambda bb: (0, 0, 0)),
                  pl.BlockSpec((2, 1, LRU_W), lambda bb: (0, 0, 0))],
        out_specs=pl.BlockSpec((1, s, LRU_W), lambda bb: (bb, 0, 0)),
        out_shape=jax.ShapeDtypeStruct((b, s, LRU_W), f32),
        scratch_shapes=[pltpu.VMEM((s + 16, LRU_W), f32), pltpu.VMEM((s, LRU_W), f32),
                        pltpu.VMEM((n_ctx, LRU_W), f32)],
        compiler_params=_params(("parallel",)),
        name="rglru",
    )(u, uc, conv_w, conv_b, wg, bg, lam)


def _attn_kernel(sink_ref, q_ref, kp_ref, kc_ref, kn_ref, vp_ref, vc_ref, vn_ref, kx_ref, vx_ref, o_ref):
    n = pl.program_id(1)
    nb = pl.num_programs(1)
    blk = ATT_BLK
    q = q_ref[0]
    qall = jnp.concatenate([q[:, j * 128:(j + 1) * 128] for j in range(4)], axis=0)
    kw = jnp.concatenate([kp_ref[0], kc_ref[0], kn_ref[0]], axis=0)
    vw = jnp.concatenate([vp_ref[0], vc_ref[0], vn_ref[0]], axis=0)
    kx = kx_ref[0]
    vx = vx_ref[0]
    n_ctx = kx.shape[0]
    lo_w = lax.broadcasted_iota(i32, (3 * blk, 128), 1) < HEAD_DIM
    lo_x = lax.broadcasted_iota(i32, (n_ctx, 128), 1) < HEAD_DIM
    qi = lax.broadcasted_iota(i32, (4 * blk, 3 * blk), 0) & (blk - 1)
    kr = lax.broadcasted_iota(i32, (4 * blk, 3 * blk), 1) - blk
    lo = jnp.where(n > 0, -blk, 0)
    hi = jnp.where(n < nb - 1, 2 * blk, blk)
    dlt = kr - qi
    pen = jnp.where(dlt >= -blk, 0.0, -jnp.inf)
    pen = jnp.where(dlt <= blk, pen, -jnp.inf)
    pen = jnp.where(kr >= lo, pen, -jnp.inf)
    pen = jnp.where(kr < hi, pen, -jnp.inf)
    rb = lax.broadcasted_iota(i32, (4 * blk, 1), 0) // blk
    zero = jnp.zeros((), bf16)
    out = jnp.zeros((4 * blk, 128), f32)
    for half in range(2):
        sel_w = lo_w if half == 0 else jnp.logical_not(lo_w)
        sel_x = lo_x if half == 0 else jnp.logical_not(lo_x)
        s_w = lax.dot_general(qall, jnp.where(sel_w, kw, zero), _NT, preferred_element_type=f32) + pen
        s_c = lax.dot_general(qall, jnp.where(sel_x, kx, zero), _NT, preferred_element_type=f32)
        sk = jnp.where(rb == 0, sink_ref[4 * half],
                       jnp.where(rb == 1, sink_ref[4 * half + 1],
                                 jnp.where(rb == 2, sink_ref[4 * half + 2], sink_ref[4 * half + 3])))
        m = jnp.maximum(jnp.maximum(jnp.max(s_w, axis=1, keepdims=True), jnp.max(s_c, axis=1, keepdims=True)), sk)
        p_w = jnp.exp(s_w - m)
        p_c = jnp.exp(s_c - m)
        den = jnp.sum(p_w, axis=1, keepdims=True) + jnp.sum(p_c, axis=1, keepdims=True) + jnp.exp(sk - m)
        o = _dot(p_w.astype(bf16), jnp.where(sel_w, vw, zero)) + _dot(p_c.astype(bf16), jnp.where(sel_x, vx, zero))
        out = out + o / den
    o_ref[0] = jnp.concatenate([out[j * blk:(j + 1) * blk] for j in range(4)], axis=1).astype(bf16)


def _attn_call(sink, q, k, v, kx, vx):
    b, s, _ = q.shape
    n_ctx = kx.shape[1]
    nb = s // ATT_BLK
    cur = lambda bb, n: (bb, n, 0)
    prev = lambda bb, n: (bb, jnp.maximum(n - 1, 0), 0)
    nxt = lambda bb, n: (bb, jnp.minimum(n + 1, nb - 1), 0)
    kvb = (1, ATT_BLK, KV_W)
    return pl.pallas_call(
        _attn_kernel,
        grid=(b, nb),
        in_specs=[pl.BlockSpec(memory_space=pltpu.SMEM),
                  pl.BlockSpec((1, ATT_BLK, Q_W), cur),
                  pl.BlockSpec(kvb, prev), pl.BlockSpec(kvb, cur), pl.BlockSpec(kvb, nxt),
                  pl.BlockSpec(kvb, prev), pl.BlockSpec(kvb, cur), pl.BlockSpec(kvb, nxt),
                  pl.BlockSpec((1, n_ctx, KV_W), lambda bb, n: (bb, 0, 0)),
                  pl.BlockSpec((1, n_ctx, KV_W), lambda bb, n: (bb, 0, 0))],
        out_specs=pl.BlockSpec((1, ATT_BLK, Q_W), cur),
        out_shape=jax.ShapeDtypeStruct((b, s, Q_W), bf16),
        compiler_params=_params(("parallel", "parallel")),
        name="attn",
    )(sink, q, k, k, k, v, v, v, kx, vx)


def _mix_out_kernel(x_ref, mod_ref, gt_ref, rec_ref, att_ref, wr_ref, wa_ref, o_ref):
    m = mod_ref[0]
    a = (_gelu_tanh(gt_ref[0]) * rec_ref[0]).astype(bf16)
    y = _dot(a, wr_ref[...]) + _dot(att_ref[0], wa_ref[...])
    o_ref[0] = x_ref[0] + m[2:3] * y


def _mix_out_call(x, mod, gt, rec, att, w_rec, w_att, tm=512):
    b, s, _ = x.shape
    row = lambda bb, i: (bb, i, 0)
    return pl.pallas_call(
        _mix_out_kernel,
        grid=(b, s // tm),
        in_specs=[pl.BlockSpec((1, tm, D), row),
                  pl.BlockSpec((1, 6, D), lambda bb, i: (bb, 0, 0)),
                  pl.BlockSpec((1, tm, LRU_W), row), pl.BlockSpec((1, tm, LRU_W), row),
                  pl.BlockSpec((1, tm, Q_W), row),
                  pl.BlockSpec((LRU_W, D), lambda bb, i: (0, 0)),
                  pl.BlockSpec((Q_W, D), lambda bb, i: (0, 0))],
        out_specs=pl.BlockSpec((1, tm, D), row),
        out_shape=jax.ShapeDtypeStruct((b, s, D), f32),
        compiler_params=_params(("parallel", "parallel")),
        name="mix_out",
    )(x, mod, gt, rec, att, w_rec, w_att)


def _conf_in_kernel(x_ref, mod_ref, g_ref, w_ref, b_ref, o_ref):
    m = mod_ref[0]
    h = _rms_mod(x_ref[0], g_ref[...], m[0:1], m[1:2]).astype(bf16)
    z = _dot(h, w_ref[...]) + b_ref[...]
    o_ref[0] = z[:, 0:D] * _sigmoid(z[:, D:2 * D])


def _conf_in_call(x, mod, g, w, bias, tm=512):
    b, s, _ = x.shape
    row = lambda bb, i: (bb, i, 0)
    return pl.pallas_call(
        _conf_in_kernel,
        grid=(b, s // tm),
        in_specs=[pl.BlockSpec((1, tm, D), row),
                  pl.BlockSpec((1, 6, D), lambda bb, i: (bb, 0, 0)),
                  pl.BlockSpec((1, D), lambda bb, i: (0, 0)),
                  pl.BlockSpec((D, 2 * D), lambda bb, i: (0, 0)),
                  pl.BlockSpec((1, 2 * D), lambda bb, i: (0, 0))],
        out_specs=pl.BlockSpec((1, tm, D), row),
        out_shape=jax.ShapeDtypeStruct((b, s, D), f32),
        compiler_params=_params(("parallel", "parallel")),
        name="conf_in",
    )(x, mod, g, w, bias)


CONF_HALO = 16
CONF_ROWS = 32


def _conf_out_kernel(x_ref, mod_ref, zc_ref, zp_ref, zn_ref, dw_ref, db_ref, lg_ref, lb_ref, w_ref, b_ref, o_ref,
                     pad_ref, sh_ref, cv_ref):
    i = pl.program_id(1)
    nt = pl.num_programs(1)
    tm = zc_ref.shape[1]
    zero = jnp.zeros((CONF_HALO, D), f32)
    pad_ref[0:CONF_HALO] = jnp.where(i > 0, zp_ref[0], zero)
    pad_ref[CONF_HALO:CONF_HALO + tm] = zc_ref[0]
    pad_ref[CONF_HALO + tm:2 * CONF_HALO + tm] = jnp.where(i < nt - 1, zn_ref[0], zero)
    for r in range(8):
        sh_ref[r] = pad_ref[r:r + tm + 24, :]

    def chunk(c, carry):
        t0 = pl.multiple_of(c * CONF_ROWS, CONF_ROWS)
        acc = jnp.broadcast_to(db_ref[...], (CONF_ROWS, D))
        for k in range(CONV_K):
            kp = k + 1
            acc = acc + dw_ref[k:k + 1, :] * sh_ref[kp % 8, pl.ds(t0 + 8 * (kp // 8), CONF_ROWS), :]
        cv_ref[pl.ds(t0, CONF_ROWS), :] = acc
        return carry

    lax.fori_loop(0, tm // CONF_ROWS, chunk, 0)
    z = cv_ref[...]
    mu = jnp.mean(z, axis=-1, keepdims=True)
    zc = z - mu
    var = jnp.mean(zc * zc, axis=-1, keepdims=True)
    zn = zc * lax.rsqrt(var + EPS) * lg_ref[...] + lb_ref[...]
    y = _dot(_silu(zn).astype(bf16), w_ref[...]) + b_ref[...]
    m = mod_ref[0]
    o_ref[0] = x_ref[0] + m[2:3] * y


def _conf_out_call(x, mod, zg, dw_w, dw_b, ln_g, ln_b, w_out, b_out, tm=256):
    b, s, _ = x.shape
    row = lambda bb, i: (bb, i, 0)
    hb = tm // CONF_HALO
    nh = s // CONF_HALO
    vec = lambda bb, i: (0, 0)
    return pl.pallas_call(
        _conf_out_kernel,
        grid=(b, s // tm),
        in_specs=[pl.BlockSpec((1, tm, D), row),
                  pl.BlockSpec((1, 6, D), lambda bb, i: (bb, 0, 0)),
                  pl.BlockSpec((1, tm, D), row),
                  pl.BlockSpec((1, CONF_HALO, D), lambda bb, i: (bb, jnp.maximum(i * hb - 1, 0), 0)),
                  pl.BlockSpec((1, CONF_HALO, D), lambda bb, i: (bb, jnp.minimum((i + 1) * hb, nh - 1), 0)),
                  pl.BlockSpec((CONV_K + 1, D), vec),
                  pl.BlockSpec((1, D), vec), pl.BlockSpec((1, D), vec), pl.BlockSpec((1, D), vec),
                  pl.BlockSpec((D, D), vec), pl.BlockSpec((1, D), vec)],
        out_specs=pl.BlockSpec((1, tm, D), row),
        out_shape=jax.ShapeDtypeStruct((b, s, D), f32),
        scratch_shapes=[pltpu.VMEM((tm + 2 * CONF_HALO, D), f32), pltpu.VMEM((8, tm + 24, D), f32),
                        pltpu.VMEM((tm, D), f32)],
        compiler_params=_params(("parallel", "parallel")),
        name="conf_out",
    )(x, mod, zg, zg, zg, dw_w, dw_b, ln_g, ln_b, w_out, b_out)


def _ffn_pre_kernel(x_ref, mod_ref, g_ref, rwh_ref, rwl_ref, s13_ref, s2_ref, hp_ref, lg_ref, sh_ref):
    m = mod_ref[0]
    h2 = _rms_mod(x_ref[0], g_ref[...], m[3:4], m[4:5])
    hb = h2.astype(bf16)
    hbf = hb.astype(f32)
    hl = (h2 - hbf).astype(bf16)
    lg_ref[...] = (lax.dot_general(rwh_ref[...], hb, _NT, preferred_element_type=f32)
                   + lax.dot_general(rwh_ref[...], hl, _NT, preferred_element_type=f32)
                   + lax.dot_general(rwl_ref[...], hb, _NT, preferred_element_type=f32))
    a = _dot(hb, s13_ref[...])
    hid = (_silu(a[:, 0:EXP_D]) * a[:, EXP_D:2 * EXP_D]).astype(bf16)
    sh_ref[...] = _dot(hid, s2_ref[...])
    lo = lax.shift_right_logical(lax.bitcast_convert_type(hbf[:, 0:512], u32), jnp.uint32(16))
    hi = lax.bitcast_convert_type(hbf[:, 512:1024], u32) & jnp.uint32(0xFFFF0000)
    word = lo | hi
    for i in range(word.shape[0] // 8):
        for c in range(PK_CHUNKS):
            hp_ref[pl.ds(8 * PK_CHUNKS * i + c, 8, stride=PK_CHUNKS), :] = word[8 * i:8 * i + 8, 128 * c:128 * c + 128]


def _ffn_pre_call(x, mod, g, rwh, rwl, s13, s2, tm=512):
    b, s, _ = x.shape
    nt = s // tm
    t = b * s
    flat = lambda bb, i: (bb * nt + i, 0)
    vec = lambda bb, i: (0, 0)
    return pl.pallas_call(
        _ffn_pre_kernel,
        grid=(b, nt),
        in_specs=[pl.BlockSpec((1, tm, D), lambda bb, i: (bb, i, 0)),
                  pl.BlockSpec((1, 6, D), lambda bb, i: (bb, 0, 0)),
                  pl.BlockSpec((1, D), vec),
                  pl.BlockSpec((N_EXP, D), vec), pl.BlockSpec((N_EXP, D), vec),
                  pl.BlockSpec((D, 2 * EXP_D), vec), pl.BlockSpec((EXP_D, D), vec)],
        out_specs=[pl.BlockSpec((tm * PK_CHUNKS, 128), flat),
                   pl.BlockSpec((N_EXP, tm), lambda bb, i: (0, bb * nt + i)),
                   pl.BlockSpec((tm, D), flat)],
        out_shape=[jax.ShapeDtypeStruct((t * PK_CHUNKS, 128), u32), jax.ShapeDtypeStruct((N_EXP, t), f32),
                   jax.ShapeDtypeStruct((t, D), f32)],
        compiler_params=_params(("parallel", "parallel")),
        name="ffn_pre",
    )(x, mod, g, rwh, rwl, s13, s2)


ROUTE_TILE = 256


def _route_kernel(lg_ref, rb_ref, tri_ref, e_ref, w_ref, r_ref, c_ref, base_ref):
    i = pl.program_id(0)
    tr = lg_ref.shape[1]

    @pl.when(i == 0)
    def _():
        base_ref[...] = jnp.zeros_like(base_ref)

    scores = _sigmoid(lg_ref[...])
    biased = scores + rb_ref[...]
    neg = -jnp.inf
    rowf = lax.broadcasted_iota(i32, (N_EXP, tr), 0).astype(f32)
    r32 = lax.broadcasted_iota(i32, (GRP_SZ, tr), 0).astype(f32)
    gs = []
    for g in range(N_GRP):
        seg = biased[g * GRP_SZ:(g + 1) * GRP_SZ]
        m1 = jnp.max(seg, axis=0, keepdims=True)
        i1 = jnp.min(jnp.where(seg == m1, r32, 2.0 * GRP_SZ), axis=0, keepdims=True)
        m2 = jnp.max(jnp.where(r32 == i1, neg, seg), axis=0, keepdims=True)
        gs.append(m1 + m2)
    allowed = []
    for g in range(N_GRP):
        beat = jnp.zeros((1, tr), f32)
        for h in range(N_GRP):
            if h < g:
                beat = beat + jnp.where(gs[h] >= gs[g], 1.0, 0.0)
            elif h > g:
                beat = beat + jnp.where(gs[h] > gs[g], 1.0, 0.0)
        allowed.append(jnp.broadcast_to(beat, (GRP_SZ, tr)))
    allowed = jnp.concatenate(allowed, axis=0)
    masked = jnp.where(allowed < float(TOPK_GRP), biased, neg)
    cnt = jnp.zeros((N_EXP, tr), f32)
    idxs, ws = [], []
    for _ in range(TOP_K):
        m = jnp.max(masked, axis=0, keepdims=True)
        idx = jnp.min(jnp.where(masked == m, rowf, 2.0 * N_EXP), axis=0, keepdims=True)
        hit = rowf == idx
        ws.append(jnp.sum(jnp.where(hit, scores, 0.0), axis=0, keepdims=True))
        masked = jnp.where(hit, neg, masked)
        cnt = cnt + jnp.where(hit, 1.0, 0.0)
        idxs.append(idx)
    wsum = ws[0]
    for k in range(1, TOP_K):
        wsum = wsum + ws[k]
    pos = _dot(cnt.astype(bf16), tri_ref[...]) + base_ref[...]
    ranks = [jnp.sum(jnp.where(rowf == idxs[k], pos, 0.0), axis=0, keepdims=True) for k in range(TOP_K)]
    e_ref[...] = jnp.concatenate(idxs, axis=0).astype(i32)
    w_ref[...] = jnp.concatenate([ROUTED_SCALE * ws[k] / wsum for k in range(TOP_K)], axis=0)
    r_ref[...] = jnp.concatenate(ranks, axis=0).astype(i32)
    base_ref[...] = base_ref[...] + jnp.sum(cnt, axis=1, keepdims=True)
    c_ref[...] = base_ref[...]


def _route_call(logits_t, router_b, tri):
    t = logits_t.shape[1]
    tr = ROUTE_TILE
    col = lambda i: (0, i)
    return pl.pallas_call(
        _route_kernel,
        grid=(t // tr,),
        in_specs=[pl.BlockSpec((N_EXP, tr), col),
                  pl.BlockSpec((N_EXP, 1), lambda i: (0, 0)),
                  pl.BlockSpec((tr, tr), lambda i: (0, 0))],
        out_specs=[pl.BlockSpec((TOP_K, tr), col), pl.BlockSpec((TOP_K, tr), col), pl.BlockSpec((TOP_K, tr), col),
                   pl.BlockSpec((N_EXP, 1), lambda i: (0, 0))],
        out_shape=[jax.ShapeDtypeStruct((TOP_K, t), i32), jax.ShapeDtypeStruct((TOP_K, t), f32),
                   jax.ShapeDtypeStruct((TOP_K, t), i32), jax.ShapeDtypeStruct((N_EXP, 1), f32)],
        scratch_shapes=[pltpu.VMEM((N_EXP, 1), f32)],
        compiler_params=_params(("arbitrary",)),
        name="route",
    )(logits_t, router_b, tri)


def _dest_kernel(e_ref, r_ref, off_ref, d_ref):
    tr = e_ref.shape[1]
    rowi = lax.broadcasted_iota(i32, (N_EXP, tr), 0)
    off = off_ref[...]
    e = e_ref[...]
    rows = [jnp.sum(jnp.where(rowi == e[k:k + 1], off, 0.0), axis=0, keepdims=True) for k in range(TOP_K)]
    d_ref[...] = jnp.concatenate(rows, axis=0).astype(i32) + r_ref[...]


def _dest_call(eidx, rank, pad_off):
    t = eidx.shape[1]
    tr = 512
    col = lambda i: (0, i)
    return pl.pallas_call(
        _dest_kernel,
        grid=(t // tr,),
        in_specs=[pl.BlockSpec((TOP_K, tr), col), pl.BlockSpec((TOP_K, tr), col),
                  pl.BlockSpec((N_EXP, 1), lambda i: (0, 0))],
        out_specs=pl.BlockSpec((TOP_K, tr), col),
        out_shape=jax.ShapeDtypeStruct((TOP_K, t), i32),
        compiler_params=_params(("parallel",)),
        name="dest",
    )(eidx, rank, pad_off)


DISPATCH_TILE = 512
_PAD_PIECES = (128, 64, 32, 16, 8, 4, 2, 1)


def _dispatch_kernel(cnt_ref, off_ref, nbt_ref, dest_hbm, h_hbm, xs_hbm, idx_ref, zero_ref, sem_idx, sem_row, sem_z):
    i = pl.program_id(0)
    nsteps = pl.num_programs(0)
    n = idx_ref.shape[0] // 2
    ts = n // TOP_K
    n_blk = xs_hbm.shape[0] // (MOE_BLK * PK_CHUNKS)

    def idx_copy(step):
        return pltpu.make_async_copy(dest_hbm.at[pl.ds(pl.multiple_of(step * n, n), n)], idx_ref.at[pl.ds(pl.multiple_of((step & 1) * n, n), n)], sem_idx.at[step & 1])

    def rows_wait(step):
        pltpu.make_async_copy(xs_hbm.at[pl.ds(0, n * PK_CHUNKS), :], xs_hbm.at[pl.ds(0, n * PK_CHUNKS), :],
                              sem_row.at[step & 1]).wait()

    def pad_copy(start_slot, p):
        return pltpu.make_async_copy(zero_ref.at[pl.ds(0, p * PK_CHUNKS), :],
                                     xs_hbm.at[pl.ds(start_slot * PK_CHUNKS, p * PK_CHUNKS), :], sem_z)

    def blk_copy(blk):
        return pltpu.make_async_copy(zero_ref, xs_hbm.at[pl.ds(blk * (MOE_BLK * PK_CHUNKS), MOE_BLK * PK_CHUNKS), :],
                                     sem_z)

    def for_each_pad_piece(fn):
        def per_expert(e, carry):
            c = cnt_ref[e]
            npad = ((c + (MOE_BLK - 1)) & (-MOE_BLK)) - c
            slot = off_ref[e] + c
            for p in _PAD_PIECES:
                @pl.when((npad & p) != 0)
                def _():
                    fn(pad_copy(slot, p))
                slot = slot + (npad & p)
            return carry

        lax.fori_loop(0, N_EXP, per_expert, 0)

    @pl.when(i == 0)
    def _():
        idx_copy(0).start()
        zero_ref[...] = jnp.zeros_like(zero_ref)
        for_each_pad_piece(lambda cp: cp.start())
        lax.fori_loop(nbt_ref[0], n_blk, lambda b, c: (blk_copy(b).start(), c)[1], 0)

    @pl.when(i + 1 < nsteps)
    def _():
        idx_copy(i + 1).start()

    idx_copy(i).wait()
    sl = i & 1

    def body(t2, carry):
        base = sl * n + t2 * (2 * TOP_K)
        ds = [idx_ref[base + j] for j in range(2 * TOP_K)]
        for j in range(2 * TOP_K):
            tok = i * ts + t2 * 2 + j // TOP_K
            pltpu.make_async_copy(h_hbm.at[pl.ds(tok * PK_CHUNKS, PK_CHUNKS), :],
                                  xs_hbm.at[pl.ds(ds[j] * PK_CHUNKS, PK_CHUNKS), :], sem_row.at[sl]).start()
        return carry

    lax.fori_loop(0, ts // 2, body, 0)

    @pl.when(i > 0)
    def _():
        rows_wait(i - 1)

    @pl.when(i == nsteps - 1)
    def _():
        rows_wait(i)
        for_each_pad_piece(lambda cp: cp.wait())
        lax.fori_loop(nbt_ref[0], n_blk, lambda b, c: (blk_copy(b).wait(), c)[1], 0)


def _dispatch_call(cnt, pad_off, nb_total, dest_flat, h2p, n_slots):
    t = h2p.shape[0] // PK_CHUNKS
    ts = DISPATCH_TILE
    gs = pltpu.PrefetchScalarGridSpec(
        num_scalar_prefetch=3,
        grid=(t // ts,),
        in_specs=[pl.BlockSpec(memory_space=pl.ANY), pl.BlockSpec(memory_space=pl.ANY)],
        out_specs=pl.BlockSpec(memory_space=pl.ANY),
        scratch_shapes=[pltpu.SMEM((2 * ts * TOP_K,), i32), pltpu.VMEM((MOE_BLK * PK_CHUNKS, 128), u32),
                        pltpu.SemaphoreType.DMA((2,)), pltpu.SemaphoreType.DMA((2,)), pltpu.SemaphoreType.DMA(())],
    )
    return pl.pallas_call(
        _dispatch_kernel,
        grid_spec=gs,
        out_shape=jax.ShapeDtypeStruct((n_slots * PK_CHUNKS, 128), u32),
        compiler_params=_params(("arbitrary",)),
        name="dispatch",
    )(cnt, pad_off, nb_total, dest_flat, h2p)


def _gmlp_kernel(nbe_ref, boff_ref, nbt_ref, w1_ref, w3_ref, w2_ref, xs_hbm, y_hbm, xbuf, ybuf, w13_s, w2_s,
                 sem_in, sem_out, sem_z):
    e = pl.program_id(0)
    nb = nbe_ref[e]
    b0 = boff_ref[e]
    total = nbt_ref[0]
    n_blk = y_hbm.shape[0] // (MOE_BLK * Y_CHUNKS)
    xrows = MOE_BLK * PK_CHUNKS
    yrows = MOE_BLK * Y_CHUNKS

    def in_copy(b):
        return pltpu.make_async_copy(xs_hbm.at[pl.ds(pl.multiple_of(b * xrows, xrows), xrows), :], xbuf.at[b & 1],
                                     sem_in.at[b & 1])

    def out_copy(b):
        return pltpu.make_async_copy(ybuf.at[b & 1], y_hbm.at[pl.ds(pl.multiple_of(b * yrows, yrows), yrows), :],
                                     sem_out.at[b & 1])

    def zero_copy(b):
        return pltpu.make_async_copy(ybuf.at[0], y_hbm.at[pl.ds(pl.multiple_of(b * yrows, yrows), yrows), :], sem_z)

    @pl.when(e == 0)
    def _():
        in_copy(0).start()

    @pl.when(nb > 0)
    def _():
        w13_s[:, 0:EXP_D] = w1_ref[0].astype(bf16)
        w13_s[:, EXP_D:2 * EXP_D] = w3_ref[0].astype(bf16)
        w2_s[...] = w2_ref[0].astype(bf16)

    def block(j, carry):
        b = b0 + j
        in_copy(b).wait()

        @pl.when(b + 1 < total)
        def _():
            in_copy(b + 1).start()

        @pl.when(b >= 2)
        def _():
            out_copy(b - 2).wait()

        xb = xbuf.at[b & 1]
        yb = ybuf.at[b & 1]
        cols = []
        for c in range(PK_CHUNKS):
            cols.append(jnp.concatenate(
                [xb[pl.ds(8 * PK_CHUNKS * g + c, 8, stride=PK_CHUNKS), :] for g in range(MOE_BLK // 8)], axis=0))
        word = jnp.concatenate(cols, axis=1)
        xlo = lax.bitcast_convert_type(lax.shift_left(word, jnp.uint32(16)), f32).astype(bf16)
        xhi = lax.bitcast_convert_type(word & jnp.uint32(0xFFFF0000), f32).astype(bf16)
        h = _dot(xlo, w13_s[0:512, :]) + _dot(xhi, w13_s[512:1024, :])
        hid = (_silu(h[:, 0:EXP_D]) * h[:, EXP_D:2 * EXP_D]).astype(bf16)
        y = _dot(hid, w2_s[...])
        for g in range(MOE_BLK // 8):
            for c in range(Y_CHUNKS):
                yb[pl.ds(8 * Y_CHUNKS * g + c, 8, stride=Y_CHUNKS), :] = y[8 * g:8 * g + 8, 128 * c:128 * c + 128]
        out_copy(b).start()
        return carry

    lax.fori_loop(0, nb, block, 0)

    @pl.when(e == pl.num_programs(0) - 1)
    def _():
        @pl.when(total >= 2)
        def _():
            out_copy(total - 2).wait()

        out_copy(total - 1).wait()
        ybuf[0] = jnp.zeros(ybuf.shape[1:], f32)
        lax.fori_loop(total, n_blk, lambda b, c: (zero_copy(b).start(), c)[1], 0)
        lax.fori_loop(total, n_blk, lambda b, c: (zero_copy(b).wait(), c)[1], 0)


def _gmlp_call(layer, nblk_e, blk_off, nb_total, xs, w1, w3, w2):
    n_slots = xs.shape[0] // PK_CHUNKS
    wsel = lambda e, *_: (layer, e, 0, 0)
    gs = pltpu.PrefetchScalarGridSpec(
        num_scalar_prefetch=3,
        grid=(N_EXP,),
        in_specs=[pl.BlockSpec((None, 1, D, EXP_D), wsel), pl.BlockSpec((None, 1, D, EXP_D), wsel),
                  pl.BlockSpec((None, 1, EXP_D, D), wsel), pl.BlockSpec(memory_space=pl.ANY)],
        out_specs=pl.BlockSpec(memory_space=pl.ANY),
        scratch_shapes=[pltpu.VMEM((2, MOE_BLK * PK_CHUNKS, 128), u32), pltpu.VMEM((2, MOE_BLK * Y_CHUNKS, 128), f32),
                        pltpu.VMEM((D, 2 * EXP_D), bf16), pltpu.VMEM((EXP_D, D), bf16),
                        pltpu.SemaphoreType.DMA((2,)), pltpu.SemaphoreType.DMA((2,)), pltpu.SemaphoreType.DMA(())],
    )
    return pl.pallas_call(
        _gmlp_kernel,
        grid_spec=gs,
        out_shape=jax.ShapeDtypeStruct((n_slots * Y_CHUNKS, 128), f32),
        compiler_params=_params(("arbitrary",)),
        name="gmlp",
    )(nblk_e, blk_off, nb_total, w1, w3, w2, xs)


COMBINE_TILE = 128


def _combine_kernel(final, dest_hbm, y_hbm, x_ref, mod_ref, sh_ref, w_ref, gf_ref, o_ref, idx_ref, buf_ref,
                    sem_idx, sem_row):
    tm = x_ref.shape[1]
    n = tm * TOP_K
    s = pl.program_id(0) * pl.num_programs(1) + pl.program_id(1)
    nsteps = pl.num_programs(0) * pl.num_programs(1)

    def idx_copy(step):
        return pltpu.make_async_copy(dest_hbm.at[pl.ds(pl.multiple_of(step * n, n), n)], idx_ref.at[pl.ds(pl.multiple_of((step & 1) * n, n), n)], sem_idx.at[step & 1])

    def issue_rows(step):
        sl = step & 1

        def body(t2, carry):
            base = sl * n + t2 * (2 * TOP_K)
            ds = [idx_ref[base + j] for j in range(2 * TOP_K)]
            for j in range(2 * TOP_K):
                t = t2 * 2 + j // TOP_K
                k = j % TOP_K
                pltpu.make_async_copy(y_hbm.at[pl.ds(ds[j] * Y_CHUNKS, Y_CHUNKS), :],
                                      buf_ref.at[sl, pl.ds((k * tm + t) * Y_CHUNKS, Y_CHUNKS), :],
                                      sem_row.at[sl]).start(priority=j % 2)
            return carry

        lax.fori_loop(0, tm // 2, body, 0)

    @pl.when(s == 0)
    def _():
        idx_copy(0).start()
        idx_copy(0).wait()
        issue_rows(0)

        @pl.when(nsteps > 1)
        def _():
            idx_copy(1).start()

    @pl.when(s + 1 < nsteps)
    def _():
        idx_copy(s + 1).wait()
        issue_rows(s + 1)

    @pl.when(s + 2 < nsteps)
    def _():
        idx_copy(s + 2).start()

    sl = s & 1
    pltpu.make_async_copy(y_hbm.at[pl.ds(0, n * Y_CHUNKS), :], buf_ref.at[sl], sem_row.at[sl]).wait()
    bs = buf_ref.at[sl]
    m = mod_ref[0]
    gate = m[5:6]

    def group(g, carry):
        r0 = pl.multiple_of(g * 8, 8)
        wg = w_ref[pl.ds(r0, 8), :]
        accs = [None] * Y_CHUNKS
        for k in range(TOP_K):
            wk = jnp.broadcast_to(wg[:, k:k + 1], (8, 128))
            for c in range(Y_CHUNKS):
                piece = bs[pl.ds((k * tm + r0) * Y_CHUNKS + c, 8, stride=Y_CHUNKS), :]
                accs[c] = wk * piece if k == 0 else accs[c] + wk * piece
        out = x_ref[0, pl.ds(r0, 8), :] + gate * (jnp.concatenate(accs, axis=1) + sh_ref[pl.ds(r0, 8), :])
        if final:
            out = _rms(out, gf_ref[...])
        o_ref[0, pl.ds(r0, 8), :] = out
        return carry

    lax.fori_loop(0, tm // 8, group, 0)


def _combine_call(dest_flat, y, x, mod, shared, w_tok, g_final, final):
    b, s, _ = x.shape
    tm = COMBINE_TILE
    nt = s // tm
    flat = lambda bb, i: (bb * nt + i, 0)
    return pl.pallas_call(
        functools.partial(_combine_kernel, final),
        grid=(b, nt),
        in_specs=[pl.BlockSpec(memory_space=pl.ANY), pl.BlockSpec(memory_space=pl.ANY),
                  pl.BlockSpec((1, tm, D), lambda bb, i: (bb, i, 0)),
                  pl.BlockSpec((1, 6, D), lambda bb, i: (bb, 0, 0)),
                  pl.BlockSpec((tm, D), flat),
                  pl.BlockSpec((tm, TOP_K), flat),
                  pl.BlockSpec((1, D), lambda bb, i: (0, 0))],
        out_specs=pl.BlockSpec((1, tm, D), lambda bb, i: (bb, i, 0)),
        out_shape=jax.ShapeDtypeStruct((b, s, D), f32),
        scratch_shapes=[pltpu.SMEM((2 * tm * TOP_K,), i32), pltpu.VMEM((2, TOP_K * tm * Y_CHUNKS, 128), f32),
                        pltpu.SemaphoreType.DMA((2,)), pltpu.SemaphoreType.DMA((2,))],
        compiler_params=_params(("arbitrary", "arbitrary")),
        name="combine",
    )(dest_flat, y, x, mod, shared, w_tok, g_final)


def _moe_layer(layer, x1, mod, norm_g, router_w, router_b, w1, w3, w2, sw1, sw3, sw2, g_final, final):
    b, s, _ = x1.shape
    t = b * s
    rwt = router_w.T
    rwh = rwt.astype(bf16)
    rwl = (rwt - rwh.astype(f32)).astype(bf16)
    s13 = jnp.concatenate([sw1, sw3], axis=1).astype(bf16)
    h2p, logits_t, shared = _ffn_pre_call(x1, mod, norm_g.reshape(1, D), rwh, rwl, s13, sw2.astype(bf16))
    tri = (lax.broadcasted_iota(i32, (ROUTE_TILE, ROUTE_TILE), 0)
           < lax.broadcasted_iota(i32, (ROUTE_TILE, ROUTE_TILE), 1)).astype(bf16)
    eidx, w_t, rank, counts = _route_call(logits_t, router_b.reshape(N_EXP, 1).astype(f32), tri)
    cnt = counts.reshape(N_EXP).astype(i32)
    nblk_e = (cnt + MOE_BLK - 1) // MOE_BLK
    blk_ends = jnp.cumsum(nblk_e)
    blk_off = blk_ends - nblk_e
    pad_off = blk_off * MOE_BLK
    nb_total = blk_ends[-1:].astype(i32)
    n_blk = t * TOP_K // MOE_BLK + N_EXP
    dest = _dest_call(eidx, rank, pad_off.astype(f32).reshape(N_EXP, 1))
    dest_flat = dest.T.reshape(t * TOP_K)
    xs = _dispatch_call(cnt, pad_off, nb_total, dest_flat, h2p, n_blk * MOE_BLK)
    y = _gmlp_call(layer, nblk_e, blk_off, nb_total, xs, w1, w3, w2)
    return _combine_call(dest_flat, y, x1, mod, shared, w_t.T, g_final.reshape(1, D), final)


def _rot_cols(w):
    d, n = w.shape
    w4 = w.reshape(d, n // 32, 2, 16)
    return jnp.stack([-w4[:, :, 1], w4[:, :, 0]], axis=2).reshape(d, n)


def _rope_tables(s):
    rows = s // GRID_W
    row = jnp.repeat(jnp.arange(rows, dtype=f32), GRID_W)
    col = jnp.tile(jnp.arange(GRID_W, dtype=f32), rows)
    n_freq = HEAD_DIM // 4
    inv = ROPE_BASE ** (-jnp.arange(n_freq, dtype=f32) / n_freq)
    ang_r = row[:, None] * inv
    ang_c = col[:, None] * inv
    cos = jnp.concatenate([jnp.cos(ang_r)] * 2 + [jnp.cos(ang_c)] * 2, axis=1)
    sin = jnp.concatenate([jnp.sin(ang_r)] * 2 + [jnp.sin(ang_c)] * 2, axis=1)
    return jnp.tile(cos, (1, 2)), jnp.tile(sin, (1, 2))


def _block_diag(w):
    h, dh, _ = w.shape
    eye = jnp.eye(h, dtype=w.dtype)
    return (eye[:, None, :, None] * w[:, :, None, :]).reshape(h * dh, h * dh)


def _even_layer_mixer(x, ctx, mod, norm_g, w_in, w_out, conv_w, conv_b, w_r, b_r, w_i, b_i, lam, sink):
    b, s, _ = x.shape
    r0, r1, r2 = LRU_W, 2 * LRU_W, 2 * LRU_W + Q_W
    wq = w_in[:, r1:r2].reshape(D, 2, 4, HEAD_DIM).transpose(0, 2, 1, 3).reshape(D, Q_W)
    wk = w_in[:, r2:r2 + KV_W]
    w_ext = jnp.concatenate([w_in[:, :r1], wq, w_in[:, r2:], _rot_cols(wq), _rot_cols(wk)], axis=1).astype(bf16)
    w_ctx = jnp.concatenate([w_in[:, :r0], w_in[:, r2:]], axis=1).astype(bf16)
    cos, sin = _rope_tables(s)
    g = norm_g.reshape(1, D)
    u, gt, q, k, v = _proj_in_call(x, mod, g, w_ext, cos, sin)
    uc, kx, vx = _proj_ctx_call(ctx, mod, g, w_ctx)
    wg = jnp.stack([jnp.concatenate([_block_diag(w_r[d]), _block_diag(w_i[d])], axis=1) for d in range(2)]).astype(bf16)
    bg = jnp.stack([jnp.concatenate([b_r[d], b_i[d]])[None, :] for d in range(2)])
    rec = _rglru_call(u, uc, conv_w, conv_b.reshape(1, LRU_W), wg, bg, lam.reshape(2, 1, LRU_W))
    att = _attn_call(sink, q, k, v, kx, vx)
    w_att = w_out[LRU_W:].reshape(2, 4, HEAD_DIM, D).transpose(1, 0, 2, 3).reshape(Q_W, D).astype(bf16)
    return _mix_out_call(x, mod, gt, rec, att, w_out[:LRU_W].astype(bf16), w_att)


def kernel(x, c, ctx, c_ctx, mod_w, mod_b, norm_mix_g, norm_ffn_g, final_norm_g, ab_w_in, ab_w_out, lru_conv_w,
           lru_conv_b, lru_wr, lru_br, lru_wi, lru_bi, lru_lambda, attn_sink, cm_w_in, cm_b_in, cm_dw_w, cm_dw_b,
           cm_ln_g, cm_ln_b, cm_w_out, cm_b_out, router_w, router_b, exp_w1, exp_w3, exp_w2, shared_w1, shared_w3,
           shared_w2):
    bsz = x.shape[0]
    depth = mod_w.shape[0]
    assert bsz + 1 <= MOD_ROWS - 7
    cc = jnp.zeros((MOD_ROWS, D), f32).at[:bsz].set(c).at[MOD_ROWS - 8].set(c_ctx)
    mod_all = _mod_call(cc, mod_w, mod_b).reshape(depth, MOD_ROWS, 6, D)
    for l in range(depth):
        mod = mod_all[l]
        last = l == depth - 1
        if l % 2 == 0:
            e = l // 2
            assert depth <= 2
            x1 = _even_layer_mixer(x, ctx, mod, norm_mix_g[l], ab_w_in[e], ab_w_out[e], lru_conv_w[e], lru_conv_b[e],
                                   lru_wr[e], lru_br[e], lru_wi[e], lru_bi[e], lru_lambda[e], attn_sink[e])
        else:
            o = l // 2
            zg = _conf_in_call(x, mod, norm_mix_g[l].reshape(1, D), cm_w_in[o].astype(bf16), cm_b_in[o].reshape(1, 2 * D))
            dw = jnp.concatenate([cm_dw_w[o], jnp.zeros((1, D), f32)], axis=0)
            x1 = _conf_out_call(x, mod, zg, dw, cm_dw_b[o].reshape(1, D), cm_ln_g[o].reshape(1, D),
                                cm_ln_b[o].reshape(1, D), cm_w_out[o].astype(bf16), cm_b_out[o].reshape(1, D))
        x = _moe_layer(l, x1, mod, norm_ffn_g[l], router_w[l], router_b[l], exp_w1, exp_w3, exp_w2,
                       shared_w1[l], shared_w3[l], shared_w2[l], final_norm_g, last)
    return x
```

```python
import functools

import jax
import jax.numpy as jnp
from jax import lax
from jax.experimental import pallas as pl
from jax.experimental.pallas import tpu as pltpu

f32 = jnp.float32
bf16 = jnp.bfloat16
i32 = jnp.int32
u32 = jnp.uint32

D = 1024
EPS = 1e-6
LRU_W = 512
LRU_C = 8.0
N_HEADS = 8
HEAD_DIM = 64
GRID_W = 64
ROPE_BASE = 10000.0
Q_W = 512
KV_W = 128
ATT_BLK = 128
CONV_K = 31
N_EXP = 256
TOP_K = 8
N_GRP = 8
TOPK_GRP = 4
GRP_SZ = N_EXP // N_GRP
EXP_D = 256
ROUTED_SCALE = 2.5
MOE_BLK = 256
PK_CHUNKS = D // 2 // 128
Y_CHUNKS = D // 128

VMEM_LIMIT_V7X = 56 * 1024 * 1024
MOD_ROWS = 24

_NT = (((1,), (1,)), ((), ()))


def _params(sem):
    return pltpu.CompilerParams(dimension_semantics=sem, vmem_limit_bytes=VMEM_LIMIT_V7X)


def _sigmoid(x):
    return 1.0 / (1.0 + jnp.exp(-x))


def _silu(x):
    return x * _sigmoid(x)


def _gelu_tanh(x):
    return 0.5 * x * (1.0 + jnp.tanh(0.7978845608028654 * (x + 0.044715 * (x * x * x))))


def _rms(x, g):
    return x * lax.rsqrt(jnp.mean(x * x, axis=-1, keepdims=True) + EPS) * g


def _rms_mod(x, g, shift, scale):
    return _rms(x, g) * (1.0 + scale) + shift


def _dot(a, b):
    return jnp.dot(a, b, preferred_element_type=f32)


def _mod_kernel(c_ref, w_ref, b_ref, o_ref):
    a = _silu(c_ref[...]).astype(bf16)
    o_ref[0] = _dot(a, w_ref[0].astype(bf16)) + b_ref[0]


def _mod_call(cc, mod_w, mod_b):
    depth, _, n = mod_w.shape
    tn = 1536
    return pl.pallas_call(
        _mod_kernel,
        grid=(depth, n // tn),
        in_specs=[pl.BlockSpec((MOD_ROWS, D), lambda l, j: (0, 0)),
                  pl.BlockSpec((1, D, tn), lambda l, j: (l, 0, j)),
                  pl.BlockSpec((1, 1, tn), lambda l, j: (l, 0, j))],
        out_specs=pl.BlockSpec((1, MOD_ROWS, tn), lambda l, j: (l, 0, j)),
        out_shape=jax.ShapeDtypeStruct((depth, MOD_ROWS, n), f32),
        compiler_params=_params(("parallel", "parallel")),
        name="mod",
    )(cc, mod_w, mod_b.reshape(depth, 1, n))


def _proj_in_kernel(x_ref, mod_ref, g_ref, w_ref, cos_ref, sin_ref, u_ref, gt_ref, q_ref, k_ref, v_ref):
    m = mod_ref[0]
    h = _rms_mod(x_ref[0], g_ref[...], m[0:1], m[1:2]).astype(bf16)
    p = _dot(h, w_ref[...])
    u_ref[0] = p[:, 0:512]
    gt_ref[0] = p[:, 512:1024]
    cos = cos_ref[...]
    sin = sin_ref[...]
    qs = []
    for j in range(4):
        qj = p[:, 1024 + j * 128:1152 + j * 128] * cos + p[:, 1792 + j * 128:1920 + j * 128] * sin
        qs.append(qj * (HEAD_DIM ** -0.5))
    q_ref[0] = jnp.concatenate(qs, axis=1).astype(bf16)
    k_ref[0] = (p[:, 1536:1664] * cos + p[:, 2304:2432] * sin).astype(bf16)
    v_ref[0] = p[:, 1664:1792].astype(bf16)


def _proj_in_call(x, mod, g, w_ext, cos, sin, tm=512):
    b, s, _ = x.shape
    nw = w_ext.shape[1]
    row = lambda bb, i: (bb, i, 0)
    return pl.pallas_call(
        _proj_in_kernel,
        grid=(b, s // tm),
        in_specs=[pl.BlockSpec((1, tm, D), row),
                  pl.BlockSpec((1, 6, D), lambda bb, i: (bb, 0, 0)),
                  pl.BlockSpec((1, D), lambda bb, i: (0, 0)),
                  pl.BlockSpec((D, nw), lambda bb, i: (0, 0)),
                  pl.BlockSpec((tm, 128), lambda bb, i: (i, 0)),
                  pl.BlockSpec((tm, 128), lambda bb, i: (i, 0))],
        out_specs=[pl.BlockSpec((1, tm, LRU_W), row), pl.BlockSpec((1, tm, LRU_W), row),
                   pl.BlockSpec((1, tm, Q_W), row), pl.BlockSpec((1, tm, KV_W), row),
                   pl.BlockSpec((1, tm, KV_W), row)],
        out_shape=[jax.ShapeDtypeStruct((b, s, LRU_W), f32), jax.ShapeDtypeStruct((b, s, LRU_W), f32),
                   jax.ShapeDtypeStruct((b, s, Q_W), bf16), jax.ShapeDtypeStruct((b, s, KV_W), bf16),
                   jax.ShapeDtypeStruct((b, s, KV_W), bf16)],
        compiler_params=_params(("parallel", "parallel")),
        name="proj_in",
    )(x, mod, g, w_ext, cos, sin)


def _proj_ctx_kernel(x_ref, mod_ref, g_ref, w_ref, u_ref, k_ref, v_ref):
    m = mod_ref[0]
    h = _rms_mod(x_ref[0], g_ref[...], m[0:1], m[1:2]).astype(bf16)
    p = _dot(h, w_ref[...])
    u_ref[0] = p[:, 0:512]
    k_ref[0] = p[:, 512:640].astype(bf16)
    v_ref[0] = p[:, 640:768].astype(bf16)


def _proj_ctx_call(ctx, mod, g, w_ctx):
    b, n_ctx, _ = ctx.shape
    row = lambda bb: (bb, 0, 0)
    return pl.pallas_call(
        _proj_ctx_kernel,
        grid=(b,),
        in_specs=[pl.BlockSpec((1, n_ctx, D), row),
                  pl.BlockSpec((1, 6, D), lambda bb: (MOD_ROWS - 8, 0, 0)),
                  pl.BlockSpec((1, D), lambda bb: (0, 0)),
                  pl.BlockSpec((D, 768), lambda bb: (0, 0))],
        out_specs=[pl.BlockSpec((1, n_ctx, LRU_W), row), pl.BlockSpec((1, n_ctx, KV_W), row),
                   pl.BlockSpec((1, n_ctx, KV_W), row)],
        out_shape=[jax.ShapeDtypeStruct((b, n_ctx, LRU_W), f32), jax.ShapeDtypeStruct((b, n_ctx, KV_W), bf16),
                   jax.ShapeDtypeStruct((b, n_ctx, KV_W), bf16)],
        compiler_params=_params(("parallel",)),
        name="proj_ctx",
    )(ctx, mod, g, w_ctx)


LRU_CHUNK = 128


def _rglru_kernel(u_ref, uc_ref, cw_ref, cb_ref, wg_ref, bg_ref, lam_ref, o_ref, pad_ref, cx_ref, cc_ref):
    s = u_ref.shape[1]
    n_ctx = uc_ref.shape[1]
    tc = LRU_CHUNK

    def conv_segment(src_ref, n, dst_ref):
        pad_ref[0:8] = jnp.zeros((8, LRU_W), f32)
        pad_ref[8:8 + n] = src_ref[0]
        pad_ref[8 + n:16 + n] = jnp.zeros((8, LRU_W), f32)
        for c in range(n // 256):
            acc = jnp.broadcast_to(cb_ref[...], (256, LRU_W))
            for k in range(4):
                acc = acc + cw_ref[k:k + 1, :] * pad_ref[c * 256 + 6 + k:c * 256 + 6 + k + 256, :]
            dst_ref[c * 256:(c + 1) * 256] = acc

    conv_segment(uc_ref, n_ctx, cc_ref)
    conv_segment(u_ref, s, cx_ref)

    rowm = lax.broadcasted_iota(i32, (tc, LRU_W), 0) & 7

    def scan_segment(src_ref, n, d, h0, write):
        lam = lam_ref[d]
        sp = jnp.maximum(-lam, 0.0) + jnp.log(1.0 + jnp.exp(-jnp.abs(lam)))
        nch = n // tc

        def chunk(ci, h):
            c = ci if d == 0 else nch - 1 - ci
            t0 = pl.multiple_of(c * tc, tc)
            uc = src_ref[pl.ds(t0, tc), :]
            gates = _dot(uc.astype(bf16), wg_ref[d]) + bg_ref[d]
            r = _sigmoid(gates[:, 0:LRU_W])
            ig = _sigmoid(gates[:, LRU_W:2 * LRU_W])
            log_a = (-LRU_C * sp) * r
            a = jnp.exp(log_a)
            bb = jnp.sqrt(-jnp.tanh(log_a) * (a * a + 1.0)) * (ig * uc)
            for sh in (1, 2, 4):
                if d == 0:
                    keep = rowm >= sh
                    a_sh = jnp.where(keep, pltpu.roll(a, sh, 0), 1.0)
                    b_sh = jnp.where(keep, pltpu.roll(bb, sh, 0), 0.0)
                else:
                    keep = rowm < 8 - sh
                    a_sh = jnp.where(keep, pltpu.roll(a, tc - sh, 0), 1.0)
                    b_sh = jnp.where(keep, pltpu.roll(bb, tc - sh, 0), 0.0)
                bb = a * b_sh + bb
                a = a * a_sh
            outs = [None] * (tc // 8)
            order = range(tc // 8) if d == 0 else range(tc // 8 - 1, -1, -1)
            for gi in order:
                hg = bb[gi * 8:(gi + 1) * 8] + a[gi * 8:(gi + 1) * 8] * h
                outs[gi] = hg
                h = hg[7:8] if d == 0 else hg[0:1]
            if write:
                hs = jnp.concatenate(outs, axis=0)
                if d == 0:
                    o_ref[0, pl.ds(t0, tc), :] = hs
                else:
                    o_ref[0, pl.ds(t0, tc), :] = o_ref[0, pl.ds(t0, tc), :] + hs
            return h

        return lax.fori_loop(0, nch, chunk, h0)

    for d in range(2):
        h = jnp.zeros((1, LRU_W), f32)
        h = scan_segment(cc_ref, n_ctx, d, h, False)
        scan_segment(cx_ref, s, d, h, True)


def _rglru_call(u, uc, conv_w, conv_b, wg, bg, lam):
    b, s, _ = u.shape
    n_ctx = uc.shape[1]
    return pl.pallas_call(
        _rglru_kernel,
        grid=(b,),
        in_specs=[pl.BlockSpec((1, s, LRU_W), lambda bb: (bb, 0, 0)),
                  pl.BlockSpec((1, n_ctx, LRU_W), lambda bb: (bb, 0, 0)),
                  pl.BlockSpec((4, LRU_W), lambda bb: (0, 0)),
                  pl.BlockSpec((1, LRU_W), lambda bb: (0, 0)),
                  pl.BlockSpec((2, LRU_W, 2 * LRU_W), lambda bb: (0, 0, 0)),
                  pl.BlockSpec((2, 1, 2 * LRU_W), lambda bb: (0, 0, 0)),
                  pl.BlockSpec((2, 1, LRU_W), lambda bb: (0, 0, 0))],
        out_specs=pl.BlockSpec((1, s, LRU_W), lambda bb: (bb, 0, 0)),
        out_shape=jax.ShapeDtypeStruct((b, s, LRU_W), f32),
        scratch_shapes=[pltpu.VMEM((s + 16, LRU_W), f32), pltpu.VMEM((s, LRU_W), f32),
                        pltpu.VMEM((n_ctx, LRU_W), f32)],
        compiler_params=_params(("parallel",)),
        name="rglru",
    )(u, uc, conv_w, conv_b, wg, bg, lam)


def _attn_kernel(sink_ref, q_ref, kp_ref, kc_ref, kn_ref, vp_ref, vc_ref, vn_ref, kx_ref, vx_ref, o_ref):
    n = pl.program_id(1)
    nb = pl.num_programs(1)
    blk = ATT_BLK
    q = q_ref[0]
    qall = jnp.concatenate([q[:, j * 128:(j + 1) * 128] for j in range(4)], axis=0)
    kw = jnp.concatenate([kp_ref[0], kc_ref[0], kn_ref[0]], axis=0)
    vw = jnp.concatenate([vp_ref[0], vc_ref[0], vn_ref[0]], axis=0)
    kx = kx_ref[0]
    vx = vx_ref[0]
    n_ctx = kx.shape[0]
    lo_w = lax.broadcasted_iota(i32, (3 * blk, 128), 1) < HEAD_DIM
    lo_x = lax.broadcasted_iota(i32, (n_ctx, 128), 1) < HEAD_DIM
    qi = lax.broadcasted_iota(i32, (4 * blk, 3 * blk), 0) & (blk - 1)
    kr = lax.broadcasted_iota(i32, (4 * blk, 3 * blk), 1) - blk
    lo = jnp.where(n > 0, -blk, 0)
    hi = jnp.where(n < nb - 1, 2 * blk, blk)
    dlt = kr - qi
    pen = jnp.where(dlt >= -blk, 0.0, -jnp.inf)
    pen = jnp.where(dlt <= blk, pen, -jnp.inf)
    pen = jnp.where(kr >= lo, pen, -jnp.inf)
    pen = jnp.where(kr < hi, pen, -jnp.inf)
    rb = lax.broadcasted_iota(i32, (4 * blk, 1), 0) // blk
    zero = jnp.zeros((), bf16)
    out = jnp.zeros((4 * blk, 128), f32)
    for half in range(2):
        sel_w = lo_w if half == 0 else jnp.logical_not(lo_w)
        sel_x = lo_x if half == 0 else jnp.logical_not(lo_x)
        s_w = lax.dot_general(qall, jnp.where(sel_w, kw, zero), _NT, preferred_element_type=f32) + pen
        s_c = lax.dot_general(qall, jnp.where(sel_x, kx, zero), _NT, preferred_element_type=f32)
        sk = jnp.where(rb == 0, sink_ref[4 * half],
                       jnp.where(rb == 1, sink_ref[4 * half + 1],
                                 jnp.where(rb == 2, sink_ref[4 * half + 2], sink_ref[4 * half + 3])))
        m = jnp.maximum(jnp.maximum(jnp.max(s_w, axis=1, keepdims=True), jnp.max(s_c, axis=1, keepdims=True)), sk)
        p_w = jnp.exp(s_w - m)
        p_c = jnp.exp(s_c - m)
        den = jnp.sum(p_w, axis=1, keepdims=True) + jnp.sum(p_c, axis=1, keepdims=True) + jnp.exp(sk - m)
        o = _dot(p_w.astype(bf16), jnp.where(sel_w, vw, zero)) + _dot(p_c.astype(bf16), jnp.where(sel_x, vx, zero))
        out = out + o / den
    o_ref[0] = jnp.concatenate([out[j * blk:(j + 1) * blk] for j in range(4)], axis=1).astype(bf16)


def _attn_call(sink, q, k, v, kx, vx):
    b, s, _ = q.shape
    n_ctx = kx.shape[1]
    nb = s // ATT_BLK
    cur = lambda bb, n: (bb, n, 0)
    prev = lambda bb, n: (bb, jnp.maximum(n - 1, 0), 0)
    nxt = lambda bb, n: (bb, jnp.minimum(n + 1, nb - 1), 0)
    kvb = (1, ATT_BLK, KV_W)
    return pl.pallas_call(
        _attn_kernel,
        grid=(b, nb),
        in_specs=[pl.BlockSpec(memory_space=pltpu.SMEM),
                  pl.BlockSpec((1, ATT_BLK, Q_W), cur),
                  pl.BlockSpec(kvb, prev), pl.BlockSpec(kvb, cur), pl.BlockSpec(kvb, nxt),
                  pl.BlockSpec(kvb, prev), pl.BlockSpec(kvb, cur), pl.BlockSpec(kvb, nxt),
                  pl.BlockSpec((1, n_ctx, KV_W), lambda bb, n: (bb, 0, 0)),
                  pl.BlockSpec((1, n_ctx, KV_W), lambda bb, n: (bb, 0, 0))],
        out_specs=pl.BlockSpec((1, ATT_BLK, Q_W), cur),
        out_shape=jax.ShapeDtypeStruct((b, s, Q_W), bf16),
        compiler_params=_params(("parallel", "parallel")),
        name="attn",
    )(sink, q, k, k, k, v, v, v, kx, vx)


def _mix_out_kernel(x_ref, mod_ref, gt_ref, rec_ref, att_ref, wr_ref, wa_ref, o_ref):
    m = mod_ref[0]
    a = (_gelu_tanh(gt_ref[0]) * rec_ref[0]).astype(bf16)
    y = _dot(a, wr_ref[...]) + _dot(att_ref[0], wa_ref[...])
    o_ref[0] = x_ref[0] + m[2:3] * y


def _mix_out_call(x, mod, gt, rec, att, w_rec, w_att, tm=512):
    b, s, _ = x.shape
    row = lambda bb, i: (bb, i, 0)
    return pl.pallas_call(
        _mix_out_kernel,
        grid=(b, s // tm),
        in_specs=[pl.BlockSpec((1, tm, D), row),
                  pl.BlockSpec((1, 6, D), lambda bb, i: (bb, 0, 0)),
                  pl.BlockSpec((1, tm, LRU_W), row), pl.BlockSpec((1, tm, LRU_W), row),
                  pl.BlockSpec((1, tm, Q_W), row),
                  pl.BlockSpec((LRU_W, D), lambda bb, i: (0, 0)),
                  pl.BlockSpec((Q_W, D), lambda bb, i: (0, 0))],
        out_specs=pl.BlockSpec((1, tm, D), row),
        out_shape=jax.ShapeDtypeStruct((b, s, D), f32),
        compiler_params=_params(("parallel", "parallel")),
        name="mix_out",
    )(x, mod, gt, rec, att, w_rec, w_att)


def _conf_in_kernel(x_ref, mod_ref, g_ref, w_ref, b_ref, o_ref):
    m = mod_ref[0]
    h = _rms_mod(x_ref[0], g_ref[...], m[0:1], m[1:2]).astype(bf16)
    z = _dot(h, w_ref[...]) + b_ref[...]
    o_ref[0] = z[:, 0:D] * _sigmoid(z[:, D:2 * D])


def _conf_in_call(x, mod, g, w, bias, tm=512):
    b, s, _ = x.shape
    row = lambda bb, i: (bb, i, 0)
    return pl.pallas_call(
        _conf_in_kernel,
        grid=(b, s // tm),
        in_specs=[pl.BlockSpec((1, tm, D), row),
                  pl.BlockSpec((1, 6, D), lambda bb, i: (bb, 0, 0)),
                  pl.BlockSpec((1, D), lambda bb, i: (0, 0)),
                  pl.BlockSpec((D, 2 * D), lambda bb, i: (0, 0)),
                  pl.BlockSpec((1, 2 * D), lambda bb, i: (0, 0))],
        out_specs=pl.BlockSpec((1, tm, D), row),
        out_shape=jax.ShapeDtypeStruct((b, s, D), f32),
        compiler_params=_params(("parallel", "parallel")),
        name="conf_in",
    )(x, mod, g, w, bias)


CONF_HALO = 16
CONF_ROWS = 32


def _conf_out_kernel(x_ref, mod_ref, zc_ref, zp_ref, zn_ref, dw_ref, db_ref, lg_ref, lb_ref, w_ref, b_ref, o_ref,
                     pad_ref, sh_ref, cv_ref):
    i = pl.program_id(1)
    nt = pl.num_programs(1)
    tm = zc_ref.shape[1]
    zero = jnp.zeros((CONF_HALO, D), f32)
    pad_ref[0:CONF_HALO] = jnp.where(i > 0, zp_ref[0], zero)
    pad_ref[CONF_HALO:CONF_HALO + tm] = zc_ref[0]
    pad_ref[CONF_HALO + tm:2 * CONF_HALO + tm] = jnp.where(i < nt - 1, zn_ref[0], zero)
    for r in range(8):
        sh_ref[r] = pad_ref[r:r + tm + 24, :]

    def chunk(c, carry):
        t0 = pl.multiple_of(c * CONF_ROWS, CONF_ROWS)
        acc = jnp.broadcast_to(db_ref[...], (CONF_ROWS, D))
        for k in range(CONV_K):
            kp = k + 1
            acc = acc + dw_ref[k:k + 1, :] * sh_ref[kp % 8, pl.ds(t0 + 8 * (kp // 8), CONF_ROWS), :]
        cv_ref[pl.ds(t0, CONF_ROWS), :] = acc
        return carry

    lax.fori_loop(0, tm // CONF_ROWS, chunk, 0)
    z = cv_ref[...]
    mu = jnp.mean(z, axis=-1, keepdims=True)
    zc = z - mu
    var = jnp.mean(zc * zc, axis=-1, keepdims=True)
    zn = zc * lax.rsqrt(var + EPS) * lg_ref[...] + lb_ref[...]
    y = _dot(_silu(zn).astype(bf16), w_ref[...]) + b_ref[...]
    m = mod_ref[0]
    o_ref[0] = x_ref[0] + m[2:3] * y


def _conf_out_call(x, mod, zg, dw_w, dw_b, ln_g, ln_b, w_out, b_out, tm=256):
    b, s, _ = x.shape
    row = lambda bb, i: (bb, i, 0)
    hb = tm // CONF_HALO
    nh = s // CONF_HALO
    vec = lambda bb, i: (0, 0)
    return pl.pallas_call(
        _conf_out_kernel,
        grid=(b, s // tm),
        in_specs=[pl.BlockSpec((1, tm, D), row),
                  pl.BlockSpec((1, 6, D), lambda bb, i: (bb, 0, 0)),
                  pl.BlockSpec((1, tm, D), row),
                  pl.BlockSpec((1, CONF_HALO, D), lambda bb, i: (bb, jnp.maximum(i * hb - 1, 0), 0)),
                  pl.BlockSpec((1, CONF_HALO, D), lambda bb, i: (bb, jnp.minimum((i + 1) * hb, nh - 1), 0)),
                  pl.BlockSpec((CONV_K + 1, D), vec),
                  pl.BlockSpec((1, D), vec), pl.BlockSpec((1, D), vec), pl.BlockSpec((1, D), vec),
                  pl.BlockSpec((D, D), vec), pl.BlockSpec((1, D), vec)],
        out_specs=pl.BlockSpec((1, tm, D), row),
        out_shape=jax.ShapeDtypeStruct((b, s, D), f32),
        scratch_shapes=[pltpu.VMEM((tm + 2 * CONF_HALO, D), f32), pltpu.VMEM((8, tm + 24, D), f32),
                        pltpu.VMEM((tm, D), f32)],
        compiler_params=_params(("parallel", "parallel")),
        name="conf_out",
    )(x, mod, zg, zg, zg, dw_w, dw_b, ln_g, ln_b, w_out, b_out)


def _ffn_pre_kernel(x_ref, mod_ref, g_ref, rwh_ref, rwl_ref, s13_ref, s2_ref, hp_ref, lg_ref, sh_ref):
    m = mod_ref[0]
    h2 = _rms_mod(x_ref[0], g_ref[...], m[3:4], m[4:5])
    hb = h2.astype(bf16)
    hbf = hb.astype(f32)
    hl = (h2 - hbf).astype(bf16)
    lg_ref[...] = (lax.dot_general(rwh_ref[...], hb, _NT, preferred_element_type=f32)
                   + lax.dot_general(rwh_ref[...], hl, _NT, preferred_element_type=f32)
                   + lax.dot_general(rwl_ref[...], hb, _NT, preferred_element_type=f32))
    a = _dot(hb, s13_ref[...])
    hid = (_silu(a[:, 0:EXP_D]) * a[:, EXP_D:2 * EXP_D]).astype(bf16)
    sh_ref[...] = _dot(hid, s2_ref[...])
    lo = lax.shift_right_logical(lax.bitcast_convert_type(hbf[:, 0:512], u32), jnp.uint32(16))
    hi = lax.bitcast_convert_type(hbf[:, 512:1024], u32) & jnp.uint32(0xFFFF0000)
    word = lo | hi
    for i in range(word.shape[0] // 8):
        for c in range(PK_CHUNKS):
            hp_ref[pl.ds(8 * PK_CHUNKS * i + c, 8, stride=PK_CHUNKS), :] = word[8 * i:8 * i + 8, 128 * c:128 * c + 128]


def _ffn_pre_call(x, mod, g, rwh, rwl, s13, s2, tm=512):
    b, s, _ = x.shape
    nt = s // tm
    t = b * s
    flat = lambda bb, i: (bb * nt + i, 0)
    vec = lambda bb, i: (0, 0)
    return pl.pallas_call(
        _ffn_pre_kernel,
        grid=(b, nt),
        in_specs=[pl.BlockSpec((1, tm, D), lambda bb, i: (bb, i, 0)),
                  pl.BlockSpec((1, 6, D), lambda bb, i: (bb, 0, 0)),
                  pl.BlockSpec((1, D), vec),
                  pl.BlockSpec((N_EXP, D), vec), pl.BlockSpec((N_EXP, D), vec),
                  pl.BlockSpec((D, 2 * EXP_D), vec), pl.BlockSpec((EXP_D, D), vec)],
        out_specs=[pl.BlockSpec((tm * PK_CHUNKS, 128), flat),
                   pl.BlockSpec((N_EXP, tm), lambda bb, i: (0, bb * nt + i)),
                   pl.BlockSpec((tm, D), flat)],
        out_shape=[jax.ShapeDtypeStruct((t * PK_CHUNKS, 128), u32), jax.ShapeDtypeStruct((N_EXP, t), f32),
                   jax.ShapeDtypeStruct((t, D), f32)],
        compiler_params=_params(("parallel", "parallel")),
        name="ffn_pre",
    )(x, mod, g, rwh, rwl, s13, s2)


ROUTE_TILE = 256


def _route_kernel(lg_ref, rb_ref, tri_ref, e_ref, w_ref, r_ref, c_ref, base_ref):
    i = pl.program_id(0)
    tr = lg_ref.shape[1]

    @pl.when(i == 0)
    def _():
        base_ref[...] = jnp.zeros_like(base_ref)

    scores = _sigmoid(lg_ref[...])
    biased = scores + rb_ref[...]
    neg = -jnp.inf
    rowf = lax.broadcasted_iota(i32, (N_EXP, tr), 0).astype(f32)
    r32 = lax.broadcasted_iota(i32, (GRP_SZ, tr), 0).astype(f32)
    gs = []
    for g in range(N_GRP):
        seg = biased[g * GRP_SZ:(g + 1) * GRP_SZ]
        m1 = jnp.max(seg, axis=0, keepdims=True)
        i1 = jnp.min(jnp.where(seg == m1, r32, 2.0 * GRP_SZ), axis=0, keepdims=True)
        m2 = jnp.max(jnp.where(r32 == i1, neg, seg), axis=0, keepdims=True)
        gs.append(m1 + m2)
    allowed = []
    for g in range(N_GRP):
        beat = jnp.zeros((1, tr), f32)
        for h in range(N_GRP):
            if h < g:
                beat = beat + jnp.where(gs[h] >= gs[g], 1.0, 0.0)
            elif h > g:
                beat = beat + jnp.where(gs[h] > gs[g], 1.0, 0.0)
        allowed.append(jnp.broadcast_to(beat, (GRP_SZ, tr)))
    allowed = jnp.concatenate(allowed, axis=0)
    masked = jnp.where(allowed < float(TOPK_GRP), biased, neg)
    cnt = jnp.zeros((N_EXP, tr), f32)
    idxs, ws = [], []
    for _ in range(TOP_K):
        m = jnp.max(masked, axis=0, keepdims=True)
        idx = jnp.min(jnp.where(masked == m, rowf, 2.0 * N_EXP), axis=0, keepdims=True)
        hit = rowf == idx
        ws.append(jnp.sum(jnp.where(hit, scores, 0.0), axis=0, keepdims=True))
        masked = jnp.where(hit, neg, masked)
        cnt = cnt + jnp.where(hit, 1.0, 0.0)
        idxs.append(idx)
    wsum = ws[0]
    for k in range(1, TOP_K):
        wsum = wsum + ws[k]
    pos = _dot(cnt.astype(bf16), tri_ref[...]) + base_ref[...]
    ranks = [jnp.sum(jnp.where(rowf == idxs[k], pos, 0.0), axis=0, keepdims=True) for k in range(TOP_K)]
    e_ref[...] = jnp.concatenate(idxs, axis=0).astype(i32)
    w_ref[...] = jnp.concatenate([ROUTED_SCALE * ws[k] / wsum for k in range(TOP_K)], axis=0)
    r_ref[...] = jnp.concatenate(ranks, axis=0).astype(i32)
    base_ref[...] = base_ref[...] + jnp.sum(cnt, axis=1, keepdims=True)
    c_ref[...] = base_ref[...]


def _route_call(logits_t, router_b, tri):
    t = logits_t.shape[1]
    tr = ROUTE_TILE
    col = lambda i: (0, i)
    return pl.pallas_call(
        _route_kernel,
        grid=(t // tr,),
        in_specs=[pl.BlockSpec((N_EXP, tr), col),
                  pl.BlockSpec((N_EXP, 1), lambda i: (0, 0)),
                  pl.BlockSpec((tr, tr), lambda i: (0, 0))],
        out_specs=[pl.BlockSpec((TOP_K, tr), col), pl.BlockSpec((TOP_K, tr), col), pl.BlockSpec((TOP_K, tr), col),
                   pl.BlockSpec((N_EXP, 1), lambda i: (0, 0))],
        out_shape=[jax.ShapeDtypeStruct((TOP_K, t), i32), jax.ShapeDtypeStruct((TOP_K, t), f32),
                   jax.ShapeDtypeStruct((TOP_K, t), i32), jax.ShapeDtypeStruct((N_EXP, 1), f32)],
        scratch_shapes=[pltpu.VMEM((N_EXP, 1), f32)],
        compiler_params=_params(("arbitrary",)),
        name="route",
    )(logits_t, router_b, tri)


def _dest_kernel(e_ref, r_ref, off_ref, d_ref):
    tr = e_ref.shape[1]
    rowi = lax.broadcasted_iota(i32, (N_EXP, tr), 0)
    off = off_ref[...]
    e = e_ref[...]
    rows = [jnp.sum(jnp.where(rowi == e[k:k + 1], off, 0.0), axis=0, keepdims=True) for k in range(TOP_K)]
    d_ref[...] = jnp.concatenate(rows, axis=0).astype(i32) + r_ref[...]


def _dest_call(eidx, rank, pad_off):
    t = eidx.shape[1]
    tr = 512
    col = lambda i: (0, i)
    return pl.pallas_call(
        _dest_kernel,
        grid=(t // tr,),
        in_specs=[pl.BlockSpec((TOP_K, tr), col), pl.BlockSpec((TOP_K, tr), col),
                  pl.BlockSpec((N_EXP, 1), lambda i: (0, 0))],
        out_specs=pl.BlockSpec((TOP_K, tr), col),
        out_shape=jax.ShapeDtypeStruct((TOP_K, t), i32),
        compiler_params=_params(("parallel",)),
        name="dest",
    )(eidx, rank, pad_off)


DISPATCH_TILE = 512
_PAD_PIECES = (128, 64, 32, 16, 8, 4, 2, 1)


DISPATCH_SLOTS = 3


def _dispatch_kernel(cnt_ref, off_ref, nbt_ref, dest_hbm, h_hbm, xs_hbm, idx_ref, hbuf, zero_ref, sem_idx, sem_tile,
                     sem_row, sem_z):
    i = pl.program_id(0)
    nsteps = pl.num_programs(0)
    n = idx_ref.shape[0] // 2
    ts = n // TOP_K
    trows = ts * PK_CHUNKS
    n_blk = xs_hbm.shape[0] // (MOE_BLK * PK_CHUNKS)

    def idx_copy(step):
        return pltpu.make_async_copy(dest_hbm.at[pl.ds(pl.multiple_of(step * n, n), n)],
                                     idx_ref.at[pl.ds(pl.multiple_of((step & 1) * n, n), n)], sem_idx.at[step & 1])

    def tile_copy(step):
        slot = lax.rem(step, DISPATCH_SLOTS)
        return pltpu.make_async_copy(h_hbm.at[pl.ds(pl.multiple_of(step * trows, trows), trows), :], hbuf.at[slot],
                                     sem_tile.at[slot])

    def rows_wait(step):
        pltpu.make_async_copy(xs_hbm.at[pl.ds(0, n * PK_CHUNKS), :], xs_hbm.at[pl.ds(0, n * PK_CHUNKS), :],
                              sem_row.at[lax.rem(step, DISPATCH_SLOTS)]).wait()

    def pad_copy(start_slot, p):
        return pltpu.make_async_copy(zero_ref.at[pl.ds(0, p * PK_CHUNKS), :],
                                     xs_hbm.at[pl.ds(start_slot * PK_CHUNKS, p * PK_CHUNKS), :], sem_z)

    def blk_copy(blk):
        return pltpu.make_async_copy(zero_ref, xs_hbm.at[pl.ds(blk * (MOE_BLK * PK_CHUNKS), MOE_BLK * PK_CHUNKS), :],
                                     sem_z)

    def for_each_pad_piece(fn):
        def per_expert(e, carry):
            c = cnt_ref[e]
            npad = ((c + (MOE_BLK - 1)) & (-MOE_BLK)) - c
            slot = off_ref[e] + c
            for p in _PAD_PIECES:
                @pl.when((npad & p) != 0)
                def _():
                    fn(pad_copy(slot, p))
                slot = slot + (npad & p)
            return carry

        lax.fori_loop(0, N_EXP, per_expert, 0)

    @pl.when(i == 0)
    def _():
        idx_copy(0).start()
        tile_copy(0).start()
        zero_ref[...] = jnp.zeros_like(zero_ref)
        for_each_pad_piece(lambda cp: cp.start())
        lax.fori_loop(nbt_ref[0], n_blk, lambda b, c: (blk_copy(b).start(), c)[1], 0)

    @pl.when(i >= DISPATCH_SLOTS - 1)
    def _():
        rows_wait(i - (DISPATCH_SLOTS - 1))

    @pl.when(i + 1 < nsteps)
    def _():
        idx_copy(i + 1).start()
        tile_copy(i + 1).start()

    idx_copy(i).wait()
    tile_copy(i).wait()
    sl = i & 1
    slot = lax.rem(i, DISPATCH_SLOTS)
    hb = hbuf.at[slot]

    def body(t2, carry):
        base = sl * n + t2 * (2 * TOP_K)
        ds = [idx_ref[base + j] for j in range(2 * TOP_K)]
        for j in range(2 * TOP_K):
            t = t2 * 2 + j // TOP_K
            pltpu.make_async_copy(hb.at[pl.ds(t * PK_CHUNKS, PK_CHUNKS), :],
                                  xs_hbm.at[pl.ds(ds[j] * PK_CHUNKS, PK_CHUNKS), :],
                                  sem_row.at[slot]).start(priority=j % 2)
        return carry

    lax.fori_loop(0, ts // 2, body, 0)

    @pl.when(i == nsteps - 1)
    def _():
        for back in range(DISPATCH_SLOTS - 2, -1, -1):
            @pl.when(i >= back)
            def _():
                rows_wait(i - back)

        for_each_pad_piece(lambda cp: cp.wait())
        lax.fori_loop(nbt_ref[0], n_blk, lambda b, c: (blk_copy(b).wait(), c)[1], 0)


def _dispatch_call(cnt, pad_off, nb_total, dest_flat, h2p, n_slots):
    t = h2p.shape[0] // PK_CHUNKS
    ts = DISPATCH_TILE
    gs = pltpu.PrefetchScalarGridSpec(
        num_scalar_prefetch=3,
        grid=(t // ts,),
        in_specs=[pl.BlockSpec(memory_space=pl.ANY), pl.BlockSpec(memory_space=pl.ANY)],
        out_specs=pl.BlockSpec(memory_space=pl.ANY),
        scratch_shapes=[pltpu.SMEM((2 * ts * TOP_K,), i32), pltpu.VMEM((DISPATCH_SLOTS, ts * PK_CHUNKS, 128), u32),
                        pltpu.VMEM((MOE_BLK * PK_CHUNKS, 128), u32),
                        pltpu.SemaphoreType.DMA((2,)), pltpu.SemaphoreType.DMA((DISPATCH_SLOTS,)),
                        pltpu.SemaphoreType.DMA((DISPATCH_SLOTS,)), pltpu.SemaphoreType.DMA(())],
    )
    return pl.pallas_call(
        _dispatch_kernel,
        grid_spec=gs,
        out_shape=jax.ShapeDtypeStruct((n_slots * PK_CHUNKS, 128), u32),
        compiler_params=_params(("arbitrary",)),
        name="dispatch",
    )(cnt, pad_off, nb_total, dest_flat, h2p)


GMLP_RING = 4


def _gmlp_kernel(nbe_ref, boff_ref, nbt_ref, w1_ref, w3_ref, w2_ref, xs_hbm, y_hbm, xbuf, ybuf, w13_s, w2_s,
                 sem_in, sem_out, sem_z):
    e = pl.program_id(0)
    nb = nbe_ref[e]
    b0 = boff_ref[e]
    total = nbt_ref[0]
    n_blk = y_hbm.shape[0] // (MOE_BLK * Y_CHUNKS)
    xrows = MOE_BLK * PK_CHUNKS
    yrows = MOE_BLK * Y_CHUNKS
    ring = GMLP_RING

    def in_copy(b):
        sl = b & (ring - 1)
        return pltpu.make_async_copy(xs_hbm.at[pl.ds(pl.multiple_of(b * xrows, xrows), xrows), :], xbuf.at[sl],
                                     sem_in.at[sl])

    def out_copy(b):
        sl = b & (ring - 1)
        return pltpu.make_async_copy(ybuf.at[sl], y_hbm.at[pl.ds(pl.multiple_of(b * yrows, yrows), yrows), :],
                                     sem_out.at[sl])

    def zero_copy(b):
        return pltpu.make_async_copy(ybuf.at[0], y_hbm.at[pl.ds(pl.multiple_of(b * yrows, yrows), yrows), :], sem_z)

    @pl.when(e == 0)
    def _():
        for b in range(ring - 1):
            @pl.when(b < total)
            def _():
                in_copy(b).start()

    @pl.when(nb > 0)
    def _():
        w13_s[:, 0:EXP_D] = w1_ref[0].astype(bf16)
        w13_s[:, EXP_D:2 * EXP_D] = w3_ref[0].astype(bf16)
        w2_s[...] = w2_ref[0].astype(bf16)

    def block(j, carry):
        b = b0 + j
        in_copy(b).wait()

        @pl.when(b + (ring - 1) < total)
        def _():
            in_copy(b + (ring - 1)).start()

        @pl.when(b >= ring)
        def _():
            out_copy(b - ring).wait()

        xb = xbuf.at[b & (ring - 1)]
        yb = ybuf.at[b & (ring - 1)]
        cols = []
        for c in range(PK_CHUNKS):
            cols.append(jnp.concatenate(
                [xb[pl.ds(8 * PK_CHUNKS * g + c, 8, stride=PK_CHUNKS), :] for g in range(MOE_BLK // 8)], axis=0))
        word = jnp.concatenate(cols, axis=1)
        xlo = lax.bitcast_convert_type(lax.shift_left(word, jnp.uint32(16)), f32).astype(bf16)
        xhi = lax.bitcast_convert_type(word & jnp.uint32(0xFFFF0000), f32).astype(bf16)
        h = _dot(xlo, w13_s[0:512, :]) + _dot(xhi, w13_s[512:1024, :])
        hid = (_silu(h[:, 0:EXP_D]) * h[:, EXP_D:2 * EXP_D]).astype(bf16)
        y = _dot(hid, w2_s[...])
        for g in range(MOE_BLK // 8):
            for c in range(Y_CHUNKS):
                yb[pl.ds(8 * Y_CHUNKS * g + c, 8, stride=Y_CHUNKS), :] = y[8 * g:8 * g + 8, 128 * c:128 * c + 128]
        out_copy(b).start()
        return carry

    lax.fori_loop(0, nb, block, 0)

    @pl.when(e == pl.num_programs(0) - 1)
    def _():
        for back in range(ring, 0, -1):
            @pl.when(total >= back)
            def _():
                out_copy(total - back).wait()

        ybuf[0] = jnp.zeros(ybuf.shape[1:], f32)
        lax.fori_loop(total, n_blk, lambda b, c: (zero_copy(b).start(), c)[1], 0)
        lax.fori_loop(total, n_blk, lambda b, c: (zero_copy(b).wait(), c)[1], 0)


def _gmlp_call(layer, nblk_e, blk_off, nb_total, xs, w1, w3, w2):
    n_slots = xs.shape[0] // PK_CHUNKS
    wsel = lambda e, *_: (layer, e, 0, 0)
    gs = pltpu.PrefetchScalarGridSpec(
        num_scalar_prefetch=3,
        grid=(N_EXP,),
        in_specs=[pl.BlockSpec((None, 1, D, EXP_D), wsel), pl.BlockSpec((None, 1, D, EXP_D), wsel),
                  pl.BlockSpec((None, 1, EXP_D, D), wsel), pl.BlockSpec(memory_space=pl.ANY)],
        out_specs=pl.BlockSpec(memory_space=pl.ANY),
        scratch_shapes=[pltpu.VMEM((GMLP_RING, MOE_BLK * PK_CHUNKS, 128), u32),
                        pltpu.VMEM((GMLP_RING, MOE_BLK * Y_CHUNKS, 128), f32),
                        pltpu.VMEM((D, 2 * EXP_D), bf16), pltpu.VMEM((EXP_D, D), bf16),
                        pltpu.SemaphoreType.DMA((GMLP_RING,)), pltpu.SemaphoreType.DMA((GMLP_RING,)),
                        pltpu.SemaphoreType.DMA(())],
    )
    return pl.pallas_call(
        _gmlp_kernel,
        grid_spec=gs,
        out_shape=jax.ShapeDtypeStruct((n_slots * Y_CHUNKS, 128), f32),
        compiler_params=_params(("arbitrary",)),
        name="gmlp",
    )(nblk_e, blk_off, nb_total, w1, w3, w2, xs)


COMBINE_TILE = 128


def _combine_kernel(final, dest_hbm, y_hbm, x_ref, mod_ref, sh_ref, w_ref, gf_ref, o_ref, idx_ref, buf_ref,
                    sem_idx, sem_row):
    tm = x_ref.shape[1]
    n = tm * TOP_K
    s = pl.program_id(0) * pl.num_programs(1) + pl.program_id(1)
    nsteps = pl.num_programs(0) * pl.num_programs(1)

    def idx_copy(step):
        return pltpu.make_async_copy(dest_hbm.at[pl.ds(pl.multiple_of(step * n, n), n)], idx_ref.at[pl.ds(pl.multiple_of((step & 1) * n, n), n)], sem_idx.at[step & 1])

    def issue_rows(step):
        sl = step & 1

        def body(t2, carry):
            base = sl * n + t2 * (2 * TOP_K)
            ds = [idx_ref[base + j] for j in range(2 * TOP_K)]
            for j in range(2 * TOP_K):
                t = t2 * 2 + j // TOP_K
                k = j % TOP_K
                pltpu.make_async_copy(y_hbm.at[pl.ds(ds[j] * Y_CHUNKS, Y_CHUNKS), :],
                                      buf_ref.at[sl, pl.ds((k * tm + t) * Y_CHUNKS, Y_CHUNKS), :],
                                      sem_row.at[sl]).start(priority=j % 2)
            return carry

        lax.fori_loop(0, tm // 2, body, 0)

    @pl.when(s == 0)
    def _():
        idx_copy(0).start()
        idx_copy(0).wait()
        issue_rows(0)

        @pl.when(nsteps > 1)
        def _():
            idx_copy(1).start()

    @pl.when(s + 1 < nsteps)
    def _():
        idx_copy(s + 1).wait()
        issue_rows(s + 1)

    @pl.when(s + 2 < nsteps)
    def _():
        idx_copy(s + 2).start()

    sl = s & 1
    pltpu.make_async_copy(y_hbm.at[pl.ds(0, n * Y_CHUNKS), :], buf_ref.at[sl], sem_row.at[sl]).wait()
    bs = buf_ref.at[sl]
    m = mod_ref[0]
    gate = m[5:6]

    def group(g, carry):
        r0 = pl.multiple_of(g * 8, 8)
        wg = w_ref[pl.ds(r0, 8), :]
        accs = [None] * Y_CHUNKS
        for k in range(TOP_K):
            wk = jnp.broadcast_to(wg[:, k:k + 1], (8, 128))
            for c in range(Y_CHUNKS):
                piece = bs[pl.ds((k * tm + r0) * Y_CHUNKS + c, 8, stride=Y_CHUNKS), :]
                accs[c] = wk * piece if k == 0 else accs[c] + wk * piece
        out = x_ref[0, pl.ds(r0, 8), :] + gate * (jnp.concatenate(accs, axis=1) + sh_ref[pl.ds(r0, 8), :])
        if final:
            out = _rms(out, gf_ref[...])
        o_ref[0, pl.ds(r0, 8), :] = out
        return carry

    lax.fori_loop(0, tm // 8, group, 0)


def _combine_call(dest_flat, y, x, mod, shared, w_tok, g_final, final):
    b, s, _ = x.shape
    tm = COMBINE_TILE
    nt = s // tm
    flat = lambda bb, i: (bb * nt + i, 0)
    return pl.pallas_call(
        functools.partial(_combine_kernel, final),
        grid=(b, nt),
        in_specs=[pl.BlockSpec(memory_space=pl.ANY), pl.BlockSpec(memory_space=pl.ANY),
                  pl.BlockSpec((1, tm, D), lambda bb, i: (bb, i, 0)),
                  pl.BlockSpec((1, 6, D), lambda bb, i: (bb, 0, 0)),
                  pl.BlockSpec((tm, D), flat),
                  pl.BlockSpec((tm, TOP_K), flat),
                  pl.BlockSpec((1, D), lambda bb, i: (0, 0))],
        out_specs=pl.BlockSpec((1, tm, D), lambda bb, i: (bb, i, 0)),
        out_shape=jax.ShapeDtypeStruct((b, s, D), f32),
        scratch_shapes=[pltpu.SMEM((2 * tm * TOP_K,), i32), pltpu.VMEM((2, TOP_K * tm * Y_CHUNKS, 128), f32),
                        pltpu.SemaphoreType.DMA((2,)), pltpu.SemaphoreType.DMA((2,))],
        compiler_params=_params(("arbitrary", "arbitrary")),
        name="combine",
    )(dest_flat, y, x, mod, shared, w_tok, g_final)


def _moe_layer(layer, x1, mod, norm_g, router_w, router_b, w1, w3, w2, sw1, sw3, sw2, g_final, final):
    b, s, _ = x1.shape
    t = b * s
    rwt = router_w.T
    rwh = rwt.astype(bf16)
    rwl = (rwt - rwh.astype(f32)).astype(bf16)
    s13 = jnp.concatenate([sw1, sw3], axis=1).astype(bf16)
    h2p, logits_t, shared = _ffn_pre_call(x1, mod, norm_g.reshape(1, D), rwh, rwl, s13, sw2.astype(bf16))
    tri = (lax.broadcasted_iota(i32, (ROUTE_TILE, ROUTE_TILE), 0)
           < lax.broadcasted_iota(i32, (ROUTE_TILE, ROUTE_TILE), 1)).astype(bf16)
    eidx, w_t, rank, counts = _route_call(logits_t, router_b.reshape(N_EXP, 1).astype(f32), tri)
    cnt = counts.reshape(N_EXP).astype(i32)
    nblk_e = (cnt + MOE_BLK - 1) // MOE_BLK
    blk_ends = jnp.cumsum(nblk_e)
    blk_off = blk_ends - nblk_e
    pad_off = blk_off * MOE_BLK
    nb_total = blk_ends[-1:].astype(i32)
    n_blk = t * TOP_K // MOE_BLK + N_EXP
    dest = _dest_call(eidx, rank, pad_off.astype(f32).reshape(N_EXP, 1))
    dest_flat = dest.T.reshape(t * TOP_K)
    xs = _dispatch_call(cnt, pad_off, nb_total, dest_flat, h2p, n_blk * MOE_BLK)
    y = _gmlp_call(layer, nblk_e, blk_off, nb_total, xs, w1, w3, w2)
    return _combine_call(dest_flat, y, x1, mod, shared, w_t.T, g_final.reshape(1, D), final)


def _rot_cols(w):
    d, n = w.shape
    w4 = w.reshape(d, n // 32, 2, 16)
    return jnp.stack([-w4[:, :, 1], w4[:, :, 0]], axis=2).reshape(d, n)


def _rope_tables(s):
    rows = s // GRID_W
    row = jnp.repeat(jnp.arange(rows, dtype=f32), GRID_W)
    col = jnp.tile(jnp.arange(GRID_W, dtype=f32), rows)
    n_freq = HEAD_DIM // 4
    inv = ROPE_BASE ** (-jnp.arange(n_freq, dtype=f32) / n_freq)
    ang_r = row[:, None] * inv
    ang_c = col[:, None] * inv
    cos = jnp.concatenate([jnp.cos(ang_r)] * 2 + [jnp.cos(ang_c)] * 2, axis=1)
    sin = jnp.concatenate([jnp.sin(ang_r)] * 2 + [jnp.sin(ang_c)] * 2, axis=1)
    return jnp.tile(cos, (1, 2)), jnp.tile(sin, (1, 2))


def _block_diag(w):
    h, dh, _ = w.shape
    eye = jnp.eye(h, dtype=w.dtype)
    return (eye[:, None, :, None] * w[:, :, None, :]).reshape(h * dh, h * dh)


def _even_layer_mixer(x, ctx, mod, norm_g, w_in, w_out, conv_w, conv_b, w_r, b_r, w_i, b_i, lam, sink):
    b, s, _ = x.shape
    r0, r1, r2 = LRU_W, 2 * LRU_W, 2 * LRU_W + Q_W
    wq = w_in[:, r1:r2].reshape(D, 2, 4, HEAD_DIM).transpose(0, 2, 1, 3).reshape(D, Q_W)
    wk = w_in[:, r2:r2 + KV_W]
    w_ext = jnp.concatenate([w_in[:, :r1], wq, w_in[:, r2:], _rot_cols(wq), _rot_cols(wk)], axis=1).astype(bf16)
    w_ctx = jnp.concatenate([w_in[:, :r0], w_in[:, r2:]], axis=1).astype(bf16)
    cos, sin = _rope_tables(s)
    g = norm_g.reshape(1, D)
    u, gt, q, k, v = _proj_in_call(x, mod, g, w_ext, cos, sin)
    uc, kx, vx = _proj_ctx_call(ctx, mod, g, w_ctx)
    wg = jnp.stack([jnp.concatenate([_block_diag(w_r[d]), _block_diag(w_i[d])], axis=1) for d in range(2)]).astype(bf16)
    bg = jnp.stack([jnp.concatenate([b_r[d], b_i[d]])[None, :] for d in range(2)])
    rec = _rglru_call(u, uc, conv_w, conv_b.reshape(1, LRU_W), wg, bg, lam.reshape(2, 1, LRU_W))
    att = _attn_call(sink, q, k, v, kx, vx)
    w_att = w_out[LRU_W:].reshape(2, 4, HEAD_DIM, D).transpose(1, 0, 2, 3).reshape(Q_W, D).astype(bf16)
    return _mix_out_call(x, mod, gt, rec, att, w_out[:LRU_W].astype(bf16), w_att)


def kernel(x, c, ctx, c_ctx, mod_w, mod_b, norm_mix_g, norm_ffn_g, final_norm_g, ab_w_in, ab_w_out, lru_conv_w,
           lru_conv_b, lru_wr, lru_br, lru_wi, lru_bi, lru_lambda, attn_sink, cm_w_in, cm_b_in, cm_dw_w, cm_dw_b,
           cm_ln_g, cm_ln_b, cm_w_out, cm_b_out, router_w, router_b, exp_w1, exp_w3, exp_w2, shared_w1, shared_w3,
           shared_w2):
    bsz = x.shape[0]
    depth = mod_w.shape[0]
    assert bsz + 1 <= MOD_ROWS - 7
    cc = jnp.zeros((MOD_ROWS, D), f32).at[:bsz].set(c).at[MOD_ROWS - 8].set(c_ctx)
    mod_all = _mod_call(cc, mod_w, mod_b).reshape(depth, MOD_ROWS, 6, D)
    for l in range(depth):
        mod = mod_all[l]
        last = l == depth - 1
        if l % 2 == 0:
            e = l // 2
            assert depth <= 2
            x1 = _even_layer_mixer(x, ctx, mod, norm_mix_g[l], ab_w_in[e], ab_w_out[e], lru_conv_w[e], lru_conv_b[e],
                                   lru_wr[e], lru_br[e], lru_wi[e], lru_bi[e], lru_lambda[e], attn_sink[e])
        else:
            o = l // 2
            zg = _conf_in_call(x, mod, norm_mix_g[l].reshape(1, D), cm_w_in[o].astype(bf16), cm_b_in[o].reshape(1, 2 * D))
            dw = jnp.concatenate([cm_dw_w[o], jnp.zeros((1, D), f32)], axis=0)
            x1 = _conf_out_call(x, mod, zg, dw, cm_dw_b[o].reshape(1, D), cm_ln_g[o].reshape(1, D),
                                cm_ln_b[o].reshape(1, D), cm_w_out[o].astype(bf16), cm_b_out[o].reshape(1, D))
        x = _moe_layer(l, x1, mod, norm_ffn_g[l], router_w[l], router_b[l], exp_w1, exp_w3, exp_w2,
                       shared_w1[l], shared_w3[l], shared_w2[l], final_norm_g, last)
    return x
```

```python
import functools

import jax
import jax.numpy as jnp
from jax import lax
from jax.experimental import pallas as pl
from jax.experimental.pallas import tpu as pltpu

f32 = jnp.float32
bf16 = jnp.bfloat16
i32 = jnp.int32
u32 = jnp.uint32

D = 1024
EPS = 1e-6
LRU_W = 512
LRU_C = 8.0
N_HEADS = 8
HEAD_DIM = 64
GRID_W = 64
ROPE_BASE = 10000.0
Q_W = 512
KV_W = 128
ATT_BLK = 128
CONV_K = 31
N_EXP = 256
TOP_K = 8
N_GRP = 8
TOPK_GRP = 4
GRP_SZ = N_EXP // N_GRP
EXP_D = 256
ROUTED_SCALE = 2.5
MOE_BLK = 256
PK_CHUNKS = D // 2 // 128

VMEM_LIMIT_V7X = 56 * 1024 * 1024
MOD_ROWS = 24

_NT = (((1,), (1,)), ((), ()))


def _params(sem):
    return pltpu.CompilerParams(dimension_semantics=sem, vmem_limit_bytes=VMEM_LIMIT_V7X)


def _sigmoid(x):
    return 1.0 / (1.0 + jnp.exp(-x))


def _silu(x):
    return x * _sigmoid(x)


def _gelu_tanh(x):
    return 0.5 * x * (1.0 + jnp.tanh(0.7978845608028654 * (x + 0.044715 * (x * x * x))))


def _rms(x, g):
    return x * lax.rsqrt(jnp.mean(x * x, axis=-1, keepdims=True) + EPS) * g


def _rms_mod(x, g, shift, scale):
    return _rms(x, g) * (1.0 + scale) + shift


def _dot(a, b):
    return jnp.dot(a, b, preferred_element_type=f32)


def _mod_kernel(c_ref, w_ref, b_ref, o_ref):
    a = _silu(c_ref[...]).astype(bf16)
    o_ref[0] = _dot(a, w_ref[0].astype(bf16)) + b_ref[0]


def _mod_call(cc, mod_w, mod_b):
    depth, _, n = mod_w.shape
    tn = 1536
    return pl.pallas_call(
        _mod_kernel,
        grid=(depth, n // tn),
        in_specs=[pl.BlockSpec((MOD_ROWS, D), lambda l, j: (0, 0)),
                  pl.BlockSpec((1, D, tn), lambda l, j: (l, 0, j)),
                  pl.BlockSpec((1, 1, tn), lambda l, j: (l, 0, j))],
        out_specs=pl.BlockSpec((1, MOD_ROWS, tn), lambda l, j: (l, 0, j)),
        out_shape=jax.ShapeDtypeStruct((depth, MOD_ROWS, n), f32),
        compiler_params=_params(("parallel", "parallel")),
        name="mod",
    )(cc, mod_w, mod_b.reshape(depth, 1, n))


def _proj_in_kernel(x_ref, mod_ref, g_ref, w_ref, cos_ref, sin_ref, u_ref, gt_ref, q_ref, k_ref, v_ref):
    m = mod_ref[0]
    h = _rms_mod(x_ref[0], g_ref[...], m[0:1], m[1:2]).astype(bf16)
    p = _dot(h, w_ref[...])
    u_ref[0] = p[:, 0:512]
    gt_ref[0] = p[:, 512:1024]
    cos = cos_ref[...]
    sin = sin_ref[...]
    qs = []
    for j in range(4):
        qj = p[:, 1024 + j * 128:1152 + j * 128] * cos + p[:, 1792 + j * 128:1920 + j * 128] * sin
        qs.append(qj * (HEAD_DIM ** -0.5))
    q_ref[0] = jnp.concatenate(qs, axis=1).astype(bf16)
    k_ref[0] = (p[:, 1536:1664] * cos + p[:, 2304:2432] * sin).astype(bf16)
    v_ref[0] = p[:, 1664:1792].astype(bf16)


def _proj_in_call(x, mod, g, w_ext, cos, sin, tm=512):
    b, s, _ = x.shape
    nw = w_ext.shape[1]
    row = lambda bb, i: (bb, i, 0)
    return pl.pallas_call(
        _proj_in_kernel,
        grid=(b, s // tm),
        in_specs=[pl.BlockSpec((1, tm, D), row),
                  pl.BlockSpec((1, 6, D), lambda bb, i: (bb, 0, 0)),
                  pl.BlockSpec((1, D), lambda bb, i: (0, 0)),
                  pl.BlockSpec((D, nw), lambda bb, i: (0, 0)),
                  pl.BlockSpec((tm, 128), lambda bb, i: (i, 0)),
                  pl.BlockSpec((tm, 128), lambda bb, i: (i, 0))],
        out_specs=[pl.BlockSpec((1, tm, LRU_W), row), pl.BlockSpec((1, tm, LRU_W), row),
                   pl.BlockSpec((1, tm, Q_W), row), pl.BlockSpec((1, tm, KV_W), row),
                   pl.BlockSpec((1, tm, KV_W), row)],
        out_shape=[jax.ShapeDtypeStruct((b, s, LRU_W), f32), jax.ShapeDtypeStruct((b, s, LRU_W), f32),
                   jax.ShapeDtypeStruct((b, s, Q_W), bf16), jax.ShapeDtypeStruct((b, s, KV_W), bf16),
                   jax.ShapeDtypeStruct((b, s, KV_W), bf16)],
        compiler_params=_params(("parallel", "parallel")),
        name="proj_in",
    )(x, mod, g, w_ext, cos, sin)


def _proj_ctx_kernel(x_ref, mod_ref, g_ref, w_ref, u_ref, k_ref, v_ref):
    m = mod_ref[0]
    h = _rms_mod(x_ref[0], g_ref[...], m[0:1], m[1:2]).astype(bf16)
    p = _dot(h, w_ref[...])
    u_ref[0] = p[:, 0:512]
    k_ref[0] = p[:, 512:640].astype(bf16)
    v_ref[0] = p[:, 640:768].astype(bf16)


def _proj_ctx_call(ctx, mod, g, w_ctx):
    b, n_ctx, _ = ctx.shape
    row = lambda bb: (bb, 0, 0)
    return pl.pallas_call(
        _proj_ctx_kernel,
        grid=(b,),
        in_specs=[pl.BlockSpec((1, n_ctx, D), row),
                  pl.BlockSpec((1, 6, D), lambda bb: (MOD_ROWS - 8, 0, 0)),
                  pl.BlockSpec((1, D), lambda bb: (0, 0)),
                  pl.BlockSpec((D, 768), lambda bb: (0, 0))],
        out_specs=[pl.BlockSpec((1, n_ctx, LRU_W), row), pl.BlockSpec((1, n_ctx, KV_W), row),
                   pl.BlockSpec((1, n_ctx, KV_W), row)],
        out_shape=[jax.ShapeDtypeStruct((b, n_ctx, LRU_W), f32), jax.ShapeDtypeStruct((b, n_ctx, KV_W), bf16),
                   jax.ShapeDtypeStruct((b, n_ctx, KV_W), bf16)],
        compiler_params=_params(("parallel",)),
        name="proj_ctx",
    )(ctx, mod, g, w_ctx)


LRU_CHUNK = 128


def _rglru_kernel(u_ref, uc_ref, cw_ref, cb_ref, wg_ref, bg_ref, lam_ref, o_ref, pad_ref, cx_ref, cc_ref):
    s = u_ref.shape[1]
    n_ctx = uc_ref.shape[1]
    tc = LRU_CHUNK

    def conv_segment(src_ref, n, dst_ref):
        pad_ref[0:8] = jnp.zeros((8, LRU_W), f32)
        pad_ref[8:8 + n] = src_ref[0]
        pad_ref[8 + n:16 + n] = jnp.zeros((8, LRU_W), f32)
        for c in range(n // 256):
            acc = jnp.broadcast_to(cb_ref[...], (256, LRU_W))
            for k in range(4):
                acc = acc + cw_ref[k:k + 1, :] * pad_ref[c * 256 + 6 + k:c * 256 + 6 + k + 256, :]
            dst_ref[c * 256:(c + 1) * 256] = acc

    conv_segment(uc_ref, n_ctx, cc_ref)
    conv_segment(u_ref, s, cx_ref)

    rowm = lax.broadcasted_iota(i32, (tc, LRU_W), 0) & 7

    def scan_segment(src_ref, n, d, h0, write):
        lam = lam_ref[d]
        sp = jnp.maximum(-lam, 0.0) + jnp.log(1.0 + jnp.exp(-jnp.abs(lam)))
        nch = n // tc

        def chunk(ci, h):
            c = ci if d == 0 else nch - 1 - ci
            t0 = pl.multiple_of(c * tc, tc)
            uc = src_ref[pl.ds(t0, tc), :]
            gates = _dot(uc.astype(bf16), wg_ref[d]) + bg_ref[d]
            r = _sigmoid(gates[:, 0:LRU_W])
            ig = _sigmoid(gates[:, LRU_W:2 * LRU_W])
            log_a = (-LRU_C * sp) * r
            a = jnp.exp(log_a)
            bb = jnp.sqrt(-jnp.tanh(log_a) * (a * a + 1.0)) * (ig * uc)
            for sh in (1, 2, 4):
                if d == 0:
                    keep = rowm >= sh
                    a_sh = jnp.where(keep, pltpu.roll(a, sh, 0), 1.0)
                    b_sh = jnp.where(keep, pltpu.roll(bb, sh, 0), 0.0)
                else:
                    keep = rowm < 8 - sh
                    a_sh = jnp.where(keep, pltpu.roll(a, tc - sh, 0), 1.0)
                    b_sh = jnp.where(keep, pltpu.roll(bb, tc - sh, 0), 0.0)
                bb = a * b_sh + bb
                a = a * a_sh
            outs = [None] * (tc // 8)
            order = range(tc // 8) if d == 0 else range(tc // 8 - 1, -1, -1)
            for gi in order:
                hg = bb[gi * 8:(gi + 1) * 8] + a[gi * 8:(gi + 1) * 8] * h
                outs[gi] = hg
                h = hg[7:8] if d == 0 else hg[0:1]
            if write:
                hs = jnp.concatenate(outs, axis=0)
                if d == 0:
                    o_ref[0, pl.ds(t0, tc), :] = hs
                else:
                    o_ref[0, pl.ds(t0, tc), :] = o_ref[0, pl.ds(t0, tc), :] + hs
            return h

        return lax.fori_loop(0, nch, chunk, h0)

    for d in range(2):
        h = jnp.zeros((1, LRU_W), f32)
        h = scan_segment(cc_ref, n_ctx, d, h, False)
        scan_segment(cx_ref, s, d, h, True)


def _rglru_call(u, uc, conv_w, conv_b, wg, bg, lam):
    b, s, _ = u.shape
    n_ctx = uc.shape[1]
    return pl.pallas_call(
        _rglru_kernel,
        grid=(b,),
        in_specs=[pl.BlockSpec((1, s, LRU_W), lambda bb: (bb, 0, 0)),
                  pl.BlockSpec((1, n_ctx, LRU_W), lambda bb: (bb, 0, 0)),
                  pl.BlockSpec((4, LRU_W), lambda bb: (0, 0)),
                  pl.BlockSpec((1, LRU_W), lambda bb: (0, 0)),
                  pl.BlockSpec((2, LRU_W, 2 * LRU_W), lambda bb: (0, 0, 0)),
                  pl.BlockSpec((2, 1, 2 * LRU_W), lambda bb: (0, 0, 0)),
                  pl.BlockSpec((2, 1, LRU_W), lambda bb: (0, 0, 0))],
        out_specs=pl.BlockSpec((1, s, LRU_W), lambda bb: (bb, 0, 0)),
        out_shape=jax.ShapeDtypeStruct((b, s, LRU_W), f32),
        scratch_shapes=[pltpu.VMEM((s + 16, LRU_W), f32), pltpu.VMEM((s, LRU_W), f32),
                        pltpu.VMEM((n_ctx, LRU_W), f32)],
        compiler_params=_params(("parallel",)),
        name="rglru",
    )(u, uc, conv_w, conv_b, wg, bg, lam)


def _attn_kernel(sink_ref, q_ref, kp_ref, kc_ref, kn_ref, vp_ref, vc_ref, vn_ref, kx_ref, vx_ref, o_ref):
    n = pl.program_id(1)
    nb = pl.num_programs(1)
    blk = ATT_BLK
    q = q_ref[0]
    qall = jnp.concatenate([q[:, j * 128:(j + 1) * 128] for j in range(4)], axis=0)
    kw = jnp.concatenate([kp_ref[0], kc_ref[0], kn_ref[0]], axis=0)
    vw = jnp.concatenate([vp_ref[0], vc_ref[0], vn_ref[0]], axis=0)
    kx = kx_ref[0]
    vx = vx_ref[0]
    n_ctx = kx.shape[0]
    lo_w = lax.broadcasted_iota(i32, (3 * blk, 128), 1) < HEAD_DIM
    lo_x = lax.broadcasted_iota(i32, (n_ctx, 128), 1) < HEAD_DIM
    qi = lax.broadcasted_iota(i32, (4 * blk, 3 * blk), 0) & (blk - 1)
    kr = lax.broadcasted_iota(i32, (4 * blk, 3 * blk), 1) - blk
    lo = jnp.where(n > 0, -blk, 0)
    hi = jnp.where(n < nb - 1, 2 * blk, blk)
    dlt = kr - qi
    pen = jnp.where(dlt >= -blk, 0.0, -jnp.inf)
    pen = jnp.where(dlt <= blk, pen, -jnp.inf)
    pen = jnp.where(kr >= lo, pen, -jnp.inf)
    pen = jnp.where(kr < hi, pen, -jnp.inf)
    rb = lax.broadcasted_iota(i32, (4 * blk, 1), 0) // blk
    zero = jnp.zeros((), bf16)
    out = jnp.zeros((4 * blk, 128), f32)
    for half in range(2):
        sel_w = lo_w if half == 0 else jnp.logical_not(lo_w)
        sel_x = lo_x if half == 0 else jnp.logical_not(lo_x)
        s_w = lax.dot_general(qall, jnp.where(sel_w, kw, zero), _NT, preferred_element_type=f32) + pen
        s_c = lax.dot_general(qall, jnp.where(sel_x, kx, zero), _NT, preferred_element_type=f32)
        sk = jnp.where(rb == 0, sink_ref[4 * half],
                       jnp.where(rb == 1, sink_ref[4 * half + 1],
                                 jnp.where(rb == 2, sink_ref[4 * half + 2], sink_ref[4 * half + 3])))
        m = jnp.maximum(jnp.maximum(jnp.max(s_w, axis=1, keepdims=True), jnp.max(s_c, axis=1, keepdims=True)), sk)
        p_w = jnp.exp(s_w - m)
        p_c = jnp.exp(s_c - m)
        den = jnp.sum(p_w, axis=1, keepdims=True) + jnp.sum(p_c, axis=1, keepdims=True) + jnp.exp(sk - m)
        o = _dot(p_w.astype(bf16), jnp.where(sel_w, vw, zero)) + _dot(p_c.astype(bf16), jnp.where(sel_x, vx, zero))
        out = out + o / den
    o_ref[0] = jnp.concatenate([out[j * blk:(j + 1) * blk] for j in range(4)], axis=1).astype(bf16)


def _attn_call(sink, q, k, v, kx, vx):
    b, s, _ = q.shape
    n_ctx = kx.shape[1]
    nb = s // ATT_BLK
    cur = lambda bb, n: (bb, n, 0)
    prev = lambda bb, n: (bb, jnp.maximum(n - 1, 0), 0)
    nxt = lambda bb, n: (bb, jnp.minimum(n + 1, nb - 1), 0)
    kvb = (1, ATT_BLK, KV_W)
    return pl.pallas_call(
        _attn_kernel,
        grid=(b, nb),
        in_specs=[pl.BlockSpec(memory_space=pltpu.SMEM),
                  pl.BlockSpec((1, ATT_BLK, Q_W), cur),
                  pl.BlockSpec(kvb, prev), pl.BlockSpec(kvb, cur), pl.BlockSpec(kvb, nxt),
                  pl.BlockSpec(kvb, prev), pl.BlockSpec(kvb, cur), pl.BlockSpec(kvb, nxt),
                  pl.BlockSpec((1, n_ctx, KV_W), lambda bb, n: (bb, 0, 0)),
                  pl.BlockSpec((1, n_ctx, KV_W), lambda bb, n: (bb, 0, 0))],
        out_specs=pl.BlockSpec((1, ATT_BLK, Q_W), cur),
        out_shape=jax.ShapeDtypeStruct((b, s, Q_W), bf16),
        compiler_params=_params(("parallel", "parallel")),
        name="attn",
    )(sink, q, k, k, k, v, v, v, kx, vx)


def _mix_out_kernel(x_ref, mod_ref, gt_ref, rec_ref, att_ref, wr_ref, wa_ref, o_ref):
    m = mod_ref[0]
    a = (_gelu_tanh(gt_ref[0]) * rec_ref[0]).astype(bf16)
    y = _dot(a, wr_ref[...]) + _dot(att_ref[0], wa_ref[...])
    o_ref[0] = x_ref[0] + m[2:3] * y


def _mix_out_call(x, mod, gt, rec, att, w_rec, w_att, tm=512):
    b, s, _ = x.shape
    row = lambda bb, i: (bb, i, 0)
    return pl.pallas_call(
        _mix_out_kernel,
        grid=(b, s // tm),
        in_specs=[pl.BlockSpec((1, tm, D), row),
                  pl.BlockSpec((1, 6, D), lambda bb, i: (bb, 0, 0)),
                  pl.BlockSpec((1, tm, LRU_W), row), pl.BlockSpec((1, tm, LRU_W), row),
                  pl.BlockSpec((1, tm, Q_W), row),
                  pl.BlockSpec((LRU_W, D), lambda bb, i: (0, 0)),
                  pl.BlockSpec((Q_W, D), lambda bb, i: (0, 0))],
        out_specs=pl.BlockSpec((1, tm, D), row),
        out_shape=jax.ShapeDtypeStruct((b, s, D), f32),
        compiler_params=_params(("parallel", "parallel")),
        name="mix_out",
    )(x, mod, gt, rec, att, w_rec, w_att)


def _conf_in_kernel(x_ref, mod_ref, g_ref, w_ref, b_ref, o_ref):
    m = mod_ref[0]
    h = _rms_mod(x_ref[0], g_ref[...], m[0:1], m[1:2]).astype(bf16)
    z = _dot(h, w_ref[...]) + b_ref[...]
    o_ref[0] = z[:, 0:D] * _sigmoid(z[:, D:2 * D])


def _conf_in_call(x, mod, g, w, bias, tm=512):
    b, s, _ = x.shape
    row = lambda bb, i: (bb, i, 0)
    return pl.pallas_call(
        _conf_in_kernel,
        grid=(b, s // tm),
        in_specs=[pl.BlockSpec((1, tm, D), row),
                  pl.BlockSpec((1, 6, D), lambda bb, i: (bb, 0, 0)),
                  pl.BlockSpec((1, D), lambda bb, i: (0, 0)),
                  pl.BlockSpec((D, 2 * D), lambda bb, i: (0, 0)),
                  pl.BlockSpec((1, 2 * D), lambda bb, i: (0, 0))],
        out_specs=pl.BlockSpec((1, tm, D), row),
        out_shape=jax.ShapeDtypeStruct((b, s, D), f32),
        compiler_params=_params(("parallel", "parallel")),
        name="conf_in",
    )(x, mod, g, w, bias)


CONF_HALO = 16
CONF_ROWS = 32


def _conf_out_kernel(x_ref, mod_ref, zc_ref, zp_ref, zn_ref, dw_ref, db_ref, lg_ref, lb_ref, w_ref, b_ref, o_ref,
                     pad_ref, sh_ref, cv_ref):
    i = pl.program_id(1)
    nt = pl.num_programs(1)
    tm = zc_ref.shape[1]
    zero = jnp.zeros((CONF_HALO, D), f32)
    pad_ref[0:CONF_HALO] = jnp.where(i > 0, zp_ref[0], zero)
    pad_ref[CONF_HALO:CONF_HALO + tm] = zc_ref[0]
    pad_ref[CONF_HALO + tm:2 * CONF_HALO + tm] = jnp.where(i < nt - 1, zn_ref[0], zero)
    for r in range(8):
        sh_ref[r] = pad_ref[r:r + tm + 24, :]

    def chunk(c, carry):
        t0 = pl.multiple_of(c * CONF_ROWS, CONF_ROWS)
        acc = jnp.broadcast_to(db_ref[...], (CONF_ROWS, D))
        for k in range(CONV_K):
            kp = k + 1
            acc = acc + dw_ref[k:k + 1, :] * sh_ref[kp % 8, pl.ds(t0 + 8 * (kp // 8), CONF_ROWS), :]
        cv_ref[pl.ds(t0, CONF_ROWS), :] = acc
        return carry

    lax.fori_loop(0, tm // CONF_ROWS, chunk, 0)
    z = cv_ref[...]
    mu = jnp.mean(z, axis=-1, keepdims=True)
    zc = z - mu
    var = jnp.mean(zc * zc, axis=-1, keepdims=True)
    zn = zc * lax.rsqrt(var + EPS) * lg_ref[...] + lb_ref[...]
    y = _dot(_silu(zn).astype(bf16), w_ref[...]) + b_ref[...]
    m = mod_ref[0]
    o_ref[0] = x_ref[0] + m[2:3] * y


def _conf_out_call(x, mod, zg, dw_w, dw_b, ln_g, ln_b, w_out, b_out, tm=256):
    b, s, _ = x.shape
    row = lambda bb, i: (bb, i, 0)
    hb = tm // CONF_HALO
    nh = s // CONF_HALO
    vec = lambda bb, i: (0, 0)
    return pl.pallas_call(
        _conf_out_kernel,
        grid=(b, s // tm),
        in_specs=[pl.BlockSpec((1, tm, D), row),
                  pl.BlockSpec((1, 6, D), lambda bb, i: (bb, 0, 0)),
                  pl.BlockSpec((1, tm, D), row),
                  pl.BlockSpec((1, CONF_HALO, D), lambda bb, i: (bb, jnp.maximum(i * hb - 1, 0), 0)),
                  pl.BlockSpec((1, CONF_HALO, D), lambda bb, i: (bb, jnp.minimum((i + 1) * hb, nh - 1), 0)),
                  pl.BlockSpec((CONV_K + 1, D), vec),
                  pl.BlockSpec((1, D), vec), pl.BlockSpec((1, D), vec), pl.BlockSpec((1, D), vec),
                  pl.BlockSpec((D, D), vec), pl.BlockSpec((1, D), vec)],
        out_specs=pl.BlockSpec((1, tm, D), row),
        out_shape=jax.ShapeDtypeStruct((b, s, D), f32),
        scratch_shapes=[pltpu.VMEM((tm + 2 * CONF_HALO, D), f32), pltpu.VMEM((8, tm + 24, D), f32),
                        pltpu.VMEM((tm, D), f32)],
        compiler_params=_params(("parallel", "parallel")),
        name="conf_out",
    )(x, mod, zg, zg, zg, dw_w, dw_b, ln_g, ln_b, w_out, b_out)


def _ffn_pre_kernel(x_ref, mod_ref, g_ref, rwh_ref, rwl_ref, s13_ref, s2_ref, hp_ref, lg_ref, sh_ref):
    m = mod_ref[0]
    h2 = _rms_mod(x_ref[0], g_ref[...], m[3:4], m[4:5])
    hb = h2.astype(bf16)
    hbf = hb.astype(f32)
    hl = (h2 - hbf).astype(bf16)
    lg_ref[...] = (lax.dot_general(rwh_ref[...], hb, _NT, preferred_element_type=f32)
                   + lax.dot_general(rwh_ref[...], hl, _NT, preferred_element_type=f32)
                   + lax.dot_general(rwl_ref[...], hb, _NT, preferred_element_type=f32))
    a = _dot(hb, s13_ref[...])
    hid = (_silu(a[:, 0:EXP_D]) * a[:, EXP_D:2 * EXP_D]).astype(bf16)
    sh_ref[...] = _dot(hid, s2_ref[...])
    lo = lax.shift_right_logical(lax.bitcast_convert_type(hbf[:, 0:512], u32), jnp.uint32(16))
    hi = lax.bitcast_convert_type(hbf[:, 512:1024], u32) & jnp.uint32(0xFFFF0000)
    word = lo | hi
    for i in range(word.shape[0] // 8):
        for c in range(PK_CHUNKS):
            hp_ref[pl.ds(8 * PK_CHUNKS * i + c, 8, stride=PK_CHUNKS), :] = word[8 * i:8 * i + 8, 128 * c:128 * c + 128]


def _ffn_pre_call(x, mod, g, rwh, rwl, s13, s2, tm=512):
    b, s, _ = x.shape
    nt = s // tm
    t = b * s
    flat = lambda bb, i: (bb * nt + i, 0)
    vec = lambda bb, i: (0, 0)
    return pl.pallas_call(
        _ffn_pre_kernel,
        grid=(b, nt),
        in_specs=[pl.BlockSpec((1, tm, D), lambda bb, i: (bb, i, 0)),
                  pl.BlockSpec((1, 6, D), lambda bb, i: (bb, 0, 0)),
                  pl.BlockSpec((1, D), vec),
                  pl.BlockSpec((N_EXP, D), vec), pl.BlockSpec((N_EXP, D), vec),
                  pl.BlockSpec((D, 2 * EXP_D), vec), pl.BlockSpec((EXP_D, D), vec)],
        out_specs=[pl.BlockSpec((tm * PK_CHUNKS, 128), flat),
                   pl.BlockSpec((N_EXP, tm), lambda bb, i: (0, bb * nt + i)),
                   pl.BlockSpec((tm, D), flat)],
        out_shape=[jax.ShapeDtypeStruct((t * PK_CHUNKS, 128), u32), jax.ShapeDtypeStruct((N_EXP, t), f32),
                   jax.ShapeDtypeStruct((t, D), f32)],
        compiler_params=_params(("parallel", "parallel")),
        name="ffn_pre",
    )(x, mod, g, rwh, rwl, s13, s2)


ROUTE_TILE = 256


def _route_kernel(lg_ref, rb_ref, tri_ref, e_ref, w_ref, r_ref, c_ref, base_ref):
    i = pl.program_id(0)
    tr = lg_ref.shape[1]

    @pl.when(i == 0)
    def _():
        base_ref[...] = jnp.zeros_like(base_ref)

    scores = _sigmoid(lg_ref[...])
    biased = scores + rb_ref[...]
    neg = -jnp.inf
    rowf = lax.broadcasted_iota(i32, (N_EXP, tr), 0).astype(f32)
    r32 = lax.broadcasted_iota(i32, (GRP_SZ, tr), 0).astype(f32)
    gs = []
    for g in range(N_GRP):
        seg = biased[g * GRP_SZ:(g + 1) * GRP_SZ]
        m1 = jnp.max(seg, axis=0, keepdims=True)
        i1 = jnp.min(jnp.where(seg == m1, r32, 2.0 * GRP_SZ), axis=0, keepdims=True)
        m2 = jnp.max(jnp.where(r32 == i1, neg, seg), axis=0, keepdims=True)
        gs.append(m1 + m2)
    allowed = []
    for g in range(N_GRP):
        beat = jnp.zeros((1, tr), f32)
        for h in range(N_GRP):
            if h < g:
                beat = beat + jnp.where(gs[h] >= gs[g], 1.0, 0.0)
            elif h > g:
                beat = beat + jnp.where(gs[h] > gs[g], 1.0, 0.0)
        allowed.append(jnp.broadcast_to(beat, (GRP_SZ, tr)))
    allowed = jnp.concatenate(allowed, axis=0)
    masked = jnp.where(allowed < float(TOPK_GRP), biased, neg)
    cnt = jnp.zeros((N_EXP, tr), f32)
    idxs, ws = [], []
    for _ in range(TOP_K):
        m = jnp.max(masked, axis=0, keepdims=True)
        idx = jnp.min(jnp.where(masked == m, rowf, 2.0 * N_EXP), axis=0, keepdims=True)
        hit = rowf == idx
        ws.append(jnp.sum(jnp.where(hit, scores, 0.0), axis=0, keepdims=True))
        masked = jnp.where(hit, neg, masked)
        cnt = cnt + jnp.where(hit, 1.0, 0.0)
        idxs.append(idx)
    wsum = ws[0]
    for k in range(1, TOP_K):
        wsum = wsum + ws[k]
    pos = _dot(cnt.astype(bf16), tri_ref[...]) + base_ref[...]
    ranks = [jnp.sum(jnp.where(rowf == idxs[k], pos, 0.0), axis=0, keepdims=True) for k in range(TOP_K)]
    e_ref[...] = jnp.concatenate(idxs, axis=0).astype(i32)
    w_ref[...] = jnp.concatenate([ROUTED_SCALE * ws[k] / wsum for k in range(TOP_K)], axis=0)
    r_ref[...] = jnp.concatenate(ranks, axis=0).astype(i32)
    base_ref[...] = base_ref[...] + jnp.sum(cnt, axis=1, keepdims=True)
    c_ref[...] = base_ref[...]


def _route_call(logits_t, router_b, tri):
    t = logits_t.shape[1]
    tr = ROUTE_TILE
    col = lambda i: (0, i)
    return pl.pallas_call(
        _route_kernel,
        grid=(t // tr,),
        in_specs=[pl.BlockSpec((N_EXP, tr), col),
                  pl.BlockSpec((N_EXP, 1), lambda i: (0, 0)),
                  pl.BlockSpec((tr, tr), lambda i: (0, 0))],
        out_specs=[pl.BlockSpec((TOP_K, tr), col), pl.BlockSpec((TOP_K, tr), col), pl.BlockSpec((TOP_K, tr), col),
                   pl.BlockSpec((N_EXP, 1), lambda i: (0, 0))],
        out_shape=[jax.ShapeDtypeStruct((TOP_K, t), i32), jax.ShapeDtypeStruct((TOP_K, t), f32),
                   jax.ShapeDtypeStruct((TOP_K, t), i32), jax.ShapeDtypeStruct((N_EXP, 1), f32)],
        scratch_shapes=[pltpu.VMEM((N_EXP, 1), f32)],
        compiler_params=_params(("arbitrary",)),
        name="route",
    )(logits_t, router_b, tri)


def _dest_kernel(e_ref, r_ref, off_ref, d_ref):
    tr = e_ref.shape[1]
    rowi = lax.broadcasted_iota(i32, (N_EXP, tr), 0)
    off = off_ref[...]
    e = e_ref[...]
    rows = [jnp.sum(jnp.where(rowi == e[k:k + 1], off, 0.0), axis=0, keepdims=True) for k in range(TOP_K)]
    d_ref[...] = jnp.concatenate(rows, axis=0).astype(i32) + r_ref[...]


def _dest_call(eidx, rank, pad_off):
    t = eidx.shape[1]
    tr = 512
    col = lambda i: (0, i)
    return pl.pallas_call(
        _dest_kernel,
        grid=(t // tr,),
        in_specs=[pl.BlockSpec((TOP_K, tr), col), pl.BlockSpec((TOP_K, tr), col),
                  pl.BlockSpec((N_EXP, 1), lambda i: (0, 0))],
        out_specs=pl.BlockSpec((TOP_K, tr), col),
        out_shape=jax.ShapeDtypeStruct((TOP_K, t), i32),
        compiler_params=_params(("parallel",)),
        name="dest",
    )(eidx, rank, pad_off)


DISPATCH_TILE = 512
_PAD_PIECES = (128, 64, 32, 16, 8, 4, 2, 1)


DISPATCH_SLOTS = 3


def _dispatch_kernel(cnt_ref, off_ref, nbt_ref, dest_hbm, h_hbm, xs_hbm, idx_ref, hbuf, zero_ref, sem_idx, sem_tile,
                     sem_row, sem_z):
    i = pl.program_id(0)
    nsteps = pl.num_programs(0)
    n = idx_ref.shape[0] // 2
    ts = n // TOP_K
    trows = ts * PK_CHUNKS
    n_blk = xs_hbm.shape[0] // (MOE_BLK * PK_CHUNKS)

    def idx_copy(step):
        return pltpu.make_async_copy(dest_hbm.at[pl.ds(pl.multiple_of(step * n, n), n)],
                                     idx_ref.at[pl.ds(pl.multiple_of((step & 1) * n, n), n)], sem_idx.at[step & 1])

    def tile_copy(step):
        slot = lax.rem(step, DISPATCH_SLOTS)
        return pltpu.make_async_copy(h_hbm.at[pl.ds(pl.multiple_of(step * trows, trows), trows), :], hbuf.at[slot],
                                     sem_tile.at[slot])

    def rows_wait(step):
        pltpu.make_async_copy(xs_hbm.at[pl.ds(0, n * PK_CHUNKS), :], xs_hbm.at[pl.ds(0, n * PK_CHUNKS), :],
                              sem_row.at[lax.rem(step, DISPATCH_SLOTS)]).wait()

    def pad_copy(start_slot, p):
        return pltpu.make_async_copy(zero_ref.at[pl.ds(0, p * PK_CHUNKS), :],
                                     xs_hbm.at[pl.ds(start_slot * PK_CHUNKS, p * PK_CHUNKS), :], sem_z)

    def blk_copy(blk):
        return pltpu.make_async_copy(zero_ref, xs_hbm.at[pl.ds(blk * (MOE_BLK * PK_CHUNKS), MOE_BLK * PK_CHUNKS), :],
                                     sem_z)

    def for_each_pad_piece(fn):
        def per_expert(e, carry):
            c = cnt_ref[e]
            npad = ((c + (MOE_BLK - 1)) & (-MOE_BLK)) - c
            slot = off_ref[e] + c
            for p in _PAD_PIECES:
                @pl.when((npad & p) != 0)
                def _():
                    fn(pad_copy(slot, p))
                slot = slot + (npad & p)
            return carry

        lax.fori_loop(0, N_EXP, per_expert, 0)

    @pl.when(i == 0)
    def _():
        idx_copy(0).start()
        tile_copy(0).start()
        zero_ref[...] = jnp.zeros_like(zero_ref)
        for_each_pad_piece(lambda cp: cp.start())
        lax.fori_loop(nbt_ref[0], n_blk, lambda b, c: (blk_copy(b).start(), c)[1], 0)

    @pl.when(i >= DISPATCH_SLOTS - 1)
    def _():
        rows_wait(i - (DISPATCH_SLOTS - 1))

    @pl.when(i + 1 < nsteps)
    def _():
        idx_copy(i + 1).start()
        tile_copy(i + 1).start()

    idx_copy(i).wait()
    tile_copy(i).wait()
    sl = i & 1
    slot = lax.rem(i, DISPATCH_SLOTS)
    hb = hbuf.at[slot]

    def body(t2, carry):
        base = sl * n + t2 * (2 * TOP_K)
        ds = [idx_ref[base + j] for j in range(2 * TOP_K)]
        for j in range(2 * TOP_K):
            t = t2 * 2 + j // TOP_K
            pltpu.make_async_copy(hb.at[pl.ds(t * PK_CHUNKS, PK_CHUNKS), :],
                                  xs_hbm.at[pl.ds(ds[j] * PK_CHUNKS, PK_CHUNKS), :],
                                  sem_row.at[slot]).start(priority=j % 2)
        return carry

    lax.fori_loop(0, ts // 2, body, 0)

    @pl.when(i == nsteps - 1)
    def _():
        for back in range(DISPATCH_SLOTS - 2, -1, -1):
            @pl.when(i >= back)
            def _():
                rows_wait(i - back)

        for_each_pad_piece(lambda cp: cp.wait())
        lax.fori_loop(nbt_ref[0], n_blk, lambda b, c: (blk_copy(b).wait(), c)[1], 0)


def _dispatch_call(cnt, pad_off, nb_total, dest_flat, h2p, n_slots):
    t = h2p.shape[0] // PK_CHUNKS
    ts = DISPATCH_TILE
    gs = pltpu.PrefetchScalarGridSpec(
        num_scalar_prefetch=3,
        grid=(t // ts,),
        in_specs=[pl.BlockSpec(memory_space=pl.ANY), pl.BlockSpec(memory_space=pl.ANY)],
        out_specs=pl.BlockSpec(memory_space=pl.ANY),
        scratch_shapes=[pltpu.SMEM((2 * ts * TOP_K,), i32), pltpu.VMEM((DISPATCH_SLOTS, ts * PK_CHUNKS, 128), u32),
                        pltpu.VMEM((MOE_BLK * PK_CHUNKS, 128), u32),
                        pltpu.SemaphoreType.DMA((2,)), pltpu.SemaphoreType.DMA((DISPATCH_SLOTS,)),
                        pltpu.SemaphoreType.DMA((DISPATCH_SLOTS,)), pltpu.SemaphoreType.DMA(())],
    )
    return pl.pallas_call(
        _dispatch_kernel,
        grid_spec=gs,
        out_shape=jax.ShapeDtypeStruct((n_slots * PK_CHUNKS, 128), u32),
        compiler_params=_params(("arbitrary",)),
        name="dispatch",
    )(cnt, pad_off, nb_total, dest_flat, h2p)


GMLP_RING = 4


def _gmlp_kernel(nbe_ref, boff_ref, nbt_ref, w1_ref, w3_ref, w2_ref, xs_hbm, y_hbm, xbuf, ybuf, w13_s, w2_s,
                 sem_in, sem_out, sem_z):
    e = pl.program_id(0)
    nb = nbe_ref[e]
    b0 = boff_ref[e]
    total = nbt_ref[0]
    n_blk = y_hbm.shape[0] // (MOE_BLK * PK_CHUNKS)
    xrows = MOE_BLK * PK_CHUNKS
    yrows = MOE_BLK * PK_CHUNKS
    ring = GMLP_RING

    def in_copy(b):
        sl = b & (ring - 1)
        return pltpu.make_async_copy(xs_hbm.at[pl.ds(pl.multiple_of(b * xrows, xrows), xrows), :], xbuf.at[sl],
                                     sem_in.at[sl])

    def out_copy(b):
        sl = b & (ring - 1)
        return pltpu.make_async_copy(ybuf.at[sl], y_hbm.at[pl.ds(pl.multiple_of(b * yrows, yrows), yrows), :],
                                     sem_out.at[sl])

    def zero_copy(b):
        return pltpu.make_async_copy(ybuf.at[0], y_hbm.at[pl.ds(pl.multiple_of(b * yrows, yrows), yrows), :], sem_z)

    @pl.when(e == 0)
    def _():
        for b in range(ring - 1):
            @pl.when(b < total)
            def _():
                in_copy(b).start()

    @pl.when(nb > 0)
    def _():
        w13_s[:, 0:EXP_D] = w1_ref[0].astype(bf16)
        w13_s[:, EXP_D:2 * EXP_D] = w3_ref[0].astype(bf16)
        w2_s[...] = w2_ref[0].astype(bf16)

    def block(j, carry):
        b = b0 + j
        in_copy(b).wait()

        @pl.when(b + (ring - 1) < total)
        def _():
            in_copy(b + (ring - 1)).start()

        @pl.when(b >= ring)
        def _():
            out_copy(b - ring).wait()

        xb = xbuf.at[b & (ring - 1)]
        yb = ybuf.at[b & (ring - 1)]
        cols = []
        for c in range(PK_CHUNKS):
            cols.append(jnp.concatenate(
                [xb[pl.ds(8 * PK_CHUNKS * g + c, 8, stride=PK_CHUNKS), :] for g in range(MOE_BLK // 8)], axis=0))
        word = jnp.concatenate(cols, axis=1)
        xlo = lax.bitcast_convert_type(lax.shift_left(word, jnp.uint32(16)), f32).astype(bf16)
        xhi = lax.bitcast_convert_type(word & jnp.uint32(0xFFFF0000), f32).astype(bf16)
        h = _dot(xlo, w13_s[0:512, :]) + _dot(xhi, w13_s[512:1024, :])
        hid = (_silu(h[:, 0:EXP_D]) * h[:, EXP_D:2 * EXP_D]).astype(bf16)
        y = _dot(hid, w2_s[...])
        ylo = lax.shift_right_logical(lax.bitcast_convert_type(y[:, 0:512].astype(bf16).astype(f32), u32),
                                      jnp.uint32(16))
        yhi = lax.bitcast_convert_type(y[:, 512:1024].astype(bf16).astype(f32), u32) & jnp.uint32(0xFFFF0000)
        yw = ylo | yhi
        for g in range(MOE_BLK // 8):
            for c in range(PK_CHUNKS):
                yb[pl.ds(8 * PK_CHUNKS * g + c, 8, stride=PK_CHUNKS), :] = yw[8 * g:8 * g + 8, 128 * c:128 * c + 128]
        out_copy(b).start()
        return carry

    lax.fori_loop(0, nb, block, 0)

    @pl.when(e == pl.num_programs(0) - 1)
    def _():
        for back in range(ring, 0, -1):
            @pl.when(total >= back)
            def _():
                out_copy(total - back).wait()

        ybuf[0] = jnp.zeros(ybuf.shape[1:], u32)
        lax.fori_loop(total, n_blk, lambda b, c: (zero_copy(b).start(), c)[1], 0)
        lax.fori_loop(total, n_blk, lambda b, c: (zero_copy(b).wait(), c)[1], 0)


def _gmlp_call(layer, nblk_e, blk_off, nb_total, xs, w1, w3, w2):
    n_slots = xs.shape[0] // PK_CHUNKS
    wsel = lambda e, *_: (layer, e, 0, 0)
    gs = pltpu.PrefetchScalarGridSpec(
        num_scalar_prefetch=3,
        grid=(N_EXP,),
        in_specs=[pl.BlockSpec((None, 1, D, EXP_D), wsel), pl.BlockSpec((None, 1, D, EXP_D), wsel),
                  pl.BlockSpec((None, 1, EXP_D, D), wsel), pl.BlockSpec(memory_space=pl.ANY)],
        out_specs=pl.BlockSpec(memory_space=pl.ANY),
        scratch_shapes=[pltpu.VMEM((GMLP_RING, MOE_BLK * PK_CHUNKS, 128), u32),
                        pltpu.VMEM((GMLP_RING, MOE_BLK * PK_CHUNKS, 128), u32),
                        pltpu.VMEM((D, 2 * EXP_D), bf16), pltpu.VMEM((EXP_D, D), bf16),
                        pltpu.SemaphoreType.DMA((GMLP_RING,)), pltpu.SemaphoreType.DMA((GMLP_RING,)),
                        pltpu.SemaphoreType.DMA(())],
    )
    return pl.pallas_call(
        _gmlp_kernel,
        grid_spec=gs,
        out_shape=jax.ShapeDtypeStruct((n_slots * PK_CHUNKS, 128), u32),
        compiler_params=_params(("arbitrary",)),
        name="gmlp",
    )(nblk_e, blk_off, nb_total, w1, w3, w2, xs)


COMBINE_TILE = 128


def _combine_kernel(final, dest_hbm, y_hbm, x_ref, mod_ref, sh_ref, w_ref, gf_ref, o_ref, idx_ref, buf_ref,
                    sem_idx, sem_row):
    tm = x_ref.shape[1]
    n = tm * TOP_K
    s = pl.program_id(0) * pl.num_programs(1) + pl.program_id(1)
    nsteps = pl.num_programs(0) * pl.num_programs(1)
    last = s == nsteps - 1
    nxt = jnp.minimum(s + 1, nsteps - 1)

    def idx_copy(step):
        return pltpu.make_async_copy(dest_hbm.at[pl.ds(pl.multiple_of(step * n, n), n)],
                                     idx_ref.at[pl.ds(pl.multiple_of((step & 1) * n, n), n)], sem_idx.at[step & 1])

    def row_copy(d, k, t, slot, prio):
        return pltpu.make_async_copy(y_hbm.at[pl.ds(d * PK_CHUNKS, PK_CHUNKS), :],
                                     buf_ref.at[slot, pl.ds((k * tm + t) * PK_CHUNKS, PK_CHUNKS), :],
                                     sem_row.at[slot]).start(priority=prio)

    def rows_wait(slot):
        pltpu.make_async_copy(y_hbm.at[pl.ds(0, n * PK_CHUNKS), :], buf_ref.at[slot], sem_row.at[slot]).wait()

    @pl.when(s == 0)
    def _():
        idx_copy(0).start()
        idx_copy(0).wait()

        def body(t2, carry):
            ds = [idx_ref[t2 * (2 * TOP_K) + j] for j in range(2 * TOP_K)]
            for j in range(2 * TOP_K):
                row_copy(ds[j], j % TOP_K, t2 * 2 + j // TOP_K, 0, j % 2)
            return carry

        lax.fori_loop(0, tm // 2, body, 0)

        @pl.when(nsteps > 1)
        def _():
            idx_copy(1).start()

    @pl.when(s + 1 < nsteps)
    def _():
        idx_copy(s + 1).wait()

    @pl.when(s + 2 < nsteps)
    def _():
        idx_copy(s + 2).start()

    sl = s & 1
    nsl = 1 - sl
    rows_wait(sl)
    bs = buf_ref.at[sl]
    m = mod_ref[0]
    gate = m[5:6]
    nbase = (nxt & 1) * n
    himask = jnp.uint32(0xFFFF0000)

    def group(g, carry):
        r0 = pl.multiple_of(g * 8, 8)
        ds = [idx_ref[nbase + r0 * TOP_K + j] for j in range(8 * TOP_K)]
        wg = w_ref[pl.ds(r0, 8), :]
        lo = [None] * PK_CHUNKS
        hi = [None] * PK_CHUNKS
        for k in range(TOP_K):
            wk = jnp.broadcast_to(wg[:, k:k + 1], (8, 128))
            for c in range(PK_CHUNKS):
                word = bs[pl.ds((k * tm + r0) * PK_CHUNKS + c, 8, stride=PK_CHUNKS), :]
                plo = wk * lax.bitcast_convert_type(lax.shift_left(word, jnp.uint32(16)), f32)
                phi = wk * lax.bitcast_convert_type(word & himask, f32)
                lo[c] = plo if k == 0 else lo[c] + plo
                hi[c] = phi if k == 0 else hi[c] + phi
        routed = jnp.concatenate(lo + hi, axis=1)
        o_ref[0, pl.ds(r0, 8), :] = x_ref[0, pl.ds(r0, 8), :] + gate * (routed + sh_ref[pl.ds(r0, 8), :])
        for j in range(8 * TOP_K):
            row_copy(ds[j], j % TOP_K, r0 + j // TOP_K, nsl, j % 2)
        return carry

    lax.fori_loop(0, tm // 8, group, 0)

    @pl.when(last)
    def _():
        rows_wait(nsl)

    if final:
        o_ref[0] = _rms(o_ref[0], gf_ref[...])


def _combine_call(dest_flat, y, x, mod, shared, w_tok, g_final, final):
    b, s, _ = x.shape
    tm = COMBINE_TILE
    nt = s // tm
    flat = lambda bb, i: (bb * nt + i, 0)
    return pl.pallas_call(
        functools.partial(_combine_kernel, final),
        grid=(b, nt),
        in_specs=[pl.BlockSpec(memory_space=pl.ANY), pl.BlockSpec(memory_space=pl.ANY),
                  pl.BlockSpec((1, tm, D), lambda bb, i: (bb, i, 0)),
                  pl.BlockSpec((1, 6, D), lambda bb, i: (bb, 0, 0)),
                  pl.BlockSpec((tm, D), flat),
                  pl.BlockSpec((tm, TOP_K), flat),
                  pl.BlockSpec((1, D), lambda bb, i: (0, 0))],
        out_specs=pl.BlockSpec((1, tm, D), lambda bb, i: (bb, i, 0)),
        out_shape=jax.ShapeDtypeStruct((b, s, D), f32),
        scratch_shapes=[pltpu.SMEM((2 * tm * TOP_K,), i32), pltpu.VMEM((2, TOP_K * tm * PK_CHUNKS, 128), u32),
                        pltpu.SemaphoreType.DMA((2,)), pltpu.SemaphoreType.DMA((2,))],
        compiler_params=_params(("arbitrary", "arbitrary")),
        name="combine",
    )(dest_flat, y, x, mod, shared, w_tok, g_final)


def _moe_layer(layer, x1, mod, norm_g, router_w, router_b, w1, w3, w2, sw1, sw3, sw2, g_final, final):
    b, s, _ = x1.shape
    t = b * s
    rwt = router_w.T
    rwh = rwt.astype(bf16)
    rwl = (rwt - rwh.astype(f32)).astype(bf16)
    s13 = jnp.concatenate([sw1, sw3], axis=1).astype(bf16)
    h2p, logits_t, shared = _ffn_pre_call(x1, mod, norm_g.reshape(1, D), rwh, rwl, s13, sw2.astype(bf16))
    tri = (lax.broadcasted_iota(i32, (ROUTE_TILE, ROUTE_TILE), 0)
           < lax.broadcasted_iota(i32, (ROUTE_TILE, ROUTE_TILE), 1)).astype(bf16)
    eidx, w_t, rank, counts = _route_call(logits_t, router_b.reshape(N_EXP, 1).astype(f32), tri)
    cnt = counts.reshape(N_EXP).astype(i32)
    nblk_e = (cnt + MOE_BLK - 1) // MOE_BLK
    blk_ends = jnp.cumsum(nblk_e)
    blk_off = blk_ends - nblk_e
    pad_off = blk_off * MOE_BLK
    nb_total = blk_ends[-1:].astype(i32)
    n_blk = t * TOP_K // MOE_BLK + N_EXP
    dest = _dest_call(eidx, rank, pad_off.astype(f32).reshape(N_EXP, 1))
    dest_flat = dest.T.reshape(t * TOP_K)
    xs = _dispatch_call(cnt, pad_off, nb_total, dest_flat, h2p, n_blk * MOE_BLK)
    y = _gmlp_call(layer, nblk_e, blk_off, nb_total, xs, w1, w3, w2)
    return _combine_call(dest_flat, y, x1, mod, shared, w_t.T, g_final.reshape(1, D), final)


def _rot_cols(w):
    d, n = w.shape
    w4 = w.reshape(d, n // 32, 2, 16)
    return jnp.stack([-w4[:, :, 1], w4[:, :, 0]], axis=2).reshape(d, n)


def _rope_tables(s):
    rows = s // GRID_W
    row = jnp.repeat(jnp.arange(rows, dtype=f32), GRID_W)
    col = jnp.tile(jnp.arange(GRID_W, dtype=f32), rows)
    n_freq = HEAD_DIM // 4
    inv = ROPE_BASE ** (-jnp.arange(n_freq, dtype=f32) / n_freq)
    ang_r = row[:, None] * inv
    ang_c = col[:, None] * inv
    cos = jnp.concatenate([jnp.cos(ang_r)] * 2 + [jnp.cos(ang_c)] * 2, axis=1)
    sin = jnp.concatenate([jnp.sin(ang_r)] * 2 + [jnp.sin(ang_c)] * 2, axis=1)
    return jnp.tile(cos, (1, 2)), jnp.tile(sin, (1, 2))


def _block_diag(w):
    h, dh, _ = w.shape
    eye = jnp.eye(h, dtype=w.dtype)
    return (eye[:, None, :, None] * w[:, :, None, :]).reshape(h * dh, h * dh)


def _even_layer_mixer(x, ctx, mod, norm_g, w_in, w_out, conv_w, conv_b, w_r, b_r, w_i, b_i, lam, sink):
    b, s, _ = x.shape
    r0, r1, r2 = LRU_W, 2 * LRU_W, 2 * LRU_W + Q_W
    wq = w_in[:, r1:r2].reshape(D, 2, 4, HEAD_DIM).transpose(0, 2, 1, 3).reshape(D, Q_W)
    wk = w_in[:, r2:r2 + KV_W]
    w_ext = jnp.concatenate([w_in[:, :r1], wq, w_in[:, r2:], _rot_cols(wq), _rot_cols(wk)], axis=1).astype(bf16)
    w_ctx = jnp.concatenate([w_in[:, :r0], w_in[:, r2:]], axis=1).astype(bf16)
    cos, sin = _rope_tables(s)
    g = norm_g.reshape(1, D)
    u, gt, q, k, v = _proj_in_call(x, mod, g, w_ext, cos, sin)
    uc, kx, vx = _proj_ctx_call(ctx, mod, g, w_ctx)
    wg = jnp.stack([jnp.concatenate([_block_diag(w_r[d]), _block_diag(w_i[d])], axis=1) for d in range(2)]).astype(bf16)
    bg = jnp.stack([jnp.concatenate([b_r[d], b_i[d]])[None, :] for d in range(2)])
    rec = _rglru_call(u, uc, conv_w, conv_b.reshape(1, LRU_W), wg, bg, lam.reshape(2, 1, LRU_W))
    att = _attn_call(sink, q, k, v, kx, vx)
    w_att = w_out[LRU_W:].reshape(2, 4, HEAD_DIM, D).transpose(1, 0, 2, 3).reshape(Q_W, D).astype(bf16)
    return _mix_out_call(x, mod, gt, rec, att, w_out[:LRU_W].astype(bf16), w_att)


def kernel(x, c, ctx, c_ctx, mod_w, mod_b, norm_mix_g, norm_ffn_g, final_norm_g, ab_w_in, ab_w_out, lru_conv_w,
           lru_conv_b, lru_wr, lru_br, lru_wi, lru_bi, lru_lambda, attn_sink, cm_w_in, cm_b_in, cm_dw_w, cm_dw_b,
           cm_ln_g, cm_ln_b, cm_w_out, cm_b_out, router_w, router_b, exp_w1, exp_w3, exp_w2, shared_w1, shared_w3,
           shared_w2):
    bsz = x.shape[0]
    depth = mod_w.shape[0]
    assert bsz + 1 <= MOD_ROWS - 7
    cc = jnp.zeros((MOD_ROWS, D), f32).at[:bsz].set(c).at[MOD_ROWS - 8].set(c_ctx)
    mod_all = _mod_call(cc, mod_w, mod_b).reshape(depth, MOD_ROWS, 6, D)
    for l in range(depth):
        mod = mod_all[l]
        last = l == depth - 1
        if l % 2 == 0:
            e = l // 2
            assert depth <= 2
            x1 = _even_layer_mixer(x, ctx, mod, norm_mix_g[l], ab_w_in[e], ab_w_out[e], lru_conv_w[e], lru_conv_b[e],
                                   lru_wr[e], lru_br[e], lru_wi[e], lru_bi[e], lru_lambda[e], attn_sink[e])
        else:
            o = l // 2
            zg = _conf_in_call(x, mod, norm_mix_g[l].reshape(1, D), cm_w_in[o].astype(bf16), cm_b_in[o].reshape(1, 2 * D))
            dw = jnp.concatenate([cm_dw_w[o], jnp.zeros((1, D), f32)], axis=0)
            x1 = _conf_out_call(x, mod, zg, dw, cm_dw_b[o].reshape(1, D), cm_ln_g[o].reshape(1, D),
                                cm_ln_b[o].reshape(1, D), cm_w_out[o].astype(bf16), cm_b_out[o].reshape(1, D))
        x = _moe_layer(l, x1, mod, norm_ffn_g[l], router_w[l], router_b[l], exp_w1, exp_w3, exp_w2,
                       shared_w1[l], shared_w3[l], shared_w2[l], final_norm_g, last)
    return x
```

```python
import functools

import jax
import jax.numpy as jnp
from jax import lax
from jax.experimental import pallas as pl
from jax.experimental.pallas import tpu as pltpu

f32 = jnp.float32
bf16 = jnp.bfloat16
i32 = jnp.int32
u32 = jnp.uint32

D = 1024
EPS = 1e-6
LRU_W = 512
LRU_C = 8.0
N_HEADS = 8
HEAD_DIM = 64
GRID_W = 64
ROPE_BASE = 10000.0
Q_W = 512
KV_W = 128
ATT_BLK = 128
CONV_K = 31
N_EXP = 256
TOP_K = 8
N_GRP = 8
TOPK_GRP = 4
GRP_SZ = N_EXP // N_GRP
EXP_D = 256
ROUTED_SCALE = 2.5
MOE_BLK = 256
PK_CHUNKS = D // 2 // 128

VMEM_LIMIT_V7X = 56 * 1024 * 1024
MOD_ROWS = 24

_NT = (((1,), (1,)), ((), ()))


def _params(sem):
    return pltpu.CompilerParams(dimension_semantics=sem, vmem_limit_bytes=VMEM_LIMIT_V7X)


def _sigmoid(x):
    return 1.0 / (1.0 + jnp.exp(-x))


def _silu(x):
    return x * _sigmoid(x)


def _gelu_tanh(x):
    return 0.5 * x * (1.0 + jnp.tanh(0.7978845608028654 * (x + 0.044715 * (x * x * x))))


def _rms(x, g):
    return x * lax.rsqrt(jnp.mean(x * x, axis=-1, keepdims=True) + EPS) * g


def _rms_mod(x, g, shift, scale):
    return _rms(x, g) * (1.0 + scale) + shift


def _dot(a, b):
    return jnp.dot(a, b, preferred_element_type=f32)


def _mod_kernel(c_ref, w_ref, b_ref, o_ref):
    a = _silu(c_ref[...]).astype(bf16)
    o_ref[0] = _dot(a, w_ref[0].astype(bf16)) + b_ref[0]


def _mod_call(cc, mod_w, mod_b):
    depth, _, n = mod_w.shape
    tn = 1536
    return pl.pallas_call(
        _mod_kernel,
        grid=(depth, n // tn),
        in_specs=[pl.BlockSpec((MOD_ROWS, D), lambda l, j: (0, 0)),
                  pl.BlockSpec((1, D, tn), lambda l, j: (l, 0, j)),
                  pl.BlockSpec((1, 1, tn), lambda l, j: (l, 0, j))],
        out_specs=pl.BlockSpec((1, MOD_ROWS, tn), lambda l, j: (l, 0, j)),
        out_shape=jax.ShapeDtypeStruct((depth, MOD_ROWS, n), f32),
        compiler_params=_params(("parallel", "parallel")),
        name="mod",
    )(cc, mod_w, mod_b.reshape(depth, 1, n))


def _proj_in_kernel(x_ref, mod_ref, g_ref, w_ref, cos_ref, sin_ref, u_ref, gt_ref, q_ref, k_ref, v_ref):
    m = mod_ref[0]
    h = _rms_mod(x_ref[0], g_ref[...], m[0:1], m[1:2]).astype(bf16)
    p = _dot(h, w_ref[...])
    u_ref[0] = p[:, 0:512]
    gt_ref[0] = p[:, 512:1024]
    cos = cos_ref[...]
    sin = sin_ref[...]
    qs = []
    for j in range(4):
        qj = p[:, 1024 + j * 128:1152 + j * 128] * cos + p[:, 1792 + j * 128:1920 + j * 128] * sin
        qs.append(qj * (HEAD_DIM ** -0.5))
    q_ref[0] = jnp.concatenate(qs, axis=1).astype(bf16)
    k_ref[0] = (p[:, 1536:1664] * cos + p[:, 2304:2432] * sin).astype(bf16)
    v_ref[0] = p[:, 1664:1792].astype(bf16)


def _proj_in_call(x, mod, g, w_ext, cos, sin, tm=512):
    b, s, _ = x.shape
    nw = w_ext.shape[1]
    row = lambda bb, i: (bb, i, 0)
    return pl.pallas_call(
        _proj_in_kernel,
        grid=(b, s // tm),
        in_specs=[pl.BlockSpec((1, tm, D), row),
                  pl.BlockSpec((1, 6, D), lambda bb, i: (bb, 0, 0)),
                  pl.BlockSpec((1, D), lambda bb, i: (0, 0)),
                  pl.BlockSpec((D, nw), lambda bb, i: (0, 0)),
                  pl.BlockSpec((tm, 128), lambda bb, i: (i, 0)),
                  pl.BlockSpec((tm, 128), lambda bb, i: (i, 0))],
        out_specs=[pl.BlockSpec((1, tm, LRU_W), row), pl.BlockSpec((1, tm, LRU_W), row),
                   pl.BlockSpec((1, tm, Q_W), row), pl.BlockSpec((1, tm, KV_W), row),
                   pl.BlockSpec((1, tm, KV_W), row)],
        out_shape=[jax.ShapeDtypeStruct((b, s, LRU_W), f32), jax.ShapeDtypeStruct((b, s, LRU_W), f32),
                   jax.ShapeDtypeStruct((b, s, Q_W), bf16), jax.ShapeDtypeStruct((b, s, KV_W), bf16),
                   jax.ShapeDtypeStruct((b, s, KV_W), bf16)],
        compiler_params=_params(("parallel", "parallel")),
        name="proj_in",
    )(x, mod, g, w_ext, cos, sin)


def _proj_ctx_kernel(x_ref, mod_ref, g_ref, w_ref, u_ref, k_ref, v_ref):
    m = mod_ref[0]
    h = _rms_mod(x_ref[0], g_ref[...], m[0:1], m[1:2]).astype(bf16)
    p = _dot(h, w_ref[...])
    u_ref[0] = p[:, 0:512]
    k_ref[0] = p[:, 512:640].astype(bf16)
    v_ref[0] = p[:, 640:768].astype(bf16)


def _proj_ctx_call(ctx, mod, g, w_ctx):
    b, n_ctx, _ = ctx.shape
    row = lambda bb: (bb, 0, 0)
    return pl.pallas_call(
        _proj_ctx_kernel,
        grid=(b,),
        in_specs=[pl.BlockSpec((1, n_ctx, D), row),
                  pl.BlockSpec((1, 6, D), lambda bb: (MOD_ROWS - 8, 0, 0)),
                  pl.BlockSpec((1, D), lambda bb: (0, 0)),
                  pl.BlockSpec((D, 768), lambda bb: (0, 0))],
        out_specs=[pl.BlockSpec((1, n_ctx, LRU_W), row), pl.BlockSpec((1, n_ctx, KV_W), row),
                   pl.BlockSpec((1, n_ctx, KV_W), row)],
        out_shape=[jax.ShapeDtypeStruct((b, n_ctx, LRU_W), f32), jax.ShapeDtypeStruct((b, n_ctx, KV_W), bf16),
                   jax.ShapeDtypeStruct((b, n_ctx, KV_W), bf16)],
        compiler_params=_params(("parallel",)),
        name="proj_ctx",
    )(ctx, mod, g, w_ctx)


LRU_CHUNK = 128
LRU_LANES = 512


def _rglru_kernel(u_ref, uc_ref, cw_ref, cb_ref, wg_ref, bg_ref, lam_ref, o_ref, pad_ref, cx_ref, cc_ref):
    s = u_ref.shape[1]
    n_ctx = uc_ref.shape[1]
    tc = LRU_CHUNK
    lw = LRU_LANES

    def conv_segment(src_ref, n, dst_ref):
        pad_ref[0:8] = jnp.zeros((8, lw), f32)
        pad_ref[8:8 + n] = src_ref[0]
        pad_ref[8 + n:16 + n] = jnp.zeros((8, lw), f32)
        for c in range(n // 256):
            acc = jnp.broadcast_to(cb_ref[...], (256, lw))
            for k in range(4):
                acc = acc + cw_ref[k:k + 1, :] * pad_ref[c * 256 + 6 + k:c * 256 + 6 + k + 256, :]
            dst_ref[c * 256:(c + 1) * 256] = acc

    conv_segment(uc_ref, n_ctx, cc_ref)
    conv_segment(u_ref, s, cx_ref)

    rowm = lax.broadcasted_iota(i32, (tc, lw), 0) & 7

    def scan_segment(src_ref, n, d, h0, write):
        lam = lam_ref[d, 0]
        sp = jnp.maximum(-lam, 0.0) + jnp.log(1.0 + jnp.exp(-jnp.abs(lam)))
        nch = n // tc

        def chunk(ci, h):
            c = ci if d == 0 else nch - 1 - ci
            t0 = pl.multiple_of(c * tc, tc)
            uc = src_ref[pl.ds(t0, tc), :]
            gates = _dot(uc.astype(bf16), wg_ref[d, 0]) + bg_ref[d, 0]
            r = _sigmoid(gates[:, 0:lw])
            ig = _sigmoid(gates[:, lw:2 * lw])
            log_a = (-LRU_C * sp) * r
            a = jnp.exp(log_a)
            bb = jnp.sqrt(-jnp.tanh(log_a) * (a * a + 1.0)) * (ig * uc)
            for sh in (1, 2, 4):
                if d == 0:
                    keep = rowm >= sh
                    a_sh = jnp.where(keep, pltpu.roll(a, sh, 0), 1.0)
                    b_sh = jnp.where(keep, pltpu.roll(bb, sh, 0), 0.0)
                else:
                    keep = rowm < 8 - sh
                    a_sh = jnp.where(keep, pltpu.roll(a, tc - sh, 0), 1.0)
                    b_sh = jnp.where(keep, pltpu.roll(bb, tc - sh, 0), 0.0)
                bb = a * b_sh + bb
                a = a * a_sh
            outs = [None] * (tc // 8)
            order = range(tc // 8) if d == 0 else range(tc // 8 - 1, -1, -1)
            for gi in order:
                hg = bb[gi * 8:(gi + 1) * 8] + a[gi * 8:(gi + 1) * 8] * h
                outs[gi] = hg
                h = hg[7:8] if d == 0 else hg[0:1]
            if write:
                hs = jnp.concatenate(outs, axis=0)
                if d == 0:
                    o_ref[0, pl.ds(t0, tc), :] = hs
                else:
                    o_ref[0, pl.ds(t0, tc), :] = o_ref[0, pl.ds(t0, tc), :] + hs
            return h

        return lax.fori_loop(0, nch, chunk, h0)

    for d in range(2):
        h = jnp.zeros((1, lw), f32)
        h = scan_segment(cc_ref, n_ctx, d, h, False)
        scan_segment(cx_ref, s, d, h, True)


def _rglru_call(u, uc, conv_w, conv_b, wg, bg, lam):
    b, s, _ = u.shape
    n_ctx = uc.shape[1]
    lw = LRU_LANES
    return pl.pallas_call(
        _rglru_kernel,
        grid=(b, LRU_W // lw),
        in_specs=[pl.BlockSpec((1, s, lw), lambda bb, g: (bb, 0, g)),
                  pl.BlockSpec((1, n_ctx, lw), lambda bb, g: (bb, 0, g)),
                  pl.BlockSpec((4, lw), lambda bb, g: (0, g)),
                  pl.BlockSpec((1, lw), lambda bb, g: (0, g)),
                  pl.BlockSpec((2, 1, lw, 2 * lw), lambda bb, g: (0, g, 0, 0)),
                  pl.BlockSpec((2, 1, 1, 2 * lw), lambda bb, g: (0, g, 0, 0)),
                  pl.BlockSpec((2, 1, 1, lw), lambda bb, g: (0, g, 0, 0))],
        out_specs=pl.BlockSpec((1, s, lw), lambda bb, g: (bb, 0, g)),
        out_shape=jax.ShapeDtypeStruct((b, s, LRU_W), f32),
        scratch_shapes=[pltpu.VMEM((s + 16, lw), f32), pltpu.VMEM((s, lw), f32), pltpu.VMEM((n_ctx, lw), f32)],
        compiler_params=_params(("parallel", "parallel")),
        name="rglru",
    )(u, uc, conv_w, conv_b, wg, bg, lam)


def _attn_kernel(sink_ref, q_ref, kp_ref, kc_ref, kn_ref, vp_ref, vc_ref, vn_ref, kx_ref, vx_ref, o_ref):
    n = pl.program_id(1)
    nb = pl.num_programs(1)
    blk = ATT_BLK
    q = q_ref[0]
    qall = jnp.concatenate([q[:, j * 128:(j + 1) * 128] for j in range(4)], axis=0)
    kw = jnp.concatenate([kp_ref[0], kc_ref[0], kn_ref[0]], axis=0)
    vw = jnp.concatenate([vp_ref[0], vc_ref[0], vn_ref[0]], axis=0)
    kx = kx_ref[0]
    vx = vx_ref[0]
    n_ctx = kx.shape[0]
    lo_w = lax.broadcasted_iota(i32, (3 * blk, 128), 1) < HEAD_DIM
    lo_x = lax.broadcasted_iota(i32, (n_ctx, 128), 1) < HEAD_DIM
    qi = lax.broadcasted_iota(i32, (4 * blk, 3 * blk), 0) & (blk - 1)
    kr = lax.broadcasted_iota(i32, (4 * blk, 3 * blk), 1) - blk
    lo = jnp.where(n > 0, -blk, 0)
    hi = jnp.where(n < nb - 1, 2 * blk, blk)
    dlt = kr - qi
    pen = jnp.where(dlt >= -blk, 0.0, -jnp.inf)
    pen = jnp.where(dlt <= blk, pen, -jnp.inf)
    pen = jnp.where(kr >= lo, pen, -jnp.inf)
    pen = jnp.where(kr < hi, pen, -jnp.inf)
    rb = lax.broadcasted_iota(i32, (4 * blk, 1), 0) // blk
    zero = jnp.zeros((), bf16)
    out = jnp.zeros((4 * blk, 128), f32)
    for half in range(2):
        sel_w = lo_w if half == 0 else jnp.logical_not(lo_w)
        sel_x = lo_x if half == 0 else jnp.logical_not(lo_x)
        s_w = lax.dot_general(qall, jnp.where(sel_w, kw, zero), _NT, preferred_element_type=f32) + pen
        s_c = lax.dot_general(qall, jnp.where(sel_x, kx, zero), _NT, preferred_element_type=f32)
        sk = jnp.where(rb == 0, sink_ref[4 * half],
                       jnp.where(rb == 1, sink_ref[4 * half + 1],
                                 jnp.where(rb == 2, sink_ref[4 * half + 2], sink_ref[4 * half + 3])))
        m = jnp.maximum(jnp.maximum(jnp.max(s_w, axis=1, keepdims=True), jnp.max(s_c, axis=1, keepdims=True)), sk)
        p_w = jnp.exp(s_w - m)
        p_c = jnp.exp(s_c - m)
        den = jnp.sum(p_w, axis=1, keepdims=True) + jnp.sum(p_c, axis=1, keepdims=True) + jnp.exp(sk - m)
        o = _dot(p_w.astype(bf16), jnp.where(sel_w, vw, zero)) + _dot(p_c.astype(bf16), jnp.where(sel_x, vx, zero))
        out = out + o / den
    o_ref[0] = jnp.concatenate([out[j * blk:(j + 1) * blk] for j in range(4)], axis=1).astype(bf16)


def _attn_call(sink, q, k, v, kx, vx):
    b, s, _ = q.shape
    n_ctx = kx.shape[1]
    nb = s // ATT_BLK
    cur = lambda bb, n: (bb, n, 0)
    prev = lambda bb, n: (bb, jnp.maximum(n - 1, 0), 0)
    nxt = lambda bb, n: (bb, jnp.minimum(n + 1, nb - 1), 0)
    kvb = (1, ATT_BLK, KV_W)
    return pl.pallas_call(
        _attn_kernel,
        grid=(b, nb),
        in_specs=[pl.BlockSpec(memory_space=pltpu.SMEM),
                  pl.BlockSpec((1, ATT_BLK, Q_W), cur),
                  pl.BlockSpec(kvb, prev), pl.BlockSpec(kvb, cur), pl.BlockSpec(kvb, nxt),
                  pl.BlockSpec(kvb, prev), pl.BlockSpec(kvb, cur), pl.BlockSpec(kvb, nxt),
                  pl.BlockSpec((1, n_ctx, KV_W), lambda bb, n: (bb, 0, 0)),
                  pl.BlockSpec((1, n_ctx, KV_W), lambda bb, n: (bb, 0, 0))],
        out_specs=pl.BlockSpec((1, ATT_BLK, Q_W), cur),
        out_shape=jax.ShapeDtypeStruct((b, s, Q_W), bf16),
        compiler_params=_params(("parallel", "parallel")),
        name="attn",
    )(sink, q, k, k, k, v, v, v, kx, vx)


def _mix_out_kernel(x_ref, mod_ref, gt_ref, rec_ref, att_ref, wr_ref, wa_ref, o_ref):
    m = mod_ref[0]
    a = (_gelu_tanh(gt_ref[0]) * rec_ref[0]).astype(bf16)
    y = _dot(a, wr_ref[...]) + _dot(att_ref[0], wa_ref[...])
    o_ref[0] = x_ref[0] + m[2:3] * y


def _mix_out_call(x, mod, gt, rec, att, w_rec, w_att, tm=512):
    b, s, _ = x.shape
    row = lambda bb, i: (bb, i, 0)
    return pl.pallas_call(
        _mix_out_kernel,
        grid=(b, s // tm),
        in_specs=[pl.BlockSpec((1, tm, D), row),
                  pl.BlockSpec((1, 6, D), lambda bb, i: (bb, 0, 0)),
                  pl.BlockSpec((1, tm, LRU_W), row), pl.BlockSpec((1, tm, LRU_W), row),
                  pl.BlockSpec((1, tm, Q_W), row),
                  pl.BlockSpec((LRU_W, D), lambda bb, i: (0, 0)),
                  pl.BlockSpec((Q_W, D), lambda bb, i: (0, 0))],
        out_specs=pl.BlockSpec((1, tm, D), row),
        out_shape=jax.ShapeDtypeStruct((b, s, D), f32),
        compiler_params=_params(("parallel", "parallel")),
        name="mix_out",
    )(x, mod, gt, rec, att, w_rec, w_att)


def _conf_in_kernel(x_ref, mod_ref, g_ref, w_ref, b_ref, o_ref):
    m = mod_ref[0]
    h = _rms_mod(x_ref[0], g_ref[...], m[0:1], m[1:2]).astype(bf16)
    z = _dot(h, w_ref[...]) + b_ref[...]
    o_ref[0] = z[:, 0:D] * _sigmoid(z[:, D:2 * D])


def _conf_in_call(x, mod, g, w, bias, tm=512):
    b, s, _ = x.shape
    row = lambda bb, i: (bb, i, 0)
    return pl.pallas_call(
        _conf_in_kernel,
        grid=(b, s // tm),
        in_specs=[pl.BlockSpec((1, tm, D), row),
                  pl.BlockSpec((1, 6, D), lambda bb, i: (bb, 0, 0)),
                  pl.BlockSpec((1, D), lambda bb, i: (0, 0)),
                  pl.BlockSpec((D, 2 * D), lambda bb, i: (0, 0)),
                  pl.BlockSpec((1, 2 * D), lambda bb, i: (0, 0))],
        out_specs=pl.BlockSpec((1, tm, D), row),
        out_shape=jax.ShapeDtypeStruct((b, s, D), f32),
        compiler_params=_params(("parallel", "parallel")),
        name="conf_in",
    )(x, mod, g, w, bias)


CONF_HALO = 16
CONF_ROWS = 128
CONF_LANE_PAD = 128


def _conf_out_kernel(x_ref, mod_ref, zc_ref, zp_ref, zn_ref, dw_ref, db_ref, lg_ref, lb_ref, w_ref, b_ref, o_ref,
                     pad_ref, sh_ref, cv_ref):
    i = pl.program_id(1)
    nt = pl.num_programs(1)
    tm = zc_ref.shape[1]
    zero = jnp.zeros((CONF_HALO, D), f32)
    pad_ref[0:CONF_HALO] = jnp.where(i > 0, zp_ref[0], zero)
    pad_ref[CONF_HALO:CONF_HALO + tm] = zc_ref[0]
    pad_ref[CONF_HALO + tm:2 * CONF_HALO + tm] = jnp.where(i < nt - 1, zn_ref[0], zero)
    for r in range(8):
        sh_ref[r, :, 0:D] = pad_ref[r:r + tm + 24, :]

    for lg in range(D // 128):
        l0 = lg * 128
        taps = [dw_ref[k:k + 1, l0:l0 + 128] for k in range(CONV_K)]
        bias = db_ref[:, l0:l0 + 128]

        def chunk(c, carry, l0=l0, taps=taps, bias=bias):
            t0 = pl.multiple_of(c * CONF_ROWS, CONF_ROWS)
            acc = jnp.broadcast_to(bias, (CONF_ROWS, 128))
            for k in range(CONV_K):
                kp = k + 1
                acc = acc + taps[k] * sh_ref[kp % 8, pl.ds(t0 + 8 * (kp // 8), CONF_ROWS), l0:l0 + 128]
            cv_ref[pl.ds(t0, CONF_ROWS), l0:l0 + 128] = acc
            return carry

        lax.fori_loop(0, tm // CONF_ROWS, chunk, 0)
    z = cv_ref[...]
    mu = jnp.mean(z, axis=-1, keepdims=True)
    zc = z - mu
    var = jnp.mean(zc * zc, axis=-1, keepdims=True)
    zn = zc * lax.rsqrt(var + EPS) * lg_ref[...] + lb_ref[...]
    y = _dot(_silu(zn).astype(bf16), w_ref[...]) + b_ref[...]
    m = mod_ref[0]
    o_ref[0] = x_ref[0] + m[2:3] * y


def _conf_out_call(x, mod, zg, dw_w, dw_b, ln_g, ln_b, w_out, b_out, tm=256):
    b, s, _ = x.shape
    row = lambda bb, i: (bb, i, 0)
    hb = tm // CONF_HALO
    nh = s // CONF_HALO
    vec = lambda bb, i: (0, 0)
    return pl.pallas_call(
        _conf_out_kernel,
        grid=(b, s // tm),
        in_specs=[pl.BlockSpec((1, tm, D), row),
                  pl.BlockSpec((1, 6, D), lambda bb, i: (bb, 0, 0)),
                  pl.BlockSpec((1, tm, D), row),
                  pl.BlockSpec((1, CONF_HALO, D), lambda bb, i: (bb, jnp.maximum(i * hb - 1, 0), 0)),
                  pl.BlockSpec((1, CONF_HALO, D), lambda bb, i: (bb, jnp.minimum((i + 1) * hb, nh - 1), 0)),
                  pl.BlockSpec((CONV_K + 1, D), vec),
                  pl.BlockSpec((1, D), vec), pl.BlockSpec((1, D), vec), pl.BlockSpec((1, D), vec),
                  pl.BlockSpec((D, D), vec), pl.BlockSpec((1, D), vec)],
        out_specs=pl.BlockSpec((1, tm, D), row),
        out_shape=jax.ShapeDtypeStruct((b, s, D), f32),
        scratch_shapes=[pltpu.VMEM((tm + 2 * CONF_HALO, D), f32), pltpu.VMEM((8, tm + 24, D + CONF_LANE_PAD), f32),
                        pltpu.VMEM((tm, D), f32)],
        compiler_params=_params(("parallel", "parallel")),
        name="conf_out",
    )(x, mod, zg, zg, zg, dw_w, dw_b, ln_g, ln_b, w_out, b_out)


def _ffn_pre_kernel(x_ref, mod_ref, g_ref, rwh_ref, rwl_ref, s13_ref, s2_ref, hp_ref, lg_ref, sh_ref):
    m = mod_ref[0]
    h2 = _rms_mod(x_ref[0], g_ref[...], m[3:4], m[4:5])
    hb = h2.astype(bf16)
    hbf = hb.astype(f32)
    hl = (h2 - hbf).astype(bf16)
    lg_ref[...] = (lax.dot_general(rwh_ref[...], hb, _NT, preferred_element_type=f32)
                   + lax.dot_general(rwh_ref[...], hl, _NT, preferred_element_type=f32)
                   + lax.dot_general(rwl_ref[...], hb, _NT, preferred_element_type=f32))
    a = _dot(hb, s13_ref[...])
    hid = (_silu(a[:, 0:EXP_D]) * a[:, EXP_D:2 * EXP_D]).astype(bf16)
    sh_ref[...] = _dot(hid, s2_ref[...])
    lo = lax.shift_right_logical(lax.bitcast_convert_type(hbf[:, 0:512], u32), jnp.uint32(16))
    hi = lax.bitcast_convert_type(hbf[:, 512:1024], u32) & jnp.uint32(0xFFFF0000)
    word = lo | hi
    for i in range(word.shape[0] // 8):
        for c in range(PK_CHUNKS):
            hp_ref[pl.ds(8 * PK_CHUNKS * i + c, 8, stride=PK_CHUNKS), :] = word[8 * i:8 * i + 8, 128 * c:128 * c + 128]


def _ffn_pre_call(x, mod, g, rwh, rwl, s13, s2, tm=512):
    b, s, _ = x.shape
    nt = s // tm
    t = b * s
    flat = lambda bb, i: (bb * nt + i, 0)
    vec = lambda bb, i: (0, 0)
    return pl.pallas_call(
        _ffn_pre_kernel,
        grid=(b, nt),
        in_specs=[pl.BlockSpec((1, tm, D), lambda bb, i: (bb, i, 0)),
                  pl.BlockSpec((1, 6, D), lambda bb, i: (bb, 0, 0)),
                  pl.BlockSpec((1, D), vec),
                  pl.BlockSpec((N_EXP, D), vec), pl.BlockSpec((N_EXP, D), vec),
                  pl.BlockSpec((D, 2 * EXP_D), vec), pl.BlockSpec((EXP_D, D), vec)],
        out_specs=[pl.BlockSpec((tm * PK_CHUNKS, 128), flat),
                   pl.BlockSpec((N_EXP, tm), lambda bb, i: (0, bb * nt + i)),
                   pl.BlockSpec((tm, D), flat)],
        out_shape=[jax.ShapeDtypeStruct((t * PK_CHUNKS, 128), u32), jax.ShapeDtypeStruct((N_EXP, t), f32),
                   jax.ShapeDtypeStruct((t, D), f32)],
        compiler_params=_params(("parallel", "parallel")),
        name="ffn_pre",
    )(x, mod, g, rwh, rwl, s13, s2)


ROUTE_TILE = 256


def _route_kernel(lg_ref, rb_ref, tri_ref, e_ref, w_ref, r_ref, c_ref, base_ref):
    i = pl.program_id(0)
    tr = lg_ref.shape[1]

    @pl.when(i == 0)
    def _():
        base_ref[...] = jnp.zeros_like(base_ref)

    scores = _sigmoid(lg_ref[...])
    biased = scores + rb_ref[...]
    neg = -jnp.inf
    rowf = lax.broadcasted_iota(i32, (N_EXP, tr), 0).astype(f32)
    r32 = lax.broadcasted_iota(i32, (GRP_SZ, tr), 0).astype(f32)
    gs = []
    for g in range(N_GRP):
        seg = biased[g * GRP_SZ:(g + 1) * GRP_SZ]
        m1 = jnp.max(seg, axis=0, keepdims=True)
        i1 = jnp.min(jnp.where(seg == m1, r32, 2.0 * GRP_SZ), axis=0, keepdims=True)
        m2 = jnp.max(jnp.where(r32 == i1, neg, seg), axis=0, keepdims=True)
        gs.append(m1 + m2)
    allowed = []
    for g in range(N_GRP):
        beat = jnp.zeros((1, tr), f32)
        for h in range(N_GRP):
            if h < g:
                beat = beat + jnp.where(gs[h] >= gs[g], 1.0, 0.0)
            elif h > g:
                beat = beat + jnp.where(gs[h] > gs[g], 1.0, 0.0)
        allowed.append(jnp.broadcast_to(beat, (GRP_SZ, tr)))
    allowed = jnp.concatenate(allowed, axis=0)
    masked = jnp.where(allowed < float(TOPK_GRP), biased, neg)
    cnt = jnp.zeros((N_EXP, tr), f32)
    idxs, ws = [], []
    for _ in range(TOP_K):
        m = jnp.max(masked, axis=0, keepdims=True)
        idx = jnp.min(jnp.where(masked == m, rowf, 2.0 * N_EXP), axis=0, keepdims=True)
        hit = rowf == idx
        ws.append(jnp.sum(jnp.where(hit, scores, 0.0), axis=0, keepdims=True))
        masked = jnp.where(hit, neg, masked)
        cnt = cnt + jnp.where(hit, 1.0, 0.0)
        idxs.append(idx)
    wsum = ws[0]
    for k in range(1, TOP_K):
        wsum = wsum + ws[k]
    pos = _dot(cnt.astype(bf16), tri_ref[...]) + base_ref[...]
    ranks = [jnp.sum(jnp.where(rowf == idxs[k], pos, 0.0), axis=0, keepdims=True) for k in range(TOP_K)]
    e_ref[...] = jnp.concatenate(idxs, axis=0).astype(i32)
    w_ref[...] = jnp.concatenate([ROUTED_SCALE * ws[k] / wsum for k in range(TOP_K)], axis=0)
    r_ref[...] = jnp.concatenate(ranks, axis=0).astype(i32)
    base_ref[...] = base_ref[...] + jnp.sum(cnt, axis=1, keepdims=True)
    c_ref[...] = base_ref[...]


def _route_call(logits_t, router_b, tri):
    t = logits_t.shape[1]
    tr = ROUTE_TILE
    col = lambda i: (0, i)
    return pl.pallas_call(
        _route_kernel,
        grid=(t // tr,),
        in_specs=[pl.BlockSpec((N_EXP, tr), col),
                  pl.BlockSpec((N_EXP, 1), lambda i: (0, 0)),
                  pl.BlockSpec((tr, tr), lambda i: (0, 0))],
        out_specs=[pl.BlockSpec((TOP_K, tr), col), pl.BlockSpec((TOP_K, tr), col), pl.BlockSpec((TOP_K, tr), col),
                   pl.BlockSpec((N_EXP, 1), lambda i: (0, 0))],
        out_shape=[jax.ShapeDtypeStruct((TOP_K, t), i32), jax.ShapeDtypeStruct((TOP_K, t), f32),
                   jax.ShapeDtypeStruct((TOP_K, t), i32), jax.ShapeDtypeStruct((N_EXP, 1), f32)],
        scratch_shapes=[pltpu.VMEM((N_EXP, 1), f32)],
        compiler_params=_params(("arbitrary",)),
        name="route",
    )(logits_t, router_b, tri)


def _dest_kernel(e_ref, r_ref, off_ref, d_ref):
    tr = e_ref.shape[1]
    rowi = lax.broadcasted_iota(i32, (N_EXP, tr), 0)
    off = off_ref[...]
    e = e_ref[...]
    rows = [jnp.sum(jnp.where(rowi == e[k:k + 1], off, 0.0), axis=0, keepdims=True) for k in range(TOP_K)]
    d_ref[...] = jnp.concatenate(rows, axis=0).astype(i32) + r_ref[...]


def _dest_call(eidx, rank, pad_off):
    t = eidx.shape[1]
    tr = 512
    col = lambda i: (0, i)
    return pl.pallas_call(
        _dest_kernel,
        grid=(t // tr,),
        in_specs=[pl.BlockSpec((TOP_K, tr), col), pl.BlockSpec((TOP_K, tr), col),
                  pl.BlockSpec((N_EXP, 1), lambda i: (0, 0))],
        out_specs=pl.BlockSpec((TOP_K, tr), col),
        out_shape=jax.ShapeDtypeStruct((TOP_K, t), i32),
        compiler_params=_params(("parallel",)),
        name="dest",
    )(eidx, rank, pad_off)


DISPATCH_TILE = 512
_PAD_PIECES = (128, 64, 32, 16, 8, 4, 2, 1)


DISPATCH_SLOTS = 3


def _dispatch_kernel(cnt_ref, off_ref, nbt_ref, dest_hbm, h_hbm, xs_hbm, idx_ref, hbuf, zero_ref, sem_idx, sem_tile,
                     sem_row, sem_z):
    i = pl.program_id(0)
    nsteps = pl.num_programs(0)
    n = idx_ref.shape[0] // 2
    ts = n // TOP_K
    trows = ts * PK_CHUNKS
    n_blk = xs_hbm.shape[0] // (MOE_BLK * PK_CHUNKS)

    def idx_copy(step):
        return pltpu.make_async_copy(dest_hbm.at[pl.ds(pl.multiple_of(step * n, n), n)],
                                     idx_ref.at[pl.ds(pl.multiple_of((step & 1) * n, n), n)], sem_idx.at[step & 1])

    def tile_copy(step):
        slot = lax.rem(step, DISPATCH_SLOTS)
        return pltpu.make_async_copy(h_hbm.at[pl.ds(pl.multiple_of(step * trows, trows), trows), :], hbuf.at[slot],
                                     sem_tile.at[slot])

    def rows_wait(step):
        pltpu.make_async_copy(xs_hbm.at[pl.ds(0, n * PK_CHUNKS), :], xs_hbm.at[pl.ds(0, n * PK_CHUNKS), :],
                              sem_row.at[lax.rem(step, DISPATCH_SLOTS)]).wait()

    def pad_copy(start_slot, p):
        return pltpu.make_async_copy(zero_ref.at[pl.ds(0, p * PK_CHUNKS), :],
                                     xs_hbm.at[pl.ds(start_slot * PK_CHUNKS, p * PK_CHUNKS), :], sem_z)

    def blk_copy(blk):
        return pltpu.make_async_copy(zero_ref, xs_hbm.at[pl.ds(blk * (MOE_BLK * PK_CHUNKS), MOE_BLK * PK_CHUNKS), :],
                                     sem_z)

    def for_each_pad_piece(fn):
        def per_expert(e, carry):
            c = cnt_ref[e]
            npad = ((c + (MOE_BLK - 1)) & (-MOE_BLK)) - c
            slot = off_ref[e] + c
            for p in _PAD_PIECES:
                @pl.when((npad & p) != 0)
                def _():
                    fn(pad_copy(slot, p))
                slot = slot + (npad & p)
            return carry

        lax.fori_loop(0, N_EXP, per_expert, 0)

    @pl.when(i == 0)
    def _():
        idx_copy(0).start()
        tile_copy(0).start()
        zero_ref[...] = jnp.zeros_like(zero_ref)
        for_each_pad_piece(lambda cp: cp.start())
        lax.fori_loop(nbt_ref[0], n_blk, lambda b, c: (blk_copy(b).start(), c)[1], 0)

    @pl.when(i >= DISPATCH_SLOTS - 1)
    def _():
        rows_wait(i - (DISPATCH_SLOTS - 1))

    @pl.when(i + 1 < nsteps)
    def _():
        idx_copy(i + 1).start()
        tile_copy(i + 1).start()

    idx_copy(i).wait()
    tile_copy(i).wait()
    sl = i & 1
    slot = lax.rem(i, DISPATCH_SLOTS)
    hb = hbuf.at[slot]

    def body(t2, carry):
        base = sl * n + t2 * (2 * TOP_K)
        ds = [idx_ref[base + j] for j in range(2 * TOP_K)]
        for j in range(2 * TOP_K):
            t = t2 * 2 + j // TOP_K
            pltpu.make_async_copy(hb.at[pl.ds(t * PK_CHUNKS, PK_CHUNKS), :],
                                  xs_hbm.at[pl.ds(ds[j] * PK_CHUNKS, PK_CHUNKS), :],
                                  sem_row.at[slot]).start(priority=j % 2)
        return carry

    lax.fori_loop(0, ts // 2, body, 0)

    @pl.when(i == nsteps - 1)
    def _():
        for back in range(DISPATCH_SLOTS - 2, -1, -1):
            @pl.when(i >= back)
            def _():
                rows_wait(i - back)

        for_each_pad_piece(lambda cp: cp.wait())
        lax.fori_loop(nbt_ref[0], n_blk, lambda b, c: (blk_copy(b).wait(), c)[1], 0)


def _dispatch_call(cnt, pad_off, nb_total, dest_flat, h2p, n_slots):
    t = h2p.shape[0] // PK_CHUNKS
    ts = DISPATCH_TILE
    gs = pltpu.PrefetchScalarGridSpec(
        num_scalar_prefetch=3,
        grid=(t // ts,),
        in_specs=[pl.BlockSpec(memory_space=pl.ANY), pl.BlockSpec(memory_space=pl.ANY)],
        out_specs=pl.BlockSpec(memory_space=pl.ANY),
        scratch_shapes=[pltpu.SMEM((2 * ts * TOP_K,), i32), pltpu.VMEM((DISPATCH_SLOTS, ts * PK_CHUNKS, 128), u32),
                        pltpu.VMEM((MOE_BLK * PK_CHUNKS, 128), u32),
                        pltpu.SemaphoreType.DMA((2,)), pltpu.SemaphoreType.DMA((DISPATCH_SLOTS,)),
                        pltpu.SemaphoreType.DMA((DISPATCH_SLOTS,)), pltpu.SemaphoreType.DMA(())],
    )
    return pl.pallas_call(
        _dispatch_kernel,
        grid_spec=gs,
        out_shape=jax.ShapeDtypeStruct((n_slots * PK_CHUNKS, 128), u32),
        compiler_params=_params(("arbitrary",)),
        name="dispatch",
    )(cnt, pad_off, nb_total, dest_flat, h2p)


GMLP_RING = 4


def _gmlp_kernel(nbe_ref, boff_ref, nbt_ref, w1_ref, w3_ref, w2_ref, xs_hbm, y_hbm, xbuf, ybuf, w13_s, w2_s,
                 sem_in, sem_out, sem_z):
    e = pl.program_id(0)
    nb = nbe_ref[e]
    b0 = boff_ref[e]
    total = nbt_ref[0]
    n_blk = y_hbm.shape[0] // (MOE_BLK * PK_CHUNKS)
    xrows = MOE_BLK * PK_CHUNKS
    yrows = MOE_BLK * PK_CHUNKS
    ring = GMLP_RING

    def in_copy(b):
        sl = b & (ring - 1)
        return pltpu.make_async_copy(xs_hbm.at[pl.ds(pl.multiple_of(b * xrows, xrows), xrows), :], xbuf.at[sl],
                                     sem_in.at[sl])

    def out_copy(b):
        sl = b & (ring - 1)
        return pltpu.make_async_copy(ybuf.at[sl], y_hbm.at[pl.ds(pl.multiple_of(b * yrows, yrows), yrows), :],
                                     sem_out.at[sl])

    def zero_copy(b):
        return pltpu.make_async_copy(ybuf.at[0], y_hbm.at[pl.ds(pl.multiple_of(b * yrows, yrows), yrows), :], sem_z)

    @pl.when(e == 0)
    def _():
        for b in range(ring - 1):
            @pl.when(b < total)
            def _():
                in_copy(b).start()

    @pl.when(nb > 0)
    def _():
        w13_s[:, 0:EXP_D] = w1_ref[0].astype(bf16)
        w13_s[:, EXP_D:2 * EXP_D] = w3_ref[0].astype(bf16)
        w2_s[...] = w2_ref[0].astype(bf16)

    def block(j, carry):
        b = b0 + j
        in_copy(b).wait()

        @pl.when(b + (ring - 1) < total)
        def _():
            in_copy(b + (ring - 1)).start()

        @pl.when(b >= ring)
        def _():
            out_copy(b - ring).wait()

        xb = xbuf.at[b & (ring - 1)]
        yb = ybuf.at[b & (ring - 1)]
        cols = []
        for c in range(PK_CHUNKS):
            cols.append(jnp.concatenate(
                [xb[pl.ds(8 * PK_CHUNKS * g + c, 8, stride=PK_CHUNKS), :] for g in range(MOE_BLK // 8)], axis=0))
        word = jnp.concatenate(cols, axis=1)
        xlo = lax.bitcast_convert_type(lax.shift_left(word, jnp.uint32(16)), f32).astype(bf16)
        xhi = lax.bitcast_convert_type(word & jnp.uint32(0xFFFF0000), f32).astype(bf16)
        h = _dot(xlo, w13_s[0:512, :]) + _dot(xhi, w13_s[512:1024, :])
        hid = (_silu(h[:, 0:EXP_D]) * h[:, EXP_D:2 * EXP_D]).astype(bf16)
        y = _dot(hid, w2_s[...])
        ylo = lax.shift_right_logical(lax.bitcast_convert_type(y[:, 0:512].astype(bf16).astype(f32), u32),
                                      jnp.uint32(16))
        yhi = lax.bitcast_convert_type(y[:, 512:1024].astype(bf16).astype(f32), u32) & jnp.uint32(0xFFFF0000)
        yw = ylo | yhi
        for g in range(MOE_BLK // 8):
            for c in range(PK_CHUNKS):
                yb[pl.ds(8 * PK_CHUNKS * g + c, 8, stride=PK_CHUNKS), :] = yw[8 * g:8 * g + 8, 128 * c:128 * c + 128]
        out_copy(b).start()
        return carry

    lax.fori_loop(0, nb, block, 0)

    @pl.when(e == pl.num_programs(0) - 1)
    def _():
        for back in range(ring, 0, -1):
            @pl.when(total >= back)
            def _():
                out_copy(total - back).wait()

        ybuf[0] = jnp.zeros(ybuf.shape[1:], u32)
        lax.fori_loop(total, n_blk, lambda b, c: (zero_copy(b).start(), c)[1], 0)
        lax.fori_loop(total, n_blk, lambda b, c: (zero_copy(b).wait(), c)[1], 0)


def _gmlp_call(layer, nblk_e, blk_off, nb_total, xs, w1, w3, w2):
    n_slots = xs.shape[0] // PK_CHUNKS
    wsel = lambda e, *_: (layer, e, 0, 0)
    gs = pltpu.PrefetchScalarGridSpec(
        num_scalar_prefetch=3,
        grid=(N_EXP,),
        in_specs=[pl.BlockSpec((None, 1, D, EXP_D), wsel), pl.BlockSpec((None, 1, D, EXP_D), wsel),
                  pl.BlockSpec((None, 1, EXP_D, D), wsel), pl.BlockSpec(memory_space=pl.ANY)],
        out_specs=pl.BlockSpec(memory_space=pl.ANY),
        scratch_shapes=[pltpu.VMEM((GMLP_RING, MOE_BLK * PK_CHUNKS, 128), u32),
                        pltpu.VMEM((GMLP_RING, MOE_BLK * PK_CHUNKS, 128), u32),
                        pltpu.VMEM((D, 2 * EXP_D), bf16), pltpu.VMEM((EXP_D, D), bf16),
                        pltpu.SemaphoreType.DMA((GMLP_RING,)), pltpu.SemaphoreType.DMA((GMLP_RING,)),
                        pltpu.SemaphoreType.DMA(())],
    )
    return pl.pallas_call(
        _gmlp_kernel,
        grid_spec=gs,
        out_shape=jax.ShapeDtypeStruct((n_slots * PK_CHUNKS, 128), u32),
        compiler_params=_params(("arbitrary",)),
        name="gmlp",
    )(nblk_e, blk_off, nb_total, w1, w3, w2, xs)


COMBINE_TILE = 128


def _combine_kernel(final, fuse_next, *refs):
    if fuse_next:
        (dest_hbm, y_hbm, x_ref, mod_ref, sh_ref, w_ref, gf_ref, nm_ref, ng_ref, nw_ref, nb_ref, o_ref, z_ref,
         idx_ref, buf_ref, sem_idx, sem_row) = refs
    else:
        dest_hbm, y_hbm, x_ref, mod_ref, sh_ref, w_ref, gf_ref, o_ref, idx_ref, buf_ref, sem_idx, sem_row = refs
    tm = x_ref.shape[1]
    n = tm * TOP_K
    s = pl.program_id(0) * pl.num_programs(1) + pl.program_id(1)
    nsteps = pl.num_programs(0) * pl.num_programs(1)
    last = s == nsteps - 1
    nxt = jnp.minimum(s + 1, nsteps - 1)

    def idx_copy(step):
        return pltpu.make_async_copy(dest_hbm.at[pl.ds(pl.multiple_of(step * n, n), n)],
                                     idx_ref.at[pl.ds(pl.multiple_of((step & 1) * n, n), n)], sem_idx.at[step & 1])

    def row_copy(d, k, t, slot, prio):
        return pltpu.make_async_copy(y_hbm.at[pl.ds(d * PK_CHUNKS, PK_CHUNKS), :],
                                     buf_ref.at[slot, pl.ds((k * tm + t) * PK_CHUNKS, PK_CHUNKS), :],
                                     sem_row.at[slot]).start(priority=prio)

    def rows_wait(slot):
        pltpu.make_async_copy(y_hbm.at[pl.ds(0, n * PK_CHUNKS), :], buf_ref.at[slot], sem_row.at[slot]).wait()

    @pl.when(s == 0)
    def _():
        idx_copy(0).start()
        idx_copy(0).wait()

        def body(t2, carry):
            ds = [idx_ref[t2 * (2 * TOP_K) + j] for j in range(2 * TOP_K)]
            for j in range(2 * TOP_K):
                row_copy(ds[j], j % TOP_K, t2 * 2 + j // TOP_K, 0, j % 2)
            return carry

        lax.fori_loop(0, tm // 2, body, 0)

        @pl.when(nsteps > 1)
        def _():
            idx_copy(1).start()

    @pl.when(s + 1 < nsteps)
    def _():
        idx_copy(s + 1).wait()

    @pl.when(s + 2 < nsteps)
    def _():
        idx_copy(s + 2).start()

    sl = s & 1
    nsl = 1 - sl
    rows_wait(sl)
    bs = buf_ref.at[sl]
    m = mod_ref[0]
    gate = m[5:6]
    nbase = (nxt & 1) * n
    himask = jnp.uint32(0xFFFF0000)

    def group(g, carry):
        r0 = pl.multiple_of(g * 8, 8)
        ds = [idx_ref[nbase + r0 * TOP_K + j] for j in range(8 * TOP_K)]
        wg = w_ref[pl.ds(r0, 8), :]
        lo = [None] * PK_CHUNKS
        hi = [None] * PK_CHUNKS
        for k in range(TOP_K):
            wk = jnp.broadcast_to(wg[:, k:k + 1], (8, 128))
            for c in range(PK_CHUNKS):
                word = bs[pl.ds((k * tm + r0) * PK_CHUNKS + c, 8, stride=PK_CHUNKS), :]
                plo = wk * lax.bitcast_convert_type(lax.shift_left(word, jnp.uint32(16)), f32)
                phi = wk * lax.bitcast_convert_type(word & himask, f32)
                lo[c] = plo if k == 0 else lo[c] + plo
                hi[c] = phi if k == 0 else hi[c] + phi
        routed = jnp.concatenate(lo + hi, axis=1)
        o_ref[0, pl.ds(r0, 8), :] = x_ref[0, pl.ds(r0, 8), :] + gate * (routed + sh_ref[pl.ds(r0, 8), :])
        for j in range(8 * TOP_K):
            row_copy(ds[j], j % TOP_K, r0 + j // TOP_K, nsl, j % 2)
        return carry

    lax.fori_loop(0, tm // 8, group, 0)

    @pl.when(last)
    def _():
        rows_wait(nsl)

    if final:
        o_ref[0] = _rms(o_ref[0], gf_ref[...])
    if fuse_next:
        nm = nm_ref[0]
        h = _rms_mod(o_ref[0], ng_ref[...], nm[0:1], nm[1:2]).astype(bf16)
        z = _dot(h, nw_ref[...]) + nb_ref[...]
        z_ref[0] = z[:, 0:D] * _sigmoid(z[:, D:2 * D])


def _combine_call(dest_flat, y, x, mod, shared, w_tok, g_final, final, nxt=None):
    b, s, _ = x.shape
    tm = COMBINE_TILE
    nt = s // tm
    flat = lambda bb, i: (bb * nt + i, 0)
    row = lambda bb, i: (bb, i, 0)
    vec = lambda bb, i: (0, 0)
    in_specs = [pl.BlockSpec(memory_space=pl.ANY), pl.BlockSpec(memory_space=pl.ANY),
                pl.BlockSpec((1, tm, D), row),
                pl.BlockSpec((1, 6, D), lambda bb, i: (bb, 0, 0)),
                pl.BlockSpec((tm, D), flat),
                pl.BlockSpec((tm, TOP_K), flat),
                pl.BlockSpec((1, D), vec)]
    out_specs = pl.BlockSpec((1, tm, D), row)
    out_shape = jax.ShapeDtypeStruct((b, s, D), f32)
    args = [dest_flat, y, x, mod, shared, w_tok, g_final]
    if nxt is not None:
        in_specs += [pl.BlockSpec((1, 6, D), lambda bb, i: (bb, 0, 0)), pl.BlockSpec((1, D), vec),
                     pl.BlockSpec((D, 2 * D), vec), pl.BlockSpec((1, 2 * D), vec)]
        out_specs = [out_specs, pl.BlockSpec((1, tm, D), row)]
        out_shape = [out_shape, jax.ShapeDtypeStruct((b, s, D), f32)]
        args += list(nxt)
    return pl.pallas_call(
        functools.partial(_combine_kernel, final, nxt is not None),
        grid=(b, nt),
        in_specs=in_specs,
        out_specs=out_specs,
        out_shape=out_shape,
        scratch_shapes=[pltpu.SMEM((2 * tm * TOP_K,), i32), pltpu.VMEM((2, TOP_K * tm * PK_CHUNKS, 128), u32),
                        pltpu.SemaphoreType.DMA((2,)), pltpu.SemaphoreType.DMA((2,))],
        compiler_params=_params(("arbitrary", "arbitrary")),
        name="combine",
    )(*args)


def _moe_layer(layer, x1, mod, norm_g, router_w, router_b, w1, w3, w2, sw1, sw3, sw2, g_final, final, nxt):
    b, s, _ = x1.shape
    t = b * s
    rwt = router_w.T
    rwh = rwt.astype(bf16)
    rwl = (rwt - rwh.astype(f32)).astype(bf16)
    s13 = jnp.concatenate([sw1, sw3], axis=1).astype(bf16)
    h2p, logits_t, shared = _ffn_pre_call(x1, mod, norm_g.reshape(1, D), rwh, rwl, s13, sw2.astype(bf16))
    tri = (lax.broadcasted_iota(i32, (ROUTE_TILE, ROUTE_TILE), 0)
           < lax.broadcasted_iota(i32, (ROUTE_TILE, ROUTE_TILE), 1)).astype(bf16)
    eidx, w_t, rank, counts = _route_call(logits_t, router_b.reshape(N_EXP, 1).astype(f32), tri)
    cnt = counts.reshape(N_EXP).astype(i32)
    nblk_e = (cnt + MOE_BLK - 1) // MOE_BLK
    blk_ends = jnp.cumsum(nblk_e)
    blk_off = blk_ends - nblk_e
    pad_off = blk_off * MOE_BLK
    nb_total = blk_ends[-1:].astype(i32)
    n_blk = t * TOP_K // MOE_BLK + N_EXP
    dest = _dest_call(eidx, rank, pad_off.astype(f32).reshape(N_EXP, 1))
    dest_flat = dest.T.reshape(t * TOP_K)
    xs = _dispatch_call(cnt, pad_off, nb_total, dest_flat, h2p, n_blk * MOE_BLK)
    y = _gmlp_call(layer, nblk_e, blk_off, nb_total, xs, w1, w3, w2)
    return _combine_call(dest_flat, y, x1, mod, shared, w_t.T, g_final.reshape(1, D), final, nxt)


def _rot_cols(w):
    d, n = w.shape
    w4 = w.reshape(d, n // 32, 2, 16)
    return jnp.stack([-w4[:, :, 1], w4[:, :, 0]], axis=2).reshape(d, n)


def _rope_tables(s):
    rows = s // GRID_W
    row = jnp.repeat(jnp.arange(rows, dtype=f32), GRID_W)
    col = jnp.tile(jnp.arange(GRID_W, dtype=f32), rows)
    n_freq = HEAD_DIM // 4
    inv = ROPE_BASE ** (-jnp.arange(n_freq, dtype=f32) / n_freq)
    ang_r = row[:, None] * inv
    ang_c = col[:, None] * inv
    cos = jnp.concatenate([jnp.cos(ang_r)] * 2 + [jnp.cos(ang_c)] * 2, axis=1)
    sin = jnp.concatenate([jnp.sin(ang_r)] * 2 + [jnp.sin(ang_c)] * 2, axis=1)
    return jnp.tile(cos, (1, 2)), jnp.tile(sin, (1, 2))


def _block_diag(w):
    h, dh, _ = w.shape
    eye = jnp.eye(h, dtype=w.dtype)
    return (eye[:, None, :, None] * w[:, :, None, :]).reshape(h * dh, h * dh)


def _even_layer_mixer(x, ctx, mod, norm_g, w_in, w_out, conv_w, conv_b, w_r, b_r, w_i, b_i, lam, sink):
    b, s, _ = x.shape
    r0, r1, r2 = LRU_W, 2 * LRU_W, 2 * LRU_W + Q_W
    wq = w_in[:, r1:r2].reshape(D, 2, 4, HEAD_DIM).transpose(0, 2, 1, 3).reshape(D, Q_W)
    wk = w_in[:, r2:r2 + KV_W]
    w_ext = jnp.concatenate([w_in[:, :r1], wq, w_in[:, r2:], _rot_cols(wq), _rot_cols(wk)], axis=1).astype(bf16)
    w_ctx = jnp.concatenate([w_in[:, :r0], w_in[:, r2:]], axis=1).astype(bf16)
    cos, sin = _rope_tables(s)
    g = norm_g.reshape(1, D)
    u, gt, q, k, v = _proj_in_call(x, mod, g, w_ext, cos, sin)
    uc, kx, vx = _proj_ctx_call(ctx, mod, g, w_ctx)
    n_lg = LRU_W // LRU_LANES
    hpg = LRU_LANES // HEAD_DIM

    def lane_groups(w):
        return jnp.stack([jnp.stack([_block_diag(w[d, g * hpg:(g + 1) * hpg]) for g in range(n_lg)]) for d in range(2)])

    wg = jnp.concatenate([lane_groups(w_r), lane_groups(w_i)], axis=-1).astype(bf16)
    bg = jnp.concatenate([b_r.reshape(2, n_lg, 1, LRU_LANES), b_i.reshape(2, n_lg, 1, LRU_LANES)], axis=-1)
    rec = _rglru_call(u, uc, conv_w, conv_b.reshape(1, LRU_W), wg, bg, lam.reshape(2, n_lg, 1, LRU_LANES))
    att = _attn_call(sink, q, k, v, kx, vx)
    w_att = w_out[LRU_W:].reshape(2, 4, HEAD_DIM, D).transpose(1, 0, 2, 3).reshape(Q_W, D).astype(bf16)
    return _mix_out_call(x, mod, gt, rec, att, w_out[:LRU_W].astype(bf16), w_att)


def kernel(x, c, ctx, c_ctx, mod_w, mod_b, norm_mix_g, norm_ffn_g, final_norm_g, ab_w_in, ab_w_out, lru_conv_w,
           lru_conv_b, lru_wr, lru_br, lru_wi, lru_bi, lru_lambda, attn_sink, cm_w_in, cm_b_in, cm_dw_w, cm_dw_b,
           cm_ln_g, cm_ln_b, cm_w_out, cm_b_out, router_w, router_b, exp_w1, exp_w3, exp_w2, shared_w1, shared_w3,
           shared_w2):
    bsz = x.shape[0]
    depth = mod_w.shape[0]
    assert bsz + 1 <= MOD_ROWS - 7
    cc = jnp.zeros((MOD_ROWS, D), f32).at[:bsz].set(c).at[MOD_ROWS - 8].set(c_ctx)
    mod_all = _mod_call(cc, mod_w, mod_b).reshape(depth, MOD_ROWS, 6, D)
    zg = None
    for l in range(depth):
        mod = mod_all[l]
        last = l == depth - 1
        nxt = None
        if not last and (l + 1) % 2 == 1:
            o1 = (l + 1) // 2
            nxt = (mod_all[l + 1], norm_mix_g[l + 1].reshape(1, D), cm_w_in[o1].astype(bf16),
                   cm_b_in[o1].reshape(1, 2 * D))
        if l % 2 == 0:
            e = l // 2
            assert depth <= 2
            x1 = _even_layer_mixer(x, ctx, mod, norm_mix_g[l], ab_w_in[e], ab_w_out[e], lru_conv_w[e], lru_conv_b[e],
                                   lru_wr[e], lru_br[e], lru_wi[e], lru_bi[e], lru_lambda[e], attn_sink[e])
        else:
            o = l // 2
            if zg is None:
                zg = _conf_in_call(x, mod, norm_mix_g[l].reshape(1, D), cm_w_in[o].astype(bf16),
                                   cm_b_in[o].reshape(1, 2 * D))
            dw = jnp.concatenate([cm_dw_w[o], jnp.zeros((1, D), f32)], axis=0)
            x1 = _conf_out_call(x, mod, zg, dw, cm_dw_b[o].reshape(1, D), cm_ln_g[o].reshape(1, D),
                                cm_ln_b[o].reshape(1, D), cm_w_out[o].astype(bf16), cm_b_out[o].reshape(1, D))
        out = _moe_layer(l, x1, mod, norm_ffn_g[l], router_w[l], router_b[l], exp_w1, exp_w3, exp_w2,
                         shared_w1[l], shared_w3[l], shared_w2[l], final_norm_g, last, nxt)
        x, zg = out if nxt is not None else (out, None)
    return x
```

```python
import functools

import jax
import jax.numpy as jnp
from jax import lax
from jax.experimental import pallas as pl
from jax.experimental.pallas import tpu as pltpu

f32 = jnp.float32
bf16 = jnp.bfloat16
i32 = jnp.int32
u32 = jnp.uint32

D = 1024
EPS = 1e-6
LRU_W = 512
LRU_C = 8.0
N_HEADS = 8
HEAD_DIM = 64
GRID_W = 64
ROPE_BASE = 10000.0
Q_W = 512
KV_W = 128
ATT_BLK = 128
CONV_K = 31
N_EXP = 256
TOP_K = 8
N_GRP = 8
TOPK_GRP = 4
GRP_SZ = N_EXP // N_GRP
EXP_D = 256
ROUTED_SCALE = 2.5
MOE_BLK = 256
PK_CHUNKS = D // 2 // 128

VMEM_LIMIT_V7X = 56 * 1024 * 1024
MOD_ROWS = 24

_NT = (((1,), (1,)), ((), ()))


def _params(sem):
    return pltpu.CompilerParams(dimension_semantics=sem, vmem_limit_bytes=VMEM_LIMIT_V7X)


def _sigmoid(x):
    return 1.0 / (1.0 + jnp.exp(-x))


def _silu(x):
    return x * _sigmoid(x)


def _gelu_tanh(x):
    return 0.5 * x * (1.0 + jnp.tanh(0.7978845608028654 * (x + 0.044715 * (x * x * x))))


def _rms(x, g):
    return x * lax.rsqrt(jnp.mean(x * x, axis=-1, keepdims=True) + EPS) * g


def _rms_mod(x, g, shift, scale):
    return _rms(x, g) * (1.0 + scale) + shift


def _dot(a, b):
    return jnp.dot(a, b, preferred_element_type=f32)


def _mod_kernel(c_ref, w_ref, b_ref, o_ref):
    a = _silu(c_ref[...]).astype(bf16)
    o_ref[0] = _dot(a, w_ref[0].astype(bf16)) + b_ref[0]


def _mod_call(cc, mod_w, mod_b):
    depth, _, n = mod_w.shape
    tn = 1536
    return pl.pallas_call(
        _mod_kernel,
        grid=(depth, n // tn),
        in_specs=[pl.BlockSpec((MOD_ROWS, D), lambda l, j: (0, 0)),
                  pl.BlockSpec((1, D, tn), lambda l, j: (l, 0, j)),
                  pl.BlockSpec((1, 1, tn), lambda l, j: (l, 0, j))],
        out_specs=pl.BlockSpec((1, MOD_ROWS, tn), lambda l, j: (l, 0, j)),
        out_shape=jax.ShapeDtypeStruct((depth, MOD_ROWS, n), f32),
        compiler_params=_params(("parallel", "parallel")),
        name="mod",
    )(cc, mod_w, mod_b.reshape(depth, 1, n))


def _proj_in_kernel(x_ref, mod_ref, g_ref, w_ref, cos_ref, sin_ref, u_ref, gt_ref, q_ref, k_ref, v_ref):
    m = mod_ref[0]
    h = _rms_mod(x_ref[0], g_ref[...], m[0:1], m[1:2]).astype(bf16)
    p = _dot(h, w_ref[...])
    u_ref[0] = p[:, 0:512]
    gt_ref[0] = p[:, 512:1024]
    cos = cos_ref[...]
    sin = sin_ref[...]
    qs = []
    for j in range(4):
        qj = p[:, 1024 + j * 128:1152 + j * 128] * cos + p[:, 1792 + j * 128:1920 + j * 128] * sin
        qs.append(qj * (HEAD_DIM ** -0.5))
    q_ref[0] = jnp.concatenate(qs, axis=1).astype(bf16)
    k_ref[0] = (p[:, 1536:1664] * cos + p[:, 2304:2432] * sin).astype(bf16)
    v_ref[0] = p[:, 1664:1792].astype(bf16)


def _proj_in_call(x, mod, g, w_ext, cos, sin, tm=512):
    b, s, _ = x.shape
    nw = w_ext.shape[1]
    row = lambda bb, i: (bb, i, 0)
    return pl.pallas_call(
        _proj_in_kernel,
        grid=(b, s // tm),
        in_specs=[pl.BlockSpec((1, tm, D), row),
                  pl.BlockSpec((1, 6, D), lambda bb, i: (bb, 0, 0)),
                  pl.BlockSpec((1, D), lambda bb, i: (0, 0)),
                  pl.BlockSpec((D, nw), lambda bb, i: (0, 0)),
                  pl.BlockSpec((tm, 128), lambda bb, i: (i, 0)),
                  pl.BlockSpec((tm, 128), lambda bb, i: (i, 0))],
        out_specs=[pl.BlockSpec((1, tm, LRU_W), row), pl.BlockSpec((1, tm, LRU_W), row),
                   pl.BlockSpec((1, tm, Q_W), row), pl.BlockSpec((1, tm, KV_W), row),
                   pl.BlockSpec((1, tm, KV_W), row)],
        out_shape=[jax.ShapeDtypeStruct((b, s, LRU_W), f32), jax.ShapeDtypeStruct((b, s, LRU_W), f32),
                   jax.ShapeDtypeStruct((b, s, Q_W), bf16), jax.ShapeDtypeStruct((b, s, KV_W), bf16),
                   jax.ShapeDtypeStruct((b, s, KV_W), bf16)],
        compiler_params=_params(("parallel", "parallel")),
        name="proj_in",
    )(x, mod, g, w_ext, cos, sin)


def _proj_ctx_kernel(x_ref, mod_ref, g_ref, w_ref, u_ref, k_ref, v_ref):
    m = mod_ref[0]
    h = _rms_mod(x_ref[0], g_ref[...], m[0:1], m[1:2]).astype(bf16)
    p = _dot(h, w_ref[...])
    u_ref[0] = p[:, 0:512]
    k_ref[0] = p[:, 512:640].astype(bf16)
    v_ref[0] = p[:, 640:768].astype(bf16)


def _proj_ctx_call(ctx, mod, g, w_ctx):
    b, n_ctx, _ = ctx.shape
    row = lambda bb: (bb, 0, 0)
    return pl.pallas_call(
        _proj_ctx_kernel,
        grid=(b,),
        in_specs=[pl.BlockSpec((1, n_ctx, D), row),
                  pl.BlockSpec((1, 6, D), lambda bb: (MOD_ROWS - 8, 0, 0)),
                  pl.BlockSpec((1, D), lambda bb: (0, 0)),
                  pl.BlockSpec((D, 768), lambda bb: (0, 0))],
        out_specs=[pl.BlockSpec((1, n_ctx, LRU_W), row), pl.BlockSpec((1, n_ctx, KV_W), row),
                   pl.BlockSpec((1, n_ctx, KV_W), row)],
        out_shape=[jax.ShapeDtypeStruct((b, n_ctx, LRU_W), f32), jax.ShapeDtypeStruct((b, n_ctx, KV_W), bf16),
                   jax.ShapeDtypeStruct((b, n_ctx, KV_W), bf16)],
        compiler_params=_params(("parallel",)),
        name="proj_ctx",
    )(ctx, mod, g, w_ctx)


LRU_CHUNK = 128
LRU_LANES = 512


def _rglru_kernel(u_ref, uc_ref, cw_ref, cb_ref, wg_ref, bg_ref, lam_ref, o_ref, pad_ref, cx_ref, cc_ref):
    s = u_ref.shape[1]
    n_ctx = uc_ref.shape[1]
    tc = LRU_CHUNK
    lw = LRU_LANES

    def conv_segment(src_ref, n, dst_ref):
        pad_ref[0:8] = jnp.zeros((8, lw), f32)
        pad_ref[8:8 + n] = src_ref[0]
        pad_ref[8 + n:16 + n] = jnp.zeros((8, lw), f32)
        for c in range(n // 256):
            acc = jnp.broadcast_to(cb_ref[...], (256, lw))
            for k in range(4):
                acc = acc + cw_ref[k:k + 1, :] * pad_ref[c * 256 + 6 + k:c * 256 + 6 + k + 256, :]
            dst_ref[c * 256:(c + 1) * 256] = acc

    conv_segment(uc_ref, n_ctx, cc_ref)
    conv_segment(u_ref, s, cx_ref)

    rowm = lax.broadcasted_iota(i32, (tc, lw), 0) & 7

    def scan_segment(src_ref, n, d, h0, write):
        lam = lam_ref[d, 0]
        sp = jnp.maximum(-lam, 0.0) + jnp.log(1.0 + jnp.exp(-jnp.abs(lam)))
        nch = n // tc

        def chunk(ci, h):
            c = ci if d == 0 else nch - 1 - ci
            t0 = pl.multiple_of(c * tc, tc)
            uc = src_ref[pl.ds(t0, tc), :]
            gates = _dot(uc.astype(bf16), wg_ref[d, 0]) + bg_ref[d, 0]
            r = _sigmoid(gates[:, 0:lw])
            ig = _sigmoid(gates[:, lw:2 * lw])
            log_a = (-LRU_C * sp) * r
            a = jnp.exp(log_a)
            bb = jnp.sqrt(-jnp.tanh(log_a) * (a * a + 1.0)) * (ig * uc)
            for sh in (1, 2, 4):
                if d == 0:
                    keep = rowm >= sh
                    a_sh = jnp.where(keep, pltpu.roll(a, sh, 0), 1.0)
                    b_sh = jnp.where(keep, pltpu.roll(bb, sh, 0), 0.0)
                else:
                    keep = rowm < 8 - sh
                    a_sh = jnp.where(keep, pltpu.roll(a, tc - sh, 0), 1.0)
                    b_sh = jnp.where(keep, pltpu.roll(bb, tc - sh, 0), 0.0)
                bb = a * b_sh + bb
                a = a * a_sh
            outs = [None] * (tc // 8)
            order = range(tc // 8) if d == 0 else range(tc // 8 - 1, -1, -1)
            for gi in order:
                hg = bb[gi * 8:(gi + 1) * 8] + a[gi * 8:(gi + 1) * 8] * h
                outs[gi] = hg
                h = hg[7:8] if d == 0 else hg[0:1]
            if write:
                hs = jnp.concatenate(outs, axis=0)
                if d == 0:
                    o_ref[0, pl.ds(t0, tc), :] = hs
                else:
                    o_ref[0, pl.ds(t0, tc), :] = o_ref[0, pl.ds(t0, tc), :] + hs
            return h

        return lax.fori_loop(0, nch, chunk, h0)

    for d in range(2):
        h = jnp.zeros((1, lw), f32)
        h = scan_segment(cc_ref, n_ctx, d, h, False)
        scan_segment(cx_ref, s, d, h, True)


def _rglru_call(u, uc, conv_w, conv_b, wg, bg, lam):
    b, s, _ = u.shape
    n_ctx = uc.shape[1]
    lw = LRU_LANES
    return pl.pallas_call(
        _rglru_kernel,
        grid=(b, LRU_W // lw),
        in_specs=[pl.BlockSpec((1, s, lw), lambda bb, g: (bb, 0, g)),
                  pl.BlockSpec((1, n_ctx, lw), lambda bb, g: (bb, 0, g)),
                  pl.BlockSpec((4, lw), lambda bb, g: (0, g)),
                  pl.BlockSpec((1, lw), lambda bb, g: (0, g)),
                  pl.BlockSpec((2, 1, lw, 2 * lw), lambda bb, g: (0, g, 0, 0)),
                  pl.BlockSpec((2, 1, 1, 2 * lw), lambda bb, g: (0, g, 0, 0)),
                  pl.BlockSpec((2, 1, 1, lw), lambda bb, g: (0, g, 0, 0))],
        out_specs=pl.BlockSpec((1, s, lw), lambda bb, g: (bb, 0, g)),
        out_shape=jax.ShapeDtypeStruct((b, s, LRU_W), f32),
        scratch_shapes=[pltpu.VMEM((s + 16, lw), f32), pltpu.VMEM((s, lw), f32), pltpu.VMEM((n_ctx, lw), f32)],
        compiler_params=_params(("parallel", "parallel")),
        name="rglru",
    )(u, uc, conv_w, conv_b, wg, bg, lam)


def _attn_kernel(sink_ref, q_ref, kp_ref, kc_ref, kn_ref, vp_ref, vc_ref, vn_ref, kx_ref, vx_ref, o_ref):
    n = pl.program_id(1)
    nb = pl.num_programs(1)
    blk = ATT_BLK
    q = q_ref[0]
    qall = jnp.concatenate([q[:, j * 128:(j + 1) * 128] for j in range(4)], axis=0)
    kw = jnp.concatenate([kp_ref[0], kc_ref[0], kn_ref[0]], axis=0)
    vw = jnp.concatenate([vp_ref[0], vc_ref[0], vn_ref[0]], axis=0)
    kx = kx_ref[0]
    vx = vx_ref[0]
    n_ctx = kx.shape[0]
    lo_w = lax.broadcasted_iota(i32, (3 * blk, 128), 1) < HEAD_DIM
    lo_x = lax.broadcasted_iota(i32, (n_ctx, 128), 1) < HEAD_DIM
    qi = lax.broadcasted_iota(i32, (4 * blk, 3 * blk), 0) & (blk - 1)
    kr = lax.broadcasted_iota(i32, (4 * blk, 3 * blk), 1) - blk
    lo = jnp.where(n > 0, -blk, 0)
    hi = jnp.where(n < nb - 1, 2 * blk, blk)
    dlt = kr - qi
    pen = jnp.where(dlt >= -blk, 0.0, -jnp.inf)
    pen = jnp.where(dlt <= blk, pen, -jnp.inf)
    pen = jnp.where(kr >= lo, pen, -jnp.inf)
    pen = jnp.where(kr < hi, pen, -jnp.inf)
    rb = lax.broadcasted_iota(i32, (4 * blk, 1), 0) // blk
    zero = jnp.zeros((), bf16)
    out = jnp.zeros((4 * blk, 128), f32)
    for half in range(2):
        sel_w = lo_w if half == 0 else jnp.logical_not(lo_w)
        sel_x = lo_x if half == 0 else jnp.logical_not(lo_x)
        s_w = lax.dot_general(qall, jnp.where(sel_w, kw, zero), _NT, preferred_element_type=f32) + pen
        s_c = lax.dot_general(qall, jnp.where(sel_x, kx, zero), _NT, preferred_element_type=f32)
        sk = jnp.where(rb == 0, sink_ref[4 * half],
                       jnp.where(rb == 1, sink_ref[4 * half + 1],
                                 jnp.where(rb == 2, sink_ref[4 * half + 2], sink_ref[4 * half + 3])))
        m = jnp.maximum(jnp.maximum(jnp.max(s_w, axis=1, keepdims=True), jnp.max(s_c, axis=1, keepdims=True)), sk)
        p_w = jnp.exp(s_w - m)
        p_c = jnp.exp(s_c - m)
        den = jnp.sum(p_w, axis=1, keepdims=True) + jnp.sum(p_c, axis=1, keepdims=True) + jnp.exp(sk - m)
        o = _dot(p_w.astype(bf16), jnp.where(sel_w, vw, zero)) + _dot(p_c.astype(bf16), jnp.where(sel_x, vx, zero))
        out = out + o / den
    o_ref[0] = jnp.concatenate([out[j * blk:(j + 1) * blk] for j in range(4)], axis=1).astype(bf16)


def _attn_call(sink, q, k, v, kx, vx):
    b, s, _ = q.shape
    n_ctx = kx.shape[1]
    nb = s // ATT_BLK
    cur = lambda bb, n: (bb, n, 0)
    prev = lambda bb, n: (bb, jnp.maximum(n - 1, 0), 0)
    nxt = lambda bb, n: (bb, jnp.minimum(n + 1, nb - 1), 0)
    kvb = (1, ATT_BLK, KV_W)
    return pl.pallas_call(
        _attn_kernel,
        grid=(b, nb),
        in_specs=[pl.BlockSpec(memory_space=pltpu.SMEM),
                  pl.BlockSpec((1, ATT_BLK, Q_W), cur),
                  pl.BlockSpec(kvb, prev), pl.BlockSpec(kvb, cur), pl.BlockSpec(kvb, nxt),
                  pl.BlockSpec(kvb, prev), pl.BlockSpec(kvb, cur), pl.BlockSpec(kvb, nxt),
                  pl.BlockSpec((1, n_ctx, KV_W), lambda bb, n: (bb, 0, 0)),
                  pl.BlockSpec((1, n_ctx, KV_W), lambda bb, n: (bb, 0, 0))],
        out_specs=pl.BlockSpec((1, ATT_BLK, Q_W), cur),
        out_shape=jax.ShapeDtypeStruct((b, s, Q_W), bf16),
        compiler_params=_params(("parallel", "parallel")),
        name="attn",
    )(sink, q, k, k, k, v, v, v, kx, vx)


def _mix_out_kernel(x_ref, mod_ref, gt_ref, rec_ref, att_ref, wr_ref, wa_ref, o_ref):
    m = mod_ref[0]
    a = (_gelu_tanh(gt_ref[0]) * rec_ref[0]).astype(bf16)
    y = _dot(a, wr_ref[...]) + _dot(att_ref[0], wa_ref[...])
    o_ref[0] = x_ref[0] + m[2:3] * y


def _mix_out_call(x, mod, gt, rec, att, w_rec, w_att, tm=512):
    b, s, _ = x.shape
    row = lambda bb, i: (bb, i, 0)
    return pl.pallas_call(
        _mix_out_kernel,
        grid=(b, s // tm),
        in_specs=[pl.BlockSpec((1, tm, D), row),
                  pl.BlockSpec((1, 6, D), lambda bb, i: (bb, 0, 0)),
                  pl.BlockSpec((1, tm, LRU_W), row), pl.BlockSpec((1, tm, LRU_W), row),
                  pl.BlockSpec((1, tm, Q_W), row),
                  pl.BlockSpec((LRU_W, D), lambda bb, i: (0, 0)),
                  pl.BlockSpec((Q_W, D), lambda bb, i: (0, 0))],
        out_specs=pl.BlockSpec((1, tm, D), row),
        out_shape=jax.ShapeDtypeStruct((b, s, D), f32),
        compiler_params=_params(("parallel", "parallel")),
        name="mix_out",
    )(x, mod, gt, rec, att, w_rec, w_att)


def _conf_in_kernel(x_ref, mod_ref, g_ref, w_ref, b_ref, o_ref):
    m = mod_ref[0]
    h = _rms_mod(x_ref[0], g_ref[...], m[0:1], m[1:2]).astype(bf16)
    z = _dot(h, w_ref[...]) + b_ref[...]
    o_ref[0] = z[:, 0:D] * _sigmoid(z[:, D:2 * D])


def _conf_in_call(x, mod, g, w, bias, tm=512):
    b, s, _ = x.shape
    row = lambda bb, i: (bb, i, 0)
    return pl.pallas_call(
        _conf_in_kernel,
        grid=(b, s // tm),
        in_specs=[pl.BlockSpec((1, tm, D), row),
                  pl.BlockSpec((1, 6, D), lambda bb, i: (bb, 0, 0)),
                  pl.BlockSpec((1, D), lambda bb, i: (0, 0)),
                  pl.BlockSpec((D, 2 * D), lambda bb, i: (0, 0)),
                  pl.BlockSpec((1, 2 * D), lambda bb, i: (0, 0))],
        out_specs=pl.BlockSpec((1, tm, D), row),
        out_shape=jax.ShapeDtypeStruct((b, s, D), f32),
        compiler_params=_params(("parallel", "parallel")),
        name="conf_in",
    )(x, mod, g, w, bias)


CONF_HALO = 16
CONF_ROWS = 128
CONF_LANE_PAD = 128


def _conf_out_kernel(x_ref, mod_ref, zc_ref, zp_ref, zn_ref, dw_ref, db_ref, lg_ref, lb_ref, w_ref, b_ref, o_ref,
                     pad_ref, sh_ref, cv_ref):
    i = pl.program_id(1)
    nt = pl.num_programs(1)
    tm = zc_ref.shape[1]
    zero = jnp.zeros((CONF_HALO, D), f32)
    pad_ref[0:CONF_HALO] = jnp.where(i > 0, zp_ref[0], zero)
    pad_ref[CONF_HALO:CONF_HALO + tm] = zc_ref[0]
    pad_ref[CONF_HALO + tm:2 * CONF_HALO + tm] = jnp.where(i < nt - 1, zn_ref[0], zero)
    for r in range(8):
        sh_ref[r, :, 0:D] = pad_ref[r:r + tm + 24, :]

    for lg in range(D // 128):
        l0 = lg * 128
        taps = [dw_ref[k:k + 1, l0:l0 + 128] for k in range(CONV_K)]
        bias = db_ref[:, l0:l0 + 128]

        def chunk(c, carry, l0=l0, taps=taps, bias=bias):
            t0 = pl.multiple_of(c * CONF_ROWS, CONF_ROWS)
            acc = jnp.broadcast_to(bias, (CONF_ROWS, 128))
            for k in range(CONV_K):
                kp = k + 1
                acc = acc + taps[k] * sh_ref[kp % 8, pl.ds(t0 + 8 * (kp // 8), CONF_ROWS), l0:l0 + 128]
            cv_ref[pl.ds(t0, CONF_ROWS), l0:l0 + 128] = acc
            return carry

        lax.fori_loop(0, tm // CONF_ROWS, chunk, 0)
    z = cv_ref[...]
    mu = jnp.mean(z, axis=-1, keepdims=True)
    zc = z - mu
    var = jnp.mean(zc * zc, axis=-1, keepdims=True)
    zn = zc * lax.rsqrt(var + EPS) * lg_ref[...] + lb_ref[...]
    y = _dot(_silu(zn).astype(bf16), w_ref[...]) + b_ref[...]
    m = mod_ref[0]
    o_ref[0] = x_ref[0] + m[2:3] * y


def _conf_out_call(x, mod, zg, dw_w, dw_b, ln_g, ln_b, w_out, b_out, tm=256):
    b, s, _ = x.shape
    row = lambda bb, i: (bb, i, 0)
    hb = tm // CONF_HALO
    nh = s // CONF_HALO
    vec = lambda bb, i: (0, 0)
    return pl.pallas_call(
        _conf_out_kernel,
        grid=(b, s // tm),
        in_specs=[pl.BlockSpec((1, tm, D), row),
                  pl.BlockSpec((1, 6, D), lambda bb, i: (bb, 0, 0)),
                  pl.BlockSpec((1, tm, D), row),
                  pl.BlockSpec((1, CONF_HALO, D), lambda bb, i: (bb, jnp.maximum(i * hb - 1, 0), 0)),
                  pl.BlockSpec((1, CONF_HALO, D), lambda bb, i: (bb, jnp.minimum((i + 1) * hb, nh - 1), 0)),
                  pl.BlockSpec((CONV_K + 1, D), vec),
                  pl.BlockSpec((1, D), vec), pl.BlockSpec((1, D), vec), pl.BlockSpec((1, D), vec),
                  pl.BlockSpec((D, D), vec), pl.BlockSpec((1, D), vec)],
        out_specs=pl.BlockSpec((1, tm, D), row),
        out_shape=jax.ShapeDtypeStruct((b, s, D), f32),
        scratch_shapes=[pltpu.VMEM((tm + 2 * CONF_HALO, D), f32), pltpu.VMEM((8, tm + 24, D + CONF_LANE_PAD), f32),
                        pltpu.VMEM((tm, D), f32)],
        compiler_params=_params(("parallel", "parallel")),
        name="conf_out",
    )(x, mod, zg, zg, zg, dw_w, dw_b, ln_g, ln_b, w_out, b_out)


def _ffn_pre_kernel(x_ref, mod_ref, g_ref, rwh_ref, rwl_ref, s13_ref, s2_ref, hp_ref, lg_ref, sh_ref):
    m = mod_ref[0]
    h2 = _rms_mod(x_ref[0], g_ref[...], m[3:4], m[4:5])
    hb = h2.astype(bf16)
    hbf = hb.astype(f32)
    hl = (h2 - hbf).astype(bf16)
    lg_ref[...] = (lax.dot_general(rwh_ref[...], hb, _NT, preferred_element_type=f32)
                   + lax.dot_general(rwh_ref[...], hl, _NT, preferred_element_type=f32)
                   + lax.dot_general(rwl_ref[...], hb, _NT, preferred_element_type=f32))
    a = _dot(hb, s13_ref[...])
    hid = (_silu(a[:, 0:EXP_D]) * a[:, EXP_D:2 * EXP_D]).astype(bf16)
    sh_ref[...] = _dot(hid, s2_ref[...])
    lo = lax.shift_right_logical(lax.bitcast_convert_type(hbf[:, 0:512], u32), jnp.uint32(16))
    hi = lax.bitcast_convert_type(hbf[:, 512:1024], u32) & jnp.uint32(0xFFFF0000)
    word = lo | hi
    for i in range(word.shape[0] // 8):
        for c in range(PK_CHUNKS):
            hp_ref[pl.ds(8 * PK_CHUNKS * i + c, 8, stride=PK_CHUNKS), :] = word[8 * i:8 * i + 8, 128 * c:128 * c + 128]


def _ffn_pre_call(x, mod, g, rwh, rwl, s13, s2, tm=512):
    b, s, _ = x.shape
    nt = s // tm
    t = b * s
    flat = lambda bb, i: (bb * nt + i, 0)
    vec = lambda bb, i: (0, 0)
    return pl.pallas_call(
        _ffn_pre_kernel,
        grid=(b, nt),
        in_specs=[pl.BlockSpec((1, tm, D), lambda bb, i: (bb, i, 0)),
                  pl.BlockSpec((1, 6, D), lambda bb, i: (bb, 0, 0)),
                  pl.BlockSpec((1, D), vec),
                  pl.BlockSpec((N_EXP, D), vec), pl.BlockSpec((N_EXP, D), vec),
                  pl.BlockSpec((D, 2 * EXP_D), vec), pl.BlockSpec((EXP_D, D), vec)],
        out_specs=[pl.BlockSpec((tm * PK_CHUNKS, 128), flat),
                   pl.BlockSpec((N_EXP, tm), lambda bb, i: (0, bb * nt + i)),
                   pl.BlockSpec((tm, D), flat)],
        out_shape=[jax.ShapeDtypeStruct((t * PK_CHUNKS, 128), u32), jax.ShapeDtypeStruct((N_EXP, t), f32),
                   jax.ShapeDtypeStruct((t, D), f32)],
        compiler_params=_params(("parallel", "parallel")),
        name="ffn_pre",
    )(x, mod, g, rwh, rwl, s13, s2)


ROUTE_TILE = 256


def _route_kernel(lg_ref, rb_ref, tri_ref, e_ref, w_ref, r_ref, c_ref, base_ref):
    i = pl.program_id(0)
    tr = lg_ref.shape[1]

    @pl.when(i == 0)
    def _():
        base_ref[...] = jnp.zeros_like(base_ref)

    scores = _sigmoid(lg_ref[...])
    biased = scores + rb_ref[...]
    neg = -jnp.inf
    rowf = lax.broadcasted_iota(i32, (N_EXP, tr), 0).astype(f32)
    r32 = lax.broadcasted_iota(i32, (GRP_SZ, tr), 0).astype(f32)
    gs = []
    for g in range(N_GRP):
        seg = biased[g * GRP_SZ:(g + 1) * GRP_SZ]
        m1 = jnp.max(seg, axis=0, keepdims=True)
        i1 = jnp.min(jnp.where(seg == m1, r32, 2.0 * GRP_SZ), axis=0, keepdims=True)
        m2 = jnp.max(jnp.where(r32 == i1, neg, seg), axis=0, keepdims=True)
        gs.append(m1 + m2)
    allowed = []
    for g in range(N_GRP):
        beat = jnp.zeros((1, tr), f32)
        for h in range(N_GRP):
            if h < g:
                beat = beat + jnp.where(gs[h] >= gs[g], 1.0, 0.0)
            elif h > g:
                beat = beat + jnp.where(gs[h] > gs[g], 1.0, 0.0)
        allowed.append(jnp.broadcast_to(beat, (GRP_SZ, tr)))
    allowed = jnp.concatenate(allowed, axis=0)
    masked = jnp.where(allowed < float(TOPK_GRP), biased, neg)
    cnt = jnp.zeros((N_EXP, tr), f32)
    idxs, ws = [], []
    for _ in range(TOP_K):
        m = jnp.max(masked, axis=0, keepdims=True)
        idx = jnp.min(jnp.where(masked == m, rowf, 2.0 * N_EXP), axis=0, keepdims=True)
        hit = rowf == idx
        ws.append(jnp.sum(jnp.where(hit, scores, 0.0), axis=0, keepdims=True))
        masked = jnp.where(hit, neg, masked)
        cnt = cnt + jnp.where(hit, 1.0, 0.0)
        idxs.append(idx)
    wsum = ws[0]
    for k in range(1, TOP_K):
        wsum = wsum + ws[k]
    pos = _dot(cnt.astype(bf16), tri_ref[...]) + base_ref[...]
    ranks = [jnp.sum(jnp.where(rowf == idxs[k], pos, 0.0), axis=0, keepdims=True) for k in range(TOP_K)]
    e_ref[...] = jnp.concatenate(idxs, axis=0).astype(i32)
    w_ref[...] = jnp.concatenate([ROUTED_SCALE * ws[k] / wsum for k in range(TOP_K)], axis=0)
    r_ref[...] = jnp.concatenate(ranks, axis=0).astype(i32)
    base_ref[...] = base_ref[...] + jnp.sum(cnt, axis=1, keepdims=True)
    c_ref[...] = base_ref[...]


def _route_call(logits_t, router_b, tri):
    t = logits_t.shape[1]
    tr = ROUTE_TILE
    col = lambda i: (0, i)
    return pl.pallas_call(
        _route_kernel,
        grid=(t // tr,),
        in_specs=[pl.BlockSpec((N_EXP, tr), col),
                  pl.BlockSpec((N_EXP, 1), lambda i: (0, 0)),
                  pl.BlockSpec((tr, tr), lambda i: (0, 0))],
        out_specs=[pl.BlockSpec((TOP_K, tr), col), pl.BlockSpec((TOP_K, tr), col), pl.BlockSpec((TOP_K, tr), col),
                   pl.BlockSpec((N_EXP, 1), lambda i: (0, 0))],
        out_shape=[jax.ShapeDtypeStruct((TOP_K, t), i32), jax.ShapeDtypeStruct((TOP_K, t), f32),
                   jax.ShapeDtypeStruct((TOP_K, t), i32), jax.ShapeDtypeStruct((N_EXP, 1), f32)],
        scratch_shapes=[pltpu.VMEM((N_EXP, 1), f32)],
        compiler_params=_params(("arbitrary",)),
        name="route",
    )(logits_t, router_b, tri)


def _dest_kernel(e_ref, r_ref, off_ref, d_ref):
    tr = e_ref.shape[1]
    rowi = lax.broadcasted_iota(i32, (N_EXP, tr), 0)
    off = off_ref[...]
    e = e_ref[...]
    rows = [jnp.sum(jnp.where(rowi == e[k:k + 1], off, 0.0), axis=0, keepdims=True) for k in range(TOP_K)]
    d_ref[...] = jnp.concatenate(rows, axis=0).astype(i32) + r_ref[...]


def _dest_call(eidx, rank, pad_off):
    t = eidx.shape[1]
    tr = 512
    col = lambda i: (0, i)
    return pl.pallas_call(
        _dest_kernel,
        grid=(t // tr,),
        in_specs=[pl.BlockSpec((TOP_K, tr), col), pl.BlockSpec((TOP_K, tr), col),
                  pl.BlockSpec((N_EXP, 1), lambda i: (0, 0))],
        out_specs=pl.BlockSpec((TOP_K, tr), col),
        out_shape=jax.ShapeDtypeStruct((TOP_K, t), i32),
        compiler_params=_params(("parallel",)),
        name="dest",
    )(eidx, rank, pad_off)


DISPATCH_TILE = 512
_PAD_PIECES = (128, 64, 32, 16, 8, 4, 2, 1)


DISPATCH_SLOTS = 3


def _dispatch_kernel(cnt_ref, off_ref, nbt_ref, dest_hbm, h_hbm, xs_hbm, idx_ref, hbuf, zero_ref, sem_idx, sem_tile,
                     sem_row, sem_z):
    i = pl.program_id(0)
    nsteps = pl.num_programs(0)
    n = idx_ref.shape[0] // 2
    ts = n // TOP_K
    trows = ts * PK_CHUNKS
    n_blk = xs_hbm.shape[0] // (MOE_BLK * PK_CHUNKS)

    def idx_copy(step):
        return pltpu.make_async_copy(dest_hbm.at[pl.ds(pl.multiple_of(step * n, n), n)],
                                     idx_ref.at[pl.ds(pl.multiple_of((step & 1) * n, n), n)], sem_idx.at[step & 1])

    def tile_copy(step):
        slot = lax.rem(step, DISPATCH_SLOTS)
        return pltpu.make_async_copy(h_hbm.at[pl.ds(pl.multiple_of(step * trows, trows), trows), :], hbuf.at[slot],
                                     sem_tile.at[slot])

    def rows_wait(step):
        pltpu.make_async_copy(xs_hbm.at[pl.ds(0, n * PK_CHUNKS), :], xs_hbm.at[pl.ds(0, n * PK_CHUNKS), :],
                              sem_row.at[lax.rem(step, DISPATCH_SLOTS)]).wait()

    def pad_copy(start_slot, p):
        return pltpu.make_async_copy(zero_ref.at[pl.ds(0, p * PK_CHUNKS), :],
                                     xs_hbm.at[pl.ds(start_slot * PK_CHUNKS, p * PK_CHUNKS), :], sem_z)

    def blk_copy(blk):
        return pltpu.make_async_copy(zero_ref, xs_hbm.at[pl.ds(blk * (MOE_BLK * PK_CHUNKS), MOE_BLK * PK_CHUNKS), :],
                                     sem_z)

    def for_each_pad_piece(fn):
        def per_expert(e, carry):
            c = cnt_ref[e]
            npad = ((c + (MOE_BLK - 1)) & (-MOE_BLK)) - c
            slot = off_ref[e] + c
            for p in _PAD_PIECES:
                @pl.when((npad & p) != 0)
                def _():
                    fn(pad_copy(slot, p))
                slot = slot + (npad & p)
            return carry

        lax.fori_loop(0, N_EXP, per_expert, 0)

    @pl.when(i == 0)
    def _():
        idx_copy(0).start()
        tile_copy(0).start()
        zero_ref[...] = jnp.zeros_like(zero_ref)
        for_each_pad_piece(lambda cp: cp.start())
        lax.fori_loop(nbt_ref[0], n_blk, lambda b, c: (blk_copy(b).start(), c)[1], 0)

    @pl.when(i >= DISPATCH_SLOTS - 1)
    def _():
        rows_wait(i - (DISPATCH_SLOTS - 1))

    @pl.when(i + 1 < nsteps)
    def _():
        idx_copy(i + 1).start()
        tile_copy(i + 1).start()

    idx_copy(i).wait()
    tile_copy(i).wait()
    sl = i & 1
    slot = lax.rem(i, DISPATCH_SLOTS)
    hb = hbuf.at[slot]

    def body(t2, carry):
        base = sl * n + t2 * (2 * TOP_K)
        ds = [idx_ref[base + j] for j in range(2 * TOP_K)]
        for j in range(2 * TOP_K):
            t = t2 * 2 + j // TOP_K
            pltpu.make_async_copy(hb.at[pl.ds(t * PK_CHUNKS, PK_CHUNKS), :],
                                  xs_hbm.at[pl.ds(ds[j] * PK_CHUNKS, PK_CHUNKS), :],
                                  sem_row.at[slot]).start(priority=j % 2)
        return carry

    lax.fori_loop(0, ts // 2, body, 0)

    @pl.when(i == nsteps - 1)
    def _():
        for back in range(DISPATCH_SLOTS - 2, -1, -1):
            @pl.when(i >= back)
            def _():
                rows_wait(i - back)

        for_each_pad_piece(lambda cp: cp.wait())
        lax.fori_loop(nbt_ref[0], n_blk, lambda b, c: (blk_copy(b).wait(), c)[1], 0)


def _dispatch_call(cnt, pad_off, nb_total, dest_flat, h2p, n_slots):
    t = h2p.shape[0] // PK_CHUNKS
    ts = DISPATCH_TILE
    gs = pltpu.PrefetchScalarGridSpec(
        num_scalar_prefetch=3,
        grid=(t // ts,),
        in_specs=[pl.BlockSpec(memory_space=pl.ANY), pl.BlockSpec(memory_space=pl.ANY)],
        out_specs=pl.BlockSpec(memory_space=pl.ANY),
        scratch_shapes=[pltpu.SMEM((2 * ts * TOP_K,), i32), pltpu.VMEM((DISPATCH_SLOTS, ts * PK_CHUNKS, 128), u32),
                        pltpu.VMEM((MOE_BLK * PK_CHUNKS, 128), u32),
                        pltpu.SemaphoreType.DMA((2,)), pltpu.SemaphoreType.DMA((DISPATCH_SLOTS,)),
                        pltpu.SemaphoreType.DMA((DISPATCH_SLOTS,)), pltpu.SemaphoreType.DMA(())],
    )
    return pl.pallas_call(
        _dispatch_kernel,
        grid_spec=gs,
        out_shape=jax.ShapeDtypeStruct((n_slots * PK_CHUNKS, 128), u32),
        compiler_params=_params(("arbitrary",)),
        name="dispatch",
    )(cnt, pad_off, nb_total, dest_flat, h2p)


GMLP_RING = 8


def _gmlp_kernel(nbe_ref, boff_ref, nbt_ref, w1_ref, w3_ref, w2_ref, xs_hbm, y_hbm, xbuf, ybuf, w13_s, w2_s,
                 sem_in, sem_out, sem_z):
    e = pl.program_id(0)
    nb = nbe_ref[e]
    b0 = boff_ref[e]
    total = nbt_ref[0]
    n_blk = y_hbm.shape[0] // (MOE_BLK * PK_CHUNKS)
    xrows = MOE_BLK * PK_CHUNKS
    yrows = MOE_BLK * PK_CHUNKS
    ring = GMLP_RING

    def in_copy(b):
        sl = b & (ring - 1)
        return pltpu.make_async_copy(xs_hbm.at[pl.ds(pl.multiple_of(b * xrows, xrows), xrows), :], xbuf.at[sl],
                                     sem_in.at[sl])

    def out_copy(b):
        sl = b & (ring - 1)
        return pltpu.make_async_copy(ybuf.at[sl], y_hbm.at[pl.ds(pl.multiple_of(b * yrows, yrows), yrows), :],
                                     sem_out.at[sl])

    def zero_copy(b):
        return pltpu.make_async_copy(ybuf.at[0], y_hbm.at[pl.ds(pl.multiple_of(b * yrows, yrows), yrows), :], sem_z)

    ahead = ring - 2

    @pl.when(e == 0)
    def _():
        for b in range(ahead):
            @pl.when(b < total)
            def _():
                in_copy(b).start()

    @pl.when(nb > 0)
    def _():
        w13_s[:, 0:EXP_D] = w1_ref[0].astype(bf16)
        w13_s[:, EXP_D:2 * EXP_D] = w3_ref[0].astype(bf16)
        w2_s[...] = w2_ref[0].astype(bf16)

    def process(b, m):
        for q in range(m):
            in_copy(b + q).wait()
        for q in range(m):
            @pl.when(b + q + ahead < total)
            def _():
                in_copy(b + q + ahead).start()

            @pl.when(b + q >= ring)
            def _():
                out_copy(b + q - ring).wait()

        cols = []
        for c in range(PK_CHUNKS):
            pieces = []
            for q in range(m):
                xb = xbuf.at[(b + q) & (ring - 1)]
                pieces += [xb[pl.ds(8 * PK_CHUNKS * g + c, 8, stride=PK_CHUNKS), :] for g in range(MOE_BLK // 8)]
            cols.append(jnp.concatenate(pieces, axis=0))
        word = jnp.concatenate(cols, axis=1)
        xlo = lax.bitcast_convert_type(lax.shift_left(word, jnp.uint32(16)), f32).astype(bf16)
        xhi = lax.bitcast_convert_type(word & jnp.uint32(0xFFFF0000), f32).astype(bf16)
        h = _dot(xlo, w13_s[0:512, :]) + _dot(xhi, w13_s[512:1024, :])
        hid = (_silu(h[:, 0:EXP_D]) * h[:, EXP_D:2 * EXP_D]).astype(bf16)
        y = _dot(hid, w2_s[...])
        ylo = lax.shift_right_logical(lax.bitcast_convert_type(y[:, 0:512].astype(bf16).astype(f32), u32),
                                      jnp.uint32(16))
        yhi = lax.bitcast_convert_type(y[:, 512:1024].astype(bf16).astype(f32), u32) & jnp.uint32(0xFFFF0000)
        yw = ylo | yhi
        for q in range(m):
            yb = ybuf.at[(b + q) & (ring - 1)]
            for g in range(MOE_BLK // 8):
                r0 = q * MOE_BLK + 8 * g
                for c in range(PK_CHUNKS):
                    yb[pl.ds(8 * PK_CHUNKS * g + c, 8, stride=PK_CHUNKS), :] = yw[r0:r0 + 8, 128 * c:128 * c + 128]
            out_copy(b + q).start()

    def pair(j, carry):
        process(b0 + 2 * j, 2)
        return carry

    lax.fori_loop(0, jnp.right_shift(nb, 1), pair, 0)

    @pl.when((nb & 1) == 1)
    def _():
        process(b0 + nb - 1, 1)

    @pl.when(e == pl.num_programs(0) - 1)
    def _():
        for back in range(ring, 0, -1):
            @pl.when(total >= back)
            def _():
                out_copy(total - back).wait()

        ybuf[0] = jnp.zeros(ybuf.shape[1:], u32)
        lax.fori_loop(total, n_blk, lambda b, c: (zero_copy(b).start(), c)[1], 0)
        lax.fori_loop(total, n_blk, lambda b, c: (zero_copy(b).wait(), c)[1], 0)


def _gmlp_call(layer, nblk_e, blk_off, nb_total, xs, w1, w3, w2):
    n_slots = xs.shape[0] // PK_CHUNKS
    wsel = lambda e, *_: (layer, e, 0, 0)
    gs = pltpu.PrefetchScalarGridSpec(
        num_scalar_prefetch=3,
        grid=(N_EXP,),
        in_specs=[pl.BlockSpec((None, 1, D, EXP_D), wsel), pl.BlockSpec((None, 1, D, EXP_D), wsel),
                  pl.BlockSpec((None, 1, EXP_D, D), wsel), pl.BlockSpec(memory_space=pl.ANY)],
        out_specs=pl.BlockSpec(memory_space=pl.ANY),
        scratch_shapes=[pltpu.VMEM((GMLP_RING, MOE_BLK * PK_CHUNKS, 128), u32),
                        pltpu.VMEM((GMLP_RING, MOE_BLK * PK_CHUNKS, 128), u32),
                        pltpu.VMEM((D, 2 * EXP_D), bf16), pltpu.VMEM((EXP_D, D), bf16),
                        pltpu.SemaphoreType.DMA((GMLP_RING,)), pltpu.SemaphoreType.DMA((GMLP_RING,)),
                        pltpu.SemaphoreType.DMA(())],
    )
    return pl.pallas_call(
        _gmlp_kernel,
        grid_spec=gs,
        out_shape=jax.ShapeDtypeStruct((n_slots * PK_CHUNKS, 128), u32),
        compiler_params=_params(("arbitrary",)),
        name="gmlp",
    )(nblk_e, blk_off, nb_total, w1, w3, w2, xs)


COMBINE_TILE = 128


def _combine_kernel(final, fuse_next, *refs):
    if fuse_next:
        (dest_hbm, y_hbm, x_ref, mod_ref, sh_ref, w_ref, gf_ref, nm_ref, ng_ref, nw_ref, nb_ref, o_ref, z_ref,
         idx_ref, buf_ref, sem_idx, sem_row) = refs
    else:
        dest_hbm, y_hbm, x_ref, mod_ref, sh_ref, w_ref, gf_ref, o_ref, idx_ref, buf_ref, sem_idx, sem_row = refs
    tm = x_ref.shape[1]
    n = tm * TOP_K
    s = pl.program_id(0) * pl.num_programs(1) + pl.program_id(1)
    nsteps = pl.num_programs(0) * pl.num_programs(1)
    last = s == nsteps - 1
    nxt = jnp.minimum(s + 1, nsteps - 1)

    def idx_copy(step):
        return pltpu.make_async_copy(dest_hbm.at[pl.ds(pl.multiple_of(step * n, n), n)],
                                     idx_ref.at[pl.ds(pl.multiple_of((step & 1) * n, n), n)], sem_idx.at[step & 1])

    def row_copy(d, k, t, slot, prio):
        return pltpu.make_async_copy(y_hbm.at[pl.ds(d * PK_CHUNKS, PK_CHUNKS), :],
                                     buf_ref.at[slot, pl.ds((k * tm + t) * PK_CHUNKS, PK_CHUNKS), :],
                                     sem_row.at[slot]).start(priority=prio)

    def rows_wait(slot):
        pltpu.make_async_copy(y_hbm.at[pl.ds(0, n * PK_CHUNKS), :], buf_ref.at[slot], sem_row.at[slot]).wait()

    @pl.when(s == 0)
    def _():
        idx_copy(0).start()
        idx_copy(0).wait()

        def body(t2, carry):
            ds = [idx_ref[t2 * (2 * TOP_K) + j] for j in range(2 * TOP_K)]
            for j in range(2 * TOP_K):
                row_copy(ds[j], j % TOP_K, t2 * 2 + j // TOP_K, 0, j % 2)
            return carry

        lax.fori_loop(0, tm // 2, body, 0)

        @pl.when(nsteps > 1)
        def _():
            idx_copy(1).start()

    @pl.when(s + 1 < nsteps)
    def _():
        idx_copy(s + 1).wait()

    @pl.when(s + 2 < nsteps)
    def _():
        idx_copy(s + 2).start()

    sl = s & 1
    nsl = 1 - sl
    rows_wait(sl)
    bs = buf_ref.at[sl]
    m = mod_ref[0]
    gate = m[5:6]
    nbase = (nxt & 1) * n
    himask = jnp.uint32(0xFFFF0000)

    def group(g, carry):
        r0 = pl.multiple_of(g * 8, 8)
        ds = [idx_ref[nbase + r0 * TOP_K + j] for j in range(8 * TOP_K)]
        wg = w_ref[pl.ds(r0, 8), :]
        lo = [None] * PK_CHUNKS
        hi = [None] * PK_CHUNKS
        for k in range(TOP_K):
            wk = jnp.broadcast_to(wg[:, k:k + 1], (8, 128))
            for c in range(PK_CHUNKS):
                word = bs[pl.ds((k * tm + r0) * PK_CHUNKS + c, 8, stride=PK_CHUNKS), :]
                plo = wk * lax.bitcast_convert_type(lax.shift_left(word, jnp.uint32(16)), f32)
                phi = wk * lax.bitcast_convert_type(word & himask, f32)
                lo[c] = plo if k == 0 else lo[c] + plo
                hi[c] = phi if k == 0 else hi[c] + phi
        routed = jnp.concatenate(lo + hi, axis=1)
        o_ref[0, pl.ds(r0, 8), :] = x_ref[0, pl.ds(r0, 8), :] + gate * (routed + sh_ref[pl.ds(r0, 8), :])
        for j in range(8 * TOP_K):
            row_copy(ds[j], j % TOP_K, r0 + j // TOP_K, nsl, j % 2)
        return carry

    lax.fori_loop(0, tm // 8, group, 0)

    @pl.when(last)
    def _():
        rows_wait(nsl)

    if final:
        o_ref[0] = _rms(o_ref[0], gf_ref[...])
    if fuse_next:
        nm = nm_ref[0]
        h = _rms_mod(o_ref[0], ng_ref[...], nm[0:1], nm[1:2]).astype(bf16)
        z = _dot(h, nw_ref[...]) + nb_ref[...]
        z_ref[0] = z[:, 0:D] * _sigmoid(z[:, D:2 * D])


def _combine_call(dest_flat, y, x, mod, shared, w_tok, g_final, final, nxt=None):
    b, s, _ = x.shape
    tm = COMBINE_TILE
    nt = s // tm
    flat = lambda bb, i: (bb * nt + i, 0)
    row = lambda bb, i: (bb, i, 0)
    vec = lambda bb, i: (0, 0)
    in_specs = [pl.BlockSpec(memory_space=pl.ANY), pl.BlockSpec(memory_space=pl.ANY),
                pl.BlockSpec((1, tm, D), row),
                pl.BlockSpec((1, 6, D), lambda bb, i: (bb, 0, 0)),
                pl.BlockSpec((tm, D), flat),
                pl.BlockSpec((tm, TOP_K), flat),
                pl.BlockSpec((1, D), vec)]
    out_specs = pl.BlockSpec((1, tm, D), row)
    out_shape = jax.ShapeDtypeStruct((b, s, D), f32)
    args = [dest_flat, y, x, mod, shared, w_tok, g_final]
    if nxt is not None:
        in_specs += [pl.BlockSpec((1, 6, D), lambda bb, i: (bb, 0, 0)), pl.BlockSpec((1, D), vec),
                     pl.BlockSpec((D, 2 * D), vec), pl.BlockSpec((1, 2 * D), vec)]
        out_specs = [out_specs, pl.BlockSpec((1, tm, D), row)]
        out_shape = [out_shape, jax.ShapeDtypeStruct((b, s, D), f32)]
        args += list(nxt)
    return pl.pallas_call(
        functools.partial(_combine_kernel, final, nxt is not None),
        grid=(b, nt),
        in_specs=in_specs,
        out_specs=out_specs,
        out_shape=out_shape,
        scratch_shapes=[pltpu.SMEM((2 * tm * TOP_K,), i32), pltpu.VMEM((2, TOP_K * tm * PK_CHUNKS, 128), u32),
                        pltpu.SemaphoreType.DMA((2,)), pltpu.SemaphoreType.DMA((2,))],
        compiler_params=_params(("arbitrary", "arbitrary")),
        name="combine",
    )(*args)


def _moe_layer(layer, x1, mod, norm_g, router_w, router_b, w1, w3, w2, sw1, sw3, sw2, g_final, final, nxt):
    b, s, _ = x1.shape
    t = b * s
    rwt = router_w.T
    rwh = rwt.astype(bf16)
    rwl = (rwt - rwh.astype(f32)).astype(bf16)
    s13 = jnp.concatenate([sw1, sw3], axis=1).astype(bf16)
    h2p, logits_t, shared = _ffn_pre_call(x1, mod, norm_g.reshape(1, D), rwh, rwl, s13, sw2.astype(bf16))
    tri = (lax.broadcasted_iota(i32, (ROUTE_TILE, ROUTE_TILE), 0)
           < lax.broadcasted_iota(i32, (ROUTE_TILE, ROUTE_TILE), 1)).astype(bf16)
    eidx, w_t, rank, counts = _route_call(logits_t, router_b.reshape(N_EXP, 1).astype(f32), tri)
    cnt = counts.reshape(N_EXP).astype(i32)
    nblk_e = (cnt + MOE_BLK - 1) // MOE_BLK
    blk_ends = jnp.cumsum(nblk_e)
    blk_off = blk_ends - nblk_e
    pad_off = blk_off * MOE_BLK
    nb_total = blk_ends[-1:].astype(i32)
    n_blk = t * TOP_K // MOE_BLK + N_EXP
    dest = _dest_call(eidx, rank, pad_off.astype(f32).reshape(N_EXP, 1))
    dest_flat = dest.T.reshape(t * TOP_K)
    xs = _dispatch_call(cnt, pad_off, nb_total, dest_flat, h2p, n_blk * MOE_BLK)
    y = _gmlp_call(layer, nblk_e, blk_off, nb_total, xs, w1, w3, w2)
    return _combine_call(dest_flat, y, x1, mod, shared, w_t.T, g_final.reshape(1, D), final, nxt)


def _rot_cols(w):
    d, n = w.shape
    w4 = w.reshape(d, n // 32, 2, 16)
    return jnp.stack([-w4[:, :, 1], w4[:, :, 0]], axis=2).reshape(d, n)


def _rope_tables(s):
    rows = s // GRID_W
    row = jnp.repeat(jnp.arange(rows, dtype=f32), GRID_W)
    col = jnp.tile(jnp.arange(GRID_W, dtype=f32), rows)
    n_freq = HEAD_DIM // 4
    inv = ROPE_BASE ** (-jnp.arange(n_freq, dtype=f32) / n_freq)
    ang_r = row[:, None] * inv
    ang_c = col[:, None] * inv
    cos = jnp.concatenate([jnp.cos(ang_r)] * 2 + [jnp.cos(ang_c)] * 2, axis=1)
    sin = jnp.concatenate([jnp.sin(ang_r)] * 2 + [jnp.sin(ang_c)] * 2, axis=1)
    return jnp.tile(cos, (1, 2)), jnp.tile(sin, (1, 2))


def _block_diag(w):
    h, dh, _ = w.shape
    eye = jnp.eye(h, dtype=w.dtype)
    return (eye[:, None, :, None] * w[:, :, None, :]).reshape(h * dh, h * dh)


def _even_layer_mixer(x, ctx, mod, norm_g, w_in, w_out, conv_w, conv_b, w_r, b_r, w_i, b_i, lam, sink):
    b, s, _ = x.shape
    r0, r1, r2 = LRU_W, 2 * LRU_W, 2 * LRU_W + Q_W
    wq = w_in[:, r1:r2].reshape(D, 2, 4, HEAD_DIM).transpose(0, 2, 1, 3).reshape(D, Q_W)
    wk = w_in[:, r2:r2 + KV_W]
    w_ext = jnp.concatenate([w_in[:, :r1], wq, w_in[:, r2:], _rot_cols(wq), _rot_cols(wk)], axis=1).astype(bf16)
    w_ctx = jnp.concatenate([w_in[:, :r0], w_in[:, r2:]], axis=1).astype(bf16)
    cos, sin = _rope_tables(s)
    g = norm_g.reshape(1, D)
    u, gt, q, k, v = _proj_in_call(x, mod, g, w_ext, cos, sin)
    uc, kx, vx = _proj_ctx_call(ctx, mod, g, w_ctx)
    n_lg = LRU_W // LRU_LANES
    hpg = LRU_LANES // HEAD_DIM

    def lane_groups(w):
        return jnp.stack([jnp.stack([_block_diag(w[d, g * hpg:(g + 1) * hpg]) for g in range(n_lg)]) for d in range(2)])

    wg = jnp.concatenate([lane_groups(w_r), lane_groups(w_i)], axis=-1).astype(bf16)
    bg = jnp.concatenate([b_r.reshape(2, n_lg, 1, LRU_LANES), b_i.reshape(2, n_lg, 1, LRU_LANES)], axis=-1)
    rec = _rglru_call(u, uc, conv_w, conv_b.reshape(1, LRU_W), wg, bg, lam.reshape(2, n_lg, 1, LRU_LANES))
    att = _attn_call(sink, q, k, v, kx, vx)
    w_att = w_out[LRU_W:].reshape(2, 4, HEAD_DIM, D).transpose(1, 0, 2, 3).reshape(Q_W, D).astype(bf16)
    return _mix_out_call(x, mod, gt, rec, att, w_out[:LRU_W].astype(bf16), w_att)


def kernel(x, c, ctx, c_ctx, mod_w, mod_b, norm_mix_g, norm_ffn_g, final_norm_g, ab_w_in, ab_w_out, lru_conv_w,
           lru_conv_b, lru_wr, lru_br, lru_wi, lru_bi, lru_lambda, attn_sink, cm_w_in, cm_b_in, cm_dw_w, cm_dw_b,
           cm_ln_g, cm_ln_b, cm_w_out, cm_b_out, router_w, router_b, exp_w1, exp_w3, exp_w2, shared_w1, shared_w3,
           shared_w2):
    bsz = x.shape[0]
    depth = mod_w.shape[0]
    assert bsz + 1 <= MOD_ROWS - 7
    cc = jnp.zeros((MOD_ROWS, D), f32).at[:bsz].set(c).at[MOD_ROWS - 8].set(c_ctx)
    mod_all = _mod_call(cc, mod_w, mod_b).reshape(depth, MOD_ROWS, 6, D)
    zg = None
    for l in range(depth):
        mod = mod_all[l]
        last = l == depth - 1
        nxt = None
        if not last and (l + 1) % 2 == 1:
            o1 = (l + 1) // 2
            nxt = (mod_all[l + 1], norm_mix_g[l + 1].reshape(1, D), cm_w_in[o1].astype(bf16),
                   cm_b_in[o1].reshape(1, 2 * D))
        if l % 2 == 0:
            e = l // 2
            assert depth <= 2
            x1 = _even_layer_mixer(x, ctx, mod, norm_mix_g[l], ab_w_in[e], ab_w_out[e], lru_conv_w[e], lru_conv_b[e],
                                   lru_wr[e], lru_br[e], lru_wi[e], lru_bi[e], lru_lambda[e], attn_sink[e])
        else:
            o = l // 2
            if zg is None:
                zg = _conf_in_call(x, mod, norm_mix_g[l].reshape(1, D), cm_w_in[o].astype(bf16),
                                   cm_b_in[o].reshape(1, 2 * D))
            dw = jnp.concatenate([cm_dw_w[o], jnp.zeros((1, D), f32)], axis=0)
            x1 = _conf_out_call(x, mod, zg, dw, cm_dw_b[o].reshape(1, D), cm_ln_g[o].reshape(1, D),
                                cm_ln_b[o].reshape(1, D), cm_w_out[o].astype(bf16), cm_b_out[o].reshape(1, D))
        out = _moe_layer(l, x1, mod, norm_ffn_g[l], router_w[l], router_b[l], exp_w1, exp_w3, exp_w2,
                         shared_w1[l], shared_w3[l], shared_w2[l], final_norm_g, last, nxt)
        x, zg = out if nxt is not None else (out, None)
    return x
```

```python
import functools

import jax
import jax.numpy as jnp
from jax import lax
from jax.experimental import pallas as pl
from jax.experimental.pallas import tpu as pltpu

f32 = jnp.float32
bf16 = jnp.bfloat16
i32 = jnp.int32
u32 = jnp.uint32

D = 1024
EPS = 1e-6
LRU_W = 512
LRU_C = 8.0
N_HEADS = 8
HEAD_DIM = 64
GRID_W = 64
ROPE_BASE = 10000.0
Q_W = 512
KV_W = 128
ATT_BLK = 128
CONV_K = 31
N_EXP = 256
TOP_K = 8
N_GRP = 8
TOPK_GRP = 4
GRP_SZ = N_EXP // N_GRP
EXP_D = 256
ROUTED_SCALE = 2.5
MOE_BLK = 256
PK_CHUNKS = D // 2 // 128

VMEM_LIMIT_V7X = 56 * 1024 * 1024
MOD_ROWS = 24

_NT = (((1,), (1,)), ((), ()))


def _params(sem):
    return pltpu.CompilerParams(dimension_semantics=sem, vmem_limit_bytes=VMEM_LIMIT_V7X)


def _sigmoid(x):
    return 1.0 / (1.0 + jnp.exp(-x))


def _silu(x):
    return x * _sigmoid(x)


def _gelu_tanh(x):
    return 0.5 * x * (1.0 + jnp.tanh(0.7978845608028654 * (x + 0.044715 * (x * x * x))))


def _rms(x, g):
    return x * lax.rsqrt(jnp.mean(x * x, axis=-1, keepdims=True) + EPS) * g


def _rms_mod(x, g, shift, scale):
    return _rms(x, g) * (1.0 + scale) + shift


def _dot(a, b):
    return jnp.dot(a, b, preferred_element_type=f32)


def _mod_kernel(c_ref, w_ref, b_ref, o_ref):
    a = _silu(c_ref[...]).astype(bf16)
    o_ref[0] = _dot(a, w_ref[0].astype(bf16)) + b_ref[0]


def _mod_call(cc, mod_w, mod_b):
    depth, _, n = mod_w.shape
    tn = 1536
    return pl.pallas_call(
        _mod_kernel,
        grid=(depth, n // tn),
        in_specs=[pl.BlockSpec((MOD_ROWS, D), lambda l, j: (0, 0)),
                  pl.BlockSpec((1, D, tn), lambda l, j: (l, 0, j)),
                  pl.BlockSpec((1, 1, tn), lambda l, j: (l, 0, j))],
        out_specs=pl.BlockSpec((1, MOD_ROWS, tn), lambda l, j: (l, 0, j)),
        out_shape=jax.ShapeDtypeStruct((depth, MOD_ROWS, n), f32),
        compiler_params=_params(("parallel", "parallel")),
        name="mod",
    )(cc, mod_w, mod_b.reshape(depth, 1, n))


def _proj_in_kernel(x_ref, mod_ref, g_ref, w_ref, cos_ref, sin_ref, u_ref, gt_ref, q_ref, k_ref, v_ref):
    m = mod_ref[0]
    h = _rms_mod(x_ref[0], g_ref[...], m[0:1], m[1:2]).astype(bf16)
    p = _dot(h, w_ref[...])
    u_ref[0] = p[:, 0:512]
    gt_ref[0] = p[:, 512:1024]
    cos = cos_ref[...]
    sin = sin_ref[...]
    qs = []
    for j in range(4):
        qj = p[:, 1024 + j * 128:1152 + j * 128] * cos + p[:, 1792 + j * 128:1920 + j * 128] * sin
        qs.append(qj * (HEAD_DIM ** -0.5))
    q_ref[0] = jnp.concatenate(qs, axis=1).astype(bf16)
    k_ref[0] = (p[:, 1536:1664] * cos + p[:, 2304:2432] * sin).astype(bf16)
    v_ref[0] = p[:, 1664:1792].astype(bf16)


def _proj_in_call(x, mod, g, w_ext, cos, sin, tm=512):
    b, s, _ = x.shape
    nw = w_ext.shape[1]
    row = lambda bb, i: (bb, i, 0)
    return pl.pallas_call(
        _proj_in_kernel,
        grid=(b, s // tm),
        in_specs=[pl.BlockSpec((1, tm, D), row),
                  pl.BlockSpec((1, 6, D), lambda bb, i: (bb, 0, 0)),
                  pl.BlockSpec((1, D), lambda bb, i: (0, 0)),
                  pl.BlockSpec((D, nw), lambda bb, i: (0, 0)),
                  pl.BlockSpec((tm, 128), lambda bb, i: (i, 0)),
                  pl.BlockSpec((tm, 128), lambda bb, i: (i, 0))],
        out_specs=[pl.BlockSpec((1, tm, LRU_W), row), pl.BlockSpec((1, tm, LRU_W), row),
                   pl.BlockSpec((1, tm, Q_W), row), pl.BlockSpec((1, tm, KV_W), row),
                   pl.BlockSpec((1, tm, KV_W), row)],
        out_shape=[jax.ShapeDtypeStruct((b, s, LRU_W), f32), jax.ShapeDtypeStruct((b, s, LRU_W), f32),
                   jax.ShapeDtypeStruct((b, s, Q_W), bf16), jax.ShapeDtypeStruct((b, s, KV_W), bf16),
                   jax.ShapeDtypeStruct((b, s, KV_W), bf16)],
        compiler_params=_params(("parallel", "parallel")),
        name="proj_in",
    )(x, mod, g, w_ext, cos, sin)


def _proj_ctx_kernel(x_ref, mod_ref, g_ref, w_ref, u_ref, k_ref, v_ref):
    m = mod_ref[0]
    h = _rms_mod(x_ref[0], g_ref[...], m[0:1], m[1:2]).astype(bf16)
    p = _dot(h, w_ref[...])
    u_ref[0] = p[:, 0:512]
    k_ref[0] = p[:, 512:640].astype(bf16)
    v_ref[0] = p[:, 640:768].astype(bf16)


def _proj_ctx_call(ctx, mod, g, w_ctx):
    b, n_ctx, _ = ctx.shape
    row = lambda bb: (bb, 0, 0)
    return pl.pallas_call(
        _proj_ctx_kernel,
        grid=(b,),
        in_specs=[pl.BlockSpec((1, n_ctx, D), row),
                  pl.BlockSpec((1, 6, D), lambda bb: (MOD_ROWS - 8, 0, 0)),
                  pl.BlockSpec((1, D), lambda bb: (0, 0)),
                  pl.BlockSpec((D, 768), lambda bb: (0, 0))],
        out_specs=[pl.BlockSpec((1, n_ctx, LRU_W), row), pl.BlockSpec((1, n_ctx, KV_W), row),
                   pl.BlockSpec((1, n_ctx, KV_W), row)],
        out_shape=[jax.ShapeDtypeStruct((b, n_ctx, LRU_W), f32), jax.ShapeDtypeStruct((b, n_ctx, KV_W), bf16),
                   jax.ShapeDtypeStruct((b, n_ctx, KV_W), bf16)],
        compiler_params=_params(("parallel",)),
        name="proj_ctx",
    )(ctx, mod, g, w_ctx)


LRU_CHUNK = 128
LRU_LANES = 512


def _rglru_kernel(u_ref, uc_ref, cw_ref, cb_ref, wg_ref, bg_ref, lam_ref, o_ref, pad_ref, cx_ref, cc_ref):
    s = u_ref.shape[1]
    n_ctx = uc_ref.shape[1]
    tc = LRU_CHUNK
    lw = LRU_LANES

    def conv_segment(src_ref, n, dst_ref):
        pad_ref[0:8] = jnp.zeros((8, lw), f32)
        pad_ref[8:8 + n] = src_ref[0]
        pad_ref[8 + n:16 + n] = jnp.zeros((8, lw), f32)
        for c in range(n // 256):
            acc = jnp.broadcast_to(cb_ref[...], (256, lw))
            for k in range(4):
                acc = acc + cw_ref[k:k + 1, :] * pad_ref[c * 256 + 6 + k:c * 256 + 6 + k + 256, :]
            dst_ref[c * 256:(c + 1) * 256] = acc

    conv_segment(uc_ref, n_ctx, cc_ref)
    conv_segment(u_ref, s, cx_ref)

    rowm = lax.broadcasted_iota(i32, (tc, lw), 0) & 7

    def scan_segment(src_ref, n, d, h0, write):
        lam = lam_ref[d, 0]
        sp = jnp.maximum(-lam, 0.0) + jnp.log(1.0 + jnp.exp(-jnp.abs(lam)))
        nch = n // tc

        def chunk(ci, h):
            c = ci if d == 0 else nch - 1 - ci
            t0 = pl.multiple_of(c * tc, tc)
            uc = src_ref[pl.ds(t0, tc), :]
            gates = _dot(uc.astype(bf16), wg_ref[d, 0]) + bg_ref[d, 0]
            r = _sigmoid(gates[:, 0:lw])
            ig = _sigmoid(gates[:, lw:2 * lw])
            log_a = (-LRU_C * sp) * r
            a = jnp.exp(log_a)
            bb = jnp.sqrt(-jnp.tanh(log_a) * (a * a + 1.0)) * (ig * uc)
            for sh in (1, 2, 4):
                if d == 0:
                    keep = rowm >= sh
                    a_sh = jnp.where(keep, pltpu.roll(a, sh, 0), 1.0)
                    b_sh = jnp.where(keep, pltpu.roll(bb, sh, 0), 0.0)
                else:
                    keep = rowm < 8 - sh
                    a_sh = jnp.where(keep, pltpu.roll(a, tc - sh, 0), 1.0)
                    b_sh = jnp.where(keep, pltpu.roll(bb, tc - sh, 0), 0.0)
                bb = a * b_sh + bb
                a = a * a_sh
            outs = [None] * (tc // 8)
            order = range(tc // 8) if d == 0 else range(tc // 8 - 1, -1, -1)
            for gi in order:
                hg = bb[gi * 8:(gi + 1) * 8] + a[gi * 8:(gi + 1) * 8] * h
                outs[gi] = hg
                h = hg[7:8] if d == 0 else hg[0:1]
            if write:
                hs = jnp.concatenate(outs, axis=0)
                if d == 0:
                    o_ref[0, pl.ds(t0, tc), :] = hs
                else:
                    o_ref[0, pl.ds(t0, tc), :] = o_ref[0, pl.ds(t0, tc), :] + hs
            return h

        return lax.fori_loop(0, nch, chunk, h0)

    for d in range(2):
        h = jnp.zeros((1, lw), f32)
        h = scan_segment(cc_ref, n_ctx, d, h, False)
        scan_segment(cx_ref, s, d, h, True)


def _rglru_call(u, uc, conv_w, conv_b, wg, bg, lam):
    b, s, _ = u.shape
    n_ctx = uc.shape[1]
    lw = LRU_LANES
    return pl.pallas_call(
        _rglru_kernel,
        grid=(b, LRU_W // lw),
        in_specs=[pl.BlockSpec((1, s, lw), lambda bb, g: (bb, 0, g)),
                  pl.BlockSpec((1, n_ctx, lw), lambda bb, g: (bb, 0, g)),
                  pl.BlockSpec((4, lw), lambda bb, g: (0, g)),
                  pl.BlockSpec((1, lw), lambda bb, g: (0, g)),
                  pl.BlockSpec((2, 1, lw, 2 * lw), lambda bb, g: (0, g, 0, 0)),
                  pl.BlockSpec((2, 1, 1, 2 * lw), lambda bb, g: (0, g, 0, 0)),
                  pl.BlockSpec((2, 1, 1, lw), lambda bb, g: (0, g, 0, 0))],
        out_specs=pl.BlockSpec((1, s, lw), lambda bb, g: (bb, 0, g)),
        out_shape=jax.ShapeDtypeStruct((b, s, LRU_W), f32),
        scratch_shapes=[pltpu.VMEM((s + 16, lw), f32), pltpu.VMEM((s, lw), f32), pltpu.VMEM((n_ctx, lw), f32)],
        compiler_params=_params(("parallel", "parallel")),
        name="rglru",
    )(u, uc, conv_w, conv_b, wg, bg, lam)


def _attn_kernel(sink_ref, q_ref, kp_ref, kc_ref, kn_ref, vp_ref, vc_ref, vn_ref, kx_ref, vx_ref, o_ref):
    n = pl.program_id(1)
    nb = pl.num_programs(1)
    blk = ATT_BLK
    q = q_ref[0]
    qall = jnp.concatenate([q[:, j * 128:(j + 1) * 128] for j in range(4)], axis=0)
    kw = jnp.concatenate([kp_ref[0], kc_ref[0], kn_ref[0]], axis=0)
    vw = jnp.concatenate([vp_ref[0], vc_ref[0], vn_ref[0]], axis=0)
    kx = kx_ref[0]
    vx = vx_ref[0]
    n_ctx = kx.shape[0]
    lo_w = lax.broadcasted_iota(i32, (3 * blk, 128), 1) < HEAD_DIM
    lo_x = lax.broadcasted_iota(i32, (n_ctx, 128), 1) < HEAD_DIM
    qi = lax.broadcasted_iota(i32, (4 * blk, 3 * blk), 0) & (blk - 1)
    kr = lax.broadcasted_iota(i32, (4 * blk, 3 * blk), 1) - blk
    lo = jnp.where(n > 0, -blk, 0)
    hi = jnp.where(n < nb - 1, 2 * blk, blk)
    dlt = kr - qi
    pen = jnp.where(dlt >= -blk, 0.0, -jnp.inf)
    pen = jnp.where(dlt <= blk, pen, -jnp.inf)
    pen = jnp.where(kr >= lo, pen, -jnp.inf)
    pen = jnp.where(kr < hi, pen, -jnp.inf)
    rb = lax.broadcasted_iota(i32, (4 * blk, 1), 0) // blk
    zero = jnp.zeros((), bf16)
    out = jnp.zeros((4 * blk, 128), f32)
    for half in range(2):
        sel_w = lo_w if half == 0 else jnp.logical_not(lo_w)
        sel_x = lo_x if half == 0 else jnp.logical_not(lo_x)
        s_w = lax.dot_general(qall, jnp.where(sel_w, kw, zero), _NT, preferred_element_type=f32) + pen
        s_c = lax.dot_general(qall, jnp.where(sel_x, kx, zero), _NT, preferred_element_type=f32)
        sk = jnp.where(rb == 0, sink_ref[4 * half],
                       jnp.where(rb == 1, sink_ref[4 * half + 1],
                                 jnp.where(rb == 2, sink_ref[4 * half + 2], sink_ref[4 * half + 3])))
        m = jnp.maximum(jnp.maximum(jnp.max(s_w, axis=1, keepdims=True), jnp.max(s_c, axis=1, keepdims=True)), sk)
        p_w = jnp.exp(s_w - m)
        p_c = jnp.exp(s_c - m)
        den = jnp.sum(p_w, axis=1, keepdims=True) + jnp.sum(p_c, axis=1, keepdims=True) + jnp.exp(sk - m)
        o = _dot(p_w.astype(bf16), jnp.where(sel_w, vw, zero)) + _dot(p_c.astype(bf16), jnp.where(sel_x, vx, zero))
        out = out + o / den
    o_ref[0] = jnp.concatenate([out[j * blk:(j + 1) * blk] for j in range(4)], axis=1).astype(bf16)


def _attn_call(sink, q, k, v, kx, vx):
    b, s, _ = q.shape
    n_ctx = kx.shape[1]
    nb = s // ATT_BLK
    cur = lambda bb, n: (bb, n, 0)
    prev = lambda bb, n: (bb, jnp.maximum(n - 1, 0), 0)
    nxt = lambda bb, n: (bb, jnp.minimum(n + 1, nb - 1), 0)
    kvb = (1, ATT_BLK, KV_W)
    return pl.pallas_call(
        _attn_kernel,
        grid=(b, nb),
        in_specs=[pl.BlockSpec(memory_space=pltpu.SMEM),
                  pl.BlockSpec((1, ATT_BLK, Q_W), cur),
                  pl.BlockSpec(kvb, prev), pl.BlockSpec(kvb, cur), pl.BlockSpec(kvb, nxt),
                  pl.BlockSpec(kvb, prev), pl.BlockSpec(kvb, cur), pl.BlockSpec(kvb, nxt),
                  pl.BlockSpec((1, n_ctx, KV_W), lambda bb, n: (bb, 0, 0)),
                  pl.BlockSpec((1, n_ctx, KV_W), lambda bb, n: (bb, 0, 0))],
        out_specs=pl.BlockSpec((1, ATT_BLK, Q_W), cur),
        out_shape=jax.ShapeDtypeStruct((b, s, Q_W), bf16),
        compiler_params=_params(("parallel", "parallel")),
        name="attn",
    )(sink, q, k, k, k, v, v, v, kx, vx)


def _mix_out_kernel(x_ref, mod_ref, gt_ref, rec_ref, att_ref, wr_ref, wa_ref, o_ref):
    m = mod_ref[0]
    a = (_gelu_tanh(gt_ref[0]) * rec_ref[0]).astype(bf16)
    y = _dot(a, wr_ref[...]) + _dot(att_ref[0], wa_ref[...])
    o_ref[0] = x_ref[0] + m[2:3] * y


def _mix_out_call(x, mod, gt, rec, att, w_rec, w_att, tm=512):
    b, s, _ = x.shape
    row = lambda bb, i: (bb, i, 0)
    return pl.pallas_call(
        _mix_out_kernel,
        grid=(b, s // tm),
        in_specs=[pl.BlockSpec((1, tm, D), row),
                  pl.BlockSpec((1, 6, D), lambda bb, i: (bb, 0, 0)),
                  pl.BlockSpec((1, tm, LRU_W), row), pl.BlockSpec((1, tm, LRU_W), row),
                  pl.BlockSpec((1, tm, Q_W), row),
                  pl.BlockSpec((LRU_W, D), lambda bb, i: (0, 0)),
                  pl.BlockSpec((Q_W, D), lambda bb, i: (0, 0))],
        out_specs=pl.BlockSpec((1, tm, D), row),
        out_shape=jax.ShapeDtypeStruct((b, s, D), f32),
        compiler_params=_params(("parallel", "parallel")),
        name="mix_out",
    )(x, mod, gt, rec, att, w_rec, w_att)


def _conf_in_kernel(x_ref, mod_ref, g_ref, w_ref, b_ref, o_ref):
    m = mod_ref[0]
    h = _rms_mod(x_ref[0], g_ref[...], m[0:1], m[1:2]).astype(bf16)
    z = _dot(h, w_ref[...]) + b_ref[...]
    o_ref[0] = z[:, 0:D] * _sigmoid(z[:, D:2 * D])


def _conf_in_call(x, mod, g, w, bias, tm=512):
    b, s, _ = x.shape
    row = lambda bb, i: (bb, i, 0)
    return pl.pallas_call(
        _conf_in_kernel,
        grid=(b, s // tm),
        in_specs=[pl.BlockSpec((1, tm, D), row),
                  pl.BlockSpec((1, 6, D), lambda bb, i: (bb, 0, 0)),
                  pl.BlockSpec((1, D), lambda bb, i: (0, 0)),
                  pl.BlockSpec((D, 2 * D), lambda bb, i: (0, 0)),
                  pl.BlockSpec((1, 2 * D), lambda bb, i: (0, 0))],
        out_specs=pl.BlockSpec((1, tm, D), row),
        out_shape=jax.ShapeDtypeStruct((b, s, D), f32),
        compiler_params=_params(("parallel", "parallel")),
        name="conf_in",
    )(x, mod, g, w, bias)


CONF_HALO = 16
CONF_ROWS = 64
CONF_LANE_PAD = 128


def _conf_out_kernel(x_ref, mod_ref, zc_ref, zp_ref, zn_ref, dw_ref, db_ref, lg_ref, lb_ref, w_ref, b_ref, o_ref,
                     pad_ref, sh_ref, cv_ref):
    i = pl.program_id(1)
    nt = pl.num_programs(1)
    tm = zc_ref.shape[1]
    zero = jnp.zeros((CONF_HALO, D), f32)
    pad_ref[0:CONF_HALO] = jnp.where(i > 0, zp_ref[0], zero)
    pad_ref[CONF_HALO:CONF_HALO + tm] = zc_ref[0]
    pad_ref[CONF_HALO + tm:2 * CONF_HALO + tm] = jnp.where(i < nt - 1, zn_ref[0], zero)
    for r in range(8):
        sh_ref[r, :, 0:D] = pad_ref[r:r + tm + 24, :]

    for lg in range(D // 128):
        l0 = lg * 128
        taps = [dw_ref[k:k + 1, l0:l0 + 128] for k in range(CONV_K)]
        bias = db_ref[:, l0:l0 + 128]

        def chunk(c, carry, l0=l0, taps=taps, bias=bias):
            t0 = pl.multiple_of(c * CONF_ROWS, CONF_ROWS)
            acc = jnp.broadcast_to(bias, (CONF_ROWS, 128))
            for r in range(8):
                win = sh_ref[r, pl.ds(t0, CONF_ROWS + 24), l0:l0 + 128]
                for a in range(4):
                    k = 8 * a + r - 1
                    if 0 <= k < CONV_K:
                        acc = acc + taps[k] * win[8 * a:8 * a + CONF_ROWS]
            cv_ref[pl.ds(t0, CONF_ROWS), l0:l0 + 128] = acc
            return carry

        lax.fori_loop(0, tm // CONF_ROWS, chunk, 0)
    z = cv_ref[...]
    mu = jnp.mean(z, axis=-1, keepdims=True)
    zc = z - mu
    var = jnp.mean(zc * zc, axis=-1, keepdims=True)
    zn = zc * lax.rsqrt(var + EPS) * lg_ref[...] + lb_ref[...]
    y = _dot(_silu(zn).astype(bf16), w_ref[...]) + b_ref[...]
    m = mod_ref[0]
    o_ref[0] = x_ref[0] + m[2:3] * y


def _conf_out_call(x, mod, zg, dw_w, dw_b, ln_g, ln_b, w_out, b_out, tm=256):
    b, s, _ = x.shape
    row = lambda bb, i: (bb, i, 0)
    hb = tm // CONF_HALO
    nh = s // CONF_HALO
    vec = lambda bb, i: (0, 0)
    return pl.pallas_call(
        _conf_out_kernel,
        grid=(b, s // tm),
        in_specs=[pl.BlockSpec((1, tm, D), row),
                  pl.BlockSpec((1, 6, D), lambda bb, i: (bb, 0, 0)),
                  pl.BlockSpec((1, tm, D), row),
                  pl.BlockSpec((1, CONF_HALO, D), lambda bb, i: (bb, jnp.maximum(i * hb - 1, 0), 0)),
                  pl.BlockSpec((1, CONF_HALO, D), lambda bb, i: (bb, jnp.minimum((i + 1) * hb, nh - 1), 0)),
                  pl.BlockSpec((CONV_K + 1, D), vec),
                  pl.BlockSpec((1, D), vec), pl.BlockSpec((1, D), vec), pl.BlockSpec((1, D), vec),
                  pl.BlockSpec((D, D), vec), pl.BlockSpec((1, D), vec)],
        out_specs=pl.BlockSpec((1, tm, D), row),
        out_shape=jax.ShapeDtypeStruct((b, s, D), f32),
        scratch_shapes=[pltpu.VMEM((tm + 2 * CONF_HALO, D), f32), pltpu.VMEM((8, tm + 24, D + CONF_LANE_PAD), f32),
                        pltpu.VMEM((tm, D), f32)],
        compiler_params=_params(("parallel", "parallel")),
        name="conf_out",
    )(x, mod, zg, zg, zg, dw_w, dw_b, ln_g, ln_b, w_out, b_out)


def _ffn_pre_kernel(x_ref, mod_ref, g_ref, rwh_ref, rwl_ref, hp_ref, lg_ref):
    m = mod_ref[0]
    h2 = _rms_mod(x_ref[0], g_ref[...], m[3:4], m[4:5])
    hb = h2.astype(bf16)
    hbf = hb.astype(f32)
    hl = (h2 - hbf).astype(bf16)
    lg_ref[...] = (lax.dot_general(rwh_ref[...], hb, _NT, preferred_element_type=f32)
                   + lax.dot_general(rwh_ref[...], hl, _NT, preferred_element_type=f32)
                   + lax.dot_general(rwl_ref[...], hb, _NT, preferred_element_type=f32))
    lo = lax.shift_right_logical(lax.bitcast_convert_type(hbf[:, 0:512], u32), jnp.uint32(16))
    hi = lax.bitcast_convert_type(hbf[:, 512:1024], u32) & jnp.uint32(0xFFFF0000)
    word = lo | hi
    for i in range(word.shape[0] // 8):
        for c in range(PK_CHUNKS):
            hp_ref[pl.ds(8 * PK_CHUNKS * i + c, 8, stride=PK_CHUNKS), :] = word[8 * i:8 * i + 8, 128 * c:128 * c + 128]


def _ffn_pre_call(x, mod, g, rwh, rwl, tm=512):
    b, s, _ = x.shape
    nt = s // tm
    t = b * s
    flat = lambda bb, i: (bb * nt + i, 0)
    vec = lambda bb, i: (0, 0)
    return pl.pallas_call(
        _ffn_pre_kernel,
        grid=(b, nt),
        in_specs=[pl.BlockSpec((1, tm, D), lambda bb, i: (bb, i, 0)),
                  pl.BlockSpec((1, 6, D), lambda bb, i: (bb, 0, 0)),
                  pl.BlockSpec((1, D), vec),
                  pl.BlockSpec((N_EXP, D), vec), pl.BlockSpec((N_EXP, D), vec)],
        out_specs=[pl.BlockSpec((tm * PK_CHUNKS, 128), flat),
                   pl.BlockSpec((N_EXP, tm), lambda bb, i: (0, bb * nt + i))],
        out_shape=[jax.ShapeDtypeStruct((t * PK_CHUNKS, 128), u32), jax.ShapeDtypeStruct((N_EXP, t), f32)],
        compiler_params=_params(("parallel", "parallel")),
        name="ffn_pre",
    )(x, mod, g, rwh, rwl)


ROUTE_TILE = 256


def _route_kernel(lg_ref, rb_ref, tri_ref, e_ref, w_ref, r_ref, c_ref, base_ref):
    i = pl.program_id(0)
    tr = lg_ref.shape[1]

    @pl.when(i == 0)
    def _():
        base_ref[...] = jnp.zeros_like(base_ref)

    scores = _sigmoid(lg_ref[...])
    biased = scores + rb_ref[...]
    neg = -jnp.inf
    rowf = lax.broadcasted_iota(i32, (N_EXP, tr), 0).astype(f32)
    r32 = lax.broadcasted_iota(i32, (GRP_SZ, tr), 0).astype(f32)
    gs = []
    for g in range(N_GRP):
        seg = biased[g * GRP_SZ:(g + 1) * GRP_SZ]
        m1 = jnp.max(seg, axis=0, keepdims=True)
        i1 = jnp.min(jnp.where(seg == m1, r32, 2.0 * GRP_SZ), axis=0, keepdims=True)
        m2 = jnp.max(jnp.where(r32 == i1, neg, seg), axis=0, keepdims=True)
        gs.append(m1 + m2)
    allowed = []
    for g in range(N_GRP):
        beat = jnp.zeros((1, tr), f32)
        for h in range(N_GRP):
            if h < g:
                beat = beat + jnp.where(gs[h] >= gs[g], 1.0, 0.0)
            elif h > g:
                beat = beat + jnp.where(gs[h] > gs[g], 1.0, 0.0)
        allowed.append(jnp.broadcast_to(beat, (GRP_SZ, tr)))
    allowed = jnp.concatenate(allowed, axis=0)
    masked = jnp.where(allowed < float(TOPK_GRP), biased, neg)
    cnt = jnp.zeros((N_EXP, tr), f32)
    idxs, ws = [], []
    for _ in range(TOP_K):
        m = jnp.max(masked, axis=0, keepdims=True)
        idx = jnp.min(jnp.where(masked == m, rowf, 2.0 * N_EXP), axis=0, keepdims=True)
        hit = rowf == idx
        ws.append(jnp.sum(jnp.where(hit, scores, 0.0), axis=0, keepdims=True))
        masked = jnp.where(hit, neg, masked)
        cnt = cnt + jnp.where(hit, 1.0, 0.0)
        idxs.append(idx)
    wsum = ws[0]
    for k in range(1, TOP_K):
        wsum = wsum + ws[k]
    pos = _dot(cnt.astype(bf16), tri_ref[...]) + base_ref[...]
    ranks = [jnp.sum(jnp.where(rowf == idxs[k], pos, 0.0), axis=0, keepdims=True) for k in range(TOP_K)]
    e_ref[...] = jnp.concatenate(idxs, axis=0).astype(i32)
    w_ref[...] = jnp.concatenate([ROUTED_SCALE * ws[k] / wsum for k in range(TOP_K)], axis=0)
    r_ref[...] = jnp.concatenate(ranks, axis=0).astype(i32)
    base_ref[...] = base_ref[...] + jnp.sum(cnt, axis=1, keepdims=True)
    c_ref[...] = base_ref[...]


def _route_call(logits_t, router_b, tri):
    t = logits_t.shape[1]
    tr = ROUTE_TILE
    col = lambda i: (0, i)
    return pl.pallas_call(
        _route_kernel,
        grid=(t // tr,),
        in_specs=[pl.BlockSpec((N_EXP, tr), col),
                  pl.BlockSpec((N_EXP, 1), lambda i: (0, 0)),
                  pl.BlockSpec((tr, tr), lambda i: (0, 0))],
        out_specs=[pl.BlockSpec((TOP_K, tr), col), pl.BlockSpec((TOP_K, tr), col), pl.BlockSpec((TOP_K, tr), col),
                   pl.BlockSpec((N_EXP, 1), lambda i: (0, 0))],
        out_shape=[jax.ShapeDtypeStruct((TOP_K, t), i32), jax.ShapeDtypeStruct((TOP_K, t), f32),
                   jax.ShapeDtypeStruct((TOP_K, t), i32), jax.ShapeDtypeStruct((N_EXP, 1), f32)],
        scratch_shapes=[pltpu.VMEM((N_EXP, 1), f32)],
        compiler_params=_params(("arbitrary",)),
        name="route",
    )(logits_t, router_b, tri)


def _dest_kernel(e_ref, r_ref, off_ref, d_ref):
    tr = e_ref.shape[1]
    rowi = lax.broadcasted_iota(i32, (N_EXP, tr), 0)
    off = off_ref[...]
    e = e_ref[...]
    rows = [jnp.sum(jnp.where(rowi == e[k:k + 1], off, 0.0), axis=0, keepdims=True) for k in range(TOP_K)]
    d_ref[...] = jnp.concatenate(rows, axis=0).astype(i32) + r_ref[...]


def _dest_call(eidx, rank, pad_off):
    t = eidx.shape[1]
    tr = 512
    col = lambda i: (0, i)
    return pl.pallas_call(
        _dest_kernel,
        grid=(t // tr,),
        in_specs=[pl.BlockSpec((TOP_K, tr), col), pl.BlockSpec((TOP_K, tr), col),
                  pl.BlockSpec((N_EXP, 1), lambda i: (0, 0))],
        out_specs=pl.BlockSpec((TOP_K, tr), col),
        out_shape=jax.ShapeDtypeStruct((TOP_K, t), i32),
        compiler_params=_params(("parallel",)),
        name="dest",
    )(eidx, rank, pad_off)


DISPATCH_TILE = 512
_PAD_PIECES = (128, 64, 32, 16, 8, 4, 2, 1)


DISPATCH_SLOTS = 3


def _dispatch_kernel(cnt_ref, off_ref, nbt_ref, dest_hbm, h_hbm, s13_ref, s2_ref, xs_hbm, sh_ref, idx_ref, hbuf,
                     zero_ref, sem_idx, sem_tile, sem_row, sem_z):
    i = pl.program_id(0)
    nsteps = pl.num_programs(0)
    n = idx_ref.shape[0] // 2
    ts = n // TOP_K
    trows = ts * PK_CHUNKS
    n_blk = xs_hbm.shape[0] // (MOE_BLK * PK_CHUNKS)

    def idx_copy(step):
        return pltpu.make_async_copy(dest_hbm.at[pl.ds(pl.multiple_of(step * n, n), n)],
                                     idx_ref.at[pl.ds(pl.multiple_of((step & 1) * n, n), n)], sem_idx.at[step & 1])

    def tile_copy(step):
        slot = lax.rem(step, DISPATCH_SLOTS)
        return pltpu.make_async_copy(h_hbm.at[pl.ds(pl.multiple_of(step * trows, trows), trows), :], hbuf.at[slot],
                                     sem_tile.at[slot])

    def rows_wait(step):
        pltpu.make_async_copy(xs_hbm.at[pl.ds(0, n * PK_CHUNKS), :], xs_hbm.at[pl.ds(0, n * PK_CHUNKS), :],
                              sem_row.at[lax.rem(step, DISPATCH_SLOTS)]).wait()

    def pad_copy(start_slot, p):
        return pltpu.make_async_copy(zero_ref.at[pl.ds(0, p * PK_CHUNKS), :],
                                     xs_hbm.at[pl.ds(start_slot * PK_CHUNKS, p * PK_CHUNKS), :], sem_z)

    def blk_copy(blk):
        return pltpu.make_async_copy(zero_ref, xs_hbm.at[pl.ds(blk * (MOE_BLK * PK_CHUNKS), MOE_BLK * PK_CHUNKS), :],
                                     sem_z)

    def for_each_pad_piece(fn):
        def per_expert(e, carry):
            c = cnt_ref[e]
            npad = ((c + (MOE_BLK - 1)) & (-MOE_BLK)) - c
            slot = off_ref[e] + c
            for p in _PAD_PIECES:
                @pl.when((npad & p) != 0)
                def _():
                    fn(pad_copy(slot, p))
                slot = slot + (npad & p)
            return carry

        lax.fori_loop(0, N_EXP, per_expert, 0)

    @pl.when(i == 0)
    def _():
        idx_copy(0).start()
        tile_copy(0).start()
        zero_ref[...] = jnp.zeros_like(zero_ref)
        for_each_pad_piece(lambda cp: cp.start())
        lax.fori_loop(nbt_ref[0], n_blk, lambda b, c: (blk_copy(b).start(), c)[1], 0)

    @pl.when(i >= DISPATCH_SLOTS - 1)
    def _():
        rows_wait(i - (DISPATCH_SLOTS - 1))

    @pl.when(i + 1 < nsteps)
    def _():
        idx_copy(i + 1).start()
        tile_copy(i + 1).start()

    idx_copy(i).wait()
    tile_copy(i).wait()
    sl = i & 1
    slot = lax.rem(i, DISPATCH_SLOTS)
    hb = hbuf.at[slot]

    def body(t2, carry):
        base = sl * n + t2 * (2 * TOP_K)
        ds = [idx_ref[base + j] for j in range(2 * TOP_K)]
        for j in range(2 * TOP_K):
            t = t2 * 2 + j // TOP_K
            pltpu.make_async_copy(hb.at[pl.ds(t * PK_CHUNKS, PK_CHUNKS), :],
                                  xs_hbm.at[pl.ds(ds[j] * PK_CHUNKS, PK_CHUNKS), :],
                                  sem_row.at[slot]).start(priority=j % 2)
        return carry

    lax.fori_loop(0, ts // 2, body, 0)

    cols = []
    for c in range(PK_CHUNKS):
        cols.append(jnp.concatenate(
            [hb[pl.ds(8 * PK_CHUNKS * g + c, 8, stride=PK_CHUNKS), :] for g in range(ts // 8)], axis=0))
    word = jnp.concatenate(cols, axis=1)
    xlo = lax.bitcast_convert_type(lax.shift_left(word, jnp.uint32(16)), f32).astype(bf16)
    xhi = lax.bitcast_convert_type(word & jnp.uint32(0xFFFF0000), f32).astype(bf16)
    a = _dot(xlo, s13_ref[0:512, :]) + _dot(xhi, s13_ref[512:1024, :])
    hid = (_silu(a[:, 0:EXP_D]) * a[:, EXP_D:2 * EXP_D]).astype(bf16)
    sh_ref[...] = _dot(hid, s2_ref[...])

    @pl.when(i == nsteps - 1)
    def _():
        for back in range(DISPATCH_SLOTS - 2, -1, -1):
            @pl.when(i >= back)
            def _():
                rows_wait(i - back)

        for_each_pad_piece(lambda cp: cp.wait())
        lax.fori_loop(nbt_ref[0], n_blk, lambda b, c: (blk_copy(b).wait(), c)[1], 0)


def _dispatch_call(cnt, pad_off, nb_total, dest_flat, h2p, s13, s2, n_slots):
    t = h2p.shape[0] // PK_CHUNKS
    ts = DISPATCH_TILE
    vec = lambda i, *_: (0, 0)
    gs = pltpu.PrefetchScalarGridSpec(
        num_scalar_prefetch=3,
        grid=(t // ts,),
        in_specs=[pl.BlockSpec(memory_space=pl.ANY), pl.BlockSpec(memory_space=pl.ANY),
                  pl.BlockSpec((D, 2 * EXP_D), vec), pl.BlockSpec((EXP_D, D), vec)],
        out_specs=[pl.BlockSpec(memory_space=pl.ANY), pl.BlockSpec((ts, D), lambda i, *_: (i, 0))],
        scratch_shapes=[pltpu.SMEM((2 * ts * TOP_K,), i32), pltpu.VMEM((DISPATCH_SLOTS, ts * PK_CHUNKS, 128), u32),
                        pltpu.VMEM((MOE_BLK * PK_CHUNKS, 128), u32),
                        pltpu.SemaphoreType.DMA((2,)), pltpu.SemaphoreType.DMA((DISPATCH_SLOTS,)),
                        pltpu.SemaphoreType.DMA((DISPATCH_SLOTS,)), pltpu.SemaphoreType.DMA(())],
    )
    return pl.pallas_call(
        _dispatch_kernel,
        grid_spec=gs,
        out_shape=[jax.ShapeDtypeStruct((n_slots * PK_CHUNKS, 128), u32), jax.ShapeDtypeStruct((t, D), f32)],
        compiler_params=_params(("arbitrary",)),
        name="dispatch",
    )(cnt, pad_off, nb_total, dest_flat, h2p, s13, s2)


GMLP_RING = 8


def _gmlp_kernel(nbe_ref, boff_ref, nbt_ref, w1_ref, w3_ref, w2_ref, xs_hbm, y_hbm, xbuf, ybuf, w13_s, w2_s,
                 sem_in, sem_out, sem_z):
    e = pl.program_id(0)
    nb = nbe_ref[e]
    b0 = boff_ref[e]
    total = nbt_ref[0]
    n_blk = y_hbm.shape[0] // (MOE_BLK * PK_CHUNKS)
    xrows = MOE_BLK * PK_CHUNKS
    yrows = MOE_BLK * PK_CHUNKS
    ring = GMLP_RING

    def in_copy(b):
        sl = b & (ring - 1)
        return pltpu.make_async_copy(xs_hbm.at[pl.ds(pl.multiple_of(b * xrows, xrows), xrows), :], xbuf.at[sl],
                                     sem_in.at[sl])

    def out_copy(b):
        sl = b & (ring - 1)
        return pltpu.make_async_copy(ybuf.at[sl], y_hbm.at[pl.ds(pl.multiple_of(b * yrows, yrows), yrows), :],
                                     sem_out.at[sl])

    def zero_copy(b):
        return pltpu.make_async_copy(ybuf.at[0], y_hbm.at[pl.ds(pl.multiple_of(b * yrows, yrows), yrows), :], sem_z)

    ahead = ring - 2

    @pl.when(e == 0)
    def _():
        for b in range(ahead):
            @pl.when(b < total)
            def _():
                in_copy(b).start()

    @pl.when(nb > 0)
    def _():
        w13_s[:, 0:EXP_D] = w1_ref[0].astype(bf16)
        w13_s[:, EXP_D:2 * EXP_D] = w3_ref[0].astype(bf16)
        w2_s[...] = w2_ref[0].astype(bf16)

    def process(b, m):
        for q in range(m):
            in_copy(b + q).wait()
        for q in range(m):
            @pl.when(b + q + ahead < total)
            def _():
                in_copy(b + q + ahead).start()

            @pl.when(b + q >= ring)
            def _():
                out_copy(b + q - ring).wait()

        cols = []
        for c in range(PK_CHUNKS):
            pieces = []
            for q in range(m):
                xb = xbuf.at[(b + q) & (ring - 1)]
                pieces += [xb[pl.ds(8 * PK_CHUNKS * g + c, 8, stride=PK_CHUNKS), :] for g in range(MOE_BLK // 8)]
            cols.append(jnp.concatenate(pieces, axis=0))
        word = jnp.concatenate(cols, axis=1)
        xlo = lax.bitcast_convert_type(lax.shift_left(word, jnp.uint32(16)), f32).astype(bf16)
        xhi = lax.bitcast_convert_type(word & jnp.uint32(0xFFFF0000), f32).astype(bf16)
        h = _dot(xlo, w13_s[0:512, :]) + _dot(xhi, w13_s[512:1024, :])
        hid = (_silu(h[:, 0:EXP_D]) * h[:, EXP_D:2 * EXP_D]).astype(bf16)
        y = _dot(hid, w2_s[...])
        ylo = lax.shift_right_logical(lax.bitcast_convert_type(y[:, 0:512].astype(bf16).astype(f32), u32),
                                      jnp.uint32(16))
        yhi = lax.bitcast_convert_type(y[:, 512:1024].astype(bf16).astype(f32), u32) & jnp.uint32(0xFFFF0000)
        yw = ylo | yhi
        for q in range(m):
            yb = ybuf.at[(b + q) & (ring - 1)]
            for g in range(MOE_BLK // 8):
                r0 = q * MOE_BLK + 8 * g
                for c in range(PK_CHUNKS):
                    yb[pl.ds(8 * PK_CHUNKS * g + c, 8, stride=PK_CHUNKS), :] = yw[r0:r0 + 8, 128 * c:128 * c + 128]
            out_copy(b + q).start()

    def pair(j, carry):
        process(b0 + 2 * j, 2)
        return carry

    lax.fori_loop(0, jnp.right_shift(nb, 1), pair, 0)

    @pl.when((nb & 1) == 1)
    def _():
        process(b0 + nb - 1, 1)

    @pl.when(e == pl.num_programs(0) - 1)
    def _():
        for back in range(ring, 0, -1):
            @pl.when(total >= back)
            def _():
                out_copy(total - back).wait()

        ybuf[0] = jnp.zeros(ybuf.shape[1:], u32)
        lax.fori_loop(total, n_blk, lambda b, c: (zero_copy(b).start(), c)[1], 0)
        lax.fori_loop(total, n_blk, lambda b, c: (zero_copy(b).wait(), c)[1], 0)


def _gmlp_call(layer, nblk_e, blk_off, nb_total, xs, w1, w3, w2):
    n_slots = xs.shape[0] // PK_CHUNKS
    wsel = lambda e, *_: (layer, e, 0, 0)
    gs = pltpu.PrefetchScalarGridSpec(
        num_scalar_prefetch=3,
        grid=(N_EXP,),
        in_specs=[pl.BlockSpec((None, 1, D, EXP_D), wsel), pl.BlockSpec((None, 1, D, EXP_D), wsel),
                  pl.BlockSpec((None, 1, EXP_D, D), wsel), pl.BlockSpec(memory_space=pl.ANY)],
        out_specs=pl.BlockSpec(memory_space=pl.ANY),
        scratch_shapes=[pltpu.VMEM((GMLP_RING, MOE_BLK * PK_CHUNKS, 128), u32),
                        pltpu.VMEM((GMLP_RING, MOE_BLK * PK_CHUNKS, 128), u32),
                        pltpu.VMEM((D, 2 * EXP_D), bf16), pltpu.VMEM((EXP_D, D), bf16),
                        pltpu.SemaphoreType.DMA((GMLP_RING,)), pltpu.SemaphoreType.DMA((GMLP_RING,)),
                        pltpu.SemaphoreType.DMA(())],
    )
    return pl.pallas_call(
        _gmlp_kernel,
        grid_spec=gs,
        out_shape=jax.ShapeDtypeStruct((n_slots * PK_CHUNKS, 128), u32),
        compiler_params=_params(("arbitrary",)),
        name="gmlp",
    )(nblk_e, blk_off, nb_total, w1, w3, w2, xs)


COMBINE_TILE = 128


def _combine_kernel(final, fuse_next, *refs):
    if fuse_next:
        (dest_hbm, y_hbm, x_ref, mod_ref, sh_ref, w_ref, gf_ref, nm_ref, ng_ref, nw_ref, nb_ref, o_ref, z_ref,
         idx_ref, buf_ref, sem_idx, sem_row) = refs
    else:
        dest_hbm, y_hbm, x_ref, mod_ref, sh_ref, w_ref, gf_ref, o_ref, idx_ref, buf_ref, sem_idx, sem_row = refs
    tm = x_ref.shape[1]
    n = tm * TOP_K
    s = pl.program_id(0) * pl.num_programs(1) + pl.program_id(1)
    nsteps = pl.num_programs(0) * pl.num_programs(1)
    last = s == nsteps - 1
    nxt = jnp.minimum(s + 1, nsteps - 1)

    def idx_copy(step):
        return pltpu.make_async_copy(dest_hbm.at[pl.ds(pl.multiple_of(step * n, n), n)],
                                     idx_ref.at[pl.ds(pl.multiple_of((step & 1) * n, n), n)], sem_idx.at[step & 1])

    def row_copy(d, k, t, slot, prio):
        return pltpu.make_async_copy(y_hbm.at[pl.ds(d * PK_CHUNKS, PK_CHUNKS), :],
                                     buf_ref.at[slot, pl.ds((k * tm + t) * PK_CHUNKS, PK_CHUNKS), :],
                                     sem_row.at[slot]).start(priority=prio)

    def rows_wait(slot):
        pltpu.make_async_copy(y_hbm.at[pl.ds(0, n * PK_CHUNKS), :], buf_ref.at[slot], sem_row.at[slot]).wait()

    @pl.when(s == 0)
    def _():
        idx_copy(0).start()
        idx_copy(0).wait()

        def body(t2, carry):
            ds = [idx_ref[t2 * (2 * TOP_K) + j] for j in range(2 * TOP_K)]
            for j in range(2 * TOP_K):
                row_copy(ds[j], j % TOP_K, t2 * 2 + j // TOP_K, 0, j % 2)
            return carry

        lax.fori_loop(0, tm // 2, body, 0)

        @pl.when(nsteps > 1)
        def _():
            idx_copy(1).start()

    @pl.when(s + 1 < nsteps)
    def _():
        idx_copy(s + 1).wait()

    @pl.when(s + 2 < nsteps)
    def _():
        idx_copy(s + 2).start()

    sl = s & 1
    nsl = 1 - sl
    rows_wait(sl)
    bs = buf_ref.at[sl]
    m = mod_ref[0]
    gate = m[5:6]
    nbase = (nxt & 1) * n
    himask = jnp.uint32(0xFFFF0000)

    def group(g, carry):
        r0 = pl.multiple_of(g * 8, 8)
        ds = [idx_ref[nbase + r0 * TOP_K + j] for j in range(8 * TOP_K)]
        wg = w_ref[pl.ds(r0, 8), :]
        lo = [None] * PK_CHUNKS
        hi = [None] * PK_CHUNKS
        for k in range(TOP_K):
            wk = jnp.broadcast_to(wg[:, k:k + 1], (8, 128))
            for c in range(PK_CHUNKS):
                word = bs[pl.ds((k * tm + r0) * PK_CHUNKS + c, 8, stride=PK_CHUNKS), :]
                plo = wk * lax.bitcast_convert_type(lax.shift_left(word, jnp.uint32(16)), f32)
                phi = wk * lax.bitcast_convert_type(word & himask, f32)
                lo[c] = plo if k == 0 else lo[c] + plo
                hi[c] = phi if k == 0 else hi[c] + phi
        routed = jnp.concatenate(lo + hi, axis=1)
        o_ref[0, pl.ds(r0, 8), :] = x_ref[0, pl.ds(r0, 8), :] + gate * (routed + sh_ref[pl.ds(r0, 8), :])
        for j in range(8 * TOP_K):
            row_copy(ds[j], j % TOP_K, r0 + j // TOP_K, nsl, j % 2)
        return carry

    lax.fori_loop(0, tm // 8, group, 0)

    @pl.when(last)
    def _():
        rows_wait(nsl)

    if final:
        o_ref[0] = _rms(o_ref[0], gf_ref[...])
    if fuse_next:
        nm = nm_ref[0]
        h = _rms_mod(o_ref[0], ng_ref[...], nm[0:1], nm[1:2]).astype(bf16)
        z = _dot(h, nw_ref[...]) + nb_ref[...]
        z_ref[0] = z[:, 0:D] * _sigmoid(z[:, D:2 * D])


def _combine_call(dest_flat, y, x, mod, shared, w_tok, g_final, final, nxt=None):
    b, s, _ = x.shape
    tm = COMBINE_TILE
    nt = s // tm
    flat = lambda bb, i: (bb * nt + i, 0)
    row = lambda bb, i: (bb, i, 0)
    vec = lambda bb, i: (0, 0)
    in_specs = [pl.BlockSpec(memory_space=pl.ANY), pl.BlockSpec(memory_space=pl.ANY),
                pl.BlockSpec((1, tm, D), row),
                pl.BlockSpec((1, 6, D), lambda bb, i: (bb, 0, 0)),
                pl.BlockSpec((tm, D), flat),
                pl.BlockSpec((tm, TOP_K), flat),
                pl.BlockSpec((1, D), vec)]
    out_specs = pl.BlockSpec((1, tm, D), row)
    out_shape = jax.ShapeDtypeStruct((b, s, D), f32)
    args = [dest_flat, y, x, mod, shared, w_tok, g_final]
    if nxt is not None:
        in_specs += [pl.BlockSpec((1, 6, D), lambda bb, i: (bb, 0, 0)), pl.BlockSpec((1, D), vec),
                     pl.BlockSpec((D, 2 * D), vec), pl.BlockSpec((1, 2 * D), vec)]
        out_specs = [out_specs, pl.BlockSpec((1, tm, D), row)]
        out_shape = [out_shape, jax.ShapeDtypeStruct((b, s, D), f32)]
        args += list(nxt)
    return pl.pallas_call(
        functools.partial(_combine_kernel, final, nxt is not None),
        grid=(b, nt),
        in_specs=in_specs,
        out_specs=out_specs,
        out_shape=out_shape,
        scratch_shapes=[pltpu.SMEM((2 * tm * TOP_K,), i32), pltpu.VMEM((2, TOP_K * tm * PK_CHUNKS, 128), u32),
                        pltpu.SemaphoreType.DMA((2,)), pltpu.SemaphoreType.DMA((2,))],
        compiler_params=_params(("arbitrary", "arbitrary")),
        name="combine",
    )(*args)


def _moe_layer(layer, x1, mod, norm_g, router_w, router_b, w1, w3, w2, sw1, sw3, sw2, g_final, final, nxt):
    b, s, _ = x1.shape
    t = b * s
    rwt = router_w.T
    rwh = rwt.astype(bf16)
    rwl = (rwt - rwh.astype(f32)).astype(bf16)
    s13 = jnp.concatenate([sw1, sw3], axis=1).astype(bf16)
    h2p, logits_t = _ffn_pre_call(x1, mod, norm_g.reshape(1, D), rwh, rwl)
    tri = (lax.broadcasted_iota(i32, (ROUTE_TILE, ROUTE_TILE), 0)
           < lax.broadcasted_iota(i32, (ROUTE_TILE, ROUTE_TILE), 1)).astype(bf16)
    eidx, w_t, rank, counts = _route_call(logits_t, router_b.reshape(N_EXP, 1).astype(f32), tri)
    cnt = counts.reshape(N_EXP).astype(i32)
    nblk_e = (cnt + MOE_BLK - 1) // MOE_BLK
    blk_ends = jnp.cumsum(nblk_e)
    blk_off = blk_ends - nblk_e
    pad_off = blk_off * MOE_BLK
    nb_total = blk_ends[-1:].astype(i32)
    n_blk = t * TOP_K // MOE_BLK + N_EXP
    dest = _dest_call(eidx, rank, pad_off.astype(f32).reshape(N_EXP, 1))
    dest_flat = dest.T.reshape(t * TOP_K)
    xs, shared = _dispatch_call(cnt, pad_off, nb_total, dest_flat, h2p, s13, sw2.astype(bf16), n_blk * MOE_BLK)
    y = _gmlp_call(layer, nblk_e, blk_off, nb_total, xs, w1, w3, w2)
    return _combine_call(dest_flat, y, x1, mod, shared, w_t.T, g_final.reshape(1, D), final, nxt)


def _rot_cols(w):
    d, n = w.shape
    w4 = w.reshape(d, n // 32, 2, 16)
    return jnp.stack([-w4[:, :, 1], w4[:, :, 0]], axis=2).reshape(d, n)


def _rope_tables(s):
    rows = s // GRID_W
    row = jnp.repeat(jnp.arange(rows, dtype=f32), GRID_W)
    col = jnp.tile(jnp.arange(GRID_W, dtype=f32), rows)
    n_freq = HEAD_DIM // 4
    inv = ROPE_BASE ** (-jnp.arange(n_freq, dtype=f32) / n_freq)
    ang_r = row[:, None] * inv
    ang_c = col[:, None] * inv
    cos = jnp.concatenate([jnp.cos(ang_r)] * 2 + [jnp.cos(ang_c)] * 2, axis=1)
    sin = jnp.concatenate([jnp.sin(ang_r)] * 2 + [jnp.sin(ang_c)] * 2, axis=1)
    return jnp.tile(cos, (1, 2)), jnp.tile(sin, (1, 2))


def _block_diag(w):
    h, dh, _ = w.shape
    eye = jnp.eye(h, dtype=w.dtype)
    return (eye[:, None, :, None] * w[:, :, None, :]).reshape(h * dh, h * dh)


def _even_layer_mixer(x, ctx, mod, norm_g, w_in, w_out, conv_w, conv_b, w_r, b_r, w_i, b_i, lam, sink):
    b, s, _ = x.shape
    r0, r1, r2 = LRU_W, 2 * LRU_W, 2 * LRU_W + Q_W
    wq = w_in[:, r1:r2].reshape(D, 2, 4, HEAD_DIM).transpose(0, 2, 1, 3).reshape(D, Q_W)
    wk = w_in[:, r2:r2 + KV_W]
    w_ext = jnp.concatenate([w_in[:, :r1], wq, w_in[:, r2:], _rot_cols(wq), _rot_cols(wk)], axis=1).astype(bf16)
    w_ctx = jnp.concatenate([w_in[:, :r0], w_in[:, r2:]], axis=1).astype(bf16)
    cos, sin = _rope_tables(s)
    g = norm_g.reshape(1, D)
    u, gt, q, k, v = _proj_in_call(x, mod, g, w_ext, cos, sin)
    uc, kx, vx = _proj_ctx_call(ctx, mod, g, w_ctx)
    n_lg = LRU_W // LRU_LANES
    hpg = LRU_LANES // HEAD_DIM

    def lane_groups(w):
        return jnp.stack([jnp.stack([_block_diag(w[d, g * hpg:(g + 1) * hpg]) for g in range(n_lg)]) for d in range(2)])

    wg = jnp.concatenate([lane_groups(w_r), lane_groups(w_i)], axis=-1).astype(bf16)
    bg = jnp.concatenate([b_r.reshape(2, n_lg, 1, LRU_LANES), b_i.reshape(2, n_lg, 1, LRU_LANES)], axis=-1)
    rec = _rglru_call(u, uc, conv_w, conv_b.reshape(1, LRU_W), wg, bg, lam.reshape(2, n_lg, 1, LRU_LANES))
    att = _attn_call(sink, q, k, v, kx, vx)
    w_att = w_out[LRU_W:].reshape(2, 4, HEAD_DIM, D).transpose(1, 0, 2, 3).reshape(Q_W, D).astype(bf16)
    return _mix_out_call(x, mod, gt, rec, att, w_out[:LRU_W].astype(bf16), w_att)


def kernel(x, c, ctx, c_ctx, mod_w, mod_b, norm_mix_g, norm_ffn_g, final_norm_g, ab_w_in, ab_w_out, lru_conv_w,
           lru_conv_b, lru_wr, lru_br, lru_wi, lru_bi, lru_lambda, attn_sink, cm_w_in, cm_b_in, cm_dw_w, cm_dw_b,
           cm_ln_g, cm_ln_b, cm_w_out, cm_b_out, router_w, router_b, exp_w1, exp_w3, exp_w2, shared_w1, shared_w3,
           shared_w2):
    bsz = x.shape[0]
    depth = mod_w.shape[0]
    assert bsz + 1 <= MOD_ROWS - 7
    cc = jnp.zeros((MOD_ROWS, D), f32).at[:bsz].set(c).at[MOD_ROWS - 8].set(c_ctx)
    mod_all = _mod_call(cc, mod_w, mod_b).reshape(depth, MOD_ROWS, 6, D)
    zg = None
    for l in range(depth):
        mod = mod_all[l]
        last = l == depth - 1
        nxt = None
        if not last and (l + 1) % 2 == 1:
            o1 = (l + 1) // 2
            nxt = (mod_all[l + 1], norm_mix_g[l + 1].reshape(1, D), cm_w_in[o1].astype(bf16),
                   cm_b_in[o1].reshape(1, 2 * D))
        if l % 2 == 0:
            e = l // 2
            assert depth <= 2
            x1 = _even_layer_mixer(x, ctx, mod, norm_mix_g[l], ab_w_in[e], ab_w_out[e], lru_conv_w[e], lru_conv_b[e],
                                   lru_wr[e], lru_br[e], lru_wi[e], lru_bi[e], lru_lambda[e], attn_sink[e])
        else:
            o = l // 2
            if zg is None:
                zg = _conf_in_call(x, mod, norm_mix_g[l].reshape(1, D), cm_w_in[o].astype(bf16),
                                   cm_b_in[o].reshape(1, 2 * D))
            dw = jnp.concatenate([cm_dw_w[o], jnp.zeros((1, D), f32)], axis=0)
            x1 = _conf_out_call(x, mod, zg, dw, cm_dw_b[o].reshape(1, D), cm_ln_g[o].reshape(1, D),
                                cm_ln_b[o].reshape(1, D), cm_w_out[o].astype(bf16), cm_b_out[o].reshape(1, D))
        out = _moe_layer(l, x1, mod, norm_ffn_g[l], router_w[l], router_b[l], exp_w1, exp_w3, exp_w2,
                         shared_w1[l], shared_w3[l], shared_w2[l], final_norm_g, last, nxt)
        x, zg = out if nxt is not None else (out, None)
    return x
```

```python
import functools

import jax
import jax.numpy as jnp
from jax import lax
from jax.experimental import pallas as pl
from jax.experimental.pallas import tpu as pltpu

f32 = jnp.float32
bf16 = jnp.bfloat16
i32 = jnp.int32
u32 = jnp.uint32

D = 1024
EPS = 1e-6
LRU_W = 512
LRU_C = 8.0
N_HEADS = 8
HEAD_DIM = 64
GRID_W = 64
ROPE_BASE = 10000.0
Q_W = 512
KV_W = 128
ATT_BLK = 128
CONV_K = 31
N_EXP = 256
TOP_K = 8
N_GRP = 8
TOPK_GRP = 4
GRP_SZ = N_EXP // N_GRP
EXP_D = 256
ROUTED_SCALE = 2.5
MOE_BLK = 256
PK_CHUNKS = D // 2 // 128

VMEM_LIMIT_V7X = 56 * 1024 * 1024
MOD_ROWS = 24

_NT = (((1,), (1,)), ((), ()))


def _params(sem):
    return pltpu.CompilerParams(dimension_semantics=sem, vmem_limit_bytes=VMEM_LIMIT_V7X)


def _sigmoid(x):
    return 1.0 / (1.0 + jnp.exp(-x))


def _silu(x):
    return x * _sigmoid(x)


def _gelu_tanh(x):
    return 0.5 * x * (1.0 + jnp.tanh(0.7978845608028654 * (x + 0.044715 * (x * x * x))))


def _rms(x, g):
    return x * lax.rsqrt(jnp.mean(x * x, axis=-1, keepdims=True) + EPS) * g


def _rms_mod(x, g, shift, scale):
    return _rms(x, g) * (1.0 + scale) + shift


def _dot(a, b):
    return jnp.dot(a, b, preferred_element_type=f32)


def _mod_kernel(c_ref, w_ref, b_ref, o_ref):
    a = _silu(c_ref[...]).astype(bf16)
    o_ref[0] = _dot(a, w_ref[0].astype(bf16)) + b_ref[0]


def _mod_call(cc, mod_w, mod_b):
    depth, _, n = mod_w.shape
    tn = 1536
    return pl.pallas_call(
        _mod_kernel,
        grid=(depth, n // tn),
        in_specs=[pl.BlockSpec((MOD_ROWS, D), lambda l, j: (0, 0)),
                  pl.BlockSpec((1, D, tn), lambda l, j: (l, 0, j)),
                  pl.BlockSpec((1, 1, tn), lambda l, j: (l, 0, j))],
        out_specs=pl.BlockSpec((1, MOD_ROWS, tn), lambda l, j: (l, 0, j)),
        out_shape=jax.ShapeDtypeStruct((depth, MOD_ROWS, n), f32),
        compiler_params=_params(("parallel", "parallel")),
        name="mod",
    )(cc, mod_w, mod_b.reshape(depth, 1, n))


def _proj_in_kernel(x_ref, mod_ref, g_ref, w_ref, cos_ref, sin_ref, u_ref, gt_ref, q_ref, k_ref, v_ref):
    m = mod_ref[0]
    h = _rms_mod(x_ref[0], g_ref[...], m[0:1], m[1:2]).astype(bf16)
    p = _dot(h, w_ref[...])
    u_ref[0] = p[:, 0:512]
    gt_ref[0] = p[:, 512:1024]
    cos = cos_ref[...]
    sin = sin_ref[...]
    qs = []
    for j in range(4):
        qj = p[:, 1024 + j * 128:1152 + j * 128] * cos + p[:, 1792 + j * 128:1920 + j * 128] * sin
        qs.append(qj * (HEAD_DIM ** -0.5))
    q_ref[0] = jnp.concatenate(qs, axis=1).astype(bf16)
    k_ref[0] = (p[:, 1536:1664] * cos + p[:, 2304:2432] * sin).astype(bf16)
    v_ref[0] = p[:, 1664:1792].astype(bf16)


def _proj_in_call(x, mod, g, w_ext, cos, sin, tm=512):
    b, s, _ = x.shape
    nw = w_ext.shape[1]
    row = lambda bb, i: (bb, i, 0)
    return pl.pallas_call(
        _proj_in_kernel,
        grid=(b, s // tm),
        in_specs=[pl.BlockSpec((1, tm, D), row),
                  pl.BlockSpec((1, 6, D), lambda bb, i: (bb, 0, 0)),
                  pl.BlockSpec((1, D), lambda bb, i: (0, 0)),
                  pl.BlockSpec((D, nw), lambda bb, i: (0, 0)),
                  pl.BlockSpec((tm, 128), lambda bb, i: (i, 0)),
                  pl.BlockSpec((tm, 128), lambda bb, i: (i, 0))],
        out_specs=[pl.BlockSpec((1, tm, LRU_W), row), pl.BlockSpec((1, tm, LRU_W), row),
                   pl.BlockSpec((1, tm, Q_W), row), pl.BlockSpec((1, tm, KV_W), row),
                   pl.BlockSpec((1, tm, KV_W), row)],
        out_shape=[jax.ShapeDtypeStruct((b, s, LRU_W), f32), jax.ShapeDtypeStruct((b, s, LRU_W), f32),
                   jax.ShapeDtypeStruct((b, s, Q_W), bf16), jax.ShapeDtypeStruct((b, s, KV_W), bf16),
                   jax.ShapeDtypeStruct((b, s, KV_W), bf16)],
        compiler_params=_params(("parallel", "parallel")),
        name="proj_in",
    )(x, mod, g, w_ext, cos, sin)


def _proj_ctx_kernel(x_ref, mod_ref, g_ref, w_ref, u_ref, k_ref, v_ref):
    m = mod_ref[0]
    h = _rms_mod(x_ref[0], g_ref[...], m[0:1], m[1:2]).astype(bf16)
    p = _dot(h, w_ref[...])
    u_ref[0] = p[:, 0:512]
    k_ref[0] = p[:, 512:640].astype(bf16)
    v_ref[0] = p[:, 640:768].astype(bf16)


def _proj_ctx_call(ctx, mod, g, w_ctx):
    b, n_ctx, _ = ctx.shape
    row = lambda bb: (bb, 0, 0)
    return pl.pallas_call(
        _proj_ctx_kernel,
        grid=(b,),
        in_specs=[pl.BlockSpec((1, n_ctx, D), row),
                  pl.BlockSpec((1, 6, D), lambda bb: (MOD_ROWS - 8, 0, 0)),
                  pl.BlockSpec((1, D), lambda bb: (0, 0)),
                  pl.BlockSpec((D, 768), lambda bb: (0, 0))],
        out_specs=[pl.BlockSpec((1, n_ctx, LRU_W), row), pl.BlockSpec((1, n_ctx, KV_W), row),
                   pl.BlockSpec((1, n_ctx, KV_W), row)],
        out_shape=[jax.ShapeDtypeStruct((b, n_ctx, LRU_W), f32), jax.ShapeDtypeStruct((b, n_ctx, KV_W), bf16),
                   jax.ShapeDtypeStruct((b, n_ctx, KV_W), bf16)],
        compiler_params=_params(("parallel",)),
        name="proj_ctx",
    )(ctx, mod, g, w_ctx)


LRU_CHUNK = 128
LRU_LANES = 512


def _rglru_kernel(u_ref, uc_ref, cw_ref, cb_ref, wg_ref, bg_ref, lam_ref, o_ref, pad_ref, cx_ref, cc_ref):
    s = u_ref.shape[1]
    n_ctx = uc_ref.shape[1]
    tc = LRU_CHUNK
    lw = LRU_LANES

    def conv_segment(src_ref, n, dst_ref):
        pad_ref[0:8] = jnp.zeros((8, lw), f32)
        pad_ref[8:8 + n] = src_ref[0]
        pad_ref[8 + n:16 + n] = jnp.zeros((8, lw), f32)
        for c in range(n // 256):
            acc = jnp.broadcast_to(cb_ref[...], (256, lw))
            for k in range(4):
                acc = acc + cw_ref[k:k + 1, :] * pad_ref[c * 256 + 6 + k:c * 256 + 6 + k + 256, :]
            dst_ref[c * 256:(c + 1) * 256] = acc

    conv_segment(uc_ref, n_ctx, cc_ref)
    conv_segment(u_ref, s, cx_ref)

    rowm = lax.broadcasted_iota(i32, (tc, lw), 0) & 7

    def scan_segment(src_ref, n, d, h0, write):
        lam = lam_ref[d, 0]
        sp = jnp.maximum(-lam, 0.0) + jnp.log(1.0 + jnp.exp(-jnp.abs(lam)))
        nch = n // tc

        def chunk(ci, h):
            c = ci if d == 0 else nch - 1 - ci
            t0 = pl.multiple_of(c * tc, tc)
            uc = src_ref[pl.ds(t0, tc), :]
            gates = _dot(uc.astype(bf16), wg_ref[d, 0]) + bg_ref[d, 0]
            r = _sigmoid(gates[:, 0:lw])
            ig = _sigmoid(gates[:, lw:2 * lw])
            log_a = (-LRU_C * sp) * r
            a = jnp.exp(log_a)
            bb = jnp.sqrt(-jnp.tanh(log_a) * (a * a + 1.0)) * (ig * uc)
            def shift(v, sh):
                amount = sh if d == 0 else 8 - sh
                return pltpu.roll(v.reshape(tc // 8, 8, lw), amount, 1).reshape(tc, lw)

            for sh in (1, 2, 4):
                keep = rowm >= sh if d == 0 else rowm < 8 - sh
                a_sh = jnp.where(keep, shift(a, sh), 1.0)
                b_sh = jnp.where(keep, shift(bb, sh), 0.0)
                bb = a * b_sh + bb
                a = a * a_sh
            outs = [None] * (tc // 8)
            order = range(tc // 8) if d == 0 else range(tc // 8 - 1, -1, -1)
            for gi in order:
                hg = bb[gi * 8:(gi + 1) * 8] + a[gi * 8:(gi + 1) * 8] * h
                outs[gi] = hg
                h = hg[7:8] if d == 0 else hg[0:1]
            if write:
                hs = jnp.concatenate(outs, axis=0)
                if d == 0:
                    o_ref[0, pl.ds(t0, tc), :] = hs
                else:
                    o_ref[0, pl.ds(t0, tc), :] = o_ref[0, pl.ds(t0, tc), :] + hs
            return h

        return lax.fori_loop(0, nch, chunk, h0)

    for d in range(2):
        h = jnp.zeros((1, lw), f32)
        h = scan_segment(cc_ref, n_ctx, d, h, False)
        scan_segment(cx_ref, s, d, h, True)


def _rglru_call(u, uc, conv_w, conv_b, wg, bg, lam):
    b, s, _ = u.shape
    n_ctx = uc.shape[1]
    lw = LRU_LANES
    return pl.pallas_call(
        _rglru_kernel,
        grid=(b, LRU_W // lw),
        in_specs=[pl.BlockSpec((1, s, lw), lambda bb, g: (bb, 0, g)),
                  pl.BlockSpec((1, n_ctx, lw), lambda bb, g: (bb, 0, g)),
                  pl.BlockSpec((4, lw), lambda bb, g: (0, g)),
                  pl.BlockSpec((1, lw), lambda bb, g: (0, g)),
                  pl.BlockSpec((2, 1, lw, 2 * lw), lambda bb, g: (0, g, 0, 0)),
                  pl.BlockSpec((2, 1, 1, 2 * lw), lambda bb, g: (0, g, 0, 0)),
                  pl.BlockSpec((2, 1, 1, lw), lambda bb, g: (0, g, 0, 0))],
        out_specs=pl.BlockSpec((1, s, lw), lambda bb, g: (bb, 0, g)),
        out_shape=jax.ShapeDtypeStruct((b, s, LRU_W), f32),
        scratch_shapes=[pltpu.VMEM((s + 16, lw), f32), pltpu.VMEM((s, lw), f32), pltpu.VMEM((n_ctx, lw), f32)],
        compiler_params=_params(("parallel", "parallel")),
        name="rglru",
    )(u, uc, conv_w, conv_b, wg, bg, lam)


def _attn_kernel(sink_ref, q_ref, kp_ref, kc_ref, kn_ref, vp_ref, vc_ref, vn_ref, kx_ref, vx_ref, o_ref):
    n = pl.program_id(1)
    nb = pl.num_programs(1)
    blk = ATT_BLK
    q = q_ref[0]
    qall = jnp.concatenate([q[:, j * 128:(j + 1) * 128] for j in range(4)], axis=0)
    kw = jnp.concatenate([kp_ref[0], kc_ref[0], kn_ref[0]], axis=0)
    vw = jnp.concatenate([vp_ref[0], vc_ref[0], vn_ref[0]], axis=0)
    kx = kx_ref[0]
    vx = vx_ref[0]
    lo_w = lax.broadcasted_iota(i32, kw.shape, 1) < HEAD_DIM
    lo_x = lax.broadcasted_iota(i32, kx.shape, 1) < HEAD_DIM
    dj = lax.broadcasted_iota(i32, (4 * blk, blk), 1) - (lax.broadcasted_iota(i32, (4 * blk, blk), 0) & (blk - 1))
    pen_prev = jnp.where(dj >= jnp.where(n > 0, 0, 2 * blk), 0.0, -jnp.inf)
    pen_next = jnp.where(dj <= jnp.where(n < nb - 1, 0, -2 * blk), 0.0, -jnp.inf)
    rb = lax.broadcasted_iota(i32, (4 * blk, 1), 0) // blk
    zero = jnp.zeros((), bf16)
    out = jnp.zeros((4 * blk, 128), f32)
    for half in range(2):
        sel_w = lo_w if half == 0 else jnp.logical_not(lo_w)
        sel_x = lo_x if half == 0 else jnp.logical_not(lo_x)
        s_w = lax.dot_general(qall, jnp.where(sel_w, kw, zero), _NT, preferred_element_type=f32)
        s_w = jnp.concatenate([s_w[:, 0:blk] + pen_prev, s_w[:, blk:2 * blk], s_w[:, 2 * blk:] + pen_next], axis=1)
        s_c = lax.dot_general(qall, jnp.where(sel_x, kx, zero), _NT, preferred_element_type=f32)
        sk = jnp.where(rb == 0, sink_ref[4 * half],
                       jnp.where(rb == 1, sink_ref[4 * half + 1],
                                 jnp.where(rb == 2, sink_ref[4 * half + 2], sink_ref[4 * half + 3])))
        m = jnp.maximum(jnp.maximum(jnp.max(s_w, axis=1, keepdims=True), jnp.max(s_c, axis=1, keepdims=True)), sk)
        p_w = jnp.exp(s_w - m)
        p_c = jnp.exp(s_c - m)
        den = jnp.sum(p_w, axis=1, keepdims=True) + jnp.sum(p_c, axis=1, keepdims=True) + jnp.exp(sk - m)
        o = _dot(p_w.astype(bf16), jnp.where(sel_w, vw, zero)) + _dot(p_c.astype(bf16), jnp.where(sel_x, vx, zero))
        out = out + o / den
    o_ref[0] = jnp.concatenate([out[j * blk:(j + 1) * blk] for j in range(4)], axis=1).astype(bf16)


def _attn_call(sink, q, k, v, kx, vx):
    b, s, _ = q.shape
    n_ctx = kx.shape[1]
    nb = s // ATT_BLK
    cur = lambda bb, n: (bb, n, 0)
    prev = lambda bb, n: (bb, jnp.maximum(n - 1, 0), 0)
    nxt = lambda bb, n: (bb, jnp.minimum(n + 1, nb - 1), 0)
    kvb = (1, ATT_BLK, KV_W)
    return pl.pallas_call(
        _attn_kernel,
        grid=(b, nb),
        in_specs=[pl.BlockSpec(memory_space=pltpu.SMEM),
                  pl.BlockSpec((1, ATT_BLK, Q_W), cur),
                  pl.BlockSpec(kvb, prev), pl.BlockSpec(kvb, cur), pl.BlockSpec(kvb, nxt),
                  pl.BlockSpec(kvb, prev), pl.BlockSpec(kvb, cur), pl.BlockSpec(kvb, nxt),
                  pl.BlockSpec((1, n_ctx, KV_W), lambda bb, n: (bb, 0, 0)),
                  pl.BlockSpec((1, n_ctx, KV_W), lambda bb, n: (bb, 0, 0))],
        out_specs=pl.BlockSpec((1, ATT_BLK, Q_W), cur),
        out_shape=jax.ShapeDtypeStruct((b, s, Q_W), bf16),
        compiler_params=_params(("parallel", "parallel")),
        name="attn",
    )(sink, q, k, k, k, v, v, v, kx, vx)


def _mix_out_kernel(x_ref, mod_ref, gt_ref, rec_ref, att_ref, wr_ref, wa_ref, o_ref):
    m = mod_ref[0]
    a = (_gelu_tanh(gt_ref[0]) * rec_ref[0]).astype(bf16)
    y = _dot(a, wr_ref[...]) + _dot(att_ref[0], wa_ref[...])
    o_ref[0] = x_ref[0] + m[2:3] * y


def _mix_out_call(x, mod, gt, rec, att, w_rec, w_att, tm=512):
    b, s, _ = x.shape
    row = lambda bb, i: (bb, i, 0)
    return pl.pallas_call(
        _mix_out_kernel,
        grid=(b, s // tm),
        in_specs=[pl.BlockSpec((1, tm, D), row),
                  pl.BlockSpec((1, 6, D), lambda bb, i: (bb, 0, 0)),
                  pl.BlockSpec((1, tm, LRU_W), row), pl.BlockSpec((1, tm, LRU_W), row),
                  pl.BlockSpec((1, tm, Q_W), row),
                  pl.BlockSpec((LRU_W, D), lambda bb, i: (0, 0)),
                  pl.BlockSpec((Q_W, D), lambda bb, i: (0, 0))],
        out_specs=pl.BlockSpec((1, tm, D), row),
        out_shape=jax.ShapeDtypeStruct((b, s, D), f32),
        compiler_params=_params(("parallel", "parallel")),
        name="mix_out",
    )(x, mod, gt, rec, att, w_rec, w_att)


def _conf_in_kernel(x_ref, mod_ref, g_ref, w_ref, b_ref, o_ref):
    m = mod_ref[0]
    h = _rms_mod(x_ref[0], g_ref[...], m[0:1], m[1:2]).astype(bf16)
    z = _dot(h, w_ref[...]) + b_ref[...]
    o_ref[0] = z[:, 0:D] * _sigmoid(z[:, D:2 * D])


def _conf_in_call(x, mod, g, w, bias, tm=512):
    b, s, _ = x.shape
    row = lambda bb, i: (bb, i, 0)
    return pl.pallas_call(
        _conf_in_kernel,
        grid=(b, s // tm),
        in_specs=[pl.BlockSpec((1, tm, D), row),
                  pl.BlockSpec((1, 6, D), lambda bb, i: (bb, 0, 0)),
                  pl.BlockSpec((1, D), lambda bb, i: (0, 0)),
                  pl.BlockSpec((D, 2 * D), lambda bb, i: (0, 0)),
                  pl.BlockSpec((1, 2 * D), lambda bb, i: (0, 0))],
        out_specs=pl.BlockSpec((1, tm, D), row),
        out_shape=jax.ShapeDtypeStruct((b, s, D), f32),
        compiler_params=_params(("parallel", "parallel")),
        name="conf_in",
    )(x, mod, g, w, bias)


CONF_HALO = 16
CONF_ROWS = 64
CONF_LANE_PAD = 128


def _conf_out_kernel(x_ref, mod_ref, zc_ref, zp_ref, zn_ref, dw_ref, db_ref, lg_ref, lb_ref, w_ref, b_ref, o_ref,
                     pad_ref, sh_ref, cv_ref):
    i = pl.program_id(1)
    nt = pl.num_programs(1)
    tm = zc_ref.shape[1]
    zero = jnp.zeros((CONF_HALO, D), f32)
    pad_ref[0:CONF_HALO] = jnp.where(i > 0, zp_ref[0], zero)
    pad_ref[CONF_HALO:CONF_HALO + tm] = zc_ref[0]
    pad_ref[CONF_HALO + tm:2 * CONF_HALO + tm] = jnp.where(i < nt - 1, zn_ref[0], zero)
    for r in range(8):
        sh_ref[r, :, 0:D] = pad_ref[r:r + tm + 24, :]

    for lg in range(D // 128):
        l0 = lg * 128
        taps = [dw_ref[k:k + 1, l0:l0 + 128] for k in range(CONV_K)]
        bias = db_ref[:, l0:l0 + 128]

        def chunk(c, carry, l0=l0, taps=taps, bias=bias):
            t0 = pl.multiple_of(c * CONF_ROWS, CONF_ROWS)
            acc = jnp.broadcast_to(bias, (CONF_ROWS, 128))
            for r in range(8):
                win = sh_ref[r, pl.ds(t0, CONF_ROWS + 24), l0:l0 + 128]
                for a in range(4):
                    k = 8 * a + r - 1
                    if 0 <= k < CONV_K:
                        acc = acc + taps[k] * win[8 * a:8 * a + CONF_ROWS]
            cv_ref[pl.ds(t0, CONF_ROWS), l0:l0 + 128] = acc
            return carry

        lax.fori_loop(0, tm // CONF_ROWS, chunk, 0)
    z = cv_ref[...]
    mu = jnp.mean(z, axis=-1, keepdims=True)
    zc = z - mu
    var = jnp.mean(zc * zc, axis=-1, keepdims=True)
    zn = zc * lax.rsqrt(var + EPS) * lg_ref[...] + lb_ref[...]
    y = _dot(_silu(zn).astype(bf16), w_ref[...]) + b_ref[...]
    m = mod_ref[0]
    o_ref[0] = x_ref[0] + m[2:3] * y


def _conf_out_call(x, mod, zg, dw_w, dw_b, ln_g, ln_b, w_out, b_out, tm=256):
    b, s, _ = x.shape
    row = lambda bb, i: (bb, i, 0)
    hb = tm // CONF_HALO
    nh = s // CONF_HALO
    vec = lambda bb, i: (0, 0)
    return pl.pallas_call(
        _conf_out_kernel,
        grid=(b, s // tm),
        in_specs=[pl.BlockSpec((1, tm, D), row),
                  pl.BlockSpec((1, 6, D), lambda bb, i: (bb, 0, 0)),
                  pl.BlockSpec((1, tm, D), row),
                  pl.BlockSpec((1, CONF_HALO, D), lambda bb, i: (bb, jnp.maximum(i * hb - 1, 0), 0)),
                  pl.BlockSpec((1, CONF_HALO, D), lambda bb, i: (bb, jnp.minimum((i + 1) * hb, nh - 1), 0)),
                  pl.BlockSpec((CONV_K + 1, D), vec),
                  pl.BlockSpec((1, D), vec), pl.BlockSpec((1, D), vec), pl.BlockSpec((1, D), vec),
                  pl.BlockSpec((D, D), vec), pl.BlockSpec((1, D), vec)],
        out_specs=pl.BlockSpec((1, tm, D), row),
        out_shape=jax.ShapeDtypeStruct((b, s, D), f32),
        scratch_shapes=[pltpu.VMEM((tm + 2 * CONF_HALO, D), f32), pltpu.VMEM((8, tm + 24, D + CONF_LANE_PAD), f32),
                        pltpu.VMEM((tm, D), f32)],
        compiler_params=_params(("parallel", "parallel")),
        name="conf_out",
    )(x, mod, zg, zg, zg, dw_w, dw_b, ln_g, ln_b, w_out, b_out)


def _ffn_pre_kernel(x_ref, mod_ref, g_ref, rwh_ref, rwl_ref, hp_ref, lg_ref):
    m = mod_ref[0]
    h2 = _rms_mod(x_ref[0], g_ref[...], m[3:4], m[4:5])
    hb = h2.astype(bf16)
    hbf = hb.astype(f32)
    hl = (h2 - hbf).astype(bf16)
    lg_ref[...] = (lax.dot_general(rwh_ref[...], hb, _NT, preferred_element_type=f32)
                   + lax.dot_general(rwh_ref[...], hl, _NT, preferred_element_type=f32)
                   + lax.dot_general(rwl_ref[...], hb, _NT, preferred_element_type=f32))
    lo = lax.shift_right_logical(lax.bitcast_convert_type(hbf[:, 0:512], u32), jnp.uint32(16))
    hi = lax.bitcast_convert_type(hbf[:, 512:1024], u32) & jnp.uint32(0xFFFF0000)
    word = lo | hi
    for i in range(word.shape[0] // 8):
        for c in range(PK_CHUNKS):
            hp_ref[pl.ds(8 * PK_CHUNKS * i + c, 8, stride=PK_CHUNKS), :] = word[8 * i:8 * i + 8, 128 * c:128 * c + 128]


def _ffn_pre_call(x, mod, g, rwh, rwl, tm=512):
    b, s, _ = x.shape
    nt = s // tm
    t = b * s
    flat = lambda bb, i: (bb * nt + i, 0)
    vec = lambda bb, i: (0, 0)
    return pl.pallas_call(
        _ffn_pre_kernel,
        grid=(b, nt),
        in_specs=[pl.BlockSpec((1, tm, D), lambda bb, i: (bb, i, 0)),
                  pl.BlockSpec((1, 6, D), lambda bb, i: (bb, 0, 0)),
                  pl.BlockSpec((1, D), vec),
                  pl.BlockSpec((N_EXP, D), vec), pl.BlockSpec((N_EXP, D), vec)],
        out_specs=[pl.BlockSpec((tm * PK_CHUNKS, 128), flat),
                   pl.BlockSpec((N_EXP, tm), lambda bb, i: (0, bb * nt + i))],
        out_shape=[jax.ShapeDtypeStruct((t * PK_CHUNKS, 128), u32), jax.ShapeDtypeStruct((N_EXP, t), f32)],
        compiler_params=_params(("parallel", "parallel")),
        name="ffn_pre",
    )(x, mod, g, rwh, rwl)


ROUTE_TILE = 256


def _route_kernel(lg_ref, rb_ref, tri_ref, e_ref, w_ref, r_ref, c_ref, base_ref):
    i = pl.program_id(0)
    tr = lg_ref.shape[1]

    @pl.when(i == 0)
    def _():
        base_ref[...] = jnp.zeros_like(base_ref)

    scores = _sigmoid(lg_ref[...])
    biased = scores + rb_ref[...]
    neg = -jnp.inf
    rowf = lax.broadcasted_iota(i32, (N_EXP, tr), 0).astype(f32)
    r32 = lax.broadcasted_iota(i32, (GRP_SZ, tr), 0).astype(f32)
    gs = []
    for g in range(N_GRP):
        seg = biased[g * GRP_SZ:(g + 1) * GRP_SZ]
        m1 = jnp.max(seg, axis=0, keepdims=True)
        i1 = jnp.min(jnp.where(seg == m1, r32, 2.0 * GRP_SZ), axis=0, keepdims=True)
        m2 = jnp.max(jnp.where(r32 == i1, neg, seg), axis=0, keepdims=True)
        gs.append(m1 + m2)
    allowed = []
    for g in range(N_GRP):
        beat = jnp.zeros((1, tr), f32)
        for h in range(N_GRP):
            if h < g:
                beat = beat + jnp.where(gs[h] >= gs[g], 1.0, 0.0)
            elif h > g:
                beat = beat + jnp.where(gs[h] > gs[g], 1.0, 0.0)
        allowed.append(jnp.broadcast_to(beat, (GRP_SZ, tr)))
    allowed = jnp.concatenate(allowed, axis=0)
    masked = jnp.where(allowed < float(TOPK_GRP), biased, neg)
    cnt = jnp.zeros((N_EXP, tr), f32)
    idxs, ws = [], []
    for _ in range(TOP_K):
        m = jnp.max(masked, axis=0, keepdims=True)
        idx = jnp.min(jnp.where(masked == m, rowf, 2.0 * N_EXP), axis=0, keepdims=True)
        hit = rowf == idx
        ws.append(jnp.sum(jnp.where(hit, scores, 0.0), axis=0, keepdims=True))
        masked = jnp.where(hit, neg, masked)
        cnt = cnt + jnp.where(hit, 1.0, 0.0)
        idxs.append(idx)
    wsum = ws[0]
    for k in range(1, TOP_K):
        wsum = wsum + ws[k]
    pos = _dot(cnt.astype(bf16), tri_ref[...]) + base_ref[...]
    ranks = [jnp.sum(jnp.where(rowf == idxs[k], pos, 0.0), axis=0, keepdims=True) for k in range(TOP_K)]
    e_ref[...] = jnp.concatenate(idxs, axis=0).astype(i32)
    w_ref[...] = jnp.concatenate([ROUTED_SCALE * ws[k] / wsum for k in range(TOP_K)], axis=0)
    r_ref[...] = jnp.concatenate(ranks, axis=0).astype(i32)
    base_ref[...] = base_ref[...] + jnp.sum(cnt, axis=1, keepdims=True)
    c_ref[...] = base_ref[...]


def _route_call(logits_t, router_b, tri):
    t = logits_t.shape[1]
    tr = ROUTE_TILE
    col = lambda i: (0, i)
    return pl.pallas_call(
        _route_kernel,
        grid=(t // tr,),
        in_specs=[pl.BlockSpec((N_EXP, tr), col),
                  pl.BlockSpec((N_EXP, 1), lambda i: (0, 0)),
                  pl.BlockSpec((tr, tr), lambda i: (0, 0))],
        out_specs=[pl.BlockSpec((TOP_K, tr), col), pl.BlockSpec((TOP_K, tr), col), pl.BlockSpec((TOP_K, tr), col),
                   pl.BlockSpec((N_EXP, 1), lambda i: (0, 0))],
        out_shape=[jax.ShapeDtypeStruct((TOP_K, t), i32), jax.ShapeDtypeStruct((TOP_K, t), f32),
                   jax.ShapeDtypeStruct((TOP_K, t), i32), jax.ShapeDtypeStruct((N_EXP, 1), f32)],
        scratch_shapes=[pltpu.VMEM((N_EXP, 1), f32)],
        compiler_params=_params(("arbitrary",)),
        name="route",
    )(logits_t, router_b, tri)


def _dest_kernel(e_ref, r_ref, off_ref, d_ref):
    tr = e_ref.shape[1]
    rowi = lax.broadcasted_iota(i32, (N_EXP, tr), 0)
    off = off_ref[...]
    e = e_ref[...]
    rows = [jnp.sum(jnp.where(rowi == e[k:k + 1], off, 0.0), axis=0, keepdims=True) for k in range(TOP_K)]
    d_ref[...] = jnp.concatenate(rows, axis=0).astype(i32) + r_ref[...]


def _dest_call(eidx, rank, pad_off):
    t = eidx.shape[1]
    tr = 512
    col = lambda i: (0, i)
    return pl.pallas_call(
        _dest_kernel,
        grid=(t // tr,),
        in_specs=[pl.BlockSpec((TOP_K, tr), col), pl.BlockSpec((TOP_K, tr), col),
                  pl.BlockSpec((N_EXP, 1), lambda i: (0, 0))],
        out_specs=pl.BlockSpec((TOP_K, tr), col),
        out_shape=jax.ShapeDtypeStruct((TOP_K, t), i32),
        compiler_params=_params(("parallel",)),
        name="dest",
    )(eidx, rank, pad_off)


DISPATCH_TILE = 512
_PAD_PIECES = (128, 64, 32, 16, 8, 4, 2, 1)


DISPATCH_SLOTS = 3


def _dispatch_kernel(cnt_ref, off_ref, nbt_ref, dest_hbm, h_hbm, s13_ref, s2_ref, xs_hbm, sh_ref, idx_ref, hbuf,
                     zero_ref, sem_idx, sem_tile, sem_row, sem_z):
    i = pl.program_id(0)
    nsteps = pl.num_programs(0)
    n = idx_ref.shape[0] // 2
    ts = n // TOP_K
    trows = ts * PK_CHUNKS
    n_blk = xs_hbm.shape[0] // (MOE_BLK * PK_CHUNKS)

    def idx_copy(step):
        return pltpu.make_async_copy(dest_hbm.at[pl.ds(pl.multiple_of(step * n, n), n)],
                                     idx_ref.at[pl.ds(pl.multiple_of((step & 1) * n, n), n)], sem_idx.at[step & 1])

    def tile_copy(step):
        slot = lax.rem(step, DISPATCH_SLOTS)
        return pltpu.make_async_copy(h_hbm.at[pl.ds(pl.multiple_of(step * trows, trows), trows), :], hbuf.at[slot],
                                     sem_tile.at[slot])

    def rows_wait(step):
        pltpu.make_async_copy(xs_hbm.at[pl.ds(0, n * PK_CHUNKS), :], xs_hbm.at[pl.ds(0, n * PK_CHUNKS), :],
                              sem_row.at[lax.rem(step, DISPATCH_SLOTS)]).wait()

    def pad_copy(start_slot, p):
        return pltpu.make_async_copy(zero_ref.at[pl.ds(0, p * PK_CHUNKS), :],
                                     xs_hbm.at[pl.ds(start_slot * PK_CHUNKS, p * PK_CHUNKS), :], sem_z)

    def blk_copy(blk):
        return pltpu.make_async_copy(zero_ref, xs_hbm.at[pl.ds(blk * (MOE_BLK * PK_CHUNKS), MOE_BLK * PK_CHUNKS), :],
                                     sem_z)

    def for_each_pad_piece(fn):
        def per_expert(e, carry):
            c = cnt_ref[e]
            npad = ((c + (MOE_BLK - 1)) & (-MOE_BLK)) - c
            slot = off_ref[e] + c
            for p in _PAD_PIECES:
                @pl.when((npad & p) != 0)
                def _():
                    fn(pad_copy(slot, p))
                slot = slot + (npad & p)
            return carry

        lax.fori_loop(0, N_EXP, per_expert, 0)

    @pl.when(i == 0)
    def _():
        idx_copy(0).start()
        tile_copy(0).start()
        zero_ref[...] = jnp.zeros_like(zero_ref)
        for_each_pad_piece(lambda cp: cp.start())
        lax.fori_loop(nbt_ref[0], n_blk, lambda b, c: (blk_copy(b).start(), c)[1], 0)

    @pl.when(i >= DISPATCH_SLOTS - 1)
    def _():
        rows_wait(i - (DISPATCH_SLOTS - 1))

    @pl.when(i + 1 < nsteps)
    def _():
        idx_copy(i + 1).start()
        tile_copy(i + 1).start()

    idx_copy(i).wait()
    tile_copy(i).wait()
    sl = i & 1
    slot = lax.rem(i, DISPATCH_SLOTS)
    hb = hbuf.at[slot]

    def body(t2, carry):
        base = sl * n + t2 * (2 * TOP_K)
        ds = [idx_ref[base + j] for j in range(2 * TOP_K)]
        for j in range(2 * TOP_K):
            t = t2 * 2 + j // TOP_K
            pltpu.make_async_copy(hb.at[pl.ds(t * PK_CHUNKS, PK_CHUNKS), :],
                                  xs_hbm.at[pl.ds(ds[j] * PK_CHUNKS, PK_CHUNKS), :],
                                  sem_row.at[slot]).start(priority=j % 2)
        return carry

    lax.fori_loop(0, ts // 2, body, 0)

    cols = []
    for c in range(PK_CHUNKS):
        cols.append(jnp.concatenate(
            [hb[pl.ds(8 * PK_CHUNKS * g + c, 8, stride=PK_CHUNKS), :] for g in range(ts // 8)], axis=0))
    word = jnp.concatenate(cols, axis=1)
    xlo = lax.bitcast_convert_type(lax.shift_left(word, jnp.uint32(16)), f32).astype(bf16)
    xhi = lax.bitcast_convert_type(word & jnp.uint32(0xFFFF0000), f32).astype(bf16)
    a = _dot(xlo, s13_ref[0:512, :]) + _dot(xhi, s13_ref[512:1024, :])
    hid = (_silu(a[:, 0:EXP_D]) * a[:, EXP_D:2 * EXP_D]).astype(bf16)
    sh_ref[...] = _dot(hid, s2_ref[...])

    @pl.when(i == nsteps - 1)
    def _():
        for back in range(DISPATCH_SLOTS - 2, -1, -1):
            @pl.when(i >= back)
            def _():
                rows_wait(i - back)

        for_each_pad_piece(lambda cp: cp.wait())
        lax.fori_loop(nbt_ref[0], n_blk, lambda b, c: (blk_copy(b).wait(), c)[1], 0)


def _dispatch_call(cnt, pad_off, nb_total, dest_flat, h2p, s13, s2, n_slots):
    t = h2p.shape[0] // PK_CHUNKS
    ts = DISPATCH_TILE
    vec = lambda i, *_: (0, 0)
    gs = pltpu.PrefetchScalarGridSpec(
        num_scalar_prefetch=3,
        grid=(t // ts,),
        in_specs=[pl.BlockSpec(memory_space=pl.ANY), pl.BlockSpec(memory_space=pl.ANY),
                  pl.BlockSpec((D, 2 * EXP_D), vec), pl.BlockSpec((EXP_D, D), vec)],
        out_specs=[pl.BlockSpec(memory_space=pl.ANY), pl.BlockSpec((ts, D), lambda i, *_: (i, 0))],
        scratch_shapes=[pltpu.SMEM((2 * ts * TOP_K,), i32), pltpu.VMEM((DISPATCH_SLOTS, ts * PK_CHUNKS, 128), u32),
                        pltpu.VMEM((MOE_BLK * PK_CHUNKS, 128), u32),
                        pltpu.SemaphoreType.DMA((2,)), pltpu.SemaphoreType.DMA((DISPATCH_SLOTS,)),
                        pltpu.SemaphoreType.DMA((DISPATCH_SLOTS,)), pltpu.SemaphoreType.DMA(())],
    )
    return pl.pallas_call(
        _dispatch_kernel,
        grid_spec=gs,
        out_shape=[jax.ShapeDtypeStruct((n_slots * PK_CHUNKS, 128), u32), jax.ShapeDtypeStruct((t, D), f32)],
        compiler_params=_params(("arbitrary",)),
        name="dispatch",
    )(cnt, pad_off, nb_total, dest_flat, h2p, s13, s2)


GMLP_RING = 8


def _gmlp_kernel(nbe_ref, boff_ref, nbt_ref, w1_ref, w3_ref, w2_ref, xs_hbm, y_hbm, xbuf, ybuf, w13_s, w2_s,
                 sem_in, sem_out, sem_z):
    e = pl.program_id(0)
    nb = nbe_ref[e]
    b0 = boff_ref[e]
    total = nbt_ref[0]
    n_blk = y_hbm.shape[0] // (MOE_BLK * PK_CHUNKS)
    xrows = MOE_BLK * PK_CHUNKS
    yrows = MOE_BLK * PK_CHUNKS
    ring = GMLP_RING

    def in_copy(b):
        sl = b & (ring - 1)
        return pltpu.make_async_copy(xs_hbm.at[pl.ds(pl.multiple_of(b * xrows, xrows), xrows), :], xbuf.at[sl],
                                     sem_in.at[sl])

    def out_copy(b):
        sl = b & (ring - 1)
        return pltpu.make_async_copy(ybuf.at[sl], y_hbm.at[pl.ds(pl.multiple_of(b * yrows, yrows), yrows), :],
                                     sem_out.at[sl])

    def zero_copy(b):
        return pltpu.make_async_copy(ybuf.at[0], y_hbm.at[pl.ds(pl.multiple_of(b * yrows, yrows), yrows), :], sem_z)

    ahead = ring - 2

    @pl.when(e == 0)
    def _():
        for b in range(ahead):
            @pl.when(b < total)
            def _():
                in_copy(b).start()

    @pl.when(nb > 0)
    def _():
        w13_s[:, 0:EXP_D] = w1_ref[0].astype(bf16)
        w13_s[:, EXP_D:2 * EXP_D] = w3_ref[0].astype(bf16)
        w2_s[...] = w2_ref[0].astype(bf16)

    def process(b, m):
        for q in range(m):
            in_copy(b + q).wait()
        for q in range(m):
            @pl.when(b + q + ahead < total)
            def _():
                in_copy(b + q + ahead).start()

            @pl.when(b + q >= ring)
            def _():
                out_copy(b + q - ring).wait()

        cols = []
        for c in range(PK_CHUNKS):
            pieces = []
            for q in range(m):
                xb = xbuf.at[(b + q) & (ring - 1)]
                pieces += [xb[pl.ds(8 * PK_CHUNKS * g + c, 8, stride=PK_CHUNKS), :] for g in range(MOE_BLK // 8)]
            cols.append(jnp.concatenate(pieces, axis=0))
        word = jnp.concatenate(cols, axis=1)
        xlo = lax.bitcast_convert_type(lax.shift_left(word, jnp.uint32(16)), f32).astype(bf16)
        xhi = lax.bitcast_convert_type(word & jnp.uint32(0xFFFF0000), f32).astype(bf16)
        h = _dot(xlo, w13_s[0:512, :]) + _dot(xhi, w13_s[512:1024, :])
        hid = (_silu(h[:, 0:EXP_D]) * h[:, EXP_D:2 * EXP_D]).astype(bf16)
        y = _dot(hid, w2_s[...])
        ylo = lax.shift_right_logical(lax.bitcast_convert_type(y[:, 0:512].astype(bf16).astype(f32), u32),
                                      jnp.uint32(16))
        yhi = lax.bitcast_convert_type(y[:, 512:1024].astype(bf16).astype(f32), u32) & jnp.uint32(0xFFFF0000)
        yw = ylo | yhi
        for q in range(m):
            yb = ybuf.at[(b + q) & (ring - 1)]
            for g in range(MOE_BLK // 8):
                r0 = q * MOE_BLK + 8 * g
                for c in range(PK_CHUNKS):
                    yb[pl.ds(8 * PK_CHUNKS * g + c, 8, stride=PK_CHUNKS), :] = yw[r0:r0 + 8, 128 * c:128 * c + 128]
            out_copy(b + q).start()

    def pair(j, carry):
        process(b0 + 2 * j, 2)
        return carry

    lax.fori_loop(0, jnp.right_shift(nb, 1), pair, 0)

    @pl.when((nb & 1) == 1)
    def _():
        process(b0 + nb - 1, 1)

    @pl.when(e == pl.num_programs(0) - 1)
    def _():
        for back in range(ring, 0, -1):
            @pl.when(total >= back)
            def _():
                out_copy(total - back).wait()

        ybuf[0] = jnp.zeros(ybuf.shape[1:], u32)
        lax.fori_loop(total, n_blk, lambda b, c: (zero_copy(b).start(), c)[1], 0)
        lax.fori_loop(total, n_blk, lambda b, c: (zero_copy(b).wait(), c)[1], 0)


def _gmlp_call(layer, nblk_e, blk_off, nb_total, xs, w1, w3, w2):
    n_slots = xs.shape[0] // PK_CHUNKS
    wsel = lambda e, *_: (layer, e, 0, 0)
    gs = pltpu.PrefetchScalarGridSpec(
        num_scalar_prefetch=3,
        grid=(N_EXP,),
        in_specs=[pl.BlockSpec((None, 1, D, EXP_D), wsel), pl.BlockSpec((None, 1, D, EXP_D), wsel),
                  pl.BlockSpec((None, 1, EXP_D, D), wsel), pl.BlockSpec(memory_space=pl.ANY)],
        out_specs=pl.BlockSpec(memory_space=pl.ANY),
        scratch_shapes=[pltpu.VMEM((GMLP_RING, MOE_BLK * PK_CHUNKS, 128), u32),
                        pltpu.VMEM((GMLP_RING, MOE_BLK * PK_CHUNKS, 128), u32),
                        pltpu.VMEM((D, 2 * EXP_D), bf16), pltpu.VMEM((EXP_D, D), bf16),
                        pltpu.SemaphoreType.DMA((GMLP_RING,)), pltpu.SemaphoreType.DMA((GMLP_RING,)),
                        pltpu.SemaphoreType.DMA(())],
    )
    return pl.pallas_call(
        _gmlp_kernel,
        grid_spec=gs,
        out_shape=jax.ShapeDtypeStruct((n_slots * PK_CHUNKS, 128), u32),
        compiler_params=_params(("arbitrary",)),
        name="gmlp",
    )(nblk_e, blk_off, nb_total, w1, w3, w2, xs)


COMBINE_TILE = 256


def _combine_kernel(final, fuse_next, *refs):
    if fuse_next:
        (dest_hbm, y_hbm, x_ref, mod_ref, sh_ref, w_ref, gf_ref, nm_ref, ng_ref, nw_ref, nb_ref, o_ref, z_ref,
         idx_ref, buf_ref, sem_idx, sem_row) = refs
    else:
        dest_hbm, y_hbm, x_ref, mod_ref, sh_ref, w_ref, gf_ref, o_ref, idx_ref, buf_ref, sem_idx, sem_row = refs
    tm = x_ref.shape[1]
    n = tm * TOP_K
    s = pl.program_id(0) * pl.num_programs(1) + pl.program_id(1)
    nsteps = pl.num_programs(0) * pl.num_programs(1)
    last = s == nsteps - 1
    nxt = jnp.minimum(s + 1, nsteps - 1)

    def idx_copy(step):
        return pltpu.make_async_copy(dest_hbm.at[pl.ds(pl.multiple_of(step * n, n), n)],
                                     idx_ref.at[pl.ds(pl.multiple_of((step & 1) * n, n), n)], sem_idx.at[step & 1])

    def row_copy(d, k, t, slot, prio):
        return pltpu.make_async_copy(y_hbm.at[pl.ds(d * PK_CHUNKS, PK_CHUNKS), :],
                                     buf_ref.at[slot, pl.ds((k * tm + t) * PK_CHUNKS, PK_CHUNKS), :],
                                     sem_row.at[slot]).start(priority=prio)

    def rows_wait(slot):
        pltpu.make_async_copy(y_hbm.at[pl.ds(0, n * PK_CHUNKS), :], buf_ref.at[slot], sem_row.at[slot]).wait()

    @pl.when(s == 0)
    def _():
        idx_copy(0).start()
        idx_copy(0).wait()

        def body(t2, carry):
            ds = [idx_ref[t2 * (2 * TOP_K) + j] for j in range(2 * TOP_K)]
            for j in range(2 * TOP_K):
                row_copy(ds[j], j % TOP_K, t2 * 2 + j // TOP_K, 0, j % 2)
            return carry

        lax.fori_loop(0, tm // 2, body, 0)

        @pl.when(nsteps > 1)
        def _():
            idx_copy(1).start()

    @pl.when(s + 1 < nsteps)
    def _():
        idx_copy(s + 1).wait()

    @pl.when(s + 2 < nsteps)
    def _():
        idx_copy(s + 2).start()

    sl = s & 1
    nsl = 1 - sl
    rows_wait(sl)
    bs = buf_ref.at[sl]
    m = mod_ref[0]
    gate = m[5:6]
    nbase = (nxt & 1) * n
    himask = jnp.uint32(0xFFFF0000)

    def group(g, carry):
        r0 = pl.multiple_of(g * 8, 8)
        ds = [idx_ref[nbase + r0 * TOP_K + j] for j in range(8 * TOP_K)]
        wg = w_ref[pl.ds(r0, 8), :]
        lo = [None] * PK_CHUNKS
        hi = [None] * PK_CHUNKS
        for k in range(TOP_K):
            wk = jnp.broadcast_to(wg[:, k:k + 1], (8, 128))
            for c in range(PK_CHUNKS):
                word = bs[pl.ds((k * tm + r0) * PK_CHUNKS + c, 8, stride=PK_CHUNKS), :]
                plo = wk * lax.bitcast_convert_type(lax.shift_left(word, jnp.uint32(16)), f32)
                phi = wk * lax.bitcast_convert_type(word & himask, f32)
                lo[c] = plo if k == 0 else lo[c] + plo
                hi[c] = phi if k == 0 else hi[c] + phi
        routed = jnp.concatenate(lo + hi, axis=1)
        o_ref[0, pl.ds(r0, 8), :] = x_ref[0, pl.ds(r0, 8), :] + gate * (routed + sh_ref[pl.ds(r0, 8), :])
        for j in range(8 * TOP_K):
            row_copy(ds[j], j % TOP_K, r0 + j // TOP_K, nsl, j % 2)
        return carry

    lax.fori_loop(0, tm // 8, group, 0)

    @pl.when(last)
    def _():
        rows_wait(nsl)

    if final:
        o_ref[0] = _rms(o_ref[0], gf_ref[...])
    if fuse_next:
        nm = nm_ref[0]
        h = _rms_mod(o_ref[0], ng_ref[...], nm[0:1], nm[1:2]).astype(bf16)
        z = _dot(h, nw_ref[...]) + nb_ref[...]
        z_ref[0] = z[:, 0:D] * _sigmoid(z[:, D:2 * D])


def _combine_call(dest_flat, y, x, mod, shared, w_tok, g_final, final, nxt=None):
    b, s, _ = x.shape
    tm = COMBINE_TILE
    nt = s // tm
    flat = lambda bb, i: (bb * nt + i, 0)
    row = lambda bb, i: (bb, i, 0)
    vec = lambda bb, i: (0, 0)
    in_specs = [pl.BlockSpec(memory_space=pl.ANY), pl.BlockSpec(memory_space=pl.ANY),
                pl.BlockSpec((1, tm, D), row),
                pl.BlockSpec((1, 6, D), lambda bb, i: (bb, 0, 0)),
                pl.BlockSpec((tm, D), flat),
                pl.BlockSpec((tm, TOP_K), flat),
                pl.BlockSpec((1, D), vec)]
    out_specs = pl.BlockSpec((1, tm, D), row)
    out_shape = jax.ShapeDtypeStruct((b, s, D), f32)
    args = [dest_flat, y, x, mod, shared, w_tok, g_final]
    if nxt is not None:
        in_specs += [pl.BlockSpec((1, 6, D), lambda bb, i: (bb, 0, 0)), pl.BlockSpec((1, D), vec),
                     pl.BlockSpec((D, 2 * D), vec), pl.BlockSpec((1, 2 * D), vec)]
        out_specs = [out_specs, pl.BlockSpec((1, tm, D), row)]
        out_shape = [out_shape, jax.ShapeDtypeStruct((b, s, D), f32)]
        args += list(nxt)
    return pl.pallas_call(
        functools.partial(_combine_kernel, final, nxt is not None),
        grid=(b, nt),
        in_specs=in_specs,
        out_specs=out_specs,
        out_shape=out_shape,
        scratch_shapes=[pltpu.SMEM((2 * tm * TOP_K,), i32), pltpu.VMEM((2, TOP_K * tm * PK_CHUNKS, 128), u32),
                        pltpu.SemaphoreType.DMA((2,)), pltpu.SemaphoreType.DMA((2,))],
        compiler_params=_params(("arbitrary", "arbitrary")),
        name="combine",
    )(*args)


def _moe_layer(layer, x1, mod, norm_g, router_w, router_b, w1, w3, w2, sw1, sw3, sw2, g_final, final, nxt):
    b, s, _ = x1.shape
    t = b * s
    rwt = router_w.T
    rwh = rwt.astype(bf16)
    rwl = (rwt - rwh.astype(f32)).astype(bf16)
    s13 = jnp.concatenate([sw1, sw3], axis=1).astype(bf16)
    h2p, logits_t = _ffn_pre_call(x1, mod, norm_g.reshape(1, D), rwh, rwl)
    tri = (lax.broadcasted_iota(i32, (ROUTE_TILE, ROUTE_TILE), 0)
           < lax.broadcasted_iota(i32, (ROUTE_TILE, ROUTE_TILE), 1)).astype(bf16)
    eidx, w_t, rank, counts = _route_call(logits_t, router_b.reshape(N_EXP, 1).astype(f32), tri)
    cnt = counts.reshape(N_EXP).astype(i32)
    nblk_e = (cnt + MOE_BLK - 1) // MOE_BLK
    blk_ends = jnp.cumsum(nblk_e)
    blk_off = blk_ends - nblk_e
    pad_off = blk_off * MOE_BLK
    nb_total = blk_ends[-1:].astype(i32)
    n_blk = t * TOP_K // MOE_BLK + N_EXP
    dest = _dest_call(eidx, rank, pad_off.astype(f32).reshape(N_EXP, 1))
    dest_flat = dest.T.reshape(t * TOP_K)
    xs, shared = _dispatch_call(cnt, pad_off, nb_total, dest_flat, h2p, s13, sw2.astype(bf16), n_blk * MOE_BLK)
    y = _gmlp_call(layer, nblk_e, blk_off, nb_total, xs, w1, w3, w2)
    return _combine_call(dest_flat, y, x1, mod, shared, w_t.T, g_final.reshape(1, D), final, nxt)


def _rot_cols(w):
    d, n = w.shape
    w4 = w.reshape(d, n // 32, 2, 16)
    return jnp.stack([-w4[:, :, 1], w4[:, :, 0]], axis=2).reshape(d, n)


def _rope_tables(s):
    rows = s // GRID_W
    row = jnp.repeat(jnp.arange(rows, dtype=f32), GRID_W)
    col = jnp.tile(jnp.arange(GRID_W, dtype=f32), rows)
    n_freq = HEAD_DIM // 4
    inv = ROPE_BASE ** (-jnp.arange(n_freq, dtype=f32) / n_freq)
    ang_r = row[:, None] * inv
    ang_c = col[:, None] * inv
    cos = jnp.concatenate([jnp.cos(ang_r)] * 2 + [jnp.cos(ang_c)] * 2, axis=1)
    sin = jnp.concatenate([jnp.sin(ang_r)] * 2 + [jnp.sin(ang_c)] * 2, axis=1)
    return jnp.tile(cos, (1, 2)), jnp.tile(sin, (1, 2))


def _block_diag(w):
    h, dh, _ = w.shape
    eye = jnp.eye(h, dtype=w.dtype)
    return (eye[:, None, :, None] * w[:, :, None, :]).reshape(h * dh, h * dh)


def _even_layer_mixer(x, ctx, mod, norm_g, w_in, w_out, conv_w, conv_b, w_r, b_r, w_i, b_i, lam, sink):
    b, s, _ = x.shape
    r0, r1, r2 = LRU_W, 2 * LRU_W, 2 * LRU_W + Q_W
    wq = w_in[:, r1:r2].reshape(D, 2, 4, HEAD_DIM).transpose(0, 2, 1, 3).reshape(D, Q_W)
    wk = w_in[:, r2:r2 + KV_W]
    w_ext = jnp.concatenate([w_in[:, :r1], wq, w_in[:, r2:], _rot_cols(wq), _rot_cols(wk)], axis=1).astype(bf16)
    w_ctx = jnp.concatenate([w_in[:, :r0], w_in[:, r2:]], axis=1).astype(bf16)
    cos, sin = _rope_tables(s)
    g = norm_g.reshape(1, D)
    u, gt, q, k, v = _proj_in_call(x, mod, g, w_ext, cos, sin)
    uc, kx, vx = _proj_ctx_call(ctx, mod, g, w_ctx)
    n_lg = LRU_W // LRU_LANES
    hpg = LRU_LANES // HEAD_DIM

    def lane_groups(w):
        return jnp.stack([jnp.stack([_block_diag(w[d, g * hpg:(g + 1) * hpg]) for g in range(n_lg)]) for d in range(2)])

    wg = jnp.concatenate([lane_groups(w_r), lane_groups(w_i)], axis=-1).astype(bf16)
    bg = jnp.concatenate([b_r.reshape(2, n_lg, 1, LRU_LANES), b_i.reshape(2, n_lg, 1, LRU_LANES)], axis=-1)
    rec = _rglru_call(u, uc, conv_w, conv_b.reshape(1, LRU_W), wg, bg, lam.reshape(2, n_lg, 1, LRU_LANES))
    att = _attn_call(sink, q, k, v, kx, vx)
    w_att = w_out[LRU_W:].reshape(2, 4, HEAD_DIM, D).transpose(1, 0, 2, 3).reshape(Q_W, D).astype(bf16)
    return _mix_out_call(x, mod, gt, rec, att, w_out[:LRU_W].astype(bf16), w_att)


def kernel(x, c, ctx, c_ctx, mod_w, mod_b, norm_mix_g, norm_ffn_g, final_norm_g, ab_w_in, ab_w_out, lru_conv_w,
           lru_conv_b, lru_wr, lru_br, lru_wi, lru_bi, lru_lambda, attn_sink, cm_w_in, cm_b_in, cm_dw_w, cm_dw_b,
           cm_ln_g, cm_ln_b, cm_w_out, cm_b_out, router_w, router_b, exp_w1, exp_w3, exp_w2, shared_w1, shared_w3,
           shared_w2):
    bsz = x.shape[0]
    depth = mod_w.shape[0]
    assert bsz + 1 <= MOD_ROWS - 7
    cc = jnp.zeros((MOD_ROWS, D), f32).at[:bsz].set(c).at[MOD_ROWS - 8].set(c_ctx)
    mod_all = _mod_call(cc, mod_w, mod_b).reshape(depth, MOD_ROWS, 6, D)
    zg = None
    for l in range(depth):
        mod = mod_all[l]
        last = l == depth - 1
        nxt = None
        if not last and (l + 1) % 2 == 1:
            o1 = (l + 1) // 2
            nxt = (mod_all[l + 1], norm_mix_g[l + 1].reshape(1, D), cm_w_in[o1].astype(bf16),
                   cm_b_in[o1].reshape(1, 2 * D))
        if l % 2 == 0:
            e = l // 2
            assert depth <= 2
            x1 = _even_layer_mixer(x, ctx, mod, norm_mix_g[l], ab_w_in[e], ab_w_out[e], lru_conv_w[e], lru_conv_b[e],
                                   lru_wr[e], lru_br[e], lru_wi[e], lru_bi[e], lru_lambda[e], attn_sink[e])
        else:
            o = l // 2
            if zg is None:
                zg = _conf_in_call(x, mod, norm_mix_g[l].reshape(1, D), cm_w_in[o].astype(bf16),
                                   cm_b_in[o].reshape(1, 2 * D))
            dw = jnp.concatenate([cm_dw_w[o], jnp.zeros((1, D), f32)], axis=0)
            x1 = _conf_out_call(x, mod, zg, dw, cm_dw_b[o].reshape(1, D), cm_ln_g[o].reshape(1, D),
                                cm_ln_b[o].reshape(1, D), cm_w_out[o].astype(bf16), cm_b_out[o].reshape(1, D))
        out = _moe_layer(l, x1, mod, norm_ffn_g[l], router_w[l], router_b[l], exp_w1, exp_w3, exp_w2,
                         shared_w1[l], shared_w3[l], shared_w2[l], final_norm_g, last, nxt)
        x, zg = out if nxt is not None else (out, None)
    return x
```

```python
import functools

import jax
import jax.numpy as jnp
from jax import lax
from jax.experimental import pallas as pl
from jax.experimental.pallas import tpu as pltpu

f32 = jnp.float32
bf16 = jnp.bfloat16
i32 = jnp.int32
u32 = jnp.uint32

D = 1024
EPS = 1e-6
LRU_W = 512
LRU_C = 8.0
N_HEADS = 8
HEAD_DIM = 64
GRID_W = 64
ROPE_BASE = 10000.0
Q_W = 512
KV_W = 128
ATT_BLK = 128
CONV_K = 31
N_EXP = 256
TOP_K = 8
N_GRP = 8
TOPK_GRP = 4
GRP_SZ = N_EXP // N_GRP
EXP_D = 256
ROUTED_SCALE = 2.5
MOE_BLK = 256
PK_CHUNKS = D // 2 // 128

VMEM_LIMIT_V7X = 56 * 1024 * 1024
MOD_ROWS = 24

_NT = (((1,), (1,)), ((), ()))


def _params(sem):
    return pltpu.CompilerParams(dimension_semantics=sem, vmem_limit_bytes=VMEM_LIMIT_V7X)


def _sigmoid(x):
    return 1.0 / (1.0 + jnp.exp(-x))


def _silu(x):
    return x * _sigmoid(x)


def _gelu_tanh(x):
    return 0.5 * x * (1.0 + jnp.tanh(0.7978845608028654 * (x + 0.044715 * (x * x * x))))


def _rms(x, g):
    return x * lax.rsqrt(jnp.mean(x * x, axis=-1, keepdims=True) + EPS) * g


def _rms_mod(x, g, shift, scale):
    return _rms(x, g) * (1.0 + scale) + shift


def _dot(a, b):
    return jnp.dot(a, b, preferred_element_type=f32)


def _mod_kernel(c_ref, w_ref, b_ref, o_ref):
    a = _silu(c_ref[...]).astype(bf16)
    o_ref[0] = _dot(a, w_ref[0].astype(bf16)) + b_ref[0]


def _mod_call(cc, mod_w, mod_b):
    depth, _, n = mod_w.shape
    tn = 1536
    return pl.pallas_call(
        _mod_kernel,
        grid=(depth, n // tn),
        in_specs=[pl.BlockSpec((MOD_ROWS, D), lambda l, j: (0, 0)),
                  pl.BlockSpec((1, D, tn), lambda l, j: (l, 0, j)),
                  pl.BlockSpec((1, 1, tn), lambda l, j: (l, 0, j))],
        out_specs=pl.BlockSpec((1, MOD_ROWS, tn), lambda l, j: (l, 0, j)),
        out_shape=jax.ShapeDtypeStruct((depth, MOD_ROWS, n), f32),
        compiler_params=_params(("parallel", "parallel")),
        name="mod",
    )(cc, mod_w, mod_b.reshape(depth, 1, n))


def _proj_in_kernel(x_ref, mod_ref, g_ref, w_ref, cos_ref, sin_ref, u_ref, gt_ref, q_ref, k_ref, v_ref):
    m = mod_ref[0]
    h = _rms_mod(x_ref[0], g_ref[...], m[0:1], m[1:2]).astype(bf16)
    p = _dot(h, w_ref[...])
    u_ref[0] = p[:, 0:512]
    gt_ref[0] = p[:, 512:1024]
    cos = cos_ref[...]
    sin = sin_ref[...]
    qs = []
    for j in range(4):
        qj = p[:, 1024 + j * 128:1152 + j * 128] * cos + p[:, 1792 + j * 128:1920 + j * 128] * sin
        qs.append(qj * (HEAD_DIM ** -0.5))
    q_ref[0] = jnp.concatenate(qs, axis=1).astype(bf16)
    k_ref[0] = (p[:, 1536:1664] * cos + p[:, 2304:2432] * sin).astype(bf16)
    v_ref[0] = p[:, 1664:1792].astype(bf16)


def _proj_in_call(x, mod, g, w_ext, cos, sin, tm=512):
    b, s, _ = x.shape
    nw = w_ext.shape[1]
    row = lambda bb, i: (bb, i, 0)
    return pl.pallas_call(
        _proj_in_kernel,
        grid=(b, s // tm),
        in_specs=[pl.BlockSpec((1, tm, D), row),
                  pl.BlockSpec((1, 6, D), lambda bb, i: (bb, 0, 0)),
                  pl.BlockSpec((1, D), lambda bb, i: (0, 0)),
                  pl.BlockSpec((D, nw), lambda bb, i: (0, 0)),
                  pl.BlockSpec((tm, 128), lambda bb, i: (i, 0)),
                  pl.BlockSpec((tm, 128), lambda bb, i: (i, 0))],
        out_specs=[pl.BlockSpec((1, tm, LRU_W), row), pl.BlockSpec((1, tm, LRU_W), row),
                   pl.BlockSpec((1, tm, Q_W), row), pl.BlockSpec((1, tm, KV_W), row),
                   pl.BlockSpec((1, tm, KV_W), row)],
        out_shape=[jax.ShapeDtypeStruct((b, s, LRU_W), f32), jax.ShapeDtypeStruct((b, s, LRU_W), f32),
                   jax.ShapeDtypeStruct((b, s, Q_W), bf16), jax.ShapeDtypeStruct((b, s, KV_W), bf16),
                   jax.ShapeDtypeStruct((b, s, KV_W), bf16)],
        compiler_params=_params(("parallel", "parallel")),
        name="proj_in",
    )(x, mod, g, w_ext, cos, sin)


def _proj_ctx_kernel(x_ref, mod_ref, g_ref, w_ref, u_ref, k_ref, v_ref):
    m = mod_ref[0]
    h = _rms_mod(x_ref[0], g_ref[...], m[0:1], m[1:2]).astype(bf16)
    p = _dot(h, w_ref[...])
    u_ref[0] = p[:, 0:512]
    k_ref[0] = p[:, 512:640].astype(bf16)
    v_ref[0] = p[:, 640:768].astype(bf16)


def _proj_ctx_call(ctx, mod, g, w_ctx):
    b, n_ctx, _ = ctx.shape
    row = lambda bb: (bb, 0, 0)
    return pl.pallas_call(
        _proj_ctx_kernel,
        grid=(b,),
        in_specs=[pl.BlockSpec((1, n_ctx, D), row),
                  pl.BlockSpec((1, 6, D), lambda bb: (MOD_ROWS - 8, 0, 0)),
                  pl.BlockSpec((1, D), lambda bb: (0, 0)),
                  pl.BlockSpec((D, 768), lambda bb: (0, 0))],
        out_specs=[pl.BlockSpec((1, n_ctx, LRU_W), row), pl.BlockSpec((1, n_ctx, KV_W), row),
                   pl.BlockSpec((1, n_ctx, KV_W), row)],
        out_shape=[jax.ShapeDtypeStruct((b, n_ctx, LRU_W), f32), jax.ShapeDtypeStruct((b, n_ctx, KV_W), bf16),
                   jax.ShapeDtypeStruct((b, n_ctx, KV_W), bf16)],
        compiler_params=_params(("parallel",)),
        name="proj_ctx",
    )(ctx, mod, g, w_ctx)


LRU_CHUNK = 128
LRU_LANES = 512


def _rglru_kernel(u_ref, uc_ref, cw_ref, cb_ref, wg_ref, bg_ref, lam_ref, o_ref, pad_ref, cx_ref, cc_ref):
    s = u_ref.shape[1]
    n_ctx = uc_ref.shape[1]
    tc = LRU_CHUNK
    lw = LRU_LANES

    def conv_segment(src_ref, n, dst_ref):
        pad_ref[0:8] = jnp.zeros((8, lw), f32)
        pad_ref[8:8 + n] = src_ref[0]
        pad_ref[8 + n:16 + n] = jnp.zeros((8, lw), f32)
        for c in range(n // 256):
            acc = jnp.broadcast_to(cb_ref[...], (256, lw))
            for k in range(4):
                acc = acc + cw_ref[k:k + 1, :] * pad_ref[c * 256 + 6 + k:c * 256 + 6 + k + 256, :]
            dst_ref[c * 256:(c + 1) * 256] = acc

    conv_segment(uc_ref, n_ctx, cc_ref)
    conv_segment(u_ref, s, cx_ref)

    rowm = lax.broadcasted_iota(i32, (tc, lw), 0) & 7

    def scan_segment(src_ref, n, d, h0, write):
        lam = lam_ref[d, 0]
        sp = jnp.maximum(-lam, 0.0) + jnp.log(1.0 + jnp.exp(-jnp.abs(lam)))
        nch = n // tc

        def chunk(ci, h):
            c = ci if d == 0 else nch - 1 - ci
            t0 = pl.multiple_of(c * tc, tc)
            uc = src_ref[pl.ds(t0, tc), :]
            gates = _dot(uc.astype(bf16), wg_ref[d, 0]) + bg_ref[d, 0]
            r = _sigmoid(gates[:, 0:lw])
            ig = _sigmoid(gates[:, lw:2 * lw])
            log_a = (-LRU_C * sp) * r
            a = jnp.exp(log_a)
            bb = jnp.sqrt(-jnp.tanh(log_a) * (a * a + 1.0)) * (ig * uc)
            def shift(v, sh):
                amount = sh if d == 0 else 8 - sh
                return pltpu.roll(v.reshape(tc // 8, 8, lw), amount, 1).reshape(tc, lw)

            for sh in (1, 2, 4):
                keep = rowm >= sh if d == 0 else rowm < 8 - sh
                a_sh = jnp.where(keep, shift(a, sh), 1.0)
                b_sh = jnp.where(keep, shift(bb, sh), 0.0)
                bb = a * b_sh + bb
                a = a * a_sh
            outs = [None] * (tc // 8)
            order = range(tc // 8) if d == 0 else range(tc // 8 - 1, -1, -1)
            for gi in order:
                hg = bb[gi * 8:(gi + 1) * 8] + a[gi * 8:(gi + 1) * 8] * h
                outs[gi] = hg
                h = hg[7:8] if d == 0 else hg[0:1]
            if write:
                hs = jnp.concatenate(outs, axis=0)
                if d == 0:
                    o_ref[0, pl.ds(t0, tc), :] = hs
                else:
                    o_ref[0, pl.ds(t0, tc), :] = o_ref[0, pl.ds(t0, tc), :] + hs
            return h

        return lax.fori_loop(0, nch, chunk, h0)

    for d in range(2):
        h = jnp.zeros((1, lw), f32)
        h = scan_segment(cc_ref, n_ctx, d, h, False)
        scan_segment(cx_ref, s, d, h, True)


def _rglru_call(u, uc, conv_w, conv_b, wg, bg, lam):
    b, s, _ = u.shape
    n_ctx = uc.shape[1]
    lw = LRU_LANES
    return pl.pallas_call(
        _rglru_kernel,
        grid=(b, LRU_W // lw),
        in_specs=[pl.BlockSpec((1, s, lw), lambda bb, g: (bb, 0, g)),
                  pl.BlockSpec((1, n_ctx, lw), lambda bb, g: (bb, 0, g)),
                  pl.BlockSpec((4, lw), lambda bb, g: (0, g)),
                  pl.BlockSpec((1, lw), lambda bb, g: (0, g)),
                  pl.BlockSpec((2, 1, lw, 2 * lw), lambda bb, g: (0, g, 0, 0)),
                  pl.BlockSpec((2, 1, 1, 2 * lw), lambda bb, g: (0, g, 0, 0)),
                  pl.BlockSpec((2, 1, 1, lw), lambda bb, g: (0, g, 0, 0))],
        out_specs=pl.BlockSpec((1, s, lw), lambda bb, g: (bb, 0, g)),
        out_shape=jax.ShapeDtypeStruct((b, s, LRU_W), f32),
        scratch_shapes=[pltpu.VMEM((s + 16, lw), f32), pltpu.VMEM((s, lw), f32), pltpu.VMEM((n_ctx, lw), f32)],
        compiler_params=_params(("parallel", "parallel")),
        name="rglru",
    )(u, uc, conv_w, conv_b, wg, bg, lam)


def _attn_kernel(sink_ref, q_ref, kp_ref, kc_ref, kn_ref, vp_ref, vc_ref, vn_ref, kx_ref, vx_ref, o_ref):
    n = pl.program_id(1)
    nb = pl.num_programs(1)
    blk = ATT_BLK
    q = q_ref[0]
    qall = jnp.concatenate([q[:, j * 128:(j + 1) * 128] for j in range(4)], axis=0)
    kw = jnp.concatenate([kp_ref[0], kc_ref[0], kn_ref[0]], axis=0)
    vw = jnp.concatenate([vp_ref[0], vc_ref[0], vn_ref[0]], axis=0)
    kx = kx_ref[0]
    vx = vx_ref[0]
    lo_w = lax.broadcasted_iota(i32, kw.shape, 1) < HEAD_DIM
    lo_x = lax.broadcasted_iota(i32, kx.shape, 1) < HEAD_DIM
    dj = lax.broadcasted_iota(i32, (4 * blk, blk), 1) - (lax.broadcasted_iota(i32, (4 * blk, blk), 0) & (blk - 1))
    pen_prev = jnp.where(dj >= jnp.where(n > 0, 0, 2 * blk), 0.0, -jnp.inf)
    pen_next = jnp.where(dj <= jnp.where(n < nb - 1, 0, -2 * blk), 0.0, -jnp.inf)
    rb = lax.broadcasted_iota(i32, (4 * blk, 1), 0) // blk
    zero = jnp.zeros((), bf16)
    out = jnp.zeros((4 * blk, 128), f32)
    for half in range(2):
        sel_w = lo_w if half == 0 else jnp.logical_not(lo_w)
        sel_x = lo_x if half == 0 else jnp.logical_not(lo_x)
        s_w = lax.dot_general(qall, jnp.where(sel_w, kw, zero), _NT, preferred_element_type=f32)
        s_w = jnp.concatenate([s_w[:, 0:blk] + pen_prev, s_w[:, blk:2 * blk], s_w[:, 2 * blk:] + pen_next], axis=1)
        s_c = lax.dot_general(qall, jnp.where(sel_x, kx, zero), _NT, preferred_element_type=f32)
        sk = jnp.where(rb == 0, sink_ref[4 * half],
                       jnp.where(rb == 1, sink_ref[4 * half + 1],
                                 jnp.where(rb == 2, sink_ref[4 * half + 2], sink_ref[4 * half + 3])))
        m = jnp.maximum(jnp.maximum(jnp.max(s_w, axis=1, keepdims=True), jnp.max(s_c, axis=1, keepdims=True)), sk)
        p_w = jnp.exp(s_w - m)
        p_c = jnp.exp(s_c - m)
        den = jnp.sum(p_w, axis=1, keepdims=True) + jnp.sum(p_c, axis=1, keepdims=True) + jnp.exp(sk - m)
        o = _dot(p_w.astype(bf16), jnp.where(sel_w, vw, zero)) + _dot(p_c.astype(bf16), jnp.where(sel_x, vx, zero))
        out = out + o / den
    o_ref[0] = jnp.concatenate([out[j * blk:(j + 1) * blk] for j in range(4)], axis=1).astype(bf16)


def _attn_call(sink, q, k, v, kx, vx):
    b, s, _ = q.shape
    n_ctx = kx.shape[1]
    nb = s // ATT_BLK
    cur = lambda bb, n: (bb, n, 0)
    prev = lambda bb, n: (bb, jnp.maximum(n - 1, 0), 0)
    nxt = lambda bb, n: (bb, jnp.minimum(n + 1, nb - 1), 0)
    kvb = (1, ATT_BLK, KV_W)
    return pl.pallas_call(
        _attn_kernel,
        grid=(b, nb),
        in_specs=[pl.BlockSpec(memory_space=pltpu.SMEM),
                  pl.BlockSpec((1, ATT_BLK, Q_W), cur),
                  pl.BlockSpec(kvb, prev), pl.BlockSpec(kvb, cur), pl.BlockSpec(kvb, nxt),
                  pl.BlockSpec(kvb, prev), pl.BlockSpec(kvb, cur), pl.BlockSpec(kvb, nxt),
                  pl.BlockSpec((1, n_ctx, KV_W), lambda bb, n: (bb, 0, 0)),
                  pl.BlockSpec((1, n_ctx, KV_W), lambda bb, n: (bb, 0, 0))],
        out_specs=pl.BlockSpec((1, ATT_BLK, Q_W), cur),
        out_shape=jax.ShapeDtypeStruct((b, s, Q_W), bf16),
        compiler_params=_params(("parallel", "parallel")),
        name="attn",
    )(sink, q, k, k, k, v, v, v, kx, vx)


def _mix_out_kernel(x_ref, mod_ref, gt_ref, rec_ref, att_ref, wr_ref, wa_ref, o_ref):
    m = mod_ref[0]
    a = (_gelu_tanh(gt_ref[0]) * rec_ref[0]).astype(bf16)
    y = _dot(a, wr_ref[...]) + _dot(att_ref[0], wa_ref[...])
    o_ref[0] = x_ref[0] + m[2:3] * y


def _mix_out_call(x, mod, gt, rec, att, w_rec, w_att, tm=512):
    b, s, _ = x.shape
    row = lambda bb, i: (bb, i, 0)
    return pl.pallas_call(
        _mix_out_kernel,
        grid=(b, s // tm),
        in_specs=[pl.BlockSpec((1, tm, D), row),
                  pl.BlockSpec((1, 6, D), lambda bb, i: (bb, 0, 0)),
                  pl.BlockSpec((1, tm, LRU_W), row), pl.BlockSpec((1, tm, LRU_W), row),
                  pl.BlockSpec((1, tm, Q_W), row),
                  pl.BlockSpec((LRU_W, D), lambda bb, i: (0, 0)),
                  pl.BlockSpec((Q_W, D), lambda bb, i: (0, 0))],
        out_specs=pl.BlockSpec((1, tm, D), row),
        out_shape=jax.ShapeDtypeStruct((b, s, D), f32),
        compiler_params=_params(("parallel", "parallel")),
        name="mix_out",
    )(x, mod, gt, rec, att, w_rec, w_att)


def _conf_in_kernel(x_ref, mod_ref, g_ref, w_ref, b_ref, o_ref):
    m = mod_ref[0]
    h = _rms_mod(x_ref[0], g_ref[...], m[0:1], m[1:2]).astype(bf16)
    z = _dot(h, w_ref[...]) + b_ref[...]
    o_ref[0] = z[:, 0:D] * _sigmoid(z[:, D:2 * D])


def _conf_in_call(x, mod, g, w, bias, tm=512):
    b, s, _ = x.shape
    row = lambda bb, i: (bb, i, 0)
    return pl.pallas_call(
        _conf_in_kernel,
        grid=(b, s // tm),
        in_specs=[pl.BlockSpec((1, tm, D), row),
                  pl.BlockSpec((1, 6, D), lambda bb, i: (bb, 0, 0)),
                  pl.BlockSpec((1, D), lambda bb, i: (0, 0)),
                  pl.BlockSpec((D, 2 * D), lambda bb, i: (0, 0)),
                  pl.BlockSpec((1, 2 * D), lambda bb, i: (0, 0))],
        out_specs=pl.BlockSpec((1, tm, D), row),
        out_shape=jax.ShapeDtypeStruct((b, s, D), f32),
        compiler_params=_params(("parallel", "parallel")),
        name="conf_in",
    )(x, mod, g, w, bias)


CONF_HALO = 16
CONF_ROWS = 64
CONF_LANE_PAD = 128


def _conf_out_kernel(x_ref, mod_ref, zc_ref, zp_ref, zn_ref, dw_ref, db_ref, lg_ref, lb_ref, w_ref, b_ref, o_ref,
                     pad_ref, sh_ref, cv_ref):
    i = pl.program_id(1)
    nt = pl.num_programs(1)
    tm = zc_ref.shape[1]
    zero = jnp.zeros((CONF_HALO, D), f32)
    pad_ref[0:CONF_HALO] = jnp.where(i > 0, zp_ref[0], zero)
    pad_ref[CONF_HALO:CONF_HALO + tm] = zc_ref[0]
    pad_ref[CONF_HALO + tm:2 * CONF_HALO + tm] = jnp.where(i < nt - 1, zn_ref[0], zero)
    for r in range(8):
        sh_ref[r, :, 0:D] = pad_ref[r:r + tm + 24, :]

    for lg in range(D // 128):
        l0 = lg * 128
        taps = [dw_ref[k:k + 1, l0:l0 + 128] for k in range(CONV_K)]
        bias = db_ref[:, l0:l0 + 128]

        def chunk(c, carry, l0=l0, taps=taps, bias=bias):
            t0 = pl.multiple_of(c * CONF_ROWS, CONF_ROWS)
            acc = jnp.broadcast_to(bias, (CONF_ROWS, 128))
            for r in range(8):
                win = sh_ref[r, pl.ds(t0, CONF_ROWS + 24), l0:l0 + 128]
                for a in range(4):
                    k = 8 * a + r - 1
                    if 0 <= k < CONV_K:
                        acc = acc + taps[k] * win[8 * a:8 * a + CONF_ROWS]
            cv_ref[pl.ds(t0, CONF_ROWS), l0:l0 + 128] = acc
            return carry

        lax.fori_loop(0, tm // CONF_ROWS, chunk, 0)
    z = cv_ref[...]
    mu = jnp.mean(z, axis=-1, keepdims=True)
    zc = z - mu
    var = jnp.mean(zc * zc, axis=-1, keepdims=True)
    zn = zc * lax.rsqrt(var + EPS) * lg_ref[...] + lb_ref[...]
    y = _dot(_silu(zn).astype(bf16), w_ref[...]) + b_ref[...]
    m = mod_ref[0]
    o_ref[0] = x_ref[0] + m[2:3] * y


def _conf_out_call(x, mod, zg, dw_w, dw_b, ln_g, ln_b, w_out, b_out, tm=256):
    b, s, _ = x.shape
    row = lambda bb, i: (bb, i, 0)
    hb = tm // CONF_HALO
    nh = s // CONF_HALO
    vec = lambda bb, i: (0, 0)
    return pl.pallas_call(
        _conf_out_kernel,
        grid=(b, s // tm),
        in_specs=[pl.BlockSpec((1, tm, D), row),
                  pl.BlockSpec((1, 6, D), lambda bb, i: (bb, 0, 0)),
                  pl.BlockSpec((1, tm, D), row),
                  pl.BlockSpec((1, CONF_HALO, D), lambda bb, i: (bb, jnp.maximum(i * hb - 1, 0), 0)),
                  pl.BlockSpec((1, CONF_HALO, D), lambda bb, i: (bb, jnp.minimum((i + 1) * hb, nh - 1), 0)),
                  pl.BlockSpec((CONV_K + 1, D), vec),
                  pl.BlockSpec((1, D), vec), pl.BlockSpec((1, D), vec), pl.BlockSpec((1, D), vec),
                  pl.BlockSpec((D, D), vec), pl.BlockSpec((1, D), vec)],
        out_specs=pl.BlockSpec((1, tm, D), row),
        out_shape=jax.ShapeDtypeStruct((b, s, D), f32),
        scratch_shapes=[pltpu.VMEM((tm + 2 * CONF_HALO, D), f32), pltpu.VMEM((8, tm + 24, D + CONF_LANE_PAD), f32),
                        pltpu.VMEM((tm, D), f32)],
        compiler_params=_params(("parallel", "parallel")),
        name="conf_out",
    )(x, mod, zg, zg, zg, dw_w, dw_b, ln_g, ln_b, w_out, b_out)


def _ffn_pre_kernel(x_ref, mod_ref, g_ref, rwh_ref, rwl_ref, hp_ref, lg_ref):
    m = mod_ref[0]
    h2 = _rms_mod(x_ref[0], g_ref[...], m[3:4], m[4:5])
    hb = h2.astype(bf16)
    hbf = hb.astype(f32)
    hl = (h2 - hbf).astype(bf16)
    lg_ref[...] = (lax.dot_general(rwh_ref[...], hb, _NT, preferred_element_type=f32)
                   + lax.dot_general(rwh_ref[...], hl, _NT, preferred_element_type=f32)
                   + lax.dot_general(rwl_ref[...], hb, _NT, preferred_element_type=f32))
    lo = lax.shift_right_logical(lax.bitcast_convert_type(hbf[:, 0:512], u32), jnp.uint32(16))
    hi = lax.bitcast_convert_type(hbf[:, 512:1024], u32) & jnp.uint32(0xFFFF0000)
    word = lo | hi
    for i in range(word.shape[0] // 8):
        for c in range(PK_CHUNKS):
            hp_ref[pl.ds(8 * PK_CHUNKS * i + c, 8, stride=PK_CHUNKS), :] = word[8 * i:8 * i + 8, 128 * c:128 * c + 128]


def _ffn_pre_call(x, mod, g, rwh, rwl, tm=512):
    b, s, _ = x.shape
    nt = s // tm
    t = b * s
    flat = lambda bb, i: (bb * nt + i, 0)
    vec = lambda bb, i: (0, 0)
    return pl.pallas_call(
        _ffn_pre_kernel,
        grid=(b, nt),
        in_specs=[pl.BlockSpec((1, tm, D), lambda bb, i: (bb, i, 0)),
                  pl.BlockSpec((1, 6, D), lambda bb, i: (bb, 0, 0)),
                  pl.BlockSpec((1, D), vec),
                  pl.BlockSpec((N_EXP, D), vec), pl.BlockSpec((N_EXP, D), vec)],
        out_specs=[pl.BlockSpec((tm * PK_CHUNKS, 128), flat),
                   pl.BlockSpec((N_EXP, tm), lambda bb, i: (0, bb * nt + i))],
        out_shape=[jax.ShapeDtypeStruct((t * PK_CHUNKS, 128), u32), jax.ShapeDtypeStruct((N_EXP, t), f32)],
        compiler_params=_params(("parallel", "parallel")),
        name="ffn_pre",
    )(x, mod, g, rwh, rwl)


ROUTE_TILE = 256


def _route_kernel(lg_ref, rb_ref, tri_ref, e_ref, w_ref, r_ref, c_ref, base_ref):
    i = pl.program_id(0)
    tr = lg_ref.shape[1]

    @pl.when(i == 0)
    def _():
        base_ref[...] = jnp.zeros_like(base_ref)

    scores = _sigmoid(lg_ref[...])
    biased = scores + rb_ref[...]
    neg = -jnp.inf
    rowf = lax.broadcasted_iota(i32, (N_EXP, tr), 0).astype(f32)
    r32 = lax.broadcasted_iota(i32, (GRP_SZ, tr), 0).astype(f32)
    gs = []
    for g in range(N_GRP):
        seg = biased[g * GRP_SZ:(g + 1) * GRP_SZ]
        m1 = jnp.max(seg, axis=0, keepdims=True)
        i1 = jnp.min(jnp.where(seg == m1, r32, 2.0 * GRP_SZ), axis=0, keepdims=True)
        m2 = jnp.max(jnp.where(r32 == i1, neg, seg), axis=0, keepdims=True)
        gs.append(m1 + m2)
    allowed = []
    for g in range(N_GRP):
        beat = jnp.zeros((1, tr), f32)
        for h in range(N_GRP):
            if h < g:
                beat = beat + jnp.where(gs[h] >= gs[g], 1.0, 0.0)
            elif h > g:
                beat = beat + jnp.where(gs[h] > gs[g], 1.0, 0.0)
        allowed.append(jnp.broadcast_to(beat, (GRP_SZ, tr)))
    allowed = jnp.concatenate(allowed, axis=0)
    masked = jnp.where(allowed < float(TOPK_GRP), biased, neg)
    cnt = jnp.zeros((N_EXP, tr), f32)
    idxs, ws = [], []
    for _ in range(TOP_K):
        m = jnp.max(masked, axis=0, keepdims=True)
        idx = jnp.min(jnp.where(masked == m, rowf, 2.0 * N_EXP), axis=0, keepdims=True)
        hit = rowf == idx
        ws.append(jnp.sum(jnp.where(hit, scores, 0.0), axis=0, keepdims=True))
        masked = jnp.where(hit, neg, masked)
        cnt = cnt + jnp.where(hit, 1.0, 0.0)
        idxs.append(idx)
    wsum = ws[0]
    for k in range(1, TOP_K):
        wsum = wsum + ws[k]
    pos = _dot(cnt.astype(bf16), tri_ref[...]) + base_ref[...]
    ranks = [jnp.sum(jnp.where(rowf == idxs[k], pos, 0.0), axis=0, keepdims=True) for k in range(TOP_K)]
    e_ref[...] = jnp.concatenate(idxs, axis=0).astype(i32)
    w_ref[...] = jnp.concatenate([ROUTED_SCALE * ws[k] / wsum for k in range(TOP_K)], axis=0)
    r_ref[...] = jnp.concatenate(ranks, axis=0).astype(i32)
    base_ref[...] = base_ref[...] + jnp.sum(cnt, axis=1, keepdims=True)
    c_ref[...] = base_ref[...]


def _route_call(logits_t, router_b, tri):
    t = logits_t.shape[1]
    tr = ROUTE_TILE
    col = lambda i: (0, i)
    return pl.pallas_call(
        _route_kernel,
        grid=(t // tr,),
        in_specs=[pl.BlockSpec((N_EXP, tr), col),
                  pl.BlockSpec((N_EXP, 1), lambda i: (0, 0)),
                  pl.BlockSpec((tr, tr), lambda i: (0, 0))],
        out_specs=[pl.BlockSpec((TOP_K, tr), col), pl.BlockSpec((TOP_K, tr), col), pl.BlockSpec((TOP_K, tr), col),
                   pl.BlockSpec((N_EXP, 1), lambda i: (0, 0))],
        out_shape=[jax.ShapeDtypeStruct((TOP_K, t), i32), jax.ShapeDtypeStruct((TOP_K, t), f32),
                   jax.ShapeDtypeStruct((TOP_K, t), i32), jax.ShapeDtypeStruct((N_EXP, 1), f32)],
        scratch_shapes=[pltpu.VMEM((N_EXP, 1), f32)],
        compiler_params=_params(("arbitrary",)),
        name="route",
    )(logits_t, router_b, tri)


def _dest_kernel(e_ref, r_ref, off_ref, d_ref):
    tr = e_ref.shape[1]
    rowi = lax.broadcasted_iota(i32, (N_EXP, tr), 0)
    off = off_ref[...]
    e = e_ref[...]
    rows = [jnp.sum(jnp.where(rowi == e[k:k + 1], off, 0.0), axis=0, keepdims=True) for k in range(TOP_K)]
    d_ref[...] = jnp.concatenate(rows, axis=0).astype(i32) + r_ref[...]


def _dest_call(eidx, rank, pad_off):
    t = eidx.shape[1]
    tr = 512
    col = lambda i: (0, i)
    return pl.pallas_call(
        _dest_kernel,
        grid=(t // tr,),
        in_specs=[pl.BlockSpec((TOP_K, tr), col), pl.BlockSpec((TOP_K, tr), col),
                  pl.BlockSpec((N_EXP, 1), lambda i: (0, 0))],
        out_specs=pl.BlockSpec((TOP_K, tr), col),
        out_shape=jax.ShapeDtypeStruct((TOP_K, t), i32),
        compiler_params=_params(("parallel",)),
        name="dest",
    )(eidx, rank, pad_off)


DISPATCH_TILE = 512
_PAD_PIECES = (128, 64, 32, 16, 8, 4, 2, 1)


DISPATCH_SLOTS = 3


def _dispatch_kernel(cnt_ref, off_ref, nbt_ref, dest_hbm, h_hbm, s13_ref, s2_ref, xs_hbm, sh_ref, idx_ref, hbuf,
                     zero_ref, sem_idx, sem_tile, sem_row, sem_z):
    i = pl.program_id(0)
    nsteps = pl.num_programs(0)
    n = idx_ref.shape[0] // 2
    ts = n // TOP_K
    trows = ts * PK_CHUNKS
    n_blk = xs_hbm.shape[0] // (MOE_BLK * PK_CHUNKS)

    def idx_copy(step):
        return pltpu.make_async_copy(dest_hbm.at[pl.ds(pl.multiple_of(step * n, n), n)],
                                     idx_ref.at[pl.ds(pl.multiple_of((step & 1) * n, n), n)], sem_idx.at[step & 1])

    def tile_copy(step):
        slot = lax.rem(step, DISPATCH_SLOTS)
        return pltpu.make_async_copy(h_hbm.at[pl.ds(pl.multiple_of(step * trows, trows), trows), :], hbuf.at[slot],
                                     sem_tile.at[slot])

    def rows_wait(step):
        pltpu.make_async_copy(xs_hbm.at[pl.ds(0, n * PK_CHUNKS), :], xs_hbm.at[pl.ds(0, n * PK_CHUNKS), :],
                              sem_row.at[lax.rem(step, DISPATCH_SLOTS)]).wait()

    def pad_copy(start_slot, p):
        return pltpu.make_async_copy(zero_ref.at[pl.ds(0, p * PK_CHUNKS), :],
                                     xs_hbm.at[pl.ds(start_slot * PK_CHUNKS, p * PK_CHUNKS), :], sem_z)

    def blk_copy(blk):
        return pltpu.make_async_copy(zero_ref, xs_hbm.at[pl.ds(blk * (MOE_BLK * PK_CHUNKS), MOE_BLK * PK_CHUNKS), :],
                                     sem_z)

    def for_each_pad_piece(fn):
        def per_expert(e, carry):
            c = cnt_ref[e]
            npad = ((c + (MOE_BLK - 1)) & (-MOE_BLK)) - c
            slot = off_ref[e] + c
            for p in _PAD_PIECES:
                @pl.when((npad & p) != 0)
                def _():
                    fn(pad_copy(slot, p))
                slot = slot + (npad & p)
            return carry

        lax.fori_loop(0, N_EXP, per_expert, 0)

    @pl.when(i == 0)
    def _():
        idx_copy(0).start()
        tile_copy(0).start()
        zero_ref[...] = jnp.zeros_like(zero_ref)
        for_each_pad_piece(lambda cp: cp.start())
        lax.fori_loop(nbt_ref[0], n_blk, lambda b, c: (blk_copy(b).start(), c)[1], 0)

    @pl.when(i >= DISPATCH_SLOTS - 1)
    def _():
        rows_wait(i - (DISPATCH_SLOTS - 1))

    @pl.when(i + 1 < nsteps)
    def _():
        idx_copy(i + 1).start()
        tile_copy(i + 1).start()

    idx_copy(i).wait()
    tile_copy(i).wait()
    sl = i & 1
    slot = lax.rem(i, DISPATCH_SLOTS)
    hb = hbuf.at[slot]

    def body(t2, carry):
        base = sl * n + t2 * (2 * TOP_K)
        ds = [idx_ref[base + j] for j in range(2 * TOP_K)]
        for j in range(2 * TOP_K):
            t = t2 * 2 + j // TOP_K
            pltpu.make_async_copy(hb.at[pl.ds(t * PK_CHUNKS, PK_CHUNKS), :],
                                  xs_hbm.at[pl.ds(ds[j] * PK_CHUNKS, PK_CHUNKS), :],
                                  sem_row.at[slot]).start(priority=j % 2)
        return carry

    lax.fori_loop(0, ts // 2, body, 0)

    cols = []
    for c in range(PK_CHUNKS):
        cols.append(jnp.concatenate(
            [hb[pl.ds(8 * PK_CHUNKS * g + c, 8, stride=PK_CHUNKS), :] for g in range(ts // 8)], axis=0))
    word = jnp.concatenate(cols, axis=1)
    xlo = lax.bitcast_convert_type(lax.shift_left(word, jnp.uint32(16)), f32).astype(bf16)
    xhi = lax.bitcast_convert_type(word & jnp.uint32(0xFFFF0000), f32).astype(bf16)
    a = _dot(xlo, s13_ref[0:512, :]) + _dot(xhi, s13_ref[512:1024, :])
    hid = (_silu(a[:, 0:EXP_D]) * a[:, EXP_D:2 * EXP_D]).astype(bf16)
    sh_ref[...] = _dot(hid, s2_ref[...])

    @pl.when(i == nsteps - 1)
    def _():
        for back in range(DISPATCH_SLOTS - 2, -1, -1):
            @pl.when(i >= back)
            def _():
                rows_wait(i - back)

        for_each_pad_piece(lambda cp: cp.wait())
        lax.fori_loop(nbt_ref[0], n_blk, lambda b, c: (blk_copy(b).wait(), c)[1], 0)


def _dispatch_call(cnt, pad_off, nb_total, dest_flat, h2p, s13, s2, n_slots):
    t = h2p.shape[0] // PK_CHUNKS
    ts = DISPATCH_TILE
    vec = lambda i, *_: (0, 0)
    gs = pltpu.PrefetchScalarGridSpec(
        num_scalar_prefetch=3,
        grid=(t // ts,),
        in_specs=[pl.BlockSpec(memory_space=pl.ANY), pl.BlockSpec(memory_space=pl.ANY),
                  pl.BlockSpec((D, 2 * EXP_D), vec), pl.BlockSpec((EXP_D, D), vec)],
        out_specs=[pl.BlockSpec(memory_space=pl.ANY), pl.BlockSpec((ts, D), lambda i, *_: (i, 0))],
        scratch_shapes=[pltpu.SMEM((2 * ts * TOP_K,), i32), pltpu.VMEM((DISPATCH_SLOTS, ts * PK_CHUNKS, 128), u32),
                        pltpu.VMEM((MOE_BLK * PK_CHUNKS, 128), u32),
                        pltpu.SemaphoreType.DMA((2,)), pltpu.SemaphoreType.DMA((DISPATCH_SLOTS,)),
                        pltpu.SemaphoreType.DMA((DISPATCH_SLOTS,)), pltpu.SemaphoreType.DMA(())],
    )
    return pl.pallas_call(
        _dispatch_kernel,
        grid_spec=gs,
        out_shape=[jax.ShapeDtypeStruct((n_slots * PK_CHUNKS, 128), u32), jax.ShapeDtypeStruct((t, D), f32)],
        compiler_params=_params(("arbitrary",)),
        name="dispatch",
    )(cnt, pad_off, nb_total, dest_flat, h2p, s13, s2)


GMLP_RING = 8


def _gmlp_kernel(nbe_ref, boff_ref, nbt_ref, w1_ref, w3_ref, w2_ref, xs_hbm, y_hbm, xbuf, ybuf, w13_s, w2_s,
                 sem_in, sem_out, sem_z):
    e = pl.program_id(0)
    nb = nbe_ref[e]
    b0 = boff_ref[e]
    total = nbt_ref[0]
    n_blk = y_hbm.shape[0] // (MOE_BLK * PK_CHUNKS)
    xrows = MOE_BLK * PK_CHUNKS
    yrows = MOE_BLK * PK_CHUNKS
    ring = GMLP_RING

    def in_copy(b):
        sl = b & (ring - 1)
        return pltpu.make_async_copy(xs_hbm.at[pl.ds(pl.multiple_of(b * xrows, xrows), xrows), :], xbuf.at[sl],
                                     sem_in.at[sl])

    def out_copy(b):
        sl = b & (ring - 1)
        return pltpu.make_async_copy(ybuf.at[sl], y_hbm.at[pl.ds(pl.multiple_of(b * yrows, yrows), yrows), :],
                                     sem_out.at[sl])

    def zero_copy(b):
        return pltpu.make_async_copy(ybuf.at[0], y_hbm.at[pl.ds(pl.multiple_of(b * yrows, yrows), yrows), :], sem_z)

    ahead = ring - 2

    @pl.when(e == 0)
    def _():
        for b in range(ahead):
            @pl.when(b < total)
            def _():
                in_copy(b).start()

    @pl.when(nb > 0)
    def _():
        w13_s[:, 0:EXP_D] = w1_ref[0].astype(bf16)
        w13_s[:, EXP_D:2 * EXP_D] = w3_ref[0].astype(bf16)
        w2_s[...] = w2_ref[0].astype(bf16)

    def process(b, m):
        for q in range(m):
            in_copy(b + q).wait()
        for q in range(m):
            @pl.when(b + q + ahead < total)
            def _():
                in_copy(b + q + ahead).start()

            @pl.when(b + q >= ring)
            def _():
                out_copy(b + q - ring).wait()

        cols = []
        for c in range(PK_CHUNKS):
            pieces = []
            for q in range(m):
                xb = xbuf.at[(b + q) & (ring - 1)]
                pieces += [xb[pl.ds(8 * PK_CHUNKS * g + c, 8, stride=PK_CHUNKS), :] for g in range(MOE_BLK // 8)]
            cols.append(jnp.concatenate(pieces, axis=0))
        word = jnp.concatenate(cols, axis=1)
        xlo = lax.bitcast_convert_type(lax.shift_left(word, jnp.uint32(16)), f32).astype(bf16)
        xhi = lax.bitcast_convert_type(word & jnp.uint32(0xFFFF0000), f32).astype(bf16)
        h = _dot(xlo, w13_s[0:512, :]) + _dot(xhi, w13_s[512:1024, :])
        hid = (_silu(h[:, 0:EXP_D]) * h[:, EXP_D:2 * EXP_D]).astype(bf16)
        y = _dot(hid, w2_s[...])
        ylo = lax.shift_right_logical(lax.bitcast_convert_type(y[:, 0:512].astype(bf16).astype(f32), u32),
                                      jnp.uint32(16))
        yhi = lax.bitcast_convert_type(y[:, 512:1024].astype(bf16).astype(f32), u32) & jnp.uint32(0xFFFF0000)
        yw = ylo | yhi
        for q in range(m):
            yb = ybuf.at[(b + q) & (ring - 1)]
            for g in range(MOE_BLK // 8):
                r0 = q * MOE_BLK + 8 * g
                for c in range(PK_CHUNKS):
                    yb[pl.ds(8 * PK_CHUNKS * g + c, 8, stride=PK_CHUNKS), :] = yw[r0:r0 + 8, 128 * c:128 * c + 128]
            out_copy(b + q).start()

    def pair(j, carry):
        process(b0 + 2 * j, 2)
        return carry

    lax.fori_loop(0, jnp.right_shift(nb, 1), pair, 0)

    @pl.when((nb & 1) == 1)
    def _():
        process(b0 + nb - 1, 1)

    @pl.when(e == pl.num_programs(0) - 1)
    def _():
        for back in range(ring, 0, -1):
            @pl.when(total >= back)
            def _():
                out_copy(total - back).wait()

        ybuf[0] = jnp.zeros(ybuf.shape[1:], u32)
        lax.fori_loop(total, n_blk, lambda b, c: (zero_copy(b).start(), c)[1], 0)
        lax.fori_loop(total, n_blk, lambda b, c: (zero_copy(b).wait(), c)[1], 0)


def _gmlp_call(layer, nblk_e, blk_off, nb_total, xs, w1, w3, w2):
    n_slots = xs.shape[0] // PK_CHUNKS
    wsel = lambda e, *_: (layer, e, 0, 0)
    gs = pltpu.PrefetchScalarGridSpec(
        num_scalar_prefetch=3,
        grid=(N_EXP,),
        in_specs=[pl.BlockSpec((None, 1, D, EXP_D), wsel), pl.BlockSpec((None, 1, D, EXP_D), wsel),
                  pl.BlockSpec((None, 1, EXP_D, D), wsel), pl.BlockSpec(memory_space=pl.ANY)],
        out_specs=pl.BlockSpec(memory_space=pl.ANY),
        scratch_shapes=[pltpu.VMEM((GMLP_RING, MOE_BLK * PK_CHUNKS, 128), u32),
                        pltpu.VMEM((GMLP_RING, MOE_BLK * PK_CHUNKS, 128), u32),
                        pltpu.VMEM((D, 2 * EXP_D), bf16), pltpu.VMEM((EXP_D, D), bf16),
                        pltpu.SemaphoreType.DMA((GMLP_RING,)), pltpu.SemaphoreType.DMA((GMLP_RING,)),
                        pltpu.SemaphoreType.DMA(())],
    )
    return pl.pallas_call(
        _gmlp_kernel,
        grid_spec=gs,
        out_shape=jax.ShapeDtypeStruct((n_slots * PK_CHUNKS, 128), u32),
        compiler_params=_params(("arbitrary",)),
        name="gmlp",
    )(nblk_e, blk_off, nb_total, w1, w3, w2, xs)


COMBINE_TILE = 512


def _combine_kernel(final, fuse_next, *refs):
    if fuse_next:
        (dest_hbm, y_hbm, x_ref, mod_ref, sh_ref, w_ref, gf_ref, nm_ref, ng_ref, nw_ref, nb_ref, o_ref, z_ref,
         idx_ref, buf_ref, sem_idx, sem_row) = refs
    else:
        dest_hbm, y_hbm, x_ref, mod_ref, sh_ref, w_ref, gf_ref, o_ref, idx_ref, buf_ref, sem_idx, sem_row = refs
    tm = x_ref.shape[1]
    n = tm * TOP_K
    s = pl.program_id(0) * pl.num_programs(1) + pl.program_id(1)
    nsteps = pl.num_programs(0) * pl.num_programs(1)
    last = s == nsteps - 1
    nxt = jnp.minimum(s + 1, nsteps - 1)

    def idx_copy(step):
        return pltpu.make_async_copy(dest_hbm.at[pl.ds(pl.multiple_of(step * n, n), n)],
                                     idx_ref.at[pl.ds(pl.multiple_of((step & 1) * n, n), n)], sem_idx.at[step & 1])

    def row_copy(d, k, t, slot, prio):
        return pltpu.make_async_copy(y_hbm.at[pl.ds(d * PK_CHUNKS, PK_CHUNKS), :],
                                     buf_ref.at[slot, pl.ds((k * tm + t) * PK_CHUNKS, PK_CHUNKS), :],
                                     sem_row.at[slot]).start(priority=prio)

    def rows_wait(slot):
        pltpu.make_async_copy(y_hbm.at[pl.ds(0, n * PK_CHUNKS), :], buf_ref.at[slot], sem_row.at[slot]).wait()

    @pl.when(s == 0)
    def _():
        idx_copy(0).start()
        idx_copy(0).wait()

        def body(t2, carry):
            ds = [idx_ref[t2 * (2 * TOP_K) + j] for j in range(2 * TOP_K)]
            for j in range(2 * TOP_K):
                row_copy(ds[j], j % TOP_K, t2 * 2 + j // TOP_K, 0, j % 2)
            return carry

        lax.fori_loop(0, tm // 2, body, 0)

        @pl.when(nsteps > 1)
        def _():
            idx_copy(1).start()

    @pl.when(s + 1 < nsteps)
    def _():
        idx_copy(s + 1).wait()

    @pl.when(s + 2 < nsteps)
    def _():
        idx_copy(s + 2).start()

    sl = s & 1
    nsl = 1 - sl
    rows_wait(sl)
    bs = buf_ref.at[sl]
    m = mod_ref[0]
    gate = m[5:6]
    nbase = (nxt & 1) * n
    himask = jnp.uint32(0xFFFF0000)

    def group(g, carry):
        r0 = pl.multiple_of(g * 8, 8)
        ds = [idx_ref[nbase + r0 * TOP_K + j] for j in range(8 * TOP_K)]
        wg = w_ref[pl.ds(r0, 8), :]
        lo = [None] * PK_CHUNKS
        hi = [None] * PK_CHUNKS
        for k in range(TOP_K):
            wk = jnp.broadcast_to(wg[:, k:k + 1], (8, 128))
            for c in range(PK_CHUNKS):
                word = bs[pl.ds((k * tm + r0) * PK_CHUNKS + c, 8, stride=PK_CHUNKS), :]
                plo = wk * lax.bitcast_convert_type(lax.shift_left(word, jnp.uint32(16)), f32)
                phi = wk * lax.bitcast_convert_type(word & himask, f32)
                lo[c] = plo if k == 0 else lo[c] + plo
                hi[c] = phi if k == 0 else hi[c] + phi
        routed = jnp.concatenate(lo + hi, axis=1)
        o_ref[0, pl.ds(r0, 8), :] = x_ref[0, pl.ds(r0, 8), :] + gate * (routed + sh_ref[pl.ds(r0, 8), :])
        for j in range(8 * TOP_K):
            row_copy(ds[j], j % TOP_K, r0 + j // TOP_K, nsl, j % 2)
        return carry

    lax.fori_loop(0, tm // 8, group, 0)

    @pl.when(last)
    def _():
        rows_wait(nsl)

    if final:
        o_ref[0] = _rms(o_ref[0], gf_ref[...])
    if fuse_next:
        nm = nm_ref[0]
        h = _rms_mod(o_ref[0], ng_ref[...], nm[0:1], nm[1:2]).astype(bf16)
        z = _dot(h, nw_ref[...]) + nb_ref[...]
        z_ref[0] = z[:, 0:D] * _sigmoid(z[:, D:2 * D])


def _combine_call(dest_flat, y, x, mod, shared, w_tok, g_final, final, nxt=None):
    b, s, _ = x.shape
    tm = COMBINE_TILE
    nt = s // tm
    flat = lambda bb, i: (bb * nt + i, 0)
    row = lambda bb, i: (bb, i, 0)
    vec = lambda bb, i: (0, 0)
    in_specs = [pl.BlockSpec(memory_space=pl.ANY), pl.BlockSpec(memory_space=pl.ANY),
                pl.BlockSpec((1, tm, D), row),
                pl.BlockSpec((1, 6, D), lambda bb, i: (bb, 0, 0)),
                pl.BlockSpec((tm, D), flat),
                pl.BlockSpec((tm, TOP_K), flat),
                pl.BlockSpec((1, D), vec)]
    out_specs = pl.BlockSpec((1, tm, D), row)
    out_shape = jax.ShapeDtypeStruct((b, s, D), f32)
    args = [dest_flat, y, x, mod, shared, w_tok, g_final]
    if nxt is not None:
        in_specs += [pl.BlockSpec((1, 6, D), lambda bb, i: (bb, 0, 0)), pl.BlockSpec((1, D), vec),
                     pl.BlockSpec((D, 2 * D), vec), pl.BlockSpec((1, 2 * D), vec)]
        out_specs = [out_specs, pl.BlockSpec((1, tm, D), row)]
        out_shape = [out_shape, jax.ShapeDtypeStruct((b, s, D), f32)]
        args += list(nxt)
    return pl.pallas_call(
        functools.partial(_combine_kernel, final, nxt is not None),
        grid=(b, nt),
        in_specs=in_specs,
        out_specs=out_specs,
        out_shape=out_shape,
        scratch_shapes=[pltpu.SMEM((2 * tm * TOP_K,), i32), pltpu.VMEM((2, TOP_K * tm * PK_CHUNKS, 128), u32),
                        pltpu.SemaphoreType.DMA((2,)), pltpu.SemaphoreType.DMA((2,))],
        compiler_params=_params(("arbitrary", "arbitrary")),
        name="combine",
    )(*args)


def _moe_layer(layer, x1, mod, norm_g, router_w, router_b, w1, w3, w2, sw1, sw3, sw2, g_final, final, nxt):
    b, s, _ = x1.shape
    t = b * s
    rwt = router_w.T
    rwh = rwt.astype(bf16)
    rwl = (rwt - rwh.astype(f32)).astype(bf16)
    s13 = jnp.concatenate([sw1, sw3], axis=1).astype(bf16)
    h2p, logits_t = _ffn_pre_call(x1, mod, norm_g.reshape(1, D), rwh, rwl)
    tri = (lax.broadcasted_iota(i32, (ROUTE_TILE, ROUTE_TILE), 0)
           < lax.broadcasted_iota(i32, (ROUTE_TILE, ROUTE_TILE), 1)).astype(bf16)
    eidx, w_t, rank, counts = _route_call(logits_t, router_b.reshape(N_EXP, 1).astype(f32), tri)
    cnt = counts.reshape(N_EXP).astype(i32)
    nblk_e = (cnt + MOE_BLK - 1) // MOE_BLK
    blk_ends = jnp.cumsum(nblk_e)
    blk_off = blk_ends - nblk_e
    pad_off = blk_off * MOE_BLK
    nb_total = blk_ends[-1:].astype(i32)
    n_blk = t * TOP_K // MOE_BLK + N_EXP
    dest = _dest_call(eidx, rank, pad_off.astype(f32).reshape(N_EXP, 1))
    dest_flat = dest.T.reshape(t * TOP_K)
    xs, shared = _dispatch_call(cnt, pad_off, nb_total, dest_flat, h2p, s13, sw2.astype(bf16), n_blk * MOE_BLK)
    y = _gmlp_call(layer, nblk_e, blk_off, nb_total, xs, w1, w3, w2)
    return _combine_call(dest_flat, y, x1, mod, shared, w_t.T, g_final.reshape(1, D), final, nxt)


def _rot_cols(w):
    d, n = w.shape
    w4 = w.reshape(d, n // 32, 2, 16)
    return jnp.stack([-w4[:, :, 1], w4[:, :, 0]], axis=2).reshape(d, n)


def _rope_tables(s):
    rows = s // GRID_W
    row = jnp.repeat(jnp.arange(rows, dtype=f32), GRID_W)
    col = jnp.tile(jnp.arange(GRID_W, dtype=f32), rows)
    n_freq = HEAD_DIM // 4
    inv = ROPE_BASE ** (-jnp.arange(n_freq, dtype=f32) / n_freq)
    ang_r = row[:, None] * inv
    ang_c = col[:, None] * inv
    cos = jnp.concatenate([jnp.cos(ang_r)] * 2 + [jnp.cos(ang_c)] * 2, axis=1)
    sin = jnp.concatenate([jnp.sin(ang_r)] * 2 + [jnp.sin(ang_c)] * 2, axis=1)
    return jnp.tile(cos, (1, 2)), jnp.tile(sin, (1, 2))


def _block_diag(w):
    h, dh, _ = w.shape
    eye = jnp.eye(h, dtype=w.dtype)
    return (eye[:, None, :, None] * w[:, :, None, :]).reshape(h * dh, h * dh)


def _even_layer_mixer(x, ctx, mod, norm_g, w_in, w_out, conv_w, conv_b, w_r, b_r, w_i, b_i, lam, sink):
    b, s, _ = x.shape
    r0, r1, r2 = LRU_W, 2 * LRU_W, 2 * LRU_W + Q_W
    wq = w_in[:, r1:r2].reshape(D, 2, 4, HEAD_DIM).transpose(0, 2, 1, 3).reshape(D, Q_W)
    wk = w_in[:, r2:r2 + KV_W]
    w_ext = jnp.concatenate([w_in[:, :r1], wq, w_in[:, r2:], _rot_cols(wq), _rot_cols(wk)], axis=1).astype(bf16)
    w_ctx = jnp.concatenate([w_in[:, :r0], w_in[:, r2:]], axis=1).astype(bf16)
    cos, sin = _rope_tables(s)
    g = norm_g.reshape(1, D)
    u, gt, q, k, v = _proj_in_call(x, mod, g, w_ext, cos, sin)
    uc, kx, vx = _proj_ctx_call(ctx, mod, g, w_ctx)
    n_lg = LRU_W // LRU_LANES
    hpg = LRU_LANES // HEAD_DIM

    def lane_groups(w):
        return jnp.stack([jnp.stack([_block_diag(w[d, g * hpg:(g + 1) * hpg]) for g in range(n_lg)]) for d in range(2)])

    wg = jnp.concatenate([lane_groups(w_r), lane_groups(w_i)], axis=-1).astype(bf16)
    bg = jnp.concatenate([b_r.reshape(2, n_lg, 1, LRU_LANES), b_i.reshape(2, n_lg, 1, LRU_LANES)], axis=-1)
    rec = _rglru_call(u, uc, conv_w, conv_b.reshape(1, LRU_W), wg, bg, lam.reshape(2, n_lg, 1, LRU_LANES))
    att = _attn_call(sink, q, k, v, kx, vx)
    w_att = w_out[LRU_W:].reshape(2, 4, HEAD_DIM, D).transpose(1, 0, 2, 3).reshape(Q_W, D).astype(bf16)
    return _mix_out_call(x, mod, gt, rec, att, w_out[:LRU_W].astype(bf16), w_att)


def kernel(x, c, ctx, c_ctx, mod_w, mod_b, norm_mix_g, norm_ffn_g, final_norm_g, ab_w_in, ab_w_out, lru_conv_w,
           lru_conv_b, lru_wr, lru_br, lru_wi, lru_bi, lru_lambda, attn_sink, cm_w_in, cm_b_in, cm_dw_w, cm_dw_b,
           cm_ln_g, cm_ln_b, cm_w_out, cm_b_out, router_w, router_b, exp_w1, exp_w3, exp_w2, shared_w1, shared_w3,
           shared_w2):
    bsz = x.shape[0]
    depth = mod_w.shape[0]
    assert bsz + 1 <= MOD_ROWS - 7
    cc = jnp.zeros((MOD_ROWS, D), f32).at[:bsz].set(c).at[MOD_ROWS - 8].set(c_ctx)
    mod_all = _mod_call(cc, mod_w, mod_b).reshape(depth, MOD_ROWS, 6, D)
    zg = None
    for l in range(depth):
        mod = mod_all[l]
        last = l == depth - 1
        nxt = None
        if not last and (l + 1) % 2 == 1:
            o1 = (l + 1) // 2
            nxt = (mod_all[l + 1], norm_mix_g[l + 1].reshape(1, D), cm_w_in[o1].astype(bf16),
                   cm_b_in[o1].reshape(1, 2 * D))
        if l % 2 == 0:
            e = l // 2
            assert depth <= 2
            x1 = _even_layer_mixer(x, ctx, mod, norm_mix_g[l], ab_w_in[e], ab_w_out[e], lru_conv_w[e], lru_conv_b[e],
                                   lru_wr[e], lru_br[e], lru_wi[e], lru_bi[e], lru_lambda[e], attn_sink[e])
        else:
            o = l // 2
            if zg is None:
                zg = _conf_in_call(x, mod, norm_mix_g[l].reshape(1, D), cm_w_in[o].astype(bf16),
                                   cm_b_in[o].reshape(1, 2 * D))
            dw = jnp.concatenate([cm_dw_w[o], jnp.zeros((1, D), f32)], axis=0)
            x1 = _conf_out_call(x, mod, zg, dw, cm_dw_b[o].reshape(1, D), cm_ln_g[o].reshape(1, D),
                                cm_ln_b[o].reshape(1, D), cm_w_out[o].astype(bf16), cm_b_out[o].reshape(1, D))
        out = _moe_layer(l, x1, mod, norm_ffn_g[l], router_w[l], router_b[l], exp_w1, exp_w3, exp_w2,
                         shared_w1[l], shared_w3[l], shared_w2[l], final_norm_g, last, nxt)
        x, zg = out if nxt is not None else (out, None)
    return x
```

```python
import functools

import jax
import jax.numpy as jnp
from jax import lax
from jax.experimental import pallas as pl
from jax.experimental.pallas import tpu as pltpu

f32 = jnp.float32
bf16 = jnp.bfloat16
i32 = jnp.int32
u32 = jnp.uint32

D = 1024
EPS = 1e-6
LRU_W = 512
LRU_C = 8.0
N_HEADS = 8
HEAD_DIM = 64
GRID_W = 64
ROPE_BASE = 10000.0
Q_W = 512
KV_W = 128
ATT_BLK = 128
CONV_K = 31
N_EXP = 256
TOP_K = 8
N_GRP = 8
TOPK_GRP = 4
GRP_SZ = N_EXP // N_GRP
EXP_D = 256
ROUTED_SCALE = 2.5
MOE_BLK = 256
PK_CHUNKS = D // 2 // 128

VMEM_LIMIT_V7X = 56 * 1024 * 1024
MOD_ROWS = 24

_NT = (((1,), (1,)), ((), ()))


def _params(sem):
    return pltpu.CompilerParams(dimension_semantics=sem, vmem_limit_bytes=VMEM_LIMIT_V7X)


def _sigmoid(x):
    return 1.0 / (1.0 + jnp.exp(-x))


def _silu(x):
    return x * _sigmoid(x)


def _gelu_tanh(x):
    return 0.5 * x * (1.0 + jnp.tanh(0.7978845608028654 * (x + 0.044715 * (x * x * x))))


def _rms(x, g):
    return x * lax.rsqrt(jnp.mean(x * x, axis=-1, keepdims=True) + EPS) * g


def _rms_mod(x, g, shift, scale):
    return _rms(x, g) * (1.0 + scale) + shift


def _dot(a, b):
    return jnp.dot(a, b, preferred_element_type=f32)


def _mod_kernel(c_ref, w_ref, b_ref, o_ref):
    a = _silu(c_ref[...]).astype(bf16)
    o_ref[0] = _dot(a, w_ref[0].astype(bf16)) + b_ref[0]


def _mod_call(cc, mod_w, mod_b):
    depth, _, n = mod_w.shape
    tn = 1536
    return pl.pallas_call(
        _mod_kernel,
        grid=(depth, n // tn),
        in_specs=[pl.BlockSpec((MOD_ROWS, D), lambda l, j: (0, 0)),
                  pl.BlockSpec((1, D, tn), lambda l, j: (l, 0, j)),
                  pl.BlockSpec((1, 1, tn), lambda l, j: (l, 0, j))],
        out_specs=pl.BlockSpec((1, MOD_ROWS, tn), lambda l, j: (l, 0, j)),
        out_shape=jax.ShapeDtypeStruct((depth, MOD_ROWS, n), f32),
        compiler_params=_params(("parallel", "parallel")),
        name="mod",
    )(cc, mod_w, mod_b.reshape(depth, 1, n))


def _proj_in_kernel(x_ref, mod_ref, g_ref, w_ref, cos_ref, sin_ref, u_ref, gt_ref, q_ref, k_ref, v_ref):
    m = mod_ref[0]
    h = _rms_mod(x_ref[0], g_ref[...], m[0:1], m[1:2]).astype(bf16)
    p = _dot(h, w_ref[...])
    u_ref[0] = p[:, 0:512]
    gt_ref[0] = p[:, 512:1024]
    cos = cos_ref[...]
    sin = sin_ref[...]
    qs = []
    for j in range(4):
        qj = p[:, 1024 + j * 128:1152 + j * 128] * cos + p[:, 1792 + j * 128:1920 + j * 128] * sin
        qs.append(qj * (HEAD_DIM ** -0.5))
    q_ref[0] = jnp.concatenate(qs, axis=1).astype(bf16)
    k_ref[0] = (p[:, 1536:1664] * cos + p[:, 2304:2432] * sin).astype(bf16)
    v_ref[0] = p[:, 1664:1792].astype(bf16)


def _proj_in_call(x, mod, g, w_ext, cos, sin, tm=512):
    b, s, _ = x.shape
    nw = w_ext.shape[1]
    row = lambda bb, i: (bb, i, 0)
    return pl.pallas_call(
        _proj_in_kernel,
        grid=(b, s // tm),
        in_specs=[pl.BlockSpec((1, tm, D), row),
                  pl.BlockSpec((1, 6, D), lambda bb, i: (bb, 0, 0)),
                  pl.BlockSpec((1, D), lambda bb, i: (0, 0)),
                  pl.BlockSpec((D, nw), lambda bb, i: (0, 0)),
                  pl.BlockSpec((tm, 128), lambda bb, i: (i, 0)),
                  pl.BlockSpec((tm, 128), lambda bb, i: (i, 0))],
        out_specs=[pl.BlockSpec((1, tm, LRU_W), row), pl.BlockSpec((1, tm, LRU_W), row),
                   pl.BlockSpec((1, tm, Q_W), row), pl.BlockSpec((1, tm, KV_W), row),
                   pl.BlockSpec((1, tm, KV_W), row)],
        out_shape=[jax.ShapeDtypeStruct((b, s, LRU_W), f32), jax.ShapeDtypeStruct((b, s, LRU_W), f32),
                   jax.ShapeDtypeStruct((b, s, Q_W), bf16), jax.ShapeDtypeStruct((b, s, KV_W), bf16),
                   jax.ShapeDtypeStruct((b, s, KV_W), bf16)],
        compiler_params=_params(("parallel", "parallel")),
        name="proj_in",
    )(x, mod, g, w_ext, cos, sin)


def _proj_ctx_kernel(x_ref, mod_ref, g_ref, w_ref, u_ref, k_ref, v_ref):
    m = mod_ref[0]
    h = _rms_mod(x_ref[0], g_ref[...], m[0:1], m[1:2]).astype(bf16)
    p = _dot(h, w_ref[...])
    u_ref[0] = p[:, 0:512]
    k_ref[0] = p[:, 512:640].astype(bf16)
    v_ref[0] = p[:, 640:768].astype(bf16)


def _proj_ctx_call(ctx, mod, g, w_ctx):
    b, n_ctx, _ = ctx.shape
    row = lambda bb: (bb, 0, 0)
    return pl.pallas_call(
        _proj_ctx_kernel,
        grid=(b,),
        in_specs=[pl.BlockSpec((1, n_ctx, D), row),
                  pl.BlockSpec((1, 6, D), lambda bb: (MOD_ROWS - 8, 0, 0)),
                  pl.BlockSpec((1, D), lambda bb: (0, 0)),
                  pl.BlockSpec((D, 768), lambda bb: (0, 0))],
        out_specs=[pl.BlockSpec((1, n_ctx, LRU_W), row), pl.BlockSpec((1, n_ctx, KV_W), row),
                   pl.BlockSpec((1, n_ctx, KV_W), row)],
        out_shape=[jax.ShapeDtypeStruct((b, n_ctx, LRU_W), f32), jax.ShapeDtypeStruct((b, n_ctx, KV_W), bf16),
                   jax.ShapeDtypeStruct((b, n_ctx, KV_W), bf16)],
        compiler_params=_params(("parallel",)),
        name="proj_ctx",
    )(ctx, mod, g, w_ctx)


LRU_CHUNK = 128
LRU_LANES = 512


def _rglru_kernel(u_ref, uc_ref, cw_ref, cb_ref, wg_ref, bg_ref, lam_ref, o_ref, pad_ref, cx_ref, cc_ref):
    s = u_ref.shape[1]
    n_ctx = uc_ref.shape[1]
    tc = LRU_CHUNK
    lw = LRU_LANES

    def conv_segment(src_ref, n, dst_ref):
        pad_ref[0:8] = jnp.zeros((8, lw), f32)
        pad_ref[8:8 + n] = src_ref[0]
        pad_ref[8 + n:16 + n] = jnp.zeros((8, lw), f32)
        for c in range(n // 256):
            acc = jnp.broadcast_to(cb_ref[...], (256, lw))
            for k in range(4):
                acc = acc + cw_ref[k:k + 1, :] * pad_ref[c * 256 + 6 + k:c * 256 + 6 + k + 256, :]
            dst_ref[c * 256:(c + 1) * 256] = acc

    conv_segment(uc_ref, n_ctx, cc_ref)
    conv_segment(u_ref, s, cx_ref)

    rowm = lax.broadcasted_iota(i32, (tc, lw), 0) & 7

    def scan_segment(src_ref, n, d, h0, write):
        lam = lam_ref[d, 0]
        sp = jnp.maximum(-lam, 0.0) + jnp.log(1.0 + jnp.exp(-jnp.abs(lam)))
        nch = n // tc

        def chunk(ci, h):
            c = ci if d == 0 else nch - 1 - ci
            t0 = pl.multiple_of(c * tc, tc)
            uc = src_ref[pl.ds(t0, tc), :]
            gates = _dot(uc.astype(bf16), wg_ref[d, 0]) + bg_ref[d, 0]
            r = _sigmoid(gates[:, 0:lw])
            ig = _sigmoid(gates[:, lw:2 * lw])
            log_a = (-LRU_C * sp) * r
            a = jnp.exp(log_a)
            bb = jnp.sqrt(-jnp.tanh(log_a) * (a * a + 1.0)) * (ig * uc)
            def shift(v, sh):
                amount = sh if d == 0 else 8 - sh
                return pltpu.roll(v.reshape(tc // 8, 8, lw), amount, 1).reshape(tc, lw)

            for sh in (1, 2, 4):
                keep = rowm >= sh if d == 0 else rowm < 8 - sh
                a_sh = jnp.where(keep, shift(a, sh), 1.0)
                b_sh = jnp.where(keep, shift(bb, sh), 0.0)
                bb = a * b_sh + bb
                a = a * a_sh
            outs = [None] * (tc // 8)
            order = range(tc // 8) if d == 0 else range(tc // 8 - 1, -1, -1)
            for gi in order:
                hg = bb[gi * 8:(gi + 1) * 8] + a[gi * 8:(gi + 1) * 8] * h
                outs[gi] = hg
                h = hg[7:8] if d == 0 else hg[0:1]
            if write:
                hs = jnp.concatenate(outs, axis=0)
                if d == 0:
                    o_ref[0, pl.ds(t0, tc), :] = hs
                else:
                    o_ref[0, pl.ds(t0, tc), :] = o_ref[0, pl.ds(t0, tc), :] + hs
            return h

        return lax.fori_loop(0, nch, chunk, h0)

    for d in range(2):
        h = jnp.zeros((1, lw), f32)
        h = scan_segment(cc_ref, n_ctx, d, h, False)
        scan_segment(cx_ref, s, d, h, True)


def _rglru_call(u, uc, conv_w, conv_b, wg, bg, lam):
    b, s, _ = u.shape
    n_ctx = uc.shape[1]
    lw = LRU_LANES
    return pl.pallas_call(
        _rglru_kernel,
        grid=(b, LRU_W // lw),
        in_specs=[pl.BlockSpec((1, s, lw), lambda bb, g: (bb, 0, g)),
                  pl.BlockSpec((1, n_ctx, lw), lambda bb, g: (bb, 0, g)),
                  pl.BlockSpec((4, lw), lambda bb, g: (0, g)),
                  pl.BlockSpec((1, lw), lambda bb, g: (0, g)),
                  pl.BlockSpec((2, 1, lw, 2 * lw), lambda bb, g: (0, g, 0, 0)),
                  pl.BlockSpec((2, 1, 1, 2 * lw), lambda bb, g: (0, g, 0, 0)),
                  pl.BlockSpec((2, 1, 1, lw), lambda bb, g: (0, g, 0, 0))],
        out_specs=pl.BlockSpec((1, s, lw), lambda bb, g: (bb, 0, g)),
        out_shape=jax.ShapeDtypeStruct((b, s, LRU_W), f32),
        scratch_shapes=[pltpu.VMEM((s + 16, lw), f32), pltpu.VMEM((s, lw), f32), pltpu.VMEM((n_ctx, lw), f32)],
        compiler_params=_params(("parallel", "parallel")),
        name="rglru",
    )(u, uc, conv_w, conv_b, wg, bg, lam)


def _attn_kernel(sink_ref, q_ref, kp_ref, kc_ref, kn_ref, vp_ref, vc_ref, vn_ref, kx_ref, vx_ref, o_ref):
    n = pl.program_id(1)
    nb = pl.num_programs(1)
    blk = ATT_BLK
    q = q_ref[0]
    qall = jnp.concatenate([q[:, j * 128:(j + 1) * 128] for j in range(4)], axis=0)
    kw = jnp.concatenate([kp_ref[0], kc_ref[0], kn_ref[0]], axis=0)
    vw = jnp.concatenate([vp_ref[0], vc_ref[0], vn_ref[0]], axis=0)
    kx = kx_ref[0]
    vx = vx_ref[0]
    lo_w = lax.broadcasted_iota(i32, kw.shape, 1) < HEAD_DIM
    lo_x = lax.broadcasted_iota(i32, kx.shape, 1) < HEAD_DIM
    dj = lax.broadcasted_iota(i32, (4 * blk, blk), 1) - (lax.broadcasted_iota(i32, (4 * blk, blk), 0) & (blk - 1))
    pen_prev = jnp.where(dj >= jnp.where(n > 0, 0, 2 * blk), 0.0, -jnp.inf)
    pen_next = jnp.where(dj <= jnp.where(n < nb - 1, 0, -2 * blk), 0.0, -jnp.inf)
    rb = lax.broadcasted_iota(i32, (4 * blk, 1), 0) // blk
    zero = jnp.zeros((), bf16)
    out = jnp.zeros((4 * blk, 128), f32)
    for half in range(2):
        sel_w = lo_w if half == 0 else jnp.logical_not(lo_w)
        sel_x = lo_x if half == 0 else jnp.logical_not(lo_x)
        s_w = lax.dot_general(qall, jnp.where(sel_w, kw, zero), _NT, preferred_element_type=f32)
        s_w = jnp.concatenate([s_w[:, 0:blk] + pen_prev, s_w[:, blk:2 * blk], s_w[:, 2 * blk:] + pen_next], axis=1)
        s_c = lax.dot_general(qall, jnp.where(sel_x, kx, zero), _NT, preferred_element_type=f32)
        sk = jnp.where(rb == 0, sink_ref[4 * half],
                       jnp.where(rb == 1, sink_ref[4 * half + 1],
                                 jnp.where(rb == 2, sink_ref[4 * half + 2], sink_ref[4 * half + 3])))
        m = jnp.maximum(jnp.maximum(jnp.max(s_w, axis=1, keepdims=True), jnp.max(s_c, axis=1, keepdims=True)), sk)
        p_w = jnp.exp(s_w - m)
        p_c = jnp.exp(s_c - m)
        den = jnp.sum(p_w, axis=1, keepdims=True) + jnp.sum(p_c, axis=1, keepdims=True) + jnp.exp(sk - m)
        o = _dot(p_w.astype(bf16), jnp.where(sel_w, vw, zero)) + _dot(p_c.astype(bf16), jnp.where(sel_x, vx, zero))
        out = out + o / den
    o_ref[0] = jnp.concatenate([out[j * blk:(j + 1) * blk] for j in range(4)], axis=1).astype(bf16)


def _attn_call(sink, q, k, v, kx, vx):
    b, s, _ = q.shape
    n_ctx = kx.shape[1]
    nb = s // ATT_BLK
    cur = lambda bb, n: (bb, n, 0)
    prev = lambda bb, n: (bb, jnp.maximum(n - 1, 0), 0)
    nxt = lambda bb, n: (bb, jnp.minimum(n + 1, nb - 1), 0)
    kvb = (1, ATT_BLK, KV_W)
    return pl.pallas_call(
        _attn_kernel,
        grid=(b, nb),
        in_specs=[pl.BlockSpec(memory_space=pltpu.SMEM),
                  pl.BlockSpec((1, ATT_BLK, Q_W), cur),
                  pl.BlockSpec(kvb, prev), pl.BlockSpec(kvb, cur), pl.BlockSpec(kvb, nxt),
                  pl.BlockSpec(kvb, prev), pl.BlockSpec(kvb, cur), pl.BlockSpec(kvb, nxt),
                  pl.BlockSpec((1, n_ctx, KV_W), lambda bb, n: (bb, 0, 0)),
                  pl.BlockSpec((1, n_ctx, KV_W), lambda bb, n: (bb, 0, 0))],
        out_specs=pl.BlockSpec((1, ATT_BLK, Q_W), cur),
        out_shape=jax.ShapeDtypeStruct((b, s, Q_W), bf16),
        compiler_params=_params(("parallel", "parallel")),
        name="attn",
    )(sink, q, k, k, k, v, v, v, kx, vx)


def _mix_out_kernel(x_ref, mod_ref, gt_ref, rec_ref, att_ref, wr_ref, wa_ref, o_ref):
    m = mod_ref[0]
    a = (_gelu_tanh(gt_ref[0]) * rec_ref[0]).astype(bf16)
    y = _dot(a, wr_ref[...]) + _dot(att_ref[0], wa_ref[...])
    o_ref[0] = x_ref[0] + m[2:3] * y


def _mix_out_call(x, mod, gt, rec, att, w_rec, w_att, tm=512):
    b, s, _ = x.shape
    row = lambda bb, i: (bb, i, 0)
    return pl.pallas_call(
        _mix_out_kernel,
        grid=(b, s // tm),
        in_specs=[pl.BlockSpec((1, tm, D), row),
                  pl.BlockSpec((1, 6, D), lambda bb, i: (bb, 0, 0)),
                  pl.BlockSpec((1, tm, LRU_W), row), pl.BlockSpec((1, tm, LRU_W), row),
                  pl.BlockSpec((1, tm, Q_W), row),
                  pl.BlockSpec((LRU_W, D), lambda bb, i: (0, 0)),
                  pl.BlockSpec((Q_W, D), lambda bb, i: (0, 0))],
        out_specs=pl.BlockSpec((1, tm, D), row),
        out_shape=jax.ShapeDtypeStruct((b, s, D), f32),
        compiler_params=_params(("parallel", "parallel")),
        name="mix_out",
    )(x, mod, gt, rec, att, w_rec, w_att)


def _conf_in_kernel(x_ref, mod_ref, g_ref, w_ref, b_ref, o_ref):
    m = mod_ref[0]
    h = _rms_mod(x_ref[0], g_ref[...], m[0:1], m[1:2]).astype(bf16)
    z = _dot(h, w_ref[...]) + b_ref[...]
    o_ref[0] = z[:, 0:D] * _sigmoid(z[:, D:2 * D])


def _conf_in_call(x, mod, g, w, bias, tm=512):
    b, s, _ = x.shape
    row = lambda bb, i: (bb, i, 0)
    return pl.pallas_call(
        _conf_in_kernel,
        grid=(b, s // tm),
        in_specs=[pl.BlockSpec((1, tm, D), row),
                  pl.BlockSpec((1, 6, D), lambda bb, i: (bb, 0, 0)),
                  pl.BlockSpec((1, D), lambda bb, i: (0, 0)),
                  pl.BlockSpec((D, 2 * D), lambda bb, i: (0, 0)),
                  pl.BlockSpec((1, 2 * D), lambda bb, i: (0, 0))],
        out_specs=pl.BlockSpec((1, tm, D), row),
        out_shape=jax.ShapeDtypeStruct((b, s, D), f32),
        compiler_params=_params(("parallel", "parallel")),
        name="conf_in",
    )(x, mod, g, w, bias)


CONF_HALO = 16
CONF_ROWS = 64
CONF_LANE_PAD = 128


def _conf_out_kernel(x_ref, mod_ref, zc_ref, zp_ref, zn_ref, dw_ref, db_ref, lg_ref, lb_ref, w_ref, b_ref, o_ref,
                     pad_ref, sh_ref, cv_ref):
    i = pl.program_id(1)
    nt = pl.num_programs(1)
    tm = zc_ref.shape[1]
    zero = jnp.zeros((CONF_HALO, D), f32)
    pad_ref[0:CONF_HALO] = jnp.where(i > 0, zp_ref[0], zero)
    pad_ref[CONF_HALO:CONF_HALO + tm] = zc_ref[0]
    pad_ref[CONF_HALO + tm:2 * CONF_HALO + tm] = jnp.where(i < nt - 1, zn_ref[0], zero)
    for r in range(8):
        sh_ref[r, :, 0:D] = pad_ref[r:r + tm + 24, :]

    for lg in range(D // 128):
        l0 = lg * 128
        taps = [dw_ref[k:k + 1, l0:l0 + 128] for k in range(CONV_K)]
        bias = db_ref[:, l0:l0 + 128]

        def chunk(c, carry, l0=l0, taps=taps, bias=bias):
            t0 = pl.multiple_of(c * CONF_ROWS, CONF_ROWS)
            acc = jnp.broadcast_to(bias, (CONF_ROWS, 128))
            for r in range(8):
                win = sh_ref[r, pl.ds(t0, CONF_ROWS + 24), l0:l0 + 128]
                for a in range(4):
                    k = 8 * a + r - 1
                    if 0 <= k < CONV_K:
                        acc = acc + taps[k] * win[8 * a:8 * a + CONF_ROWS]
            cv_ref[pl.ds(t0, CONF_ROWS), l0:l0 + 128] = acc
            return carry

        lax.fori_loop(0, tm // CONF_ROWS, chunk, 0)
    z = cv_ref[...]
    mu = jnp.mean(z, axis=-1, keepdims=True)
    zc = z - mu
    var = jnp.mean(zc * zc, axis=-1, keepdims=True)
    zn = zc * lax.rsqrt(var + EPS) * lg_ref[...] + lb_ref[...]
    y = _dot(_silu(zn).astype(bf16), w_ref[...]) + b_ref[...]
    m = mod_ref[0]
    o_ref[0] = x_ref[0] + m[2:3] * y


def _conf_out_call(x, mod, zg, dw_w, dw_b, ln_g, ln_b, w_out, b_out, tm=256):
    b, s, _ = x.shape
    row = lambda bb, i: (bb, i, 0)
    hb = tm // CONF_HALO
    nh = s // CONF_HALO
    vec = lambda bb, i: (0, 0)
    return pl.pallas_call(
        _conf_out_kernel,
        grid=(b, s // tm),
        in_specs=[pl.BlockSpec((1, tm, D), row),
                  pl.BlockSpec((1, 6, D), lambda bb, i: (bb, 0, 0)),
                  pl.BlockSpec((1, tm, D), row),
                  pl.BlockSpec((1, CONF_HALO, D), lambda bb, i: (bb, jnp.maximum(i * hb - 1, 0), 0)),
                  pl.BlockSpec((1, CONF_HALO, D), lambda bb, i: (bb, jnp.minimum((i + 1) * hb, nh - 1), 0)),
                  pl.BlockSpec((CONV_K + 1, D), vec),
                  pl.BlockSpec((1, D), vec), pl.BlockSpec((1, D), vec), pl.BlockSpec((1, D), vec),
                  pl.BlockSpec((D, D), vec), pl.BlockSpec((1, D), vec)],
        out_specs=pl.BlockSpec((1, tm, D), row),
        out_shape=jax.ShapeDtypeStruct((b, s, D), f32),
        scratch_shapes=[pltpu.VMEM((tm + 2 * CONF_HALO, D), f32), pltpu.VMEM((8, tm + 24, D + CONF_LANE_PAD), f32),
                        pltpu.VMEM((tm, D), f32)],
        compiler_params=_params(("parallel", "parallel")),
        name="conf_out",
    )(x, mod, zg, zg, zg, dw_w, dw_b, ln_g, ln_b, w_out, b_out)


def _ffn_pre_kernel(x_ref, mod_ref, g_ref, rwh_ref, rwl_ref, hp_ref, lg_ref):
    m = mod_ref[0]
    h2 = _rms_mod(x_ref[0], g_ref[...], m[3:4], m[4:5])
    hb = h2.astype(bf16)
    hbf = hb.astype(f32)
    hl = (h2 - hbf).astype(bf16)
    lg_ref[...] = (lax.dot_general(rwh_ref[...], hb, _NT, preferred_element_type=f32)
                   + lax.dot_general(rwh_ref[...], hl, _NT, preferred_element_type=f32)
                   + lax.dot_general(rwl_ref[...], hb, _NT, preferred_element_type=f32))
    lo = lax.shift_right_logical(lax.bitcast_convert_type(hbf[:, 0:512], u32), jnp.uint32(16))
    hi = lax.bitcast_convert_type(hbf[:, 512:1024], u32) & jnp.uint32(0xFFFF0000)
    word = lo | hi
    for i in range(word.shape[0] // 8):
        for c in range(PK_CHUNKS):
            hp_ref[pl.ds(8 * PK_CHUNKS * i + c, 8, stride=PK_CHUNKS), :] = word[8 * i:8 * i + 8, 128 * c:128 * c + 128]


def _ffn_pre_call(x, mod, g, rwh, rwl, tm=512):
    b, s, _ = x.shape
    nt = s // tm
    t = b * s
    flat = lambda bb, i: (bb * nt + i, 0)
    vec = lambda bb, i: (0, 0)
    return pl.pallas_call(
        _ffn_pre_kernel,
        grid=(b, nt),
        in_specs=[pl.BlockSpec((1, tm, D), lambda bb, i: (bb, i, 0)),
                  pl.BlockSpec((1, 6, D), lambda bb, i: (bb, 0, 0)),
                  pl.BlockSpec((1, D), vec),
                  pl.BlockSpec((N_EXP, D), vec), pl.BlockSpec((N_EXP, D), vec)],
        out_specs=[pl.BlockSpec((tm * PK_CHUNKS, 128), flat),
                   pl.BlockSpec((N_EXP, tm), lambda bb, i: (0, bb * nt + i))],
        out_shape=[jax.ShapeDtypeStruct((t * PK_CHUNKS, 128), u32), jax.ShapeDtypeStruct((N_EXP, t), f32)],
        compiler_params=_params(("parallel", "parallel")),
        name="ffn_pre",
    )(x, mod, g, rwh, rwl)


ROUTE_TILE = 256


def _route_kernel(lg_ref, rb_ref, tri_ref, e_ref, w_ref, r_ref, c_ref, base_ref):
    i = pl.program_id(0)
    tr = lg_ref.shape[1]

    @pl.when(i == 0)
    def _():
        base_ref[...] = jnp.zeros_like(base_ref)

    scores = _sigmoid(lg_ref[...])
    biased = scores + rb_ref[...]
    neg = -jnp.inf
    rowf = lax.broadcasted_iota(i32, (N_EXP, tr), 0).astype(f32)
    r32 = lax.broadcasted_iota(i32, (GRP_SZ, tr), 0).astype(f32)
    gs = []
    for g in range(N_GRP):
        seg = biased[g * GRP_SZ:(g + 1) * GRP_SZ]
        m1 = jnp.max(seg, axis=0, keepdims=True)
        i1 = jnp.min(jnp.where(seg == m1, r32, 2.0 * GRP_SZ), axis=0, keepdims=True)
        m2 = jnp.max(jnp.where(r32 == i1, neg, seg), axis=0, keepdims=True)
        gs.append(m1 + m2)
    allowed = []
    for g in range(N_GRP):
        beat = jnp.zeros((1, tr), f32)
        for h in range(N_GRP):
            if h < g:
                beat = beat + jnp.where(gs[h] >= gs[g], 1.0, 0.0)
            elif h > g:
                beat = beat + jnp.where(gs[h] > gs[g], 1.0, 0.0)
        allowed.append(jnp.broadcast_to(beat, (GRP_SZ, tr)))
    allowed = jnp.concatenate(allowed, axis=0)
    masked = jnp.where(allowed < float(TOPK_GRP), biased, neg)
    cnt = jnp.zeros((N_EXP, tr), f32)
    idxs, ws = [], []
    for _ in range(TOP_K):
        m = jnp.max(masked, axis=0, keepdims=True)
        idx = jnp.min(jnp.where(masked == m, rowf, 2.0 * N_EXP), axis=0, keepdims=True)
        hit = rowf == idx
        ws.append(jnp.sum(jnp.where(hit, scores, 0.0), axis=0, keepdims=True))
        masked = jnp.where(hit, neg, masked)
        cnt = cnt + jnp.where(hit, 1.0, 0.0)
        idxs.append(idx)
    wsum = ws[0]
    for k in range(1, TOP_K):
        wsum = wsum + ws[k]
    pos = _dot(cnt.astype(bf16), tri_ref[...]) + base_ref[...]
    ranks = [jnp.sum(jnp.where(rowf == idxs[k], pos, 0.0), axis=0, keepdims=True) for k in range(TOP_K)]
    e_ref[...] = jnp.concatenate(idxs, axis=0).astype(i32)
    w_ref[...] = jnp.concatenate([ROUTED_SCALE * ws[k] / wsum for k in range(TOP_K)], axis=0)
    r_ref[...] = jnp.concatenate(ranks, axis=0).astype(i32)
    base_ref[...] = base_ref[...] + jnp.sum(cnt, axis=1, keepdims=True)
    c_ref[...] = base_ref[...]


def _route_call(logits_t, router_b, tri):
    t = logits_t.shape[1]
    tr = ROUTE_TILE
    col = lambda i: (0, i)
    return pl.pallas_call(
        _route_kernel,
        grid=(t // tr,),
        in_specs=[pl.BlockSpec((N_EXP, tr), col),
                  pl.BlockSpec((N_EXP, 1), lambda i: (0, 0)),
                  pl.BlockSpec((tr, tr), lambda i: (0, 0))],
        out_specs=[pl.BlockSpec((TOP_K, tr), col), pl.BlockSpec((TOP_K, tr), col), pl.BlockSpec((TOP_K, tr), col),
                   pl.BlockSpec((N_EXP, 1), lambda i: (0, 0))],
        out_shape=[jax.ShapeDtypeStruct((TOP_K, t), i32), jax.ShapeDtypeStruct((TOP_K, t), f32),
                   jax.ShapeDtypeStruct((TOP_K, t), i32), jax.ShapeDtypeStruct((N_EXP, 1), f32)],
        scratch_shapes=[pltpu.VMEM((N_EXP, 1), f32)],
        compiler_params=_params(("arbitrary",)),
        name="route",
    )(logits_t, router_b, tri)


def _dest_kernel(e_ref, r_ref, off_ref, d_ref):
    tr = e_ref.shape[1]
    rowi = lax.broadcasted_iota(i32, (N_EXP, tr), 0)
    off = off_ref[...]
    e = e_ref[...]
    rows = [jnp.sum(jnp.where(rowi == e[k:k + 1], off, 0.0), axis=0, keepdims=True) for k in range(TOP_K)]
    d_ref[...] = jnp.concatenate(rows, axis=0).astype(i32) + r_ref[...]


def _dest_call(eidx, rank, pad_off):
    t = eidx.shape[1]
    tr = 512
    col = lambda i: (0, i)
    return pl.pallas_call(
        _dest_kernel,
        grid=(t // tr,),
        in_specs=[pl.BlockSpec((TOP_K, tr), col), pl.BlockSpec((TOP_K, tr), col),
                  pl.BlockSpec((N_EXP, 1), lambda i: (0, 0))],
        out_specs=pl.BlockSpec((TOP_K, tr), col),
        out_shape=jax.ShapeDtypeStruct((TOP_K, t), i32),
        compiler_params=_params(("parallel",)),
        name="dest",
    )(eidx, rank, pad_off)


DISPATCH_TILE = 1024
_PAD_PIECES = (128, 64, 32, 16, 8, 4, 2, 1)


DISPATCH_SLOTS = 3


def _dispatch_kernel(cnt_ref, off_ref, nbt_ref, dest_hbm, h_hbm, s13_ref, s2_ref, xs_hbm, sh_ref, idx_ref, hbuf,
                     zero_ref, sem_idx, sem_tile, sem_row, sem_z):
    i = pl.program_id(0)
    nsteps = pl.num_programs(0)
    n = idx_ref.shape[0] // 2
    ts = n // TOP_K
    trows = ts * PK_CHUNKS
    n_blk = xs_hbm.shape[0] // (MOE_BLK * PK_CHUNKS)

    def idx_copy(step):
        return pltpu.make_async_copy(dest_hbm.at[pl.ds(pl.multiple_of(step * n, n), n)],
                                     idx_ref.at[pl.ds(pl.multiple_of((step & 1) * n, n), n)], sem_idx.at[step & 1])

    def tile_copy(step):
        slot = lax.rem(step, DISPATCH_SLOTS)
        return pltpu.make_async_copy(h_hbm.at[pl.ds(pl.multiple_of(step * trows, trows), trows), :], hbuf.at[slot],
                                     sem_tile.at[slot])

    def rows_wait(step):
        pltpu.make_async_copy(xs_hbm.at[pl.ds(0, n * PK_CHUNKS), :], xs_hbm.at[pl.ds(0, n * PK_CHUNKS), :],
                              sem_row.at[lax.rem(step, DISPATCH_SLOTS)]).wait()

    def pad_copy(start_slot, p):
        return pltpu.make_async_copy(zero_ref.at[pl.ds(0, p * PK_CHUNKS), :],
                                     xs_hbm.at[pl.ds(start_slot * PK_CHUNKS, p * PK_CHUNKS), :], sem_z)

    def blk_copy(blk):
        return pltpu.make_async_copy(zero_ref, xs_hbm.at[pl.ds(blk * (MOE_BLK * PK_CHUNKS), MOE_BLK * PK_CHUNKS), :],
                                     sem_z)

    def for_each_pad_piece(fn):
        def per_expert(e, carry):
            c = cnt_ref[e]
            npad = ((c + (MOE_BLK - 1)) & (-MOE_BLK)) - c
            slot = off_ref[e] + c
            for p in _PAD_PIECES:
                @pl.when((npad & p) != 0)
                def _():
                    fn(pad_copy(slot, p))
                slot = slot + (npad & p)
            return carry

        lax.fori_loop(0, N_EXP, per_expert, 0)

    @pl.when(i == 0)
    def _():
        idx_copy(0).start()
        tile_copy(0).start()
        zero_ref[...] = jnp.zeros_like(zero_ref)
        for_each_pad_piece(lambda cp: cp.start())
        lax.fori_loop(nbt_ref[0], n_blk, lambda b, c: (blk_copy(b).start(), c)[1], 0)

    @pl.when(i >= DISPATCH_SLOTS - 1)
    def _():
        rows_wait(i - (DISPATCH_SLOTS - 1))

    @pl.when(i + 1 < nsteps)
    def _():
        idx_copy(i + 1).start()
        tile_copy(i + 1).start()

    idx_copy(i).wait()
    tile_copy(i).wait()
    sl = i & 1
    slot = lax.rem(i, DISPATCH_SLOTS)
    hb = hbuf.at[slot]

    def body(t2, carry):
        base = sl * n + t2 * (2 * TOP_K)
        ds = [idx_ref[base + j] for j in range(2 * TOP_K)]
        for j in range(2 * TOP_K):
            t = t2 * 2 + j // TOP_K
            pltpu.make_async_copy(hb.at[pl.ds(t * PK_CHUNKS, PK_CHUNKS), :],
                                  xs_hbm.at[pl.ds(ds[j] * PK_CHUNKS, PK_CHUNKS), :],
                                  sem_row.at[slot]).start(priority=j % 2)
        return carry

    lax.fori_loop(0, ts // 2, body, 0)

    cols = []
    for c in range(PK_CHUNKS):
        cols.append(jnp.concatenate(
            [hb[pl.ds(8 * PK_CHUNKS * g + c, 8, stride=PK_CHUNKS), :] for g in range(ts // 8)], axis=0))
    word = jnp.concatenate(cols, axis=1)
    xlo = lax.bitcast_convert_type(lax.shift_left(word, jnp.uint32(16)), f32).astype(bf16)
    xhi = lax.bitcast_convert_type(word & jnp.uint32(0xFFFF0000), f32).astype(bf16)
    a = _dot(xlo, s13_ref[0:512, :]) + _dot(xhi, s13_ref[512:1024, :])
    hid = (_silu(a[:, 0:EXP_D]) * a[:, EXP_D:2 * EXP_D]).astype(bf16)
    sh_ref[...] = _dot(hid, s2_ref[...])

    @pl.when(i == nsteps - 1)
    def _():
        for back in range(DISPATCH_SLOTS - 2, -1, -1):
            @pl.when(i >= back)
            def _():
                rows_wait(i - back)

        for_each_pad_piece(lambda cp: cp.wait())
        lax.fori_loop(nbt_ref[0], n_blk, lambda b, c: (blk_copy(b).wait(), c)[1], 0)


def _dispatch_call(cnt, pad_off, nb_total, dest_flat, h2p, s13, s2, n_slots):
    t = h2p.shape[0] // PK_CHUNKS
    ts = DISPATCH_TILE
    vec = lambda i, *_: (0, 0)
    gs = pltpu.PrefetchScalarGridSpec(
        num_scalar_prefetch=3,
        grid=(t // ts,),
        in_specs=[pl.BlockSpec(memory_space=pl.ANY), pl.BlockSpec(memory_space=pl.ANY),
                  pl.BlockSpec((D, 2 * EXP_D), vec), pl.BlockSpec((EXP_D, D), vec)],
        out_specs=[pl.BlockSpec(memory_space=pl.ANY), pl.BlockSpec((ts, D), lambda i, *_: (i, 0))],
        scratch_shapes=[pltpu.SMEM((2 * ts * TOP_K,), i32), pltpu.VMEM((DISPATCH_SLOTS, ts * PK_CHUNKS, 128), u32),
                        pltpu.VMEM((MOE_BLK * PK_CHUNKS, 128), u32),
                        pltpu.SemaphoreType.DMA((2,)), pltpu.SemaphoreType.DMA((DISPATCH_SLOTS,)),
                        pltpu.SemaphoreType.DMA((DISPATCH_SLOTS,)), pltpu.SemaphoreType.DMA(())],
    )
    return pl.pallas_call(
        _dispatch_kernel,
        grid_spec=gs,
        out_shape=[jax.ShapeDtypeStruct((n_slots * PK_CHUNKS, 128), u32), jax.ShapeDtypeStruct((t, D), f32)],
        compiler_params=_params(("arbitrary",)),
        name="dispatch",
    )(cnt, pad_off, nb_total, dest_flat, h2p, s13, s2)


GMLP_RING = 8


def _gmlp_kernel(nbe_ref, boff_ref, nbt_ref, w1_ref, w3_ref, w2_ref, xs_hbm, y_hbm, xbuf, ybuf, w13_s, w2_s,
                 sem_in, sem_out, sem_z):
    e = pl.program_id(0)
    nb = nbe_ref[e]
    b0 = boff_ref[e]
    total = nbt_ref[0]
    n_blk = y_hbm.shape[0] // (MOE_BLK * PK_CHUNKS)
    xrows = MOE_BLK * PK_CHUNKS
    yrows = MOE_BLK * PK_CHUNKS
    ring = GMLP_RING

    def in_copy(b):
        sl = b & (ring - 1)
        return pltpu.make_async_copy(xs_hbm.at[pl.ds(pl.multiple_of(b * xrows, xrows), xrows), :], xbuf.at[sl],
                                     sem_in.at[sl])

    def out_copy(b):
        sl = b & (ring - 1)
        return pltpu.make_async_copy(ybuf.at[sl], y_hbm.at[pl.ds(pl.multiple_of(b * yrows, yrows), yrows), :],
                                     sem_out.at[sl])

    def zero_copy(b):
        return pltpu.make_async_copy(ybuf.at[0], y_hbm.at[pl.ds(pl.multiple_of(b * yrows, yrows), yrows), :], sem_z)

    ahead = ring - 2

    @pl.when(e == 0)
    def _():
        for b in range(ahead):
            @pl.when(b < total)
            def _():
                in_copy(b).start()

    @pl.when(nb > 0)
    def _():
        w13_s[:, 0:EXP_D] = w1_ref[0].astype(bf16)
        w13_s[:, EXP_D:2 * EXP_D] = w3_ref[0].astype(bf16)
        w2_s[...] = w2_ref[0].astype(bf16)

    def process(b, m):
        for q in range(m):
            in_copy(b + q).wait()
        for q in range(m):
            @pl.when(b + q + ahead < total)
            def _():
                in_copy(b + q + ahead).start()

            @pl.when(b + q >= ring)
            def _():
                out_copy(b + q - ring).wait()

        cols = []
        for c in range(PK_CHUNKS):
            pieces = []
            for q in range(m):
                xb = xbuf.at[(b + q) & (ring - 1)]
                pieces += [xb[pl.ds(8 * PK_CHUNKS * g + c, 8, stride=PK_CHUNKS), :] for g in range(MOE_BLK // 8)]
            cols.append(jnp.concatenate(pieces, axis=0))
        word = jnp.concatenate(cols, axis=1)
        xlo = lax.bitcast_convert_type(lax.shift_left(word, jnp.uint32(16)), f32).astype(bf16)
        xhi = lax.bitcast_convert_type(word & jnp.uint32(0xFFFF0000), f32).astype(bf16)
        h = _dot(xlo, w13_s[0:512, :]) + _dot(xhi, w13_s[512:1024, :])
        hid = (_silu(h[:, 0:EXP_D]) * h[:, EXP_D:2 * EXP_D]).astype(bf16)
        y = _dot(hid, w2_s[...])
        ylo = lax.shift_right_logical(lax.bitcast_convert_type(y[:, 0:512].astype(bf16).astype(f32), u32),
                                      jnp.uint32(16))
        yhi = lax.bitcast_convert_type(y[:, 512:1024].astype(bf16).astype(f32), u32) & jnp.uint32(0xFFFF0000)
        yw = ylo | yhi
        for q in range(m):
            yb = ybuf.at[(b + q) & (ring - 1)]
            for g in range(MOE_BLK // 8):
                r0 = q * MOE_BLK + 8 * g
                for c in range(PK_CHUNKS):
                    yb[pl.ds(8 * PK_CHUNKS * g + c, 8, stride=PK_CHUNKS), :] = yw[r0:r0 + 8, 128 * c:128 * c + 128]
            out_copy(b + q).start()

    def pair(j, carry):
        process(b0 + 2 * j, 2)
        return carry

    lax.fori_loop(0, jnp.right_shift(nb, 1), pair, 0)

    @pl.when((nb & 1) == 1)
    def _():
        process(b0 + nb - 1, 1)

    @pl.when(e == pl.num_programs(0) - 1)
    def _():
        for back in range(ring, 0, -1):
            @pl.when(total >= back)
            def _():
                out_copy(total - back).wait()

        ybuf[0] = jnp.zeros(ybuf.shape[1:], u32)
        lax.fori_loop(total, n_blk, lambda b, c: (zero_copy(b).start(), c)[1], 0)
        lax.fori_loop(total, n_blk, lambda b, c: (zero_copy(b).wait(), c)[1], 0)


def _gmlp_call(layer, nblk_e, blk_off, nb_total, xs, w1, w3, w2):
    n_slots = xs.shape[0] // PK_CHUNKS
    wsel = lambda e, *_: (layer, e, 0, 0)
    gs = pltpu.PrefetchScalarGridSpec(
        num_scalar_prefetch=3,
        grid=(N_EXP,),
        in_specs=[pl.BlockSpec((None, 1, D, EXP_D), wsel), pl.BlockSpec((None, 1, D, EXP_D), wsel),
                  pl.BlockSpec((None, 1, EXP_D, D), wsel), pl.BlockSpec(memory_space=pl.ANY)],
        out_specs=pl.BlockSpec(memory_space=pl.ANY),
        scratch_shapes=[pltpu.VMEM((GMLP_RING, MOE_BLK * PK_CHUNKS, 128), u32),
                        pltpu.VMEM((GMLP_RING, MOE_BLK * PK_CHUNKS, 128), u32),
                        pltpu.VMEM((D, 2 * EXP_D), bf16), pltpu.VMEM((EXP_D, D), bf16),
                        pltpu.SemaphoreType.DMA((GMLP_RING,)), pltpu.SemaphoreType.DMA((GMLP_RING,)),
                        pltpu.SemaphoreType.DMA(())],
    )
    return pl.pallas_call(
        _gmlp_kernel,
        grid_spec=gs,
        out_shape=jax.ShapeDtypeStruct((n_slots * PK_CHUNKS, 128), u32),
        compiler_params=_params(("arbitrary",)),
        name="gmlp",
    )(nblk_e, blk_off, nb_total, w1, w3, w2, xs)


COMBINE_TILE = 512
COMBINE_TILE_FUSED = 256


def _combine_kernel(final, fuse_next, *refs):
    if fuse_next:
        (dest_hbm, y_hbm, x_ref, mod_ref, sh_ref, w_ref, gf_ref, nm_ref, ng_ref, nw_ref, nb_ref, o_ref, z_ref,
         idx_ref, buf_ref, sem_idx, sem_row) = refs
    else:
        dest_hbm, y_hbm, x_ref, mod_ref, sh_ref, w_ref, gf_ref, o_ref, idx_ref, buf_ref, sem_idx, sem_row = refs
    tm = x_ref.shape[1]
    n = tm * TOP_K
    s = pl.program_id(0) * pl.num_programs(1) + pl.program_id(1)
    nsteps = pl.num_programs(0) * pl.num_programs(1)
    last = s == nsteps - 1
    nxt = jnp.minimum(s + 1, nsteps - 1)

    def idx_copy(step):
        return pltpu.make_async_copy(dest_hbm.at[pl.ds(pl.multiple_of(step * n, n), n)],
                                     idx_ref.at[pl.ds(pl.multiple_of((step & 1) * n, n), n)], sem_idx.at[step & 1])

    def row_copy(d, k, t, slot, prio):
        return pltpu.make_async_copy(y_hbm.at[pl.ds(d * PK_CHUNKS, PK_CHUNKS), :],
                                     buf_ref.at[slot, pl.ds((k * tm + t) * PK_CHUNKS, PK_CHUNKS), :],
                                     sem_row.at[slot]).start(priority=prio)

    def rows_wait(slot):
        pltpu.make_async_copy(y_hbm.at[pl.ds(0, n * PK_CHUNKS), :], buf_ref.at[slot], sem_row.at[slot]).wait()

    @pl.when(s == 0)
    def _():
        idx_copy(0).start()
        idx_copy(0).wait()

        def body(t2, carry):
            ds = [idx_ref[t2 * (2 * TOP_K) + j] for j in range(2 * TOP_K)]
            for j in range(2 * TOP_K):
                row_copy(ds[j], j % TOP_K, t2 * 2 + j // TOP_K, 0, j % 2)
            return carry

        lax.fori_loop(0, tm // 2, body, 0)

        @pl.when(nsteps > 1)
        def _():
            idx_copy(1).start()

    @pl.when(s + 1 < nsteps)
    def _():
        idx_copy(s + 1).wait()

    @pl.when(s + 2 < nsteps)
    def _():
        idx_copy(s + 2).start()

    sl = s & 1
    nsl = 1 - sl
    rows_wait(sl)
    bs = buf_ref.at[sl]
    m = mod_ref[0]
    gate = m[5:6]
    nbase = (nxt & 1) * n
    himask = jnp.uint32(0xFFFF0000)

    def group(g, carry):
        r0 = pl.multiple_of(g * 8, 8)
        ds = [idx_ref[nbase + r0 * TOP_K + j] for j in range(8 * TOP_K)]
        wg = w_ref[pl.ds(r0, 8), :]
        lo = [None] * PK_CHUNKS
        hi = [None] * PK_CHUNKS
        for k in range(TOP_K):
            wk = jnp.broadcast_to(wg[:, k:k + 1], (8, 128))
            for c in range(PK_CHUNKS):
                word = bs[pl.ds((k * tm + r0) * PK_CHUNKS + c, 8, stride=PK_CHUNKS), :]
                plo = wk * lax.bitcast_convert_type(lax.shift_left(word, jnp.uint32(16)), f32)
                phi = wk * lax.bitcast_convert_type(word & himask, f32)
                lo[c] = plo if k == 0 else lo[c] + plo
                hi[c] = phi if k == 0 else hi[c] + phi
        routed = jnp.concatenate(lo + hi, axis=1)
        o_ref[0, pl.ds(r0, 8), :] = x_ref[0, pl.ds(r0, 8), :] + gate * (routed + sh_ref[pl.ds(r0, 8), :])
        for j in range(8 * TOP_K):
            row_copy(ds[j], j % TOP_K, r0 + j // TOP_K, nsl, j % 2)
        return carry

    lax.fori_loop(0, tm // 8, group, 0)

    @pl.when(last)
    def _():
        rows_wait(nsl)

    if final:
        o_ref[0] = _rms(o_ref[0], gf_ref[...])
    if fuse_next:
        nm = nm_ref[0]
        h = _rms_mod(o_ref[0], ng_ref[...], nm[0:1], nm[1:2]).astype(bf16)
        z = _dot(h, nw_ref[...]) + nb_ref[...]
        z_ref[0] = z[:, 0:D] * _sigmoid(z[:, D:2 * D])


def _combine_call(dest_flat, y, x, mod, shared, w_tok, g_final, final, nxt=None):
    b, s, _ = x.shape
    tm = COMBINE_TILE_FUSED if nxt is not None else COMBINE_TILE
    nt = s // tm
    flat = lambda bb, i: (bb * nt + i, 0)
    row = lambda bb, i: (bb, i, 0)
    vec = lambda bb, i: (0, 0)
    in_specs = [pl.BlockSpec(memory_space=pl.ANY), pl.BlockSpec(memory_space=pl.ANY),
                pl.BlockSpec((1, tm, D), row),
                pl.BlockSpec((1, 6, D), lambda bb, i: (bb, 0, 0)),
                pl.BlockSpec((tm, D), flat),
                pl.BlockSpec((tm, TOP_K), flat),
                pl.BlockSpec((1, D), vec)]
    out_specs = pl.BlockSpec((1, tm, D), row)
    out_shape = jax.ShapeDtypeStruct((b, s, D), f32)
    args = [dest_flat, y, x, mod, shared, w_tok, g_final]
    if nxt is not None:
        in_specs += [pl.BlockSpec((1, 6, D), lambda bb, i: (bb, 0, 0)), pl.BlockSpec((1, D), vec),
                     pl.BlockSpec((D, 2 * D), vec), pl.BlockSpec((1, 2 * D), vec)]
        out_specs = [out_specs, pl.BlockSpec((1, tm, D), row)]
        out_shape = [out_shape, jax.ShapeDtypeStruct((b, s, D), f32)]
        args += list(nxt)
    return pl.pallas_call(
        functools.partial(_combine_kernel, final, nxt is not None),
        grid=(b, nt),
        in_specs=in_specs,
        out_specs=out_specs,
        out_shape=out_shape,
        scratch_shapes=[pltpu.SMEM((2 * tm * TOP_K,), i32), pltpu.VMEM((2, TOP_K * tm * PK_CHUNKS, 128), u32),
                        pltpu.SemaphoreType.DMA((2,)), pltpu.SemaphoreType.DMA((2,))],
        compiler_params=_params(("arbitrary", "arbitrary")),
        name="combine",
    )(*args)


def _moe_layer(layer, x1, mod, norm_g, router_w, router_b, w1, w3, w2, sw1, sw3, sw2, g_final, final, nxt):
    b, s, _ = x1.shape
    t = b * s
    rwt = router_w.T
    rwh = rwt.astype(bf16)
    rwl = (rwt - rwh.astype(f32)).astype(bf16)
    s13 = jnp.concatenate([sw1, sw3], axis=1).astype(bf16)
    h2p, logits_t = _ffn_pre_call(x1, mod, norm_g.reshape(1, D), rwh, rwl)
    tri = (lax.broadcasted_iota(i32, (ROUTE_TILE, ROUTE_TILE), 0)
           < lax.broadcasted_iota(i32, (ROUTE_TILE, ROUTE_TILE), 1)).astype(bf16)
    eidx, w_t, rank, counts = _route_call(logits_t, router_b.reshape(N_EXP, 1).astype(f32), tri)
    cnt = counts.reshape(N_EXP).astype(i32)
    nblk_e = (cnt + MOE_BLK - 1) // MOE_BLK
    blk_ends = jnp.cumsum(nblk_e)
    blk_off = blk_ends - nblk_e
    pad_off = blk_off * MOE_BLK
    nb_total = blk_ends[-1:].astype(i32)
    n_blk = t * TOP_K // MOE_BLK + N_EXP
    dest = _dest_call(eidx, rank, pad_off.astype(f32).reshape(N_EXP, 1))
    dest_flat = dest.T.reshape(t * TOP_K)
    xs, shared = _dispatch_call(cnt, pad_off, nb_total, dest_flat, h2p, s13, sw2.astype(bf16), n_blk * MOE_BLK)
    y = _gmlp_call(layer, nblk_e, blk_off, nb_total, xs, w1, w3, w2)
    return _combine_call(dest_flat, y, x1, mod, shared, w_t.T, g_final.reshape(1, D), final, nxt)


def _rot_cols(w):
    d, n = w.shape
    w4 = w.reshape(d, n // 32, 2, 16)
    return jnp.stack([-w4[:, :, 1], w4[:, :, 0]], axis=2).reshape(d, n)


def _rope_tables(s):
    rows = s // GRID_W
    row = jnp.repeat(jnp.arange(rows, dtype=f32), GRID_W)
    col = jnp.tile(jnp.arange(GRID_W, dtype=f32), rows)
    n_freq = HEAD_DIM // 4
    inv = ROPE_BASE ** (-jnp.arange(n_freq, dtype=f32) / n_freq)
    ang_r = row[:, None] * inv
    ang_c = col[:, None] * inv
    cos = jnp.concatenate([jnp.cos(ang_r)] * 2 + [jnp.cos(ang_c)] * 2, axis=1)
    sin = jnp.concatenate([jnp.sin(ang_r)] * 2 + [jnp.sin(ang_c)] * 2, axis=1)
    return jnp.tile(cos, (1, 2)), jnp.tile(sin, (1, 2))


def _block_diag(w):
    h, dh, _ = w.shape
    eye = jnp.eye(h, dtype=w.dtype)
    return (eye[:, None, :, None] * w[:, :, None, :]).reshape(h * dh, h * dh)


def _even_layer_mixer(x, ctx, mod, norm_g, w_in, w_out, conv_w, conv_b, w_r, b_r, w_i, b_i, lam, sink):
    b, s, _ = x.shape
    r0, r1, r2 = LRU_W, 2 * LRU_W, 2 * LRU_W + Q_W
    wq = w_in[:, r1:r2].reshape(D, 2, 4, HEAD_DIM).transpose(0, 2, 1, 3).reshape(D, Q_W)
    wk = w_in[:, r2:r2 + KV_W]
    w_ext = jnp.concatenate([w_in[:, :r1], wq, w_in[:, r2:], _rot_cols(wq), _rot_cols(wk)], axis=1).astype(bf16)
    w_ctx = jnp.concatenate([w_in[:, :r0], w_in[:, r2:]], axis=1).astype(bf16)
    cos, sin = _rope_tables(s)
    g = norm_g.reshape(1, D)
    u, gt, q, k, v = _proj_in_call(x, mod, g, w_ext, cos, sin)
    uc, kx, vx = _proj_ctx_call(ctx, mod, g, w_ctx)
    n_lg = LRU_W // LRU_LANES
    hpg = LRU_LANES // HEAD_DIM

    def lane_groups(w):
        return jnp.stack([jnp.stack([_block_diag(w[d, g * hpg:(g + 1) * hpg]) for g in range(n_lg)]) for d in range(2)])

    wg = jnp.concatenate([lane_groups(w_r), lane_groups(w_i)], axis=-1).astype(bf16)
    bg = jnp.concatenate([b_r.reshape(2, n_lg, 1, LRU_LANES), b_i.reshape(2, n_lg, 1, LRU_LANES)], axis=-1)
    rec = _rglru_call(u, uc, conv_w, conv_b.reshape(1, LRU_W), wg, bg, lam.reshape(2, n_lg, 1, LRU_LANES))
    att = _attn_call(sink, q, k, v, kx, vx)
    w_att = w_out[LRU_W:].reshape(2, 4, HEAD_DIM, D).transpose(1, 0, 2, 3).reshape(Q_W, D).astype(bf16)
    return _mix_out_call(x, mod, gt, rec, att, w_out[:LRU_W].astype(bf16), w_att)


def kernel(x, c, ctx, c_ctx, mod_w, mod_b, norm_mix_g, norm_ffn_g, final_norm_g, ab_w_in, ab_w_out, lru_conv_w,
           lru_conv_b, lru_wr, lru_br, lru_wi, lru_bi, lru_lambda, attn_sink, cm_w_in, cm_b_in, cm_dw_w, cm_dw_b,
           cm_ln_g, cm_ln_b, cm_w_out, cm_b_out, router_w, router_b, exp_w1, exp_w3, exp_w2, shared_w1, shared_w3,
           shared_w2):
    bsz = x.shape[0]
    depth = mod_w.shape[0]
    assert bsz + 1 <= MOD_ROWS - 7
    cc = jnp.zeros((MOD_ROWS, D), f32).at[:bsz].set(c).at[MOD_ROWS - 8].set(c_ctx)
    mod_all = _mod_call(cc, mod_w, mod_b).reshape(depth, MOD_ROWS, 6, D)
    zg = None
    for l in range(depth):
        mod = mod_all[l]
        last = l == depth - 1
        nxt = None
        if not last and (l + 1) % 2 == 1:
            o1 = (l + 1) // 2
            nxt = (mod_all[l + 1], norm_mix_g[l + 1].reshape(1, D), cm_w_in[o1].astype(bf16),
                   cm_b_in[o1].reshape(1, 2 * D))
        if l % 2 == 0:
            e = l // 2
            assert depth <= 2
            x1 = _even_layer_mixer(x, ctx, mod, norm_mix_g[l], ab_w_in[e], ab_w_out[e], lru_conv_w[e], lru_conv_b[e],
                                   lru_wr[e], lru_br[e], lru_wi[e], lru_bi[e], lru_lambda[e], attn_sink[e])
        else:
            o = l // 2
            if zg is None:
                zg = _conf_in_call(x, mod, norm_mix_g[l].reshape(1, D), cm_w_in[o].astype(bf16),
                                   cm_b_in[o].reshape(1, 2 * D))
            dw = jnp.concatenate([cm_dw_w[o], jnp.zeros((1, D), f32)], axis=0)
            x1 = _conf_out_call(x, mod, zg, dw, cm_dw_b[o].reshape(1, D), cm_ln_g[o].reshape(1, D),
                                cm_ln_b[o].reshape(1, D), cm_w_out[o].astype(bf16), cm_b_out[o].reshape(1, D))
        out = _moe_layer(l, x1, mod, norm_ffn_g[l], router_w[l], router_b[l], exp_w1, exp_w3, exp_w2,
                         shared_w1[l], shared_w3[l], shared_w2[l], final_norm_g, last, nxt)
        x, zg = out if nxt is not None else (out, None)
    return x
```

```python
import functools

import jax
import jax.numpy as jnp
from jax import lax
from jax.experimental import pallas as pl
from jax.experimental.pallas import tpu as pltpu

f32 = jnp.float32
bf16 = jnp.bfloat16
i32 = jnp.int32
u32 = jnp.uint32

D = 1024
EPS = 1e-6
LRU_W = 512
LRU_C = 8.0
N_HEADS = 8
HEAD_DIM = 64
GRID_W = 64
ROPE_BASE = 10000.0
Q_W = 512
KV_W = 128
ATT_BLK = 128
CONV_K = 31
N_EXP = 256
TOP_K = 8
N_GRP = 8
TOPK_GRP = 4
GRP_SZ = N_EXP // N_GRP
EXP_D = 256
ROUTED_SCALE = 2.5
MOE_BLK = 256
PK_CHUNKS = D // 2 // 128

VMEM_LIMIT_V7X = 56 * 1024 * 1024
MOD_ROWS = 24

_NT = (((1,), (1,)), ((), ()))


def _params(sem):
    return pltpu.CompilerParams(dimension_semantics=sem, vmem_limit_bytes=VMEM_LIMIT_V7X)


def _sigmoid(x):
    return 1.0 / (1.0 + jnp.exp(-x))


def _silu(x):
    return x * _sigmoid(x)


def _gelu_tanh(x):
    return 0.5 * x * (1.0 + jnp.tanh(0.7978845608028654 * (x + 0.044715 * (x * x * x))))


def _rms(x, g):
    return x * lax.rsqrt(jnp.mean(x * x, axis=-1, keepdims=True) + EPS) * g


def _rms_mod(x, g, shift, scale):
    return _rms(x, g) * (1.0 + scale) + shift


def _dot(a, b):
    return jnp.dot(a, b, preferred_element_type=f32)


def _mod_kernel(c_ref, w_ref, b_ref, o_ref):
    a = _silu(c_ref[...]).astype(bf16)
    o_ref[0] = _dot(a, w_ref[0].astype(bf16)) + b_ref[0]


def _mod_call(cc, mod_w, mod_b):
    depth, _, n = mod_w.shape
    tn = 1536
    return pl.pallas_call(
        _mod_kernel,
        grid=(depth, n // tn),
        in_specs=[pl.BlockSpec((MOD_ROWS, D), lambda l, j: (0, 0)),
                  pl.BlockSpec((1, D, tn), lambda l, j: (l, 0, j)),
                  pl.BlockSpec((1, 1, tn), lambda l, j: (l, 0, j))],
        out_specs=pl.BlockSpec((1, MOD_ROWS, tn), lambda l, j: (l, 0, j)),
        out_shape=jax.ShapeDtypeStruct((depth, MOD_ROWS, n), f32),
        compiler_params=_params(("parallel", "parallel")),
        name="mod",
    )(cc, mod_w, mod_b.reshape(depth, 1, n))


def _proj_in_kernel(x_ref, mod_ref, g_ref, w_ref, cos_ref, sin_ref, u_ref, gt_ref, q_ref, k_ref, v_ref):
    m = mod_ref[0]
    h = _rms_mod(x_ref[0], g_ref[...], m[0:1], m[1:2]).astype(bf16)
    p = _dot(h, w_ref[...])
    u_ref[0] = p[:, 0:512]
    gt_ref[0] = p[:, 512:1024]
    cos = cos_ref[...]
    sin = sin_ref[...]
    qs = []
    for j in range(4):
        qj = p[:, 1024 + j * 128:1152 + j * 128] * cos + p[:, 1792 + j * 128:1920 + j * 128] * sin
        qs.append(qj * (HEAD_DIM ** -0.5))
    q_ref[0] = jnp.concatenate(qs, axis=1).astype(bf16)
    k_ref[0] = (p[:, 1536:1664] * cos + p[:, 2304:2432] * sin).astype(bf16)
    v_ref[0] = p[:, 1664:1792].astype(bf16)


def _proj_in_call(x, mod, g, w_ext, cos, sin, tm=512):
    b, s, _ = x.shape
    nw = w_ext.shape[1]
    row = lambda bb, i: (bb, i, 0)
    return pl.pallas_call(
        _proj_in_kernel,
        grid=(b, s // tm),
        in_specs=[pl.BlockSpec((1, tm, D), row),
                  pl.BlockSpec((1, 6, D), lambda bb, i: (bb, 0, 0)),
                  pl.BlockSpec((1, D), lambda bb, i: (0, 0)),
                  pl.BlockSpec((D, nw), lambda bb, i: (0, 0)),
                  pl.BlockSpec((tm, 128), lambda bb, i: (i, 0)),
                  pl.BlockSpec((tm, 128), lambda bb, i: (i, 0))],
        out_specs=[pl.BlockSpec((1, tm, LRU_W), row), pl.BlockSpec((1, tm, LRU_W), row),
                   pl.BlockSpec((1, tm, Q_W), row), pl.BlockSpec((1, tm, KV_W), row),
                   pl.BlockSpec((1, tm, KV_W), row)],
        out_shape=[jax.ShapeDtypeStruct((b, s, LRU_W), f32), jax.ShapeDtypeStruct((b, s, LRU_W), f32),
                   jax.ShapeDtypeStruct((b, s, Q_W), bf16), jax.ShapeDtypeStruct((b, s, KV_W), bf16),
                   jax.ShapeDtypeStruct((b, s, KV_W), bf16)],
        compiler_params=_params(("parallel", "parallel")),
        name="proj_in",
    )(x, mod, g, w_ext, cos, sin)


def _proj_ctx_kernel(x_ref, mod_ref, g_ref, w_ref, u_ref, k_ref, v_ref):
    m = mod_ref[0]
    h = _rms_mod(x_ref[0], g_ref[...], m[0:1], m[1:2]).astype(bf16)
    p = _dot(h, w_ref[...])
    u_ref[0] = p[:, 0:512]
    k_ref[0] = p[:, 512:640].astype(bf16)
    v_ref[0] = p[:, 640:768].astype(bf16)


def _proj_ctx_call(ctx, mod, g, w_ctx):
    b, n_ctx, _ = ctx.shape
    row = lambda bb: (bb, 0, 0)
    return pl.pallas_call(
        _proj_ctx_kernel,
        grid=(b,),
        in_specs=[pl.BlockSpec((1, n_ctx, D), row),
                  pl.BlockSpec((1, 6, D), lambda bb: (MOD_ROWS - 8, 0, 0)),
                  pl.BlockSpec((1, D), lambda bb: (0, 0)),
                  pl.BlockSpec((D, 768), lambda bb: (0, 0))],
        out_specs=[pl.BlockSpec((1, n_ctx, LRU_W), row), pl.BlockSpec((1, n_ctx, KV_W), row),
                   pl.BlockSpec((1, n_ctx, KV_W), row)],
        out_shape=[jax.ShapeDtypeStruct((b, n_ctx, LRU_W), f32), jax.ShapeDtypeStruct((b, n_ctx, KV_W), bf16),
                   jax.ShapeDtypeStruct((b, n_ctx, KV_W), bf16)],
        compiler_params=_params(("parallel",)),
        name="proj_ctx",
    )(ctx, mod, g, w_ctx)


LRU_CHUNK = 128
LRU_LANES = 512


def _rglru_kernel(u_ref, uc_ref, cw_ref, cb_ref, wg_ref, bg_ref, lam_ref, o_ref, pad_ref, cx_ref, cc_ref):
    s = u_ref.shape[1]
    n_ctx = uc_ref.shape[1]
    tc = LRU_CHUNK
    lw = LRU_LANES

    def conv_segment(src_ref, n, dst_ref):
        pad_ref[0:8] = jnp.zeros((8, lw), f32)
        pad_ref[8:8 + n] = src_ref[0]
        pad_ref[8 + n:16 + n] = jnp.zeros((8, lw), f32)
        for c in range(n // 256):
            acc = jnp.broadcast_to(cb_ref[...], (256, lw))
            for k in range(4):
                acc = acc + cw_ref[k:k + 1, :] * pad_ref[c * 256 + 6 + k:c * 256 + 6 + k + 256, :]
            dst_ref[c * 256:(c + 1) * 256] = acc

    conv_segment(uc_ref, n_ctx, cc_ref)
    conv_segment(u_ref, s, cx_ref)

    rowm = lax.broadcasted_iota(i32, (tc, lw), 0) & 7

    def scan_segment(src_ref, n, d, h0, write):
        lam = lam_ref[d, 0]
        sp = jnp.maximum(-lam, 0.0) + jnp.log(1.0 + jnp.exp(-jnp.abs(lam)))
        nch = n // tc

        def chunk(ci, h):
            c = ci if d == 0 else nch - 1 - ci
            t0 = pl.multiple_of(c * tc, tc)
            uc = src_ref[pl.ds(t0, tc), :]
            gates = _dot(uc.astype(bf16), wg_ref[d, 0]) + bg_ref[d, 0]
            r = _sigmoid(gates[:, 0:lw])
            ig = _sigmoid(gates[:, lw:2 * lw])
            log_a = (-LRU_C * sp) * r
            a = jnp.exp(log_a)
            bb = jnp.sqrt(-jnp.tanh(log_a) * (a * a + 1.0)) * (ig * uc)
            def shift(v, sh):
                amount = sh if d == 0 else 8 - sh
                return pltpu.roll(v.reshape(tc // 8, 8, lw), amount, 1).reshape(tc, lw)

            for sh in (1, 2, 4):
                keep = rowm >= sh if d == 0 else rowm < 8 - sh
                a_sh = jnp.where(keep, shift(a, sh), 1.0)
                b_sh = jnp.where(keep, shift(bb, sh), 0.0)
                bb = a * b_sh + bb
                a = a * a_sh
            outs = [None] * (tc // 8)
            order = range(tc // 8) if d == 0 else range(tc // 8 - 1, -1, -1)
            for gi in order:
                hg = bb[gi * 8:(gi + 1) * 8] + a[gi * 8:(gi + 1) * 8] * h
                outs[gi] = hg
                h = hg[7:8] if d == 0 else hg[0:1]
            if write:
                hs = jnp.concatenate(outs, axis=0)
                if d == 0:
                    o_ref[0, pl.ds(t0, tc), :] = hs
                else:
                    o_ref[0, pl.ds(t0, tc), :] = o_ref[0, pl.ds(t0, tc), :] + hs
            return h

        return lax.fori_loop(0, nch, chunk, h0)

    for d in range(2):
        h = jnp.zeros((1, lw), f32)
        h = scan_segment(cc_ref, n_ctx, d, h, False)
        scan_segment(cx_ref, s, d, h, True)


def _rglru_call(u, uc, conv_w, conv_b, wg, bg, lam):
    b, s, _ = u.shape
    n_ctx = uc.shape[1]
    lw = LRU_LANES
    return pl.pallas_call(
        _rglru_kernel,
        grid=(b, LRU_W // lw),
        in_specs=[pl.BlockSpec((1, s, lw), lambda bb, g: (bb, 0, g)),
                  pl.BlockSpec((1, n_ctx, lw), lambda bb, g: (bb, 0, g)),
                  pl.BlockSpec((4, lw), lambda bb, g: (0, g)),
                  pl.BlockSpec((1, lw), lambda bb, g: (0, g)),
                  pl.BlockSpec((2, 1, lw, 2 * lw), lambda bb, g: (0, g, 0, 0)),
                  pl.BlockSpec((2, 1, 1, 2 * lw), lambda bb, g: (0, g, 0, 0)),
                  pl.BlockSpec((2, 1, 1, lw), lambda bb, g: (0, g, 0, 0))],
        out_specs=pl.BlockSpec((1, s, lw), lambda bb, g: (bb, 0, g)),
        out_shape=jax.ShapeDtypeStruct((b, s, LRU_W), f32),
        scratch_shapes=[pltpu.VMEM((s + 16, lw), f32), pltpu.VMEM((s, lw), f32), pltpu.VMEM((n_ctx, lw), f32)],
        compiler_params=_params(("parallel", "parallel")),
        name="rglru",
    )(u, uc, conv_w, conv_b, wg, bg, lam)


def _attn_kernel(sink_ref, q_ref, kp_ref, kc_ref, kn_ref, vp_ref, vc_ref, vn_ref, kx_ref, vx_ref, o_ref):
    n = pl.program_id(1)
    nb = pl.num_programs(1)
    blk = ATT_BLK
    q = q_ref[0]
    qall = jnp.concatenate([q[:, j * 128:(j + 1) * 128] for j in range(4)], axis=0)
    kw = jnp.concatenate([kp_ref[0], kc_ref[0], kn_ref[0]], axis=0)
    vw = jnp.concatenate([vp_ref[0], vc_ref[0], vn_ref[0]], axis=0)
    kx = kx_ref[0]
    vx = vx_ref[0]
    lo_w = lax.broadcasted_iota(i32, kw.shape, 1) < HEAD_DIM
    lo_x = lax.broadcasted_iota(i32, kx.shape, 1) < HEAD_DIM
    dj = lax.broadcasted_iota(i32, (4 * blk, blk), 1) - (lax.broadcasted_iota(i32, (4 * blk, blk), 0) & (blk - 1))
    pen_prev = jnp.where(dj >= jnp.where(n > 0, 0, 2 * blk), 0.0, -jnp.inf)
    pen_next = jnp.where(dj <= jnp.where(n < nb - 1, 0, -2 * blk), 0.0, -jnp.inf)
    rb = lax.broadcasted_iota(i32, (4 * blk, 1), 0) // blk
    zero = jnp.zeros((), bf16)
    out = jnp.zeros((4 * blk, 128), f32)
    for half in range(2):
        sel_w = lo_w if half == 0 else jnp.logical_not(lo_w)
        sel_x = lo_x if half == 0 else jnp.logical_not(lo_x)
        s_w = lax.dot_general(qall, jnp.where(sel_w, kw, zero), _NT, preferred_element_type=f32)
        s_w = jnp.concatenate([s_w[:, 0:blk] + pen_prev, s_w[:, blk:2 * blk], s_w[:, 2 * blk:] + pen_next], axis=1)
        s_c = lax.dot_general(qall, jnp.where(sel_x, kx, zero), _NT, preferred_element_type=f32)
        sk = jnp.where(rb == 0, sink_ref[4 * half],
                       jnp.where(rb == 1, sink_ref[4 * half + 1],
                                 jnp.where(rb == 2, sink_ref[4 * half + 2], sink_ref[4 * half + 3])))
        m = jnp.maximum(jnp.maximum(jnp.max(s_w, axis=1, keepdims=True), jnp.max(s_c, axis=1, keepdims=True)), sk)
        p_w = jnp.exp(s_w - m)
        p_c = jnp.exp(s_c - m)
        den = jnp.sum(p_w, axis=1, keepdims=True) + jnp.sum(p_c, axis=1, keepdims=True) + jnp.exp(sk - m)
        o = _dot(p_w.astype(bf16), jnp.where(sel_w, vw, zero)) + _dot(p_c.astype(bf16), jnp.where(sel_x, vx, zero))
        out = out + o / den
    o_ref[0] = jnp.concatenate([out[j * blk:(j + 1) * blk] for j in range(4)], axis=1).astype(bf16)


def _attn_call(sink, q, k, v, kx, vx):
    b, s, _ = q.shape
    n_ctx = kx.shape[1]
    nb = s // ATT_BLK
    cur = lambda bb, n: (bb, n, 0)
    prev = lambda bb, n: (bb, jnp.maximum(n - 1, 0), 0)
    nxt = lambda bb, n: (bb, jnp.minimum(n + 1, nb - 1), 0)
    kvb = (1, ATT_BLK, KV_W)
    return pl.pallas_call(
        _attn_kernel,
        grid=(b, nb),
        in_specs=[pl.BlockSpec(memory_space=pltpu.SMEM),
                  pl.BlockSpec((1, ATT_BLK, Q_W), cur),
                  pl.BlockSpec(kvb, prev), pl.BlockSpec(kvb, cur), pl.BlockSpec(kvb, nxt),
                  pl.BlockSpec(kvb, prev), pl.BlockSpec(kvb, cur), pl.BlockSpec(kvb, nxt),
                  pl.BlockSpec((1, n_ctx, KV_W), lambda bb, n: (bb, 0, 0)),
                  pl.BlockSpec((1, n_ctx, KV_W), lambda bb, n: (bb, 0, 0))],
        out_specs=pl.BlockSpec((1, ATT_BLK, Q_W), cur),
        out_shape=jax.ShapeDtypeStruct((b, s, Q_W), bf16),
        compiler_params=_params(("parallel", "parallel")),
        name="attn",
    )(sink, q, k, k, k, v, v, v, kx, vx)


def _mix_out_kernel(x_ref, mod_ref, gt_ref, rec_ref, att_ref, wr_ref, wa_ref, o_ref):
    m = mod_ref[0]
    a = (_gelu_tanh(gt_ref[0]) * rec_ref[0]).astype(bf16)
    y = _dot(a, wr_ref[...]) + _dot(att_ref[0], wa_ref[...])
    o_ref[0] = x_ref[0] + m[2:3] * y


def _mix_out_call(x, mod, gt, rec, att, w_rec, w_att, tm=512):
    b, s, _ = x.shape
    row = lambda bb, i: (bb, i, 0)
    return pl.pallas_call(
        _mix_out_kernel,
        grid=(b, s // tm),
        in_specs=[pl.BlockSpec((1, tm, D), row),
                  pl.BlockSpec((1, 6, D), lambda bb, i: (bb, 0, 0)),
                  pl.BlockSpec((1, tm, LRU_W), row), pl.BlockSpec((1, tm, LRU_W), row),
                  pl.BlockSpec((1, tm, Q_W), row),
                  pl.BlockSpec((LRU_W, D), lambda bb, i: (0, 0)),
                  pl.BlockSpec((Q_W, D), lambda bb, i: (0, 0))],
        out_specs=pl.BlockSpec((1, tm, D), row),
        out_shape=jax.ShapeDtypeStruct((b, s, D), f32),
        compiler_params=_params(("parallel", "parallel")),
        name="mix_out",
    )(x, mod, gt, rec, att, w_rec, w_att)


def _conf_in_kernel(x_ref, mod_ref, g_ref, w_ref, b_ref, o_ref):
    m = mod_ref[0]
    h = _rms_mod(x_ref[0], g_ref[...], m[0:1], m[1:2]).astype(bf16)
    z = _dot(h, w_ref[...]) + b_ref[...]
    o_ref[0] = z[:, 0:D] * _sigmoid(z[:, D:2 * D])


def _conf_in_call(x, mod, g, w, bias, tm=512):
    b, s, _ = x.shape
    row = lambda bb, i: (bb, i, 0)
    return pl.pallas_call(
        _conf_in_kernel,
        grid=(b, s // tm),
        in_specs=[pl.BlockSpec((1, tm, D), row),
                  pl.BlockSpec((1, 6, D), lambda bb, i: (bb, 0, 0)),
                  pl.BlockSpec((1, D), lambda bb, i: (0, 0)),
                  pl.BlockSpec((D, 2 * D), lambda bb, i: (0, 0)),
                  pl.BlockSpec((1, 2 * D), lambda bb, i: (0, 0))],
        out_specs=pl.BlockSpec((1, tm, D), row),
        out_shape=jax.ShapeDtypeStruct((b, s, D), f32),
        compiler_params=_params(("parallel", "parallel")),
        name="conf_in",
    )(x, mod, g, w, bias)


CONF_HALO = 16
CONF_ROWS = 64
CONF_LANE_PAD = 128


def _conf_out_kernel(x_ref, mod_ref, zc_ref, zp_ref, zn_ref, dw_ref, db_ref, lg_ref, lb_ref, w_ref, b_ref, o_ref,
                     pad_ref, sh_ref, cv_ref):
    i = pl.program_id(1)
    nt = pl.num_programs(1)
    tm = zc_ref.shape[1]
    zero = jnp.zeros((CONF_HALO, D), f32)
    pad_ref[0:CONF_HALO] = jnp.where(i > 0, zp_ref[0], zero)
    pad_ref[CONF_HALO:CONF_HALO + tm] = zc_ref[0]
    pad_ref[CONF_HALO + tm:2 * CONF_HALO + tm] = jnp.where(i < nt - 1, zn_ref[0], zero)
    for r in range(8):
        sh_ref[r, :, 0:D] = pad_ref[r:r + tm + 24, :]

    for lg in range(D // 128):
        l0 = lg * 128
        taps = [dw_ref[k:k + 1, l0:l0 + 128] for k in range(CONV_K)]
        bias = db_ref[:, l0:l0 + 128]

        def chunk(c, carry, l0=l0, taps=taps, bias=bias):
            t0 = pl.multiple_of(c * CONF_ROWS, CONF_ROWS)
            acc = jnp.broadcast_to(bias, (CONF_ROWS, 128))
            for r in range(8):
                win = sh_ref[r, pl.ds(t0, CONF_ROWS + 24), l0:l0 + 128]
                for a in range(4):
                    k = 8 * a + r - 1
                    if 0 <= k < CONV_K:
                        acc = acc + taps[k] * win[8 * a:8 * a + CONF_ROWS]
            cv_ref[pl.ds(t0, CONF_ROWS), l0:l0 + 128] = acc
            return carry

        lax.fori_loop(0, tm // CONF_ROWS, chunk, 0)
    z = cv_ref[...]
    mu = jnp.mean(z, axis=-1, keepdims=True)
    zc = z - mu
    var = jnp.mean(zc * zc, axis=-1, keepdims=True)
    zn = zc * lax.rsqrt(var + EPS) * lg_ref[...] + lb_ref[...]
    y = _dot(_silu(zn).astype(bf16), w_ref[...]) + b_ref[...]
    m = mod_ref[0]
    o_ref[0] = x_ref[0] + m[2:3] * y


def _conf_out_call(x, mod, zg, dw_w, dw_b, ln_g, ln_b, w_out, b_out, tm=256):
    b, s, _ = x.shape
    row = lambda bb, i: (bb, i, 0)
    hb = tm // CONF_HALO
    nh = s // CONF_HALO
    vec = lambda bb, i: (0, 0)
    return pl.pallas_call(
        _conf_out_kernel,
        grid=(b, s // tm),
        in_specs=[pl.BlockSpec((1, tm, D), row),
                  pl.BlockSpec((1, 6, D), lambda bb, i: (bb, 0, 0)),
                  pl.BlockSpec((1, tm, D), row),
                  pl.BlockSpec((1, CONF_HALO, D), lambda bb, i: (bb, jnp.maximum(i * hb - 1, 0), 0)),
                  pl.BlockSpec((1, CONF_HALO, D), lambda bb, i: (bb, jnp.minimum((i + 1) * hb, nh - 1), 0)),
                  pl.BlockSpec((CONV_K + 1, D), vec),
                  pl.BlockSpec((1, D), vec), pl.BlockSpec((1, D), vec), pl.BlockSpec((1, D), vec),
                  pl.BlockSpec((D, D), vec), pl.BlockSpec((1, D), vec)],
        out_specs=pl.BlockSpec((1, tm, D), row),
        out_shape=jax.ShapeDtypeStruct((b, s, D), f32),
        scratch_shapes=[pltpu.VMEM((tm + 2 * CONF_HALO, D), f32), pltpu.VMEM((8, tm + 24, D + CONF_LANE_PAD), f32),
                        pltpu.VMEM((tm, D), f32)],
        compiler_params=_params(("parallel", "parallel")),
        name="conf_out",
    )(x, mod, zg, zg, zg, dw_w, dw_b, ln_g, ln_b, w_out, b_out)


def _ffn_pre_kernel(x_ref, mod_ref, g_ref, rwh_ref, rwl_ref, hp_ref, lg_ref):
    m = mod_ref[0]
    h2 = _rms_mod(x_ref[0], g_ref[...], m[3:4], m[4:5])
    hb = h2.astype(bf16)
    hbf = hb.astype(f32)
    hl = (h2 - hbf).astype(bf16)
    lg_ref[...] = (lax.dot_general(rwh_ref[...], hb, _NT, preferred_element_type=f32)
                   + lax.dot_general(rwh_ref[...], hl, _NT, preferred_element_type=f32)
                   + lax.dot_general(rwl_ref[...], hb, _NT, preferred_element_type=f32))
    lo = lax.shift_right_logical(lax.bitcast_convert_type(hbf[:, 0:512], u32), jnp.uint32(16))
    hi = lax.bitcast_convert_type(hbf[:, 512:1024], u32) & jnp.uint32(0xFFFF0000)
    word = lo | hi
    for i in range(word.shape[0] // 8):
        for c in range(PK_CHUNKS):
            hp_ref[pl.ds(8 * PK_CHUNKS * i + c, 8, stride=PK_CHUNKS), :] = word[8 * i:8 * i + 8, 128 * c:128 * c + 128]


def _ffn_pre_call(x, mod, g, rwh, rwl, tm=512):
    b, s, _ = x.shape
    nt = s // tm
    t = b * s
    flat = lambda bb, i: (bb * nt + i, 0)
    vec = lambda bb, i: (0, 0)
    return pl.pallas_call(
        _ffn_pre_kernel,
        grid=(b, nt),
        in_specs=[pl.BlockSpec((1, tm, D), lambda bb, i: (bb, i, 0)),
                  pl.BlockSpec((1, 6, D), lambda bb, i: (bb, 0, 0)),
                  pl.BlockSpec((1, D), vec),
                  pl.BlockSpec((N_EXP, D), vec), pl.BlockSpec((N_EXP, D), vec)],
        out_specs=[pl.BlockSpec((tm * PK_CHUNKS, 128), flat),
                   pl.BlockSpec((N_EXP, tm), lambda bb, i: (0, bb * nt + i))],
        out_shape=[jax.ShapeDtypeStruct((t * PK_CHUNKS, 128), u32), jax.ShapeDtypeStruct((N_EXP, t), f32)],
        compiler_params=_params(("parallel", "parallel")),
        name="ffn_pre",
    )(x, mod, g, rwh, rwl)


ROUTE_TILE = 256


def _route_kernel(lg_ref, rb_ref, tri_ref, e_ref, w_ref, r_ref, c_ref, base_ref):
    i = pl.program_id(0)
    tr = lg_ref.shape[1]

    @pl.when(i == 0)
    def _():
        base_ref[...] = jnp.zeros_like(base_ref)

    scores = _sigmoid(lg_ref[...])
    biased = scores + rb_ref[...]
    neg = -jnp.inf
    rowf = lax.broadcasted_iota(i32, (N_EXP, tr), 0).astype(f32)
    r32 = lax.broadcasted_iota(i32, (GRP_SZ, tr), 0).astype(f32)
    gs = []
    for g in range(N_GRP):
        seg = biased[g * GRP_SZ:(g + 1) * GRP_SZ]
        m1 = jnp.max(seg, axis=0, keepdims=True)
        i1 = jnp.min(jnp.where(seg == m1, r32, 2.0 * GRP_SZ), axis=0, keepdims=True)
        m2 = jnp.max(jnp.where(r32 == i1, neg, seg), axis=0, keepdims=True)
        gs.append(m1 + m2)
    allowed = []
    for g in range(N_GRP):
        beat = jnp.zeros((1, tr), f32)
        for h in range(N_GRP):
            if h < g:
                beat = beat + jnp.where(gs[h] >= gs[g], 1.0, 0.0)
            elif h > g:
                beat = beat + jnp.where(gs[h] > gs[g], 1.0, 0.0)
        allowed.append(jnp.broadcast_to(beat, (GRP_SZ, tr)))
    allowed = jnp.concatenate(allowed, axis=0)
    masked = jnp.where(allowed < float(TOPK_GRP), biased, neg)
    cnt = jnp.zeros((N_EXP, tr), f32)
    idxs, ws = [], []
    for _ in range(TOP_K):
        m = jnp.max(masked, axis=0, keepdims=True)
        idx = jnp.min(jnp.where(masked == m, rowf, 2.0 * N_EXP), axis=0, keepdims=True)
        hit = rowf == idx
        ws.append(jnp.sum(jnp.where(hit, scores, 0.0), axis=0, keepdims=True))
        masked = jnp.where(hit, neg, masked)
        cnt = cnt + jnp.where(hit, 1.0, 0.0)
        idxs.append(idx)
    wsum = ws[0]
    for k in range(1, TOP_K):
        wsum = wsum + ws[k]
    pos = _dot(cnt.astype(bf16), tri_ref[...]) + base_ref[...]
    ranks = [jnp.sum(jnp.where(rowf == idxs[k], pos, 0.0), axis=0, keepdims=True) for k in range(TOP_K)]
    e_ref[...] = jnp.concatenate(idxs, axis=0).astype(i32)
    w_ref[...] = jnp.concatenate([ROUTED_SCALE * ws[k] / wsum for k in range(TOP_K)], axis=0)
    r_ref[...] = jnp.concatenate(ranks, axis=0).astype(i32)
    base_ref[...] = base_ref[...] + jnp.sum(cnt, axis=1, keepdims=True)
    c_ref[...] = base_ref[...]


def _route_call(logits_t, router_b, tri):
    t = logits_t.shape[1]
    tr = ROUTE_TILE
    col = lambda i: (0, i)
    return pl.pallas_call(
        _route_kernel,
        grid=(t // tr,),
        in_specs=[pl.BlockSpec((N_EXP, tr), col),
                  pl.BlockSpec((N_EXP, 1), lambda i: (0, 0)),
                  pl.BlockSpec((tr, tr), lambda i: (0, 0))],
        out_specs=[pl.BlockSpec((TOP_K, tr), col), pl.BlockSpec((TOP_K, tr), col), pl.BlockSpec((TOP_K, tr), col),
                   pl.BlockSpec((N_EXP, 1), lambda i: (0, 0))],
        out_shape=[jax.ShapeDtypeStruct((TOP_K, t), i32), jax.ShapeDtypeStruct((TOP_K, t), f32),
                   jax.ShapeDtypeStruct((TOP_K, t), i32), jax.ShapeDtypeStruct((N_EXP, 1), f32)],
        scratch_shapes=[pltpu.VMEM((N_EXP, 1), f32)],
        compiler_params=_params(("arbitrary",)),
        name="route",
    )(logits_t, router_b, tri)


def _dest_kernel(e_ref, r_ref, off_ref, d_ref):
    tr = e_ref.shape[1]
    rowi = lax.broadcasted_iota(i32, (N_EXP, tr), 0)
    off = off_ref[...]
    e = e_ref[...]
    rows = [jnp.sum(jnp.where(rowi == e[k:k + 1], off, 0.0), axis=0, keepdims=True) for k in range(TOP_K)]
    d_ref[...] = jnp.concatenate(rows, axis=0).astype(i32) + r_ref[...]


def _dest_call(eidx, rank, pad_off):
    t = eidx.shape[1]
    tr = 512
    col = lambda i: (0, i)
    return pl.pallas_call(
        _dest_kernel,
        grid=(t // tr,),
        in_specs=[pl.BlockSpec((TOP_K, tr), col), pl.BlockSpec((TOP_K, tr), col),
                  pl.BlockSpec((N_EXP, 1), lambda i: (0, 0))],
        out_specs=pl.BlockSpec((TOP_K, tr), col),
        out_shape=jax.ShapeDtypeStruct((TOP_K, t), i32),
        compiler_params=_params(("parallel",)),
        name="dest",
    )(eidx, rank, pad_off)


DISPATCH_TILE = 512
_PAD_PIECES = (128, 64, 32, 16, 8, 4, 2, 1)


DISPATCH_SLOTS = 3


def _dispatch_kernel(cnt_ref, off_ref, nbt_ref, dest_hbm, h_hbm, s13_ref, s2_ref, xs_hbm, sh_ref, idx_ref, hbuf,
                     zero_ref, sem_idx, sem_tile, sem_row, sem_z):
    i = pl.program_id(0)
    nsteps = pl.num_programs(0)
    n = idx_ref.shape[0] // 2
    ts = n // TOP_K
    trows = ts * PK_CHUNKS
    n_blk = xs_hbm.shape[0] // (MOE_BLK * PK_CHUNKS)

    def idx_copy(step):
        return pltpu.make_async_copy(dest_hbm.at[pl.ds(pl.multiple_of(step * n, n), n)],
                                     idx_ref.at[pl.ds(pl.multiple_of((step & 1) * n, n), n)], sem_idx.at[step & 1])

    def tile_copy(step):
        slot = lax.rem(step, DISPATCH_SLOTS)
        return pltpu.make_async_copy(h_hbm.at[pl.ds(pl.multiple_of(step * trows, trows), trows), :], hbuf.at[slot],
                                     sem_tile.at[slot])

    def rows_wait(step):
        pltpu.make_async_copy(xs_hbm.at[pl.ds(0, n * PK_CHUNKS), :], xs_hbm.at[pl.ds(0, n * PK_CHUNKS), :],
                              sem_row.at[lax.rem(step, DISPATCH_SLOTS)]).wait()

    def pad_copy(start_slot, p):
        return pltpu.make_async_copy(zero_ref.at[pl.ds(0, p * PK_CHUNKS), :],
                                     xs_hbm.at[pl.ds(start_slot * PK_CHUNKS, p * PK_CHUNKS), :], sem_z)

    def blk_copy(blk):
        return pltpu.make_async_copy(zero_ref, xs_hbm.at[pl.ds(blk * (MOE_BLK * PK_CHUNKS), MOE_BLK * PK_CHUNKS), :],
                                     sem_z)

    def for_each_pad_piece(fn):
        def per_expert(e, carry):
            c = cnt_ref[e]
            npad = ((c + (MOE_BLK - 1)) & (-MOE_BLK)) - c
            slot = off_ref[e] + c
            for p in _PAD_PIECES:
                @pl.when((npad & p) != 0)
                def _():
                    fn(pad_copy(slot, p))
                slot = slot + (npad & p)
            return carry

        lax.fori_loop(0, N_EXP, per_expert, 0)

    @pl.when(i == 0)
    def _():
        idx_copy(0).start()
        tile_copy(0).start()
        zero_ref[...] = jnp.zeros_like(zero_ref)
        for_each_pad_piece(lambda cp: cp.start())
        lax.fori_loop(nbt_ref[0], n_blk, lambda b, c: (blk_copy(b).start(), c)[1], 0)

    @pl.when(i >= DISPATCH_SLOTS - 1)
    def _():
        rows_wait(i - (DISPATCH_SLOTS - 1))

    @pl.when(i + 1 < nsteps)
    def _():
        idx_copy(i + 1).start()
        tile_copy(i + 1).start()

    idx_copy(i).wait()
    tile_copy(i).wait()
    sl = i & 1
    slot = lax.rem(i, DISPATCH_SLOTS)
    hb = hbuf.at[slot]

    def body(t2, carry):
        base = sl * n + t2 * (2 * TOP_K)
        ds = [idx_ref[base + j] for j in range(2 * TOP_K)]
        for j in range(2 * TOP_K):
            t = t2 * 2 + j // TOP_K
            pltpu.make_async_copy(hb.at[pl.ds(t * PK_CHUNKS, PK_CHUNKS), :],
                                  xs_hbm.at[pl.ds(ds[j] * PK_CHUNKS, PK_CHUNKS), :],
                                  sem_row.at[slot]).start(priority=j % 2)
        return carry

    halves = 2
    hrows = ts // halves
    for hf in range(halves):
        lax.fori_loop(hf * (hrows // 2), (hf + 1) * (hrows // 2), body, 0)
        cols = []
        for c in range(PK_CHUNKS):
            cols.append(jnp.concatenate(
                [hb[pl.ds(8 * PK_CHUNKS * g + c, 8, stride=PK_CHUNKS), :]
                 for g in range(hf * hrows // 8, (hf + 1) * hrows // 8)], axis=0))
        word = jnp.concatenate(cols, axis=1)
        xlo = lax.bitcast_convert_type(lax.shift_left(word, jnp.uint32(16)), f32).astype(bf16)
        xhi = lax.bitcast_convert_type(word & jnp.uint32(0xFFFF0000), f32).astype(bf16)
        a = _dot(xlo, s13_ref[0:512, :]) + _dot(xhi, s13_ref[512:1024, :])
        hid = (_silu(a[:, 0:EXP_D]) * a[:, EXP_D:2 * EXP_D]).astype(bf16)
        sh_ref[hf * hrows:(hf + 1) * hrows, :] = _dot(hid, s2_ref[...])

    @pl.when(i == nsteps - 1)
    def _():
        for back in range(DISPATCH_SLOTS - 2, -1, -1):
            @pl.when(i >= back)
            def _():
                rows_wait(i - back)

        for_each_pad_piece(lambda cp: cp.wait())
        lax.fori_loop(nbt_ref[0], n_blk, lambda b, c: (blk_copy(b).wait(), c)[1], 0)


def _dispatch_call(cnt, pad_off, nb_total, dest_flat, h2p, s13, s2, n_slots):
    t = h2p.shape[0] // PK_CHUNKS
    ts = DISPATCH_TILE
    vec = lambda i, *_: (0, 0)
    gs = pltpu.PrefetchScalarGridSpec(
        num_scalar_prefetch=3,
        grid=(t // ts,),
        in_specs=[pl.BlockSpec(memory_space=pl.ANY), pl.BlockSpec(memory_space=pl.ANY),
                  pl.BlockSpec((D, 2 * EXP_D), vec), pl.BlockSpec((EXP_D, D), vec)],
        out_specs=[pl.BlockSpec(memory_space=pl.ANY), pl.BlockSpec((ts, D), lambda i, *_: (i, 0))],
        scratch_shapes=[pltpu.SMEM((2 * ts * TOP_K,), i32), pltpu.VMEM((DISPATCH_SLOTS, ts * PK_CHUNKS, 128), u32),
                        pltpu.VMEM((MOE_BLK * PK_CHUNKS, 128), u32),
                        pltpu.SemaphoreType.DMA((2,)), pltpu.SemaphoreType.DMA((DISPATCH_SLOTS,)),
                        pltpu.SemaphoreType.DMA((DISPATCH_SLOTS,)), pltpu.SemaphoreType.DMA(())],
    )
    return pl.pallas_call(
        _dispatch_kernel,
        grid_spec=gs,
        out_shape=[jax.ShapeDtypeStruct((n_slots * PK_CHUNKS, 128), u32), jax.ShapeDtypeStruct((t, D), f32)],
        compiler_params=_params(("arbitrary",)),
        name="dispatch",
    )(cnt, pad_off, nb_total, dest_flat, h2p, s13, s2)


GMLP_RING = 8


def _gmlp_kernel(nbe_ref, boff_ref, nbt_ref, w1_ref, w3_ref, w2_ref, xs_hbm, y_hbm, xbuf, ybuf, w13_s, w2_s,
                 sem_in, sem_out, sem_z):
    e = pl.program_id(0)
    nb = nbe_ref[e]
    b0 = boff_ref[e]
    total = nbt_ref[0]
    n_blk = y_hbm.shape[0] // (MOE_BLK * PK_CHUNKS)
    xrows = MOE_BLK * PK_CHUNKS
    yrows = MOE_BLK * PK_CHUNKS
    ring = GMLP_RING

    def in_copy(b):
        sl = b & (ring - 1)
        return pltpu.make_async_copy(xs_hbm.at[pl.ds(pl.multiple_of(b * xrows, xrows), xrows), :], xbuf.at[sl],
                                     sem_in.at[sl])

    def out_copy(b):
        sl = b & (ring - 1)
        return pltpu.make_async_copy(ybuf.at[sl], y_hbm.at[pl.ds(pl.multiple_of(b * yrows, yrows), yrows), :],
                                     sem_out.at[sl])

    def zero_copy(b):
        return pltpu.make_async_copy(ybuf.at[0], y_hbm.at[pl.ds(pl.multiple_of(b * yrows, yrows), yrows), :], sem_z)

    ahead = ring - 2

    @pl.when(e == 0)
    def _():
        for b in range(ahead):
            @pl.when(b < total)
            def _():
                in_copy(b).start()

    @pl.when(nb > 0)
    def _():
        w13_s[:, 0:EXP_D] = w1_ref[0].astype(bf16)
        w13_s[:, EXP_D:2 * EXP_D] = w3_ref[0].astype(bf16)
        w2_s[...] = w2_ref[0].astype(bf16)

    def process(b, m):
        for q in range(m):
            in_copy(b + q).wait()
        for q in range(m):
            @pl.when(b + q + ahead < total)
            def _():
                in_copy(b + q + ahead).start()

            @pl.when(b + q >= ring)
            def _():
                out_copy(b + q - ring).wait()

        cols = []
        for c in range(PK_CHUNKS):
            pieces = []
            for q in range(m):
                xb = xbuf.at[(b + q) & (ring - 1)]
                pieces += [xb[pl.ds(8 * PK_CHUNKS * g + c, 8, stride=PK_CHUNKS), :] for g in range(MOE_BLK // 8)]
            cols.append(jnp.concatenate(pieces, axis=0))
        word = jnp.concatenate(cols, axis=1)
        xlo = lax.bitcast_convert_type(lax.shift_left(word, jnp.uint32(16)), f32).astype(bf16)
        xhi = lax.bitcast_convert_type(word & jnp.uint32(0xFFFF0000), f32).astype(bf16)
        h = _dot(xlo, w13_s[0:512, :]) + _dot(xhi, w13_s[512:1024, :])
        hid = (_silu(h[:, 0:EXP_D]) * h[:, EXP_D:2 * EXP_D]).astype(bf16)
        y = _dot(hid, w2_s[...])
        ylo = lax.shift_right_logical(lax.bitcast_convert_type(y[:, 0:512].astype(bf16).astype(f32), u32),
                                      jnp.uint32(16))
        yhi = lax.bitcast_convert_type(y[:, 512:1024].astype(bf16).astype(f32), u32) & jnp.uint32(0xFFFF0000)
        yw = ylo | yhi
        for q in range(m):
            yb = ybuf.at[(b + q) & (ring - 1)]
            for g in range(MOE_BLK // 8):
                r0 = q * MOE_BLK + 8 * g
                for c in range(PK_CHUNKS):
                    yb[pl.ds(8 * PK_CHUNKS * g + c, 8, stride=PK_CHUNKS), :] = yw[r0:r0 + 8, 128 * c:128 * c + 128]
            out_copy(b + q).start()

    def pair(j, carry):
        process(b0 + 2 * j, 2)
        return carry

    lax.fori_loop(0, jnp.right_shift(nb, 1), pair, 0)

    @pl.when((nb & 1) == 1)
    def _():
        process(b0 + nb - 1, 1)

    @pl.when(e == pl.num_programs(0) - 1)
    def _():
        for back in range(ring, 0, -1):
            @pl.when(total >= back)
            def _():
                out_copy(total - back).wait()

        ybuf[0] = jnp.zeros(ybuf.shape[1:], u32)
        lax.fori_loop(total, n_blk, lambda b, c: (zero_copy(b).start(), c)[1], 0)
        lax.fori_loop(total, n_blk, lambda b, c: (zero_copy(b).wait(), c)[1], 0)


def _gmlp_call(layer, nblk_e, blk_off, nb_total, xs, w1, w3, w2):
    n_slots = xs.shape[0] // PK_CHUNKS
    wsel = lambda e, *_: (layer, e, 0, 0)
    gs = pltpu.PrefetchScalarGridSpec(
        num_scalar_prefetch=3,
        grid=(N_EXP,),
        in_specs=[pl.BlockSpec((None, 1, D, EXP_D), wsel), pl.BlockSpec((None, 1, D, EXP_D), wsel),
                  pl.BlockSpec((None, 1, EXP_D, D), wsel), pl.BlockSpec(memory_space=pl.ANY)],
        out_specs=pl.BlockSpec(memory_space=pl.ANY),
        scratch_shapes=[pltpu.VMEM((GMLP_RING, MOE_BLK * PK_CHUNKS, 128), u32),
                        pltpu.VMEM((GMLP_RING, MOE_BLK * PK_CHUNKS, 128), u32),
                        pltpu.VMEM((D, 2 * EXP_D), bf16), pltpu.VMEM((EXP_D, D), bf16),
                        pltpu.SemaphoreType.DMA((GMLP_RING,)), pltpu.SemaphoreType.DMA((GMLP_RING,)),
                        pltpu.SemaphoreType.DMA(())],
    )
    return pl.pallas_call(
        _gmlp_kernel,
        grid_spec=gs,
        out_shape=jax.ShapeDtypeStruct((n_slots * PK_CHUNKS, 128), u32),
        compiler_params=_params(("arbitrary",)),
        name="gmlp",
    )(nblk_e, blk_off, nb_total, w1, w3, w2, xs)


COMBINE_TILE = 512
COMBINE_TILE_FUSED = 256


def _combine_kernel(final, fuse_next, *refs):
    if fuse_next:
        (dest_hbm, y_hbm, x_ref, mod_ref, sh_ref, w_ref, gf_ref, nm_ref, ng_ref, nw_ref, nb_ref, o_ref, z_ref,
         idx_ref, buf_ref, sem_idx, sem_row) = refs
    else:
        dest_hbm, y_hbm, x_ref, mod_ref, sh_ref, w_ref, gf_ref, o_ref, idx_ref, buf_ref, sem_idx, sem_row = refs
    tm = x_ref.shape[1]
    n = tm * TOP_K
    s = pl.program_id(0) * pl.num_programs(1) + pl.program_id(1)
    nsteps = pl.num_programs(0) * pl.num_programs(1)
    last = s == nsteps - 1
    nxt = jnp.minimum(s + 1, nsteps - 1)

    def idx_copy(step):
        return pltpu.make_async_copy(dest_hbm.at[pl.ds(pl.multiple_of(step * n, n), n)],
                                     idx_ref.at[pl.ds(pl.multiple_of((step & 1) * n, n), n)], sem_idx.at[step & 1])

    def row_copy(d, k, t, slot, prio):
        return pltpu.make_async_copy(y_hbm.at[pl.ds(d * PK_CHUNKS, PK_CHUNKS), :],
                                     buf_ref.at[slot, pl.ds((k * tm + t) * PK_CHUNKS, PK_CHUNKS), :],
                                     sem_row.at[slot]).start(priority=prio)

    def rows_wait(slot):
        pltpu.make_async_copy(y_hbm.at[pl.ds(0, n * PK_CHUNKS), :], buf_ref.at[slot], sem_row.at[slot]).wait()

    @pl.when(s == 0)
    def _():
        idx_copy(0).start()
        idx_copy(0).wait()

        def body(t2, carry):
            ds = [idx_ref[t2 * (2 * TOP_K) + j] for j in range(2 * TOP_K)]
            for j in range(2 * TOP_K):
                row_copy(ds[j], j % TOP_K, t2 * 2 + j // TOP_K, 0, j % 2)
            return carry

        lax.fori_loop(0, tm // 2, body, 0)

        @pl.when(nsteps > 1)
        def _():
            idx_copy(1).start()

    @pl.when(s + 1 < nsteps)
    def _():
        idx_copy(s + 1).wait()

    @pl.when(s + 2 < nsteps)
    def _():
        idx_copy(s + 2).start()

    sl = s & 1
    nsl = 1 - sl
    rows_wait(sl)
    bs = buf_ref.at[sl]
    m = mod_ref[0]
    gate = m[5:6]
    nbase = (nxt & 1) * n
    himask = jnp.uint32(0xFFFF0000)

    def group(g, carry):
        r0 = pl.multiple_of(g * 8, 8)
        ds = [idx_ref[nbase + r0 * TOP_K + j] for j in range(8 * TOP_K)]
        wg = w_ref[pl.ds(r0, 8), :]
        lo = [None] * PK_CHUNKS
        hi = [None] * PK_CHUNKS
        for k in range(TOP_K):
            wk = jnp.broadcast_to(wg[:, k:k + 1], (8, 128))
            for c in range(PK_CHUNKS):
                word = bs[pl.ds((k * tm + r0) * PK_CHUNKS + c, 8, stride=PK_CHUNKS), :]
                plo = wk * lax.bitcast_convert_type(lax.shift_left(word, jnp.uint32(16)), f32)
                phi = wk * lax.bitcast_convert_type(word & himask, f32)
                lo[c] = plo if k == 0 else lo[c] + plo
                hi[c] = phi if k == 0 else hi[c] + phi
        routed = jnp.concatenate(lo + hi, axis=1)
        o_ref[0, pl.ds(r0, 8), :] = x_ref[0, pl.ds(r0, 8), :] + gate * (routed + sh_ref[pl.ds(r0, 8), :])
        for j in range(8 * TOP_K):
            row_copy(ds[j], j % TOP_K, r0 + j // TOP_K, nsl, j % 2)
        return carry

    lax.fori_loop(0, tm // 8, group, 0)

    @pl.when(last)
    def _():
        rows_wait(nsl)

    if final:
        o_ref[0] = _rms(o_ref[0], gf_ref[...])
    if fuse_next:
        nm = nm_ref[0]
        h = _rms_mod(o_ref[0], ng_ref[...], nm[0:1], nm[1:2]).astype(bf16)
        z = _dot(h, nw_ref[...]) + nb_ref[...]
        z_ref[0] = z[:, 0:D] * _sigmoid(z[:, D:2 * D])


def _combine_call(dest_flat, y, x, mod, shared, w_tok, g_final, final, nxt=None):
    b, s, _ = x.shape
    tm = COMBINE_TILE_FUSED if nxt is not None else COMBINE_TILE
    nt = s // tm
    flat = lambda bb, i: (bb * nt + i, 0)
    row = lambda bb, i: (bb, i, 0)
    vec = lambda bb, i: (0, 0)
    in_specs = [pl.BlockSpec(memory_space=pl.ANY), pl.BlockSpec(memory_space=pl.ANY),
                pl.BlockSpec((1, tm, D), row),
                pl.BlockSpec((1, 6, D), lambda bb, i: (bb, 0, 0)),
                pl.BlockSpec((tm, D), flat),
                pl.BlockSpec((tm, TOP_K), flat),
                pl.BlockSpec((1, D), vec)]
    out_specs = pl.BlockSpec((1, tm, D), row)
    out_shape = jax.ShapeDtypeStruct((b, s, D), f32)
    args = [dest_flat, y, x, mod, shared, w_tok, g_final]
    if nxt is not None:
        in_specs += [pl.BlockSpec((1, 6, D), lambda bb, i: (bb, 0, 0)), pl.BlockSpec((1, D), vec),
                     pl.BlockSpec((D, 2 * D), vec), pl.BlockSpec((1, 2 * D), vec)]
        out_specs = [out_specs, pl.BlockSpec((1, tm, D), row)]
        out_shape = [out_shape, jax.ShapeDtypeStruct((b, s, D), f32)]
        args += list(nxt)
    return pl.pallas_call(
        functools.partial(_combine_kernel, final, nxt is not None),
        grid=(b, nt),
        in_specs=in_specs,
        out_specs=out_specs,
        out_shape=out_shape,
        scratch_shapes=[pltpu.SMEM((2 * tm * TOP_K,), i32), pltpu.VMEM((2, TOP_K * tm * PK_CHUNKS, 128), u32),
                        pltpu.SemaphoreType.DMA((2,)), pltpu.SemaphoreType.DMA((2,))],
        compiler_params=_params(("arbitrary", "arbitrary")),
        name="combine",
    )(*args)


def _moe_layer(layer, x1, mod, norm_g, router_w, router_b, w1, w3, w2, sw1, sw3, sw2, g_final, final, nxt):
    b, s, _ = x1.shape
    t = b * s
    rwt = router_w.T
    rwh = rwt.astype(bf16)
    rwl = (rwt - rwh.astype(f32)).astype(bf16)
    s13 = jnp.concatenate([sw1, sw3], axis=1).astype(bf16)
    h2p, logits_t = _ffn_pre_call(x1, mod, norm_g.reshape(1, D), rwh, rwl)
    tri = (lax.broadcasted_iota(i32, (ROUTE_TILE, ROUTE_TILE), 0)
           < lax.broadcasted_iota(i32, (ROUTE_TILE, ROUTE_TILE), 1)).astype(bf16)
    eidx, w_t, rank, counts = _route_call(logits_t, router_b.reshape(N_EXP, 1).astype(f32), tri)
    cnt = counts.reshape(N_EXP).astype(i32)
    nblk_e = (cnt + MOE_BLK - 1) // MOE_BLK
    blk_ends = jnp.cumsum(nblk_e)
    blk_off = blk_ends - nblk_e
    pad_off = blk_off * MOE_BLK
    nb_total = blk_ends[-1:].astype(i32)
    n_blk = t * TOP_K // MOE_BLK + N_EXP
    dest = _dest_call(eidx, rank, pad_off.astype(f32).reshape(N_EXP, 1))
    dest_flat = dest.T.reshape(t * TOP_K)
    xs, shared = _dispatch_call(cnt, pad_off, nb_total, dest_flat, h2p, s13, sw2.astype(bf16), n_blk * MOE_BLK)
    y = _gmlp_call(layer, nblk_e, blk_off, nb_total, xs, w1, w3, w2)
    return _combine_call(dest_flat, y, x1, mod, shared, w_t.T, g_final.reshape(1, D), final, nxt)


def _rot_cols(w):
    d, n = w.shape
    w4 = w.reshape(d, n // 32, 2, 16)
    return jnp.stack([-w4[:, :, 1], w4[:, :, 0]], axis=2).reshape(d, n)


def _rope_tables(s):
    rows = s // GRID_W
    row = jnp.repeat(jnp.arange(rows, dtype=f32), GRID_W)
    col = jnp.tile(jnp.arange(GRID_W, dtype=f32), rows)
    n_freq = HEAD_DIM // 4
    inv = ROPE_BASE ** (-jnp.arange(n_freq, dtype=f32) / n_freq)
    ang_r = row[:, None] * inv
    ang_c = col[:, None] * inv
    cos = jnp.concatenate([jnp.cos(ang_r)] * 2 + [jnp.cos(ang_c)] * 2, axis=1)
    sin = jnp.concatenate([jnp.sin(ang_r)] * 2 + [jnp.sin(ang_c)] * 2, axis=1)
    return jnp.tile(cos, (1, 2)), jnp.tile(sin, (1, 2))


def _block_diag(w):
    h, dh, _ = w.shape
    eye = jnp.eye(h, dtype=w.dtype)
    return (eye[:, None, :, None] * w[:, :, None, :]).reshape(h * dh, h * dh)


def _even_layer_mixer(x, ctx, mod, norm_g, w_in, w_out, conv_w, conv_b, w_r, b_r, w_i, b_i, lam, sink):
    b, s, _ = x.shape
    r0, r1, r2 = LRU_W, 2 * LRU_W, 2 * LRU_W + Q_W
    wq = w_in[:, r1:r2].reshape(D, 2, 4, HEAD_DIM).transpose(0, 2, 1, 3).reshape(D, Q_W)
    wk = w_in[:, r2:r2 + KV_W]
    w_ext = jnp.concatenate([w_in[:, :r1], wq, w_in[:, r2:], _rot_cols(wq), _rot_cols(wk)], axis=1).astype(bf16)
    w_ctx = jnp.concatenate([w_in[:, :r0], w_in[:, r2:]], axis=1).astype(bf16)
    cos, sin = _rope_tables(s)
    g = norm_g.reshape(1, D)
    u, gt, q, k, v = _proj_in_call(x, mod, g, w_ext, cos, sin)
    uc, kx, vx = _proj_ctx_call(ctx, mod, g, w_ctx)
    n_lg = LRU_W // LRU_LANES
    hpg = LRU_LANES // HEAD_DIM

    def lane_groups(w):
        return jnp.stack([jnp.stack([_block_diag(w[d, g * hpg:(g + 1) * hpg]) for g in range(n_lg)]) for d in range(2)])

    wg = jnp.concatenate([lane_groups(w_r), lane_groups(w_i)], axis=-1).astype(bf16)
    bg = jnp.concatenate([b_r.reshape(2, n_lg, 1, LRU_LANES), b_i.reshape(2, n_lg, 1, LRU_LANES)], axis=-1)
    rec = _rglru_call(u, uc, conv_w, conv_b.reshape(1, LRU_W), wg, bg, lam.reshape(2, n_lg, 1, LRU_LANES))
    att = _attn_call(sink, q, k, v, kx, vx)
    w_att = w_out[LRU_W:].reshape(2, 4, HEAD_DIM, D).transpose(1, 0, 2, 3).reshape(Q_W, D).astype(bf16)
    return _mix_out_call(x, mod, gt, rec, att, w_out[:LRU_W].astype(bf16), w_att)


def kernel(x, c, ctx, c_ctx, mod_w, mod_b, norm_mix_g, norm_ffn_g, final_norm_g, ab_w_in, ab_w_out, lru_conv_w,
           lru_conv_b, lru_wr, lru_br, lru_wi, lru_bi, lru_lambda, attn_sink, cm_w_in, cm_b_in, cm_dw_w, cm_dw_b,
           cm_ln_g, cm_ln_b, cm_w_out, cm_b_out, router_w, router_b, exp_w1, exp_w3, exp_w2, shared_w1, shared_w3,
           shared_w2):
    bsz = x.shape[0]
    depth = mod_w.shape[0]
    assert bsz + 1 <= MOD_ROWS - 7
    cc = jnp.zeros((MOD_ROWS, D), f32).at[:bsz].set(c).at[MOD_ROWS - 8].set(c_ctx)
    mod_all = _mod_call(cc, mod_w, mod_b).reshape(depth, MOD_ROWS, 6, D)
    zg = None
    for l in range(depth):
        mod = mod_all[l]
        last = l == depth - 1
        nxt = None
        if not last and (l + 1) % 2 == 1:
            o1 = (l + 1) // 2
            nxt = (mod_all[l + 1], norm_mix_g[l + 1].reshape(1, D), cm_w_in[o1].astype(bf16),
                   cm_b_in[o1].reshape(1, 2 * D))
        if l % 2 == 0:
            e = l // 2
            assert depth <= 2
            x1 = _even_layer_mixer(x, ctx, mod, norm_mix_g[l], ab_w_in[e], ab_w_out[e], lru_conv_w[e], lru_conv_b[e],
                                   lru_wr[e], lru_br[e], lru_wi[e], lru_bi[e], lru_lambda[e], attn_sink[e])
        else:
            o = l // 2
            if zg is None:
                zg = _conf_in_call(x, mod, norm_mix_g[l].reshape(1, D), cm_w_in[o].astype(bf16),
                                   cm_b_in[o].reshape(1, 2 * D))
            dw = jnp.concatenate([cm_dw_w[o], jnp.zeros((1, D), f32)], axis=0)
            x1 = _conf_out_call(x, mod, zg, dw, cm_dw_b[o].reshape(1, D), cm_ln_g[o].reshape(1, D),
                                cm_ln_b[o].reshape(1, D), cm_w_out[o].astype(bf16), cm_b_out[o].reshape(1, D))
        out = _moe_layer(l, x1, mod, norm_ffn_g[l], router_w[l], router_b[l], exp_w1, exp_w3, exp_w2,
                         shared_w1[l], shared_w3[l], shared_w2[l], final_norm_g, last, nxt)
        x, zg = out if nxt is not None else (out, None)
    return x
```
